```python
import math
import jax, jax.numpy as jnp
from jax import lax
import numpy as np

D_MODEL = 1024
BATCH = 2
SEQ = 16384
DEPTH = 2

GRID_W = 64
CTX_LEN = 256
Q_BLOCK = 128
ROPE_BASE = 10000.0
EPS = 1e-6

DIFF_HEADS = 4
DIFF_HEAD_DIM = 64
DIFF_V_DIM = 2 * DIFF_HEAD_DIM
HYENA_WIDTH = 256
HYENA_ORDER = 2
HYENA_SHORT = 3
FILT_EMB = 33
FILT_HIDDEN = 64
DECAY_TARGET = 1e-2
FAST_DECAY = 0.3
SLOW_DECAY = 1.5
CONF_WIDTH = 256
CONF_KERNEL = 31
MLA_HEADS = 4
MLA_Q_RANK = 256
MLA_KV_RANK = 128
MLA_NOPE = 64
MLA_ROPE = 32
MLA_V = 64
MLA_SCALE = (MLA_NOPE + MLA_ROPE) ** -0.5
N_BRANCH = 4
N_EXPERTS = 16
EC_CAPACITY = 2
EXPERT_HIDDEN = 1024

DIFF_QK_W = DIFF_HEADS * 2 * DIFF_HEAD_DIM
DIFF_V_W = DIFF_HEADS * DIFF_V_DIM
HYENA_PROJ = (HYENA_ORDER + 1) * HYENA_WIDTH
CONF_PROJ = 2 * CONF_WIDTH
GATE_PROJ = N_BRANCH * D_MODEL
IN_SPLITS = (DIFF_QK_W, DIFF_QK_W, DIFF_V_W, HYENA_PROJ, CONF_PROJ, MLA_Q_RANK, MLA_KV_RANK, MLA_ROPE, GATE_PROJ)
IN_WIDTH = DIFF_QK_W * 2 + DIFF_V_W + HYENA_PROJ + CONF_PROJ + MLA_Q_RANK + MLA_KV_RANK + MLA_ROPE + GATE_PROJ
BRANCH_WIDTHS = (DIFF_V_W, HYENA_WIDTH, CONF_WIDTH, MLA_HEADS * MLA_V)
MIX_WIDTH = DIFF_V_W + HYENA_WIDTH + CONF_WIDTH + MLA_HEADS * MLA_V

kernel_name = 'hybrid_diffusion_parallel_mixers_ec_moe'


def _split(z, sizes):
    out, start = [], 0
    for s in sizes:
        out.append(z[..., start:start + s])
        start += s
    return out


def rms_norm(x, g):
    xf = x.astype(jnp.float32)
    y = xf * lax.rsqrt(jnp.mean(xf * xf, axis=-1, keepdims=True) + EPS)
    return y.astype(x.dtype) * g


def layer_norm(x, g, b):
    xf = x.astype(jnp.float32)
    mu = jnp.mean(xf, axis=-1, keepdims=True)
    var = jnp.mean(jnp.square(xf - mu), axis=-1, keepdims=True)
    return ((xf - mu) * lax.rsqrt(var + EPS)).astype(x.dtype) * g + b


def modulate(x, shift, scale):
    return x * (1.0 + scale) + shift


def axial_rope_tables(n_tok, rot_dim):
    rows = n_tok // GRID_W
    row = jnp.repeat(jnp.arange(rows), GRID_W).astype(jnp.float32)
    col = jnp.tile(jnp.arange(GRID_W), rows).astype(jnp.float32)
    nf = rot_dim // 4
    inv = ROPE_BASE ** (-jnp.arange(nf, dtype=jnp.float32) / nf)
    ang = jnp.concatenate([row[:, None] * inv, col[:, None] * inv], axis=-1)
    return jnp.cos(ang), jnp.sin(ang)


def apply_rope(x, cos, sin):
    half = x.shape[-1] // 2
    x1, x2 = x[..., :half], x[..., half:]
    return jnp.concatenate([x1 * cos - x2 * sin, x1 * sin + x2 * cos], axis=-1).astype(x.dtype)


def depthwise_conv(u, w):
    k, c = w.shape
    return lax.conv_general_dilated(u, w[:, None, :].astype(u.dtype), window_strides=(1,),
                                    padding=[(k // 2, k // 2)], dimension_numbers=('NWC', 'WIO', 'NWC'),
                                    feature_group_count=c)


def attn_probs(q, k, scale):
    s = jnp.einsum('...qd,...kd->...qk', q, k).astype(jnp.float32) * scale
    return jax.nn.softmax(s, axis=-1)


def softmax_attend(q, k, v, scale):
    p = attn_probs(q, k, scale)
    return jnp.einsum('bhqk,bhkv->bhqv', p.astype(v.dtype), v)


def diff_attend(q, k, v, lam):
    p = attn_probs(q, k, DIFF_HEAD_DIM ** -0.5)
    a = p[:, :, 0] - lam * p[:, :, 1]
    return jnp.einsum('bhqk,bhkv->bhqv', a.astype(v.dtype), v)


def sweep_query_blocks(fn, q):
    *lead, s, d = q.shape
    nb = s // Q_BLOCK
    qb = jnp.moveaxis(q.reshape(*lead, nb, Q_BLOCK, d), -3, 0)
    out = jnp.moveaxis(lax.map(fn, qb), 0, -3)
    return out.reshape(*out.shape[:-3], s, out.shape[-1])


def diff_heads(z):
    b, n, _ = z.shape
    return z.reshape(b, n, DIFF_HEADS, 2, DIFF_HEAD_DIM).transpose(0, 2, 3, 1, 4)


def value_heads(z, n_heads):
    b, n, _ = z.shape
    return z.reshape(b, n, n_heads, -1).transpose(0, 2, 1, 3)


def merge_heads(o):
    b, h, n, dv = o.shape
    return o.transpose(0, 2, 1, 3).reshape(b, n, h * dv)


def diff_finish(o, g, lam_init):
    return merge_heads(rms_norm(o, g) * (1.0 - lam_init))


def mla_queries(cq, p, rope):
    q = value_heads(rms_norm(cq, p['mla_q_norm_g']) @ p['mla_w_uq'], MLA_HEADS)
    q_nope, q_rope = q[..., :MLA_NOPE], q[..., MLA_NOPE:]
    if rope is not None:
        q_rope = apply_rope(q_rope, *rope)
    return jnp.concatenate([q_nope, q_rope], axis=-1)


def mla_keys_values(ckv, kr, p, rope):
    kv = value_heads(rms_norm(ckv, p['mla_kv_norm_g']) @ p['mla_w_ukv'], MLA_HEADS)
    k_nope, v = kv[..., :MLA_NOPE], kv[..., MLA_NOPE:]
    kr = kr[:, None]
    if rope is not None:
        kr = apply_rope(kr, *rope)
    k = jnp.concatenate([k_nope, jnp.broadcast_to(kr, k_nope.shape[:-1] + (MLA_ROPE,))], axis=-1)
    return k, v


def hyena_filters(n, p):
    t = jnp.linspace(0.0, 1.0, n, dtype=jnp.float32)[:, None]
    bands = (FILT_EMB - 1) // 2
    w = (2.0 * math.pi / n) * jnp.arange(n, dtype=jnp.float32)[:, None]
    f = jnp.linspace(1e-4, bands - 1, bands, dtype=jnp.float32)[None, :]
    z = jnp.concatenate([t, jnp.cos(f * w), -jnp.sin(f * w)], axis=-1)
    hid = jnp.sin(p['filt_freq'][0] * (z @ p['filt_w1'] + p['filt_b1']))
    hid = jnp.sin(p['filt_freq'][1] * (hid @ p['filt_w2'] + p['filt_b2']))
    h = (hid @ p['filt_w3']).astype(jnp.float32).reshape(n, HYENA_ORDER, 2, HYENA_WIDTH)
    deltas = jnp.abs(jnp.linspace(math.log(DECAY_TARGET) / SLOW_DECAY, math.log(DECAY_TARGET) / FAST_DECAY,
                                  HYENA_WIDTH, dtype=jnp.float32))
    h = h * jnp.exp(-t * deltas)[:, None, None, :]
    return h / jnp.sum(jnp.abs(h), axis=0, keepdims=True)


def centred_long_conv(u, h_fwd, h_bwd):
    n = u.shape[1]
    k = jnp.concatenate([h_fwd, jnp.zeros_like(h_fwd[:1]), h_bwd[:-1][::-1]], axis=0)
    u_f = jnp.fft.rfft(u.astype(jnp.float32), n=2 * n, axis=1)
    k_f = jnp.fft.rfft(k, axis=0)
    y = jnp.fft.irfft(u_f * k_f[None], n=2 * n, axis=1)[:, :n]
    return y.astype(u.dtype)


def hyena_branch(z, p):
    n = z.shape[1]
    z = depthwise_conv(z, p['hyena_short_w']) + p['hyena_short_b']
    x1, x2, v = _split(z, (HYENA_WIDTH, HYENA_WIDTH, HYENA_WIDTH))
    h = hyena_filters(n, p)
    for o, gate in enumerate((x1, x2)):
        v = gate * (centred_long_conv(v, h[:, o, 0], h[:, o, 1]) + p['hyena_skip'][o] * v)
    return v


def conformer_branch(z, p):
    a, b = _split(z, (CONF_WIDTH, CONF_WIDTH))
    u = depthwise_conv(a * jax.nn.sigmoid(b), p['conf_dw_w'])
    return jax.nn.silu(layer_norm(u, p['conf_ln_g'], p['conf_ln_b']))


def merge_branches(ys, g, p):
    gates = jax.nn.sigmoid(g.reshape(*g.shape[:-1], N_BRANCH, D_MODEL))
    acc, start = 0.0, 0
    for i, (y, w) in enumerate(zip(ys, BRANCH_WIDTHS)):
        acc = acc + gates[..., i, :] * (y @ p['w_branch'][start:start + w])
        start += w
    return acc @ p['w_out']


def token_mixer(u_lat, u_ctx, p, lam, lam_init, rope_diff, rope_mla, with_ctx_out):
    dq_l, dk_l, dv_l, hy_l, cf_l, cq_l, ckv_l, kr_l, g_l = _split(u_lat @ p['w_in'], IN_SPLITS)
    dq_c, dk_c, dv_c, hy_c, cf_c, cq_c, ckv_c, kr_c, g_c = _split(u_ctx @ p['w_in'], IN_SPLITS)
    dk_c, dv_c = diff_heads(dk_c), value_heads(dv_c, DIFF_HEADS)
    mk_c, mv_c = mla_keys_values(ckv_c, kr_c, p, None)
    dk_all = jnp.concatenate([dk_c, apply_rope(diff_heads(dk_l), *rope_diff)], axis=-2)
    dv_all = jnp.concatenate([dv_c, value_heads(dv_l, DIFF_HEADS)], axis=-2)
    mk_l, mv_l = mla_keys_values(ckv_l, kr_l, p, rope_mla)
    mk_all = jnp.concatenate([mk_c, mk_l], axis=-2)
    mv_all = jnp.concatenate([mv_c, mv_l], axis=-2)
    dq = apply_rope(diff_heads(dq_l), *rope_diff)
    a_lat = sweep_query_blocks(lambda qb: diff_attend(qb, dk_all, dv_all, lam), dq)
    m_lat = sweep_query_blocks(lambda qb: softmax_attend(qb, mk_all, mv_all, MLA_SCALE), mla_queries(cq_l, p, rope_mla))
    y_lat = merge_branches((diff_finish(a_lat, p['diff_subln_g'], lam_init), hyena_branch(hy_l, p),
                            conformer_branch(cf_l, p), merge_heads(m_lat)), g_l, p)
    if not with_ctx_out:
        return y_lat, None
    a_ctx = diff_attend(diff_heads(dq_c), dk_c, dv_c, lam)
    m_ctx = softmax_attend(mla_queries(cq_c, p, None), mk_c, mv_c, MLA_SCALE)
    y_ctx = merge_branches((diff_finish(a_ctx, p['diff_subln_g'], lam_init), hyena_branch(hy_c, p),
                            conformer_branch(cf_c, p), merge_heads(m_ctx)), g_c, p)
    return y_lat, y_ctx


def expert_choice_ffn(u, p):
    b, n, d = u.shape
    cap = max(1, EC_CAPACITY * n // N_EXPERTS)
    aff = jax.nn.softmax((u @ p['w_router']).astype(jnp.float32), axis=-1)
    gate, idx = lax.top_k(jnp.swapaxes(aff, 1, 2), cap)
    xe = jax.vmap(lambda ub, ib: ub[ib])(u, idx)
    hg, hu = jnp.split(jnp.einsum('becd,edf->becf', xe, p['w_exp_in']), 2, axis=-1)
    ye = jnp.einsum('becf,efd->becd', jax.nn.silu(hg) * hu, p['w_exp_out']) * gate[..., None].astype(u.dtype)
    return jax.vmap(lambda yb, ib: jnp.zeros((n, d), yb.dtype).at[ib.reshape(-1)].add(yb.reshape(-1, d)))(ye, idx)


def setup_inputs(seed: int = 0) -> dict:
    key = jax.random.key(seed)
    ks = iter(jax.random.split(key, 40))

    def nrm(shape, scale):
        return scale * jax.random.normal(next(ks), shape, jnp.float32)

    def gain(shape):
        return 1.0 + nrm(shape, 0.02)

    L, D = DEPTH, D_MODEL
    return {
        'x': nrm((BATCH, SEQ, D), 1.0),
        'c': nrm((BATCH, D), 1.0),
        'ctx': nrm((BATCH, CTX_LEN, D), 1.0),
        'c_ctx': nrm((D,), 1.0),
        'ada_w': nrm((L, D, 6 * D), 0.5 * D ** -0.5),
        'ada_b': nrm((L, 6 * D), 0.02),
        'norm_mix_g': gain((L, D)),
        'norm_ffn_g': gain((L, D)),
        'w_in': nrm((L, D, IN_WIDTH), D ** -0.5),
        'diff_lambda': nrm((L, 4, DIFF_HEAD_DIM), 0.1),
        'diff_subln_g': gain((L, DIFF_V_DIM)),
        'hyena_short_w': nrm((L, HYENA_SHORT, HYENA_PROJ), HYENA_SHORT ** -0.5),
        'hyena_short_b': nrm((L, HYENA_PROJ), 0.02),
        'filt_w1': nrm((L, FILT_EMB, FILT_HIDDEN), FILT_EMB ** -0.5),
        'filt_b1': nrm((L, FILT_HIDDEN), 0.02),
        'filt_freq': gain((L, 2, FILT_HIDDEN)),
        'filt_w2': nrm((L, FILT_HIDDEN, FILT_HIDDEN), FILT_HIDDEN ** -0.5),
        'filt_b2': nrm((L, FILT_HIDDEN), 0.02),
        'filt_w3': nrm((L, FILT_HIDDEN, HYENA_ORDER * 2 * HYENA_WIDTH), FILT_HIDDEN ** -0.5),
        'hyena_skip': nrm((L, HYENA_ORDER, HYENA_WIDTH), 0.5),
        'conf_dw_w': nrm((L, CONF_KERNEL, CONF_WIDTH), CONF_KERNEL ** -0.5),
        'conf_ln_g': gain((L, CONF_WIDTH)),
        'conf_ln_b': nrm((L, CONF_WIDTH), 0.02),
        'mla_q_norm_g': gain((L, MLA_Q_RANK)),
        'mla_kv_norm_g': gain((L, MLA_KV_RANK)),
        'mla_w_uq': nrm((L, MLA_Q_RANK, MLA_HEADS * (MLA_NOPE + MLA_ROPE)), MLA_Q_RANK ** -0.5),
        'mla_w_ukv': nrm((L, MLA_KV_RANK, MLA_HEADS * (MLA_NOPE + MLA_V)), MLA_KV_RANK ** -0.5),
        'w_branch': nrm((L, MIX_WIDTH, D), 512 ** -0.5),
        'w_out': nrm((L, D, D), D ** -0.5),
        'w_router': nrm((L, D, N_EXPERTS), D ** -0.5),
        'w_exp_in': nrm((L, N_EXPERTS, D, 2 * EXPERT_HIDDEN), D ** -0.5),
        'w_exp_out': nrm((L, N_EXPERTS, EXPERT_HIDDEN, D), EXPERT_HIDDEN ** -0.5),
        'final_norm_g': gain((D,)),
    }


def reference(x, c, ctx, c_ctx, ada_w, ada_b, norm_mix_g, norm_ffn_g, w_in, diff_lambda, diff_subln_g,
              hyena_short_w, hyena_short_b, filt_w1, filt_b1, filt_freq, filt_w2, filt_b2, filt_w3, hyena_skip,
              conf_dw_w, conf_ln_g, conf_ln_b, mla_q_norm_g, mla_kv_norm_g, mla_w_uq, mla_w_ukv,
              w_branch, w_out, w_router, w_exp_in, w_exp_out, final_norm_g):
    n_lat = x.shape[1]
    rope_diff = axial_rope_tables(n_lat, DIFF_HEAD_DIM)
    rope_mla = axial_rope_tables(n_lat, MLA_ROPE)
    s_lat = jax.nn.silu(c)
    s_ctx = jax.nn.silu(c_ctx)[None]
    h_lat, h_ctx = x, ctx
    for l in range(DEPTH):
        last = l == DEPTH - 1
        p = dict(w_in=w_in[l], diff_subln_g=diff_subln_g[l], hyena_short_w=hyena_short_w[l],
                 hyena_short_b=hyena_short_b[l], filt_w1=filt_w1[l], filt_b1=filt_b1[l], filt_freq=filt_freq[l],
                 filt_w2=filt_w2[l], filt_b2=filt_b2[l], filt_w3=filt_w3[l], hyena_skip=hyena_skip[l],
                 conf_dw_w=conf_dw_w[l], conf_ln_g=conf_ln_g[l], conf_ln_b=conf_ln_b[l],
                 mla_q_norm_g=mla_q_norm_g[l], mla_kv_norm_g=mla_kv_norm_g[l], mla_w_uq=mla_w_uq[l],
                 mla_w_ukv=mla_w_ukv[l], w_branch=w_branch[l], w_out=w_out[l], w_router=w_router[l],
                 w_exp_in=w_exp_in[l], w_exp_out=w_exp_out[l])
        mod_lat = jnp.split((s_lat @ ada_w[l] + ada_b[l])[:, None, :], 6, axis=-1)
        mod_ctx = jnp.split((s_ctx @ ada_w[l] + ada_b[l])[:, None, :], 6, axis=-1)
        lam_init = 0.8 - 0.6 * math.exp(-0.3 * l)
        lq1, lk1, lq2, lk2 = diff_lambda[l].astype(jnp.float32)
        lam = jnp.exp(jnp.sum(lq1 * lk1)) - jnp.exp(jnp.sum(lq2 * lk2)) + lam_init
        u_lat = modulate(rms_norm(h_lat, norm_mix_g[l]), mod_lat[0], mod_lat[1])
        u_ctx = modulate(rms_norm(h_ctx, norm_mix_g[l]), mod_ctx[0], mod_ctx[1])
        y_lat, y_ctx = token_mixer(u_lat, u_ctx, p, lam, lam_init, rope_diff, rope_mla, not last)
        h_lat = h_lat + mod_lat[2] * y_lat
        u_lat = modulate(rms_norm(h_lat, norm_ffn_g[l]), mod_lat[3], mod_lat[4])
        h_lat = h_lat + mod_lat[5] * expert_choice_ffn(u_lat, p)
        if not last:
            h_ctx = h_ctx + mod_ctx[2] * y_ctx
            u_ctx = modulate(rms_norm(h_ctx, norm_ffn_g[l]), mod_ctx[3], mod_ctx[4])
            h_ctx = h_ctx + mod_ctx[5] * expert_choice_ffn(u_ctx, p)
    return rms_norm(h_lat, final_norm_g)
```

```python
import functools
import math

import jax
import jax.numpy as jnp
from jax import lax
from jax.experimental import pallas as pl
from jax.experimental.pallas import tpu as pltpu

GRID_W = 64
ROPE_BASE = 10000.0
EPS = 1e-6

DIFF_HEADS = 4
DIFF_HEAD_DIM = 64
DIFF_V_DIM = 2 * DIFF_HEAD_DIM
HYENA_WIDTH = 256
HYENA_ORDER = 2
FILT_EMB = 33
DECAY_TARGET = 1e-2
FAST_DECAY = 0.3
SLOW_DECAY = 1.5
CONF_WIDTH = 256
MLA_HEADS = 4
MLA_Q_RANK = 256
MLA_KV_RANK = 128
MLA_NOPE = 64
MLA_ROPE = 32
MLA_V = 64
MLA_SCALE = (MLA_NOPE + MLA_ROPE) ** -0.5
N_BRANCH = 4
N_EXPERTS = 16
EC_CAPACITY = 2

DIFF_QK_W = DIFF_HEADS * 2 * DIFF_HEAD_DIM
DIFF_V_W = DIFF_HEADS * DIFF_V_DIM
HYENA_PROJ = (HYENA_ORDER + 1) * HYENA_WIDTH
CONF_PROJ = 2 * CONF_WIDTH
IN_SPLITS = (DIFF_QK_W, DIFF_QK_W, DIFF_V_W, HYENA_PROJ, CONF_PROJ, MLA_Q_RANK, MLA_KV_RANK, MLA_ROPE)
BRANCH_WIDTHS = (DIFF_V_W, HYENA_WIDTH, CONF_WIDTH, MLA_HEADS * MLA_V)

HEAD_LANES = 128
LOG2E = 1.4426950408889634
VMEM_LIMIT_BYTES = 48 * 1024 * 1024

F32 = jnp.float32
BF16 = jnp.bfloat16


def _flash_kernel(lam_ref, qT_ref, kc_ref, vcT_ref, *rest, n_maps, n_lat_chunks):
    if n_lat_chunks:
        kl_ref, vlT_ref, oT_ref, acc_ref, m_ref, l_ref = rest
    else:
        oT_ref, acc_ref, m_ref, l_ref = rest
    qT = qT_ref[...]
    tq = qT.shape[1]
    if n_maps == 2:
        row = lax.broadcasted_iota(jnp.int32, qT.shape, 0)
        zero = jnp.zeros_like(qT)
        q2 = jnp.concatenate([jnp.where(row < DIFF_HEAD_DIM, qT, zero),
                              jnp.where(row >= DIFF_HEAD_DIM, qT, zero)], axis=1)
    else:
        q2 = qT
    m_ref[...] = jnp.full(m_ref.shape, -jnp.inf, F32)
    l_ref[...] = jnp.zeros(l_ref.shape, F32)
    acc_ref[...] = jnp.zeros(acc_ref.shape, F32)

    def step(k, vT):
        s = jnp.dot(k, q2, preferred_element_type=F32)
        m_prev = m_ref[...]
        m_new = jnp.maximum(m_prev, jnp.max(s, axis=0, keepdims=True))
        alpha = jnp.exp2(m_prev - m_new)
        p = jnp.exp2(s - m_new)
        l_ref[...] = alpha * l_ref[...] + jnp.sum(p, axis=0, keepdims=True)
        acc_ref[...] = alpha * acc_ref[...] + jnp.dot(vT, p.astype(BF16), preferred_element_type=F32)
        m_ref[...] = m_new

    step(kc_ref[...], vcT_ref[...])
    if n_lat_chunks:
        def body(c, carry):
            step(kl_ref[c], vlT_ref[c])
            return carry
        lax.fori_loop(0, n_lat_chunks, body, 0)
    o = acc_ref[...] / l_ref[...]
    if n_maps == 2:
        o = o[:, :tq] - lam_ref[0] * o[:, tq:]
    oT_ref[...] = o


def _flash_attention(lam, qT, kc, vcT, kl, vlT, *, n_maps, tq):
    b, h, _, s = qT.shape
    lc = kc.shape[2]
    r = n_maps * tq
    in_specs = [
        pl.BlockSpec(memory_space=pltpu.SMEM),
        pl.BlockSpec((None, None, HEAD_LANES, tq), lambda bi, hi, qi: (bi, hi, 0, qi)),
        pl.BlockSpec((None, None, lc, HEAD_LANES), lambda bi, hi, qi: (bi, hi, 0, 0)),
        pl.BlockSpec((None, None, HEAD_LANES, lc), lambda bi, hi, qi: (bi, hi, 0, 0)),
    ]
    args = [lam, qT, kc, vcT]
    n_lat_chunks = 0
    if kl is not None:
        n_lat_chunks, tk = kl.shape[2], kl.shape[3]
        in_specs += [
            pl.BlockSpec((None, None, n_lat_chunks, tk, HEAD_LANES), lambda bi, hi, qi: (bi, hi, 0, 0, 0)),
            pl.BlockSpec((None, None, n_lat_chunks, HEAD_LANES, tk), lambda bi, hi, qi: (bi, hi, 0, 0, 0)),
        ]
        args += [kl, vlT]
    return pl.pallas_call(
        functools.partial(_flash_kernel, n_maps=n_maps, n_lat_chunks=n_lat_chunks),
        grid=(b, h, s // tq),
        in_specs=in_specs,
        out_specs=pl.BlockSpec((None, None, HEAD_LANES, tq), lambda bi, hi, qi: (bi, hi, 0, qi)),
        out_shape=jax.ShapeDtypeStruct((b, h, HEAD_LANES, s), F32),
        scratch_shapes=[pltpu.VMEM((HEAD_LANES, r), F32), pltpu.VMEM((1, r), F32), pltpu.VMEM((1, r), F32)],
        compiler_params=pltpu.CompilerParams(
            dimension_semantics=("arbitrary", "arbitrary", "arbitrary"),
            vmem_limit_bytes=VMEM_LIMIT_BYTES),
        name=f"flash_attention_{n_maps}map",
    )(*args)


def _chunk_keys(k, tk):
    b, h, n, d = k.shape
    return k.reshape(b, h, n // tk, tk, d)


def _chunk_values_t(v, tk):
    b, h, n, d = v.shape
    return v.reshape(b, h, n // tk, tk, d).transpose(0, 1, 2, 4, 3)


def _split(z, sizes):
    out, start = [], 0
    for s in sizes:
        out.append(z[..., start:start + s])
        start += s
    return out


def _rms_norm(x, g):
    xf = x.astype(F32)
    return xf * lax.rsqrt(jnp.mean(xf * xf, axis=-1, keepdims=True) + EPS) * g


def _layer_norm(x, g, b):
    mu = jnp.mean(x, axis=-1, keepdims=True)
    var = jnp.mean(jnp.square(x - mu), axis=-1, keepdims=True)
    return (x - mu) * lax.rsqrt(var + EPS) * g + b


def _rope_tables(n_tok, rot_dim):
    rows = n_tok // GRID_W
    row = jnp.repeat(jnp.arange(rows), GRID_W).astype(F32)
    col = jnp.tile(jnp.arange(GRID_W), rows).astype(F32)
    nf = rot_dim // 4
    inv = ROPE_BASE ** (-jnp.arange(nf, dtype=F32) / nf)
    ang = jnp.concatenate([row[:, None] * inv, col[:, None] * inv], axis=-1)
    return jnp.cos(ang), jnp.sin(ang)


def _apply_rope(x, cos, sin):
    half = x.shape[-1] // 2
    x1, x2 = x[..., :half], x[..., half:]
    return jnp.concatenate([x1 * cos - x2 * sin, x1 * sin + x2 * cos], axis=-1)


def _depthwise_conv(u, w):
    k, c = w.shape
    return lax.conv_general_dilated(u, w[:, None, :], window_strides=(1,), padding=[(k // 2, k // 2)],
                                    dimension_numbers=('NWC', 'WIO', 'NWC'), feature_group_count=c)


def _hyena_filters(n, p):
    t = jnp.linspace(0.0, 1.0, n, dtype=F32)[:, None]
    bands = (FILT_EMB - 1) // 2
    w = (2.0 * math.pi / n) * jnp.arange(n, dtype=F32)[:, None]
    f = jnp.linspace(1e-4, bands - 1, bands, dtype=F32)[None, :]
    z = jnp.concatenate([t, jnp.cos(f * w), -jnp.sin(f * w)], axis=-1)
    hid = jnp.sin(p['filt_freq'][0] * (z @ p['filt_w1'] + p['filt_b1']))
    hid = jnp.sin(p['filt_freq'][1] * (hid @ p['filt_w2'] + p['filt_b2']))
    h = (hid @ p['filt_w3']).reshape(n, HYENA_ORDER, 2, HYENA_WIDTH)
    deltas = jnp.abs(jnp.linspace(math.log(DECAY_TARGET) / SLOW_DECAY, math.log(DECAY_TARGET) / FAST_DECAY,
                                  HYENA_WIDTH, dtype=F32))
    h = h * jnp.exp(-t * deltas)[:, None, None, :]
    return h / jnp.sum(jnp.abs(h), axis=0, keepdims=True)


def _centred_long_conv(u, h_fwd, h_bwd):
    n = u.shape[1]
    k = jnp.concatenate([h_fwd, jnp.zeros_like(h_fwd[:1]), h_bwd[:-1][::-1]], axis=0)
    u_f = jnp.fft.rfft(u, n=2 * n, axis=1)
    k_f = jnp.fft.rfft(k, axis=0)
    return jnp.fft.irfft(u_f * k_f[None], n=2 * n, axis=1)[:, :n]


def _hyena_branch(z, p):
    n = z.shape[1]
    z = _depthwise_conv(z, p['hyena_short_w']) + p['hyena_short_b']
    x1, x2, v = _split(z, (HYENA_WIDTH, HYENA_WIDTH, HYENA_WIDTH))
    h = _hyena_filters(n, p)
    for o, gate in enumerate((x1, x2)):
        v = gate * (_centred_long_conv(v, h[:, o, 0], h[:, o, 1]) + p['hyena_skip'][o] * v)
    return v


def _conformer_branch(z, p):
    a, b = _split(z, (CONF_WIDTH, CONF_WIDTH))
    u = _depthwise_conv(a * jax.nn.sigmoid(b), p['conf_dw_w'])
    return jax.nn.silu(_layer_norm(u, p['conf_ln_g'], p['conf_ln_b']))


def _merge_branches(ys, g, p):
    gates = jax.nn.sigmoid(g.reshape(*g.shape[:-1], N_BRANCH, g.shape[-1] // N_BRANCH))
    acc, start = 0.0, 0
    for i, (y, w) in enumerate(zip(ys, BRANCH_WIDTHS)):
        acc = acc + gates[..., i, :] * (y @ p['w_branch'][start:start + w])
        start += w
    return acc @ p['w_out']


def _expert_choice_ffn(u, p):
    b, n, d = u.shape
    cap = max(1, EC_CAPACITY * n // N_EXPERTS)
    aff = jax.nn.softmax((u @ p['w_router']).astype(F32), axis=-1)
    gate, idx = lax.top_k(jnp.swapaxes(aff, 1, 2), cap)
    xe = jax.vmap(lambda ub, ib: ub[ib])(u, idx)
    hg, hu = jnp.split(jnp.einsum('becd,edf->becf', xe, p['w_exp_in']), 2, axis=-1)
    ye = jnp.einsum('becf,efd->becd', jax.nn.silu(hg) * hu, p['w_exp_out']) * gate[..., None]
    return jax.vmap(lambda yb, ib: jnp.zeros((n, d), yb.dtype).at[ib.reshape(-1)].add(yb.reshape(-1, d)))(ye, idx)


def _diff_heads(z):
    b, n, _ = z.shape
    return z.reshape(b, n, DIFF_HEADS, 2, DIFF_HEAD_DIM).transpose(0, 2, 3, 1, 4)


def _value_heads(z, n_heads):
    b, n, _ = z.shape
    return z.reshape(b, n, n_heads, -1).transpose(0, 2, 1, 3)


def _merge_heads(o):
    b, h, n, dv = o.shape
    return o.transpose(0, 2, 1, 3).reshape(b, n, h * dv)


def _maps_to_lanes(z):
    b, h, _, n, d = z.shape
    return z.transpose(0, 1, 3, 2, 4).reshape(b, h, n, 2 * d)


def _pad_lanes(z):
    return jnp.pad(z, [(0, 0)] * (z.ndim - 1) + [(0, HEAD_LANES - z.shape[-1])])


def _mla_queries(cq, p, rope):
    q = _value_heads(_rms_norm(cq, p['mla_q_norm_g']) @ p['mla_w_uq'], MLA_HEADS)
    q_nope, q_rope = q[..., :MLA_NOPE], q[..., MLA_NOPE:]
    if rope is not None:
        q_rope = _apply_rope(q_rope, *rope)
    return jnp.concatenate([q_nope, q_rope], axis=-1)


def _mla_keys_values(ckv, kr, p, rope):
    kv = _value_heads(_rms_norm(ckv, p['mla_kv_norm_g']) @ p['mla_w_ukv'], MLA_HEADS)
    k_nope, v = kv[..., :MLA_NOPE], kv[..., MLA_NOPE:]
    kr = kr[:, None]
    if rope is not None:
        kr = _apply_rope(kr, *rope)
    k = jnp.concatenate([k_nope, jnp.broadcast_to(kr, k_nope.shape[:-1] + (MLA_ROPE,))], axis=-1)
    return k, v


def _lat_tiles(s):
    tk = min(1024, s)
    return tk


def _token_mixer(u_lat, u_ctx, p, lam, lam_init, rope_diff, rope_mla, with_ctx_out):
    s = u_lat.shape[1]
    tk = _lat_tiles(s)
    w_in_main, w_in_gate = p['w_in'][:, :sum(IN_SPLITS)], p['w_in'][:, sum(IN_SPLITS):]
    z_l, z_c = u_lat @ w_in_main, u_ctx @ w_in_main
    g_l = u_lat @ w_in_gate
    dq_l, dk_l, dv_l, hy_l, cf_l, cq_l, ckv_l, kr_l = _split(z_l, IN_SPLITS)
    dq_c, dk_c, dv_c, hy_c, cf_c, cq_c, ckv_c, kr_c = _split(z_c, IN_SPLITS)
    lam_arr = jnp.reshape(lam, (1,)).astype(F32)

    dscale = DIFF_HEAD_DIM ** -0.5 * LOG2E
    dk_c_h = _maps_to_lanes(_diff_heads(dk_c)).astype(BF16)
    dv_c_h = _value_heads(dv_c, DIFF_HEADS).astype(BF16)
    dk_l_h = _maps_to_lanes(_apply_rope(_diff_heads(dk_l), *rope_diff)).astype(BF16)
    dv_l_h = _value_heads(dv_l, DIFF_HEADS).astype(BF16)
    dq_l_t = (_maps_to_lanes(_apply_rope(_diff_heads(dq_l), *rope_diff)) * dscale).astype(BF16).transpose(0, 1, 3, 2)
    dkc, dvcT = dk_c_h, dv_c_h.transpose(0, 1, 3, 2)
    a_lat = _flash_attention(lam_arr, dq_l_t, dkc, dvcT, _chunk_keys(dk_l_h, tk), _chunk_values_t(dv_l_h, tk),
                             n_maps=2, tq=min(256, s))
    a_lat = a_lat.transpose(0, 1, 3, 2)

    mscale = MLA_SCALE * LOG2E
    mk_c, mv_c = _mla_keys_values(ckv_c, kr_c, p, None)
    mk_l, mv_l = _mla_keys_values(ckv_l, kr_l, p, rope_mla)
    mq_l = _mla_queries(cq_l, p, rope_mla)
    mkc, mvcT = _pad_lanes(mk_c).astype(BF16), _pad_lanes(mv_c).astype(BF16).transpose(0, 1, 3, 2)
    mq_l_t = (_pad_lanes(mq_l) * mscale).astype(BF16).transpose(0, 1, 3, 2)
    m_lat = _flash_attention(lam_arr, mq_l_t, mkc, mvcT, _chunk_keys(_pad_lanes(mk_l).astype(BF16), tk),
                             _chunk_values_t(_pad_lanes(mv_l).astype(BF16), tk), n_maps=1, tq=min(512, s))
    m_lat = m_lat.transpose(0, 1, 3, 2)[..., :MLA_V]

    def finish(a):
        return _merge_heads(_rms_norm(a, p['diff_subln_g']) * (1.0 - lam_init))

    y_lat = _merge_branches((finish(a_lat), _hyena_branch(hy_l, p), _conformer_branch(cf_l, p),
                             _merge_heads(m_lat)), g_l, p)
    if not with_ctx_out:
        return y_lat, None
    g_c = u_ctx @ w_in_gate
    nc = u_ctx.shape[1]
    dq_c_t = (_maps_to_lanes(_diff_heads(dq_c)) * dscale).astype(BF16).transpose(0, 1, 3, 2)
    a_ctx = _flash_attention(lam_arr, dq_c_t, dkc, dvcT, None, None, n_maps=2, tq=nc).transpose(0, 1, 3, 2)
    mq_c_t = (_pad_lanes(_mla_queries(cq_c, p, None)) * mscale).astype(BF16).transpose(0, 1, 3, 2)
    m_ctx = _flash_attention(lam_arr, mq_c_t, mkc, mvcT, None, None, n_maps=1, tq=nc)
    m_ctx = m_ctx.transpose(0, 1, 3, 2)[..., :MLA_V]
    y_ctx = _merge_branches((finish(a_ctx), _hyena_branch(hy_c, p), _conformer_branch(cf_c, p),
                             _merge_heads(m_ctx)), g_c, p)
    return y_lat, y_ctx


def kernel(x, c, ctx, c_ctx, ada_w, ada_b, norm_mix_g, norm_ffn_g, w_in, diff_lambda, diff_subln_g, hyena_short_w, hyena_short_b, filt_w1, filt_b1, filt_freq, filt_w2, filt_b2, filt_w3, hyena_skip, conf_dw_w, conf_ln_g, conf_ln_b, mla_q_norm_g, mla_kv_norm_g, mla_w_uq, mla_w_ukv, w_branch, w_out, w_router, w_exp_in, w_exp_out, final_norm_g):
    depth = w_in.shape[0]
    n_lat = x.shape[1]
    rope_diff = _rope_tables(n_lat, DIFF_HEAD_DIM)
    rope_mla = _rope_tables(n_lat, MLA_ROPE)
    s_lat = jax.nn.silu(c)
    s_ctx = jax.nn.silu(c_ctx)[None]
    h_lat, h_ctx = x, ctx
    for l in range(depth):
        last = l == depth - 1
        p = dict(w_in=w_in[l], diff_subln_g=diff_subln_g[l], hyena_short_w=hyena_short_w[l],
                 hyena_short_b=hyena_short_b[l], filt_w1=filt_w1[l], filt_b1=filt_b1[l], filt_freq=filt_freq[l],
                 filt_w2=filt_w2[l], filt_b2=filt_b2[l], filt_w3=filt_w3[l], hyena_skip=hyena_skip[l],
                 conf_dw_w=conf_dw_w[l], conf_ln_g=conf_ln_g[l], conf_ln_b=conf_ln_b[l],
                 mla_q_norm_g=mla_q_norm_g[l], mla_kv_norm_g=mla_kv_norm_g[l], mla_w_uq=mla_w_uq[l],
                 mla_w_ukv=mla_w_ukv[l], w_branch=w_branch[l], w_out=w_out[l], w_router=w_router[l],
                 w_exp_in=w_exp_in[l], w_exp_out=w_exp_out[l])
        mod_lat = jnp.split((s_lat @ ada_w[l] + ada_b[l])[:, None, :], 6, axis=-1)
        mod_ctx = jnp.split((s_ctx @ ada_w[l] + ada_b[l])[:, None, :], 6, axis=-1)
        lam_init = 0.8 - 0.6 * math.exp(-0.3 * l)
        lq1, lk1, lq2, lk2 = diff_lambda[l].astype(F32)
        lam = jnp.exp(jnp.sum(lq1 * lk1)) - jnp.exp(jnp.sum(lq2 * lk2)) + lam_init
        u_lat = _rms_norm(h_lat, norm_mix_g[l]) * (1.0 + mod_lat[1]) + mod_lat[0]
        u_ctx = _rms_norm(h_ctx, norm_mix_g[l]) * (1.0 + mod_ctx[1]) + mod_ctx[0]
        y_lat, y_ctx = _token_mixer(u_lat, u_ctx, p, lam, lam_init, rope_diff, rope_mla, not last)
        h_lat = h_lat + mod_lat[2] * y_lat
        u_lat = _rms_norm(h_lat, norm_ffn_g[l]) * (1.0 + mod_lat[4]) + mod_lat[3]
        h_lat = h_lat + mod_lat[5] * _expert_choice_ffn(u_lat, p)
        if not last:
            h_ctx = h_ctx + mod_ctx[2] * y_ctx
            u_ctx = _rms_norm(h_ctx, norm_ffn_g[l]) * (1.0 + mod_ctx[4]) + mod_ctx[3]
            h_ctx = h_ctx + mod_ctx[5] * _expert_choice_ffn(u_ctx, p)
    return _rms_norm(h_lat, final_norm_g)
```

```python
import functools
import math

import jax
import jax.numpy as jnp
from jax import lax
from jax.experimental import pallas as pl
from jax.experimental.pallas import tpu as pltpu

GRID_W = 64
ROPE_BASE = 10000.0
EPS = 1e-6

DIFF_HEADS = 4
DIFF_HEAD_DIM = 64
DIFF_V_DIM = 2 * DIFF_HEAD_DIM
HYENA_WIDTH = 256
HYENA_ORDER = 2
FILT_EMB = 33
DECAY_TARGET = 1e-2
FAST_DECAY = 0.3
SLOW_DECAY = 1.5
CONF_WIDTH = 256
MLA_HEADS = 4
MLA_Q_RANK = 256
MLA_KV_RANK = 128
MLA_NOPE = 64
MLA_ROPE = 32
MLA_V = 64
MLA_SCALE = (MLA_NOPE + MLA_ROPE) ** -0.5
N_BRANCH = 4
N_EXPERTS = 16
EC_CAPACITY = 2

DIFF_QK_W = DIFF_HEADS * 2 * DIFF_HEAD_DIM
DIFF_V_W = DIFF_HEADS * DIFF_V_DIM
HYENA_PROJ = (HYENA_ORDER + 1) * HYENA_WIDTH
CONF_PROJ = 2 * CONF_WIDTH
IN_SPLITS = (DIFF_QK_W, DIFF_QK_W, DIFF_V_W, HYENA_PROJ, CONF_PROJ, MLA_Q_RANK, MLA_KV_RANK, MLA_ROPE)
BRANCH_WIDTHS = (DIFF_V_W, HYENA_WIDTH, CONF_WIDTH, MLA_HEADS * MLA_V)

HEAD_LANES = 128
LOG2E = 1.4426950408889634
VMEM_LIMIT_BYTES = 48 * 1024 * 1024

F32 = jnp.float32
BF16 = jnp.bfloat16


def _flash_kernel(lam_ref, qT_ref, kc_ref, vcT_ref, *rest, n_maps, n_lat_chunks):
    if n_lat_chunks:
        kl_ref, vlT_ref, oT_ref, acc_ref, m_ref, l_ref, q2_ref, s_ref = rest
    else:
        oT_ref, acc_ref, m_ref, l_ref, q2_ref = rest
    qT = qT_ref[...]
    tq = qT.shape[1]
    if n_maps == 2:
        row = lax.broadcasted_iota(jnp.int32, qT.shape, 0)
        zero = jnp.zeros_like(qT)
        q2_ref[:, :tq] = jnp.where(row < DIFF_HEAD_DIM, qT, zero)
        q2_ref[:, tq:] = jnp.where(row >= DIFF_HEAD_DIM, qT, zero)
    else:
        q2_ref[...] = qT
    m_ref[...] = jnp.full(m_ref.shape, -jnp.inf, F32)
    l_ref[...] = jnp.zeros(l_ref.shape, F32)
    acc_ref[...] = jnp.zeros(acc_ref.shape, F32)

    def scores(k):
        return jnp.dot(k, q2_ref[...], preferred_element_type=F32)

    def absorb(s, vT):
        m_prev = m_ref[...]
        m_new = jnp.maximum(m_prev, jnp.max(s, axis=0, keepdims=True))
        alpha = jnp.exp2(m_prev - m_new)
        p = jnp.exp2(s - m_new)
        l_ref[...] = alpha * l_ref[...] + jnp.sum(p, axis=0, keepdims=True)
        acc_ref[...] = alpha * acc_ref[...] + jnp.dot(vT, p.astype(BF16), preferred_element_type=F32)
        m_ref[...] = m_new

    absorb(scores(kc_ref[...]), vcT_ref[...])
    if n_lat_chunks:
        s_ref[0] = scores(kl_ref[0])

        def pair(j, carry):
            c = 2 * j
            s_ref[1] = scores(kl_ref[c + 1])
            absorb(s_ref[0], vlT_ref[c])
            s_ref[0] = scores(kl_ref[c + 2])
            absorb(s_ref[1], vlT_ref[c + 1])
            return carry
        lax.fori_loop(0, n_lat_chunks // 2 - 1, pair, 0)
        s_ref[1] = scores(kl_ref[n_lat_chunks - 1])
        absorb(s_ref[0], vlT_ref[n_lat_chunks - 2])
        absorb(s_ref[1], vlT_ref[n_lat_chunks - 1])
    o = acc_ref[...] / l_ref[...]
    if n_maps == 2:
        o = o[:, :tq] - lam_ref[0] * o[:, tq:]
    oT_ref[...] = o


def _flash_attention(lam, qT, kc, vcT, kl, vlT, *, n_maps, tq):
    b, h, _, s = qT.shape
    lc = kc.shape[2]
    r = n_maps * tq
    in_specs = [
        pl.BlockSpec(memory_space=pltpu.SMEM),
        pl.BlockSpec((None, None, HEAD_LANES, tq), lambda bi, hi, qi: (bi, hi, 0, qi)),
        pl.BlockSpec((None, None, lc, HEAD_LANES), lambda bi, hi, qi: (bi, hi, 0, 0)),
        pl.BlockSpec((None, None, HEAD_LANES, lc), lambda bi, hi, qi: (bi, hi, 0, 0)),
    ]
    args = [lam, qT, kc, vcT]
    scratch = [pltpu.VMEM((HEAD_LANES, r), F32), pltpu.VMEM((1, r), F32), pltpu.VMEM((1, r), F32),
               pltpu.VMEM((HEAD_LANES, r), BF16)]
    n_lat_chunks = 0
    if kl is not None:
        n_lat_chunks, tk = kl.shape[2], kl.shape[3]
        assert n_lat_chunks % 2 == 0
        scratch.append(pltpu.VMEM((2, tk, r), F32))
        in_specs += [
            pl.BlockSpec((None, None, n_lat_chunks, tk, HEAD_LANES), lambda bi, hi, qi: (bi, hi, 0, 0, 0)),
            pl.BlockSpec((None, None, n_lat_chunks, HEAD_LANES, tk), lambda bi, hi, qi: (bi, hi, 0, 0, 0)),
        ]
        args += [kl, vlT]
    return pl.pallas_call(
        functools.partial(_flash_kernel, n_maps=n_maps, n_lat_chunks=n_lat_chunks),
        grid=(b, h, s // tq),
        in_specs=in_specs,
        out_specs=pl.BlockSpec((None, None, HEAD_LANES, tq), lambda bi, hi, qi: (bi, hi, 0, qi)),
        out_shape=jax.ShapeDtypeStruct((b, h, HEAD_LANES, s), F32),
        scratch_shapes=scratch,
        compiler_params=pltpu.CompilerParams(
            dimension_semantics=("arbitrary", "arbitrary", "arbitrary"),
            vmem_limit_bytes=VMEM_LIMIT_BYTES),
        name=f"flash_attention_{n_maps}map",
    )(*args)


def _chunk_keys(k, tk):
    b, h, n, d = k.shape
    return k.reshape(b, h, n // tk, tk, d)


def _chunk_values_t(v, tk):
    b, h, n, d = v.shape
    return v.reshape(b, h, n // tk, tk, d).transpose(0, 1, 2, 4, 3)


def _split(z, sizes):
    out, start = [], 0
    for s in sizes:
        out.append(z[..., start:start + s])
        start += s
    return out


def _rms_norm(x, g):
    xf = x.astype(F32)
    return xf * lax.rsqrt(jnp.mean(xf * xf, axis=-1, keepdims=True) + EPS) * g


def _layer_norm(x, g, b):
    mu = jnp.mean(x, axis=-1, keepdims=True)
    var = jnp.mean(jnp.square(x - mu), axis=-1, keepdims=True)
    return (x - mu) * lax.rsqrt(var + EPS) * g + b


def _rope_tables(n_tok, rot_dim):
    rows = n_tok // GRID_W
    row = jnp.repeat(jnp.arange(rows), GRID_W).astype(F32)
    col = jnp.tile(jnp.arange(GRID_W), rows).astype(F32)
    nf = rot_dim // 4
    inv = ROPE_BASE ** (-jnp.arange(nf, dtype=F32) / nf)
    ang = jnp.concatenate([row[:, None] * inv, col[:, None] * inv], axis=-1)
    return jnp.cos(ang), jnp.sin(ang)


def _apply_rope(x, cos, sin):
    half = x.shape[-1] // 2
    x1, x2 = x[..., :half], x[..., half:]
    return jnp.concatenate([x1 * cos - x2 * sin, x1 * sin + x2 * cos], axis=-1)


def _depthwise_conv(u, w):
    k, c = w.shape
    return lax.conv_general_dilated(u, w[:, None, :], window_strides=(1,), padding=[(k // 2, k // 2)],
                                    dimension_numbers=('NWC', 'WIO', 'NWC'), feature_group_count=c)


def _hyena_filters(n, p):
    t = jnp.linspace(0.0, 1.0, n, dtype=F32)[:, None]
    bands = (FILT_EMB - 1) // 2
    w = (2.0 * math.pi / n) * jnp.arange(n, dtype=F32)[:, None]
    f = jnp.linspace(1e-4, bands - 1, bands, dtype=F32)[None, :]
    z = jnp.concatenate([t, jnp.cos(f * w), -jnp.sin(f * w)], axis=-1)
    hid = jnp.sin(p['filt_freq'][0] * (z @ p['filt_w1'] + p['filt_b1']))
    hid = jnp.sin(p['filt_freq'][1] * (hid @ p['filt_w2'] + p['filt_b2']))
    h = (hid @ p['filt_w3']).reshape(n, HYENA_ORDER, 2, HYENA_WIDTH)
    deltas = jnp.abs(jnp.linspace(math.log(DECAY_TARGET) / SLOW_DECAY, math.log(DECAY_TARGET) / FAST_DECAY,
                                  HYENA_WIDTH, dtype=F32))
    h = h * jnp.exp(-t * deltas)[:, None, None, :]
    return h / jnp.sum(jnp.abs(h), axis=0, keepdims=True)


def _centred_long_conv(u, h_fwd, h_bwd):
    n = u.shape[1]
    k = jnp.concatenate([h_fwd, jnp.zeros_like(h_fwd[:1]), h_bwd[:-1][::-1]], axis=0)
    u_f = jnp.fft.rfft(u, n=2 * n, axis=1)
    k_f = jnp.fft.rfft(k, axis=0)
    return jnp.fft.irfft(u_f * k_f[None], n=2 * n, axis=1)[:, :n]


def _hyena_branch(z, p):
    n = z.shape[1]
    z = _depthwise_conv(z, p['hyena_short_w']) + p['hyena_short_b']
    x1, x2, v = _split(z, (HYENA_WIDTH, HYENA_WIDTH, HYENA_WIDTH))
    h = _hyena_filters(n, p)
    for o, gate in enumerate((x1, x2)):
        v = gate * (_centred_long_conv(v, h[:, o, 0], h[:, o, 1]) + p['hyena_skip'][o] * v)
    return v


def _conformer_branch(z, p):
    a, b = _split(z, (CONF_WIDTH, CONF_WIDTH))
    u = _depthwise_conv(a * jax.nn.sigmoid(b), p['conf_dw_w'])
    return jax.nn.silu(_layer_norm(u, p['conf_ln_g'], p['conf_ln_b']))


def _merge_branches(ys, g, p):
    gates = jax.nn.sigmoid(g.reshape(*g.shape[:-1], N_BRANCH, g.shape[-1] // N_BRANCH))
    acc, start = 0.0, 0
    for i, (y, w) in enumerate(zip(ys, BRANCH_WIDTHS)):
        acc = acc + gates[..., i, :] * (y @ p['w_branch'][start:start + w])
        start += w
    return acc @ p['w_out']


def _expert_choice_ffn(u, p):
    b, n, d = u.shape
    cap = max(1, EC_CAPACITY * n // N_EXPERTS)
    aff = jax.nn.softmax((u @ p['w_router']).astype(F32), axis=-1)
    gate, idx = lax.top_k(jnp.swapaxes(aff, 1, 2), cap)
    xe = jax.vmap(lambda ub, ib: ub[ib])(u, idx)
    hg, hu = jnp.split(jnp.einsum('becd,edf->becf', xe, p['w_exp_in']), 2, axis=-1)
    ye = jnp.einsum('becf,efd->becd', jax.nn.silu(hg) * hu, p['w_exp_out']) * gate[..., None]
    return jax.vmap(lambda yb, ib: jnp.zeros((n, d), yb.dtype).at[ib.reshape(-1)].add(yb.reshape(-1, d)))(ye, idx)


def _diff_heads(z):
    b, n, _ = z.shape
    return z.reshape(b, n, DIFF_HEADS, 2, DIFF_HEAD_DIM).transpose(0, 2, 3, 1, 4)


def _value_heads(z, n_heads):
    b, n, _ = z.shape
    return z.reshape(b, n, n_heads, -1).transpose(0, 2, 1, 3)


def _merge_heads(o):
    b, h, n, dv = o.shape
    return o.transpose(0, 2, 1, 3).reshape(b, n, h * dv)


def _maps_to_lanes(z):
    b, h, _, n, d = z.shape
    return z.transpose(0, 1, 3, 2, 4).reshape(b, h, n, 2 * d)


def _pad_lanes(z):
    return jnp.pad(z, [(0, 0)] * (z.ndim - 1) + [(0, HEAD_LANES - z.shape[-1])])


def _mla_queries(cq, p, rope):
    q = _value_heads(_rms_norm(cq, p['mla_q_norm_g']) @ p['mla_w_uq'], MLA_HEADS)
    q_nope, q_rope = q[..., :MLA_NOPE], q[..., MLA_NOPE:]
    if rope is not None:
        q_rope = _apply_rope(q_rope, *rope)
    return jnp.concatenate([q_nope, q_rope], axis=-1)


def _mla_keys_values(ckv, kr, p, rope):
    kv = _value_heads(_rms_norm(ckv, p['mla_kv_norm_g']) @ p['mla_w_ukv'], MLA_HEADS)
    k_nope, v = kv[..., :MLA_NOPE], kv[..., MLA_NOPE:]
    kr = kr[:, None]
    if rope is not None:
        kr = _apply_rope(kr, *rope)
    k = jnp.concatenate([k_nope, jnp.broadcast_to(kr, k_nope.shape[:-1] + (MLA_ROPE,))], axis=-1)
    return k, v


def _lat_tiles(s):
    return min(1024, s // 2)


def _token_mixer(u_lat, u_ctx, p, lam, lam_init, rope_diff, rope_mla, with_ctx_out):
    s = u_lat.shape[1]
    tk = _lat_tiles(s)
    w_in_main, w_in_gate = p['w_in'][:, :sum(IN_SPLITS)], p['w_in'][:, sum(IN_SPLITS):]
    z_l, z_c = u_lat @ w_in_main, u_ctx @ w_in_main
    g_l = u_lat @ w_in_gate
    dq_l, dk_l, dv_l, hy_l, cf_l, cq_l, ckv_l, kr_l = _split(z_l, IN_SPLITS)
    dq_c, dk_c, dv_c, hy_c, cf_c, cq_c, ckv_c, kr_c = _split(z_c, IN_SPLITS)
    lam_arr = jnp.reshape(lam, (1,)).astype(F32)

    dscale = DIFF_HEAD_DIM ** -0.5 * LOG2E
    dk_c_h = _maps_to_lanes(_diff_heads(dk_c)).astype(BF16)
    dv_c_h = _value_heads(dv_c, DIFF_HEADS).astype(BF16)
    dk_l_h = _maps_to_lanes(_apply_rope(_diff_heads(dk_l), *rope_diff)).astype(BF16)
    dv_l_h = _value_heads(dv_l, DIFF_HEADS).astype(BF16)
    dq_l_t = (_maps_to_lanes(_apply_rope(_diff_heads(dq_l), *rope_diff)) * dscale).astype(BF16).transpose(0, 1, 3, 2)
    dkc, dvcT = dk_c_h, dv_c_h.transpose(0, 1, 3, 2)
    a_lat = _flash_attention(lam_arr, dq_l_t, dkc, dvcT, _chunk_keys(dk_l_h, tk), _chunk_values_t(dv_l_h, tk),
                             n_maps=2, tq=min(256, s))
    a_lat = a_lat.transpose(0, 1, 3, 2)

    mscale = MLA_SCALE * LOG2E
    mk_c, mv_c = _mla_keys_values(ckv_c, kr_c, p, None)
    mk_l, mv_l = _mla_keys_values(ckv_l, kr_l, p, rope_mla)
    mq_l = _mla_queries(cq_l, p, rope_mla)
    mkc, mvcT = _pad_lanes(mk_c).astype(BF16), _pad_lanes(mv_c).astype(BF16).transpose(0, 1, 3, 2)
    mq_l_t = (_pad_lanes(mq_l) * mscale).astype(BF16).transpose(0, 1, 3, 2)
    m_lat = _flash_attention(lam_arr, mq_l_t, mkc, mvcT, _chunk_keys(_pad_lanes(mk_l).astype(BF16), tk),
                             _chunk_values_t(_pad_lanes(mv_l).astype(BF16), tk), n_maps=1, tq=min(512, s))
    m_lat = m_lat.transpose(0, 1, 3, 2)[..., :MLA_V]

    def finish(a):
        return _merge_heads(_rms_norm(a, p['diff_subln_g']) * (1.0 - lam_init))

    y_lat = _merge_branches((finish(a_lat), _hyena_branch(hy_l, p), _conformer_branch(cf_l, p),
                             _merge_heads(m_lat)), g_l, p)
    if not with_ctx_out:
        return y_lat, None
    g_c = u_ctx @ w_in_gate
    nc = u_ctx.shape[1]
    dq_c_t = (_maps_to_lanes(_diff_heads(dq_c)) * dscale).astype(BF16).transpose(0, 1, 3, 2)
    a_ctx = _flash_attention(lam_arr, dq_c_t, dkc, dvcT, None, None, n_maps=2, tq=nc).transpose(0, 1, 3, 2)
    mq_c_t = (_pad_lanes(_mla_queries(cq_c, p, None)) * mscale).astype(BF16).transpose(0, 1, 3, 2)
    m_ctx = _flash_attention(lam_arr, mq_c_t, mkc, mvcT, None, None, n_maps=1, tq=nc)
    m_ctx = m_ctx.transpose(0, 1, 3, 2)[..., :MLA_V]
    y_ctx = _merge_branches((finish(a_ctx), _hyena_branch(hy_c, p), _conformer_branch(cf_c, p),
                             _merge_heads(m_ctx)), g_c, p)
    return y_lat, y_ctx


def kernel(x, c, ctx, c_ctx, ada_w, ada_b, norm_mix_g, norm_ffn_g, w_in, diff_lambda, diff_subln_g, hyena_short_w, hyena_short_b, filt_w1, filt_b1, filt_freq, filt_w2, filt_b2, filt_w3, hyena_skip, conf_dw_w, conf_ln_g, conf_ln_b, mla_q_norm_g, mla_kv_norm_g, mla_w_uq, mla_w_ukv, w_branch, w_out, w_router, w_exp_in, w_exp_out, final_norm_g):
    depth = w_in.shape[0]
    n_lat = x.shape[1]
    rope_diff = _rope_tables(n_lat, DIFF_HEAD_DIM)
    rope_mla = _rope_tables(n_lat, MLA_ROPE)
    s_lat = jax.nn.silu(c)
    s_ctx = jax.nn.silu(c_ctx)[None]
    h_lat, h_ctx = x, ctx
    for l in range(depth):
        last = l == depth - 1
        p = dict(w_in=w_in[l], diff_subln_g=diff_subln_g[l], hyena_short_w=hyena_short_w[l],
                 hyena_short_b=hyena_short_b[l], filt_w1=filt_w1[l], filt_b1=filt_b1[l], filt_freq=filt_freq[l],
                 filt_w2=filt_w2[l], filt_b2=filt_b2[l], filt_w3=filt_w3[l], hyena_skip=hyena_skip[l],
                 conf_dw_w=conf_dw_w[l], conf_ln_g=conf_ln_g[l], conf_ln_b=conf_ln_b[l],
                 mla_q_norm_g=mla_q_norm_g[l], mla_kv_norm_g=mla_kv_norm_g[l], mla_w_uq=mla_w_uq[l],
                 mla_w_ukv=mla_w_ukv[l], w_branch=w_branch[l], w_out=w_out[l], w_router=w_router[l],
                 w_exp_in=w_exp_in[l], w_exp_out=w_exp_out[l])
        mod_lat = jnp.split((s_lat @ ada_w[l] + ada_b[l])[:, None, :], 6, axis=-1)
        mod_ctx = jnp.split((s_ctx @ ada_w[l] + ada_b[l])[:, None, :], 6, axis=-1)
        lam_init = 0.8 - 0.6 * math.exp(-0.3 * l)
        lq1, lk1, lq2, lk2 = diff_lambda[l].astype(F32)
        lam = jnp.exp(jnp.sum(lq1 * lk1)) - jnp.exp(jnp.sum(lq2 * lk2)) + lam_init
        u_lat = _rms_norm(h_lat, norm_mix_g[l]) * (1.0 + mod_lat[1]) + mod_lat[0]
        u_ctx = _rms_norm(h_ctx, norm_mix_g[l]) * (1.0 + mod_ctx[1]) + mod_ctx[0]
        y_lat, y_ctx = _token_mixer(u_lat, u_ctx, p, lam, lam_init, rope_diff, rope_mla, not last)
        h_lat = h_lat + mod_lat[2] * y_lat
        u_lat = _rms_norm(h_lat, norm_ffn_g[l]) * (1.0 + mod_lat[4]) + mod_lat[3]
        h_lat = h_lat + mod_lat[5] * _expert_choice_ffn(u_lat, p)
        if not last:
            h_ctx = h_ctx + mod_ctx[2] * y_ctx
            u_ctx = _rms_norm(h_ctx, norm_ffn_g[l]) * (1.0 + mod_ctx[4]) + mod_ctx[3]
            h_ctx = h_ctx + mod_ctx[5] * _expert_choice_ffn(u_ctx, p)
    return _rms_norm(h_lat, final_norm_g)
```

```python
import functools
import math

import jax
import jax.numpy as jnp
from jax import lax
from jax.experimental import pallas as pl
from jax.experimental.pallas import tpu as pltpu

GRID_W = 64
ROPE_BASE = 10000.0
EPS = 1e-6

DIFF_HEADS = 4
DIFF_HEAD_DIM = 64
DIFF_V_DIM = 2 * DIFF_HEAD_DIM
HYENA_WIDTH = 256
HYENA_ORDER = 2
FILT_EMB = 33
DECAY_TARGET = 1e-2
FAST_DECAY = 0.3
SLOW_DECAY = 1.5
CONF_WIDTH = 256
MLA_HEADS = 4
MLA_Q_RANK = 256
MLA_KV_RANK = 128
MLA_NOPE = 64
MLA_ROPE = 32
MLA_V = 64
MLA_SCALE = (MLA_NOPE + MLA_ROPE) ** -0.5
N_BRANCH = 4
N_EXPERTS = 16
EC_CAPACITY = 2

DIFF_QK_W = DIFF_HEADS * 2 * DIFF_HEAD_DIM
DIFF_V_W = DIFF_HEADS * DIFF_V_DIM
HYENA_PROJ = (HYENA_ORDER + 1) * HYENA_WIDTH
CONF_PROJ = 2 * CONF_WIDTH
IN_SPLITS = (DIFF_QK_W, DIFF_QK_W, DIFF_V_W, HYENA_PROJ, CONF_PROJ, MLA_Q_RANK, MLA_KV_RANK, MLA_ROPE)
BRANCH_WIDTHS = (DIFF_V_W, HYENA_WIDTH, CONF_WIDTH, MLA_HEADS * MLA_V)

HEAD_LANES = 128
ATT_W = DIFF_HEADS * HEAD_LANES
LOG2E = 1.4426950408889634
VMEM_LIMIT_BYTES = 48 * 1024 * 1024
MOD_ROWS = 8

F32 = jnp.float32
BF16 = jnp.bfloat16
_NT = (((1,), (1,)), ((), ()))


def _params(n_axes, vmem=VMEM_LIMIT_BYTES):
    return pltpu.CompilerParams(dimension_semantics=("arbitrary",) * n_axes, vmem_limit_bytes=vmem)


def _flash_kernel(lam_ref, qT_ref, kc_ref, vcT_ref, *rest, n_maps, n_lat_chunks, tk):
    if n_lat_chunks:
        kl_ref, vlT_ref, o_ref, acc_ref, m_ref, l_ref, q2_ref, s_ref = rest
    else:
        o_ref, acc_ref, m_ref, l_ref, q2_ref = rest
    qT = qT_ref[...]
    tq = qT.shape[1]
    if n_maps == 2:
        row = lax.broadcasted_iota(jnp.int32, qT.shape, 0)
        zero = jnp.zeros_like(qT)
        q2_ref[:, :tq] = jnp.where(row < DIFF_HEAD_DIM, qT, zero)
        q2_ref[:, tq:] = jnp.where(row >= DIFF_HEAD_DIM, qT, zero)
    else:
        q2_ref[...] = qT
    m_ref[...] = jnp.full(m_ref.shape, -jnp.inf, F32)
    l_ref[...] = jnp.zeros(l_ref.shape, F32)
    acc_ref[...] = jnp.zeros(acc_ref.shape, F32)

    def scores(k):
        return jnp.dot(k, q2_ref[...], preferred_element_type=F32)

    def absorb(s, vT):
        m_prev = m_ref[...]
        m_new = jnp.maximum(m_prev, jnp.max(s, axis=0, keepdims=True))
        alpha = jnp.exp2(m_prev - m_new)
        p = jnp.exp2(s - m_new)
        l_ref[...] = alpha * l_ref[...] + jnp.sum(p, axis=0, keepdims=True)
        acc_ref[...] = alpha * acc_ref[...] + jnp.dot(vT, p.astype(BF16), preferred_element_type=F32)
        m_ref[...] = m_new

    def chunk(c):
        return pl.ds(c * tk if isinstance(c, int) else pl.multiple_of(c * tk, tk), tk)

    def keys(c):
        return kl_ref[chunk(c), :]

    def values_t(c):
        return vlT_ref[:, chunk(c)]

    absorb(scores(kc_ref[...]), vcT_ref[...])
    if n_lat_chunks:
        s_ref[0] = scores(keys(0))

        def pair(j, carry):
            c = 2 * j
            s_ref[1] = scores(keys(c + 1))
            absorb(s_ref[0], values_t(c))
            s_ref[0] = scores(keys(c + 2))
            absorb(s_ref[1], values_t(c + 1))
            return carry
        lax.fori_loop(0, n_lat_chunks // 2 - 1, pair, 0)
        s_ref[1] = scores(keys(n_lat_chunks - 1))
        absorb(s_ref[0], values_t(n_lat_chunks - 2))
        absorb(s_ref[1], values_t(n_lat_chunks - 1))
    o = acc_ref[...] / l_ref[...]
    if n_maps == 2:
        o = o[:, :tq] - lam_ref[0] * o[:, tq:]
        o = o * lax.rsqrt(jnp.mean(o * o, axis=0, keepdims=True) + EPS)
    o_ref[...] = o.T.astype(BF16)


def _flash_attention(lam, qT, kc, vcT, kl, vlT, *, n_maps, tq):
    b, _, s = qT.shape
    lc = kc.shape[1]
    r = n_maps * tq
    in_specs = [
        pl.BlockSpec(memory_space=pltpu.SMEM),
        pl.BlockSpec((None, HEAD_LANES, tq), lambda bi, hi, qi: (bi, hi, qi)),
        pl.BlockSpec((None, lc, HEAD_LANES), lambda bi, hi, qi: (bi, 0, hi)),
        pl.BlockSpec((None, HEAD_LANES, lc), lambda bi, hi, qi: (bi, hi, 0)),
    ]
    args = [lam, qT, kc, vcT]
    scratch = [pltpu.VMEM((HEAD_LANES, r), F32), pltpu.VMEM((1, r), F32), pltpu.VMEM((1, r), F32),
               pltpu.VMEM((HEAD_LANES, r), BF16)]
    n_lat_chunks, tk = 0, 0
    if kl is not None:
        sl = kl.shape[1]
        tk = _lat_chunk(sl)
        n_lat_chunks = sl // tk
        assert n_lat_chunks % 2 == 0 and n_lat_chunks * tk == sl
        scratch.append(pltpu.VMEM((2, tk, r), F32))
        in_specs += [
            pl.BlockSpec((None, sl, HEAD_LANES), lambda bi, hi, qi: (bi, 0, hi)),
            pl.BlockSpec((None, HEAD_LANES, sl), lambda bi, hi, qi: (bi, hi, 0)),
        ]
        args += [kl, vlT]
    return pl.pallas_call(
        functools.partial(_flash_kernel, n_maps=n_maps, n_lat_chunks=n_lat_chunks, tk=tk),
        grid=(b, DIFF_HEADS, s // tq),
        in_specs=in_specs,
        out_specs=pl.BlockSpec((None, tq, HEAD_LANES), lambda bi, hi, qi: (bi, qi, hi)),
        out_shape=jax.ShapeDtypeStruct((b, s, ATT_W), BF16),
        scratch_shapes=scratch,
        compiler_params=_params(3),
        name=f"flash_attention_{n_maps}map",
    )(*args)


def _lat_chunk(s):
    return min(1024, s // 2)


W_NAT_SPLITS = (DIFF_QK_W, HYENA_PROJ, CONF_PROJ, MLA_Q_RANK, MLA_KV_RANK, HEAD_LANES)


def _modulated_norm(h, a, shift):
    return h * lax.rsqrt(jnp.mean(h * h, axis=-1, keepdims=True) + EPS) * a + shift


def _rope_lanes(x, tab_ref, shift):
    return (x * tab_ref[0] + pltpu.roll(x, shift, 1) * tab_ref[1]
            + pltpu.roll(x, HEAD_LANES - shift, 1) * tab_ref[2])


def _inproj_kernel(h_ref, mod_ref, wnat_ref, wT_ref, wuqT_ref, wukvk_ref, wuvT_ref, gq_ref, gkv_ref,
                   ropeT_d_ref, rope_kd_ref, ropeT_m_ref, rope_km_ref,
                   qdT_ref, kd_ref, vdT_ref, qmT_ref, km_ref, vmT_ref, hy_ref, glu_ref):
    u = _modulated_norm(h_ref[...], mod_ref[0:1, :], mod_ref[1:2, :]).astype(BF16)
    z = jnp.dot(u, wnat_ref[...], preferred_element_type=F32)
    zT = lax.dot_general(wT_ref[...], u, _NT, preferred_element_type=F32)
    offs = [0]
    for w in W_NAT_SPLITS:
        offs.append(offs[-1] + w)
    dk, hy, cf, cq, ckv, krp = (z[:, offs[i]:offs[i + 1]] for i in range(len(W_NAT_SPLITS)))

    for hd in range(DIFF_HEADS):
        sl = slice(hd * HEAD_LANES, (hd + 1) * HEAD_LANES)
        kd_ref[:, sl] = _rope_lanes(dk[:, sl], rope_kd_ref, DIFF_HEAD_DIM // 2).astype(BF16)
    cos_d, sin_d = ropeT_d_ref[0], ropeT_d_ref[1]
    half = DIFF_HEAD_DIM // 2
    for g in range(2 * DIFF_HEADS):
        x1 = zT[g * DIFF_HEAD_DIM:g * DIFF_HEAD_DIM + half]
        x2 = zT[g * DIFF_HEAD_DIM + half:(g + 1) * DIFF_HEAD_DIM]
        qdT_ref[g * DIFF_HEAD_DIM:g * DIFF_HEAD_DIM + half, :] = (x1 * cos_d - x2 * sin_d).astype(BF16)
        qdT_ref[g * DIFF_HEAD_DIM + half:(g + 1) * DIFF_HEAD_DIM, :] = (x1 * sin_d + x2 * cos_d).astype(BF16)
    vdT_ref[...] = zT[DIFF_QK_W:].astype(BF16)

    hy_ref[...] = hy
    glu_ref[...] = cf[:, :CONF_WIDTH] * jax.nn.sigmoid(cf[:, CONF_WIDTH:])

    cqn = (cq * lax.rsqrt(jnp.mean(cq * cq, axis=-1, keepdims=True) + EPS) * gq_ref[...]).astype(BF16)
    ckvn = (ckv * lax.rsqrt(jnp.mean(ckv * ckv, axis=-1, keepdims=True) + EPS) * gkv_ref[...]).astype(BF16)
    qT = lax.dot_general(wuqT_ref[...], cqn, _NT, preferred_element_type=F32)
    cos_m, sin_m = ropeT_m_ref[0], ropeT_m_ref[1]
    hr = MLA_ROPE // 2
    for hd in range(MLA_HEADS):
        base = hd * HEAD_LANES
        r1 = base + MLA_NOPE
        x1, x2 = qT[r1:r1 + hr], qT[r1 + hr:r1 + 2 * hr]
        qmT_ref[base:r1, :] = qT[base:r1].astype(BF16)
        qmT_ref[r1:r1 + hr, :] = (x1 * cos_m - x2 * sin_m).astype(BF16)
        qmT_ref[r1 + hr:r1 + 2 * hr, :] = (x1 * sin_m + x2 * cos_m).astype(BF16)
        qmT_ref[r1 + 2 * hr:base + HEAD_LANES, :] = jnp.zeros((HEAD_LANES - MLA_NOPE - MLA_ROPE, qT.shape[1]), BF16)
    kn = jnp.dot(ckvn, wukvk_ref[...], preferred_element_type=F32)
    kr = _rope_lanes(krp, rope_km_ref, hr)
    for hd in range(MLA_HEADS):
        sl = slice(hd * HEAD_LANES, (hd + 1) * HEAD_LANES)
        km_ref[:, sl] = (kn[:, sl] + kr).astype(BF16)
    vmT_ref[...] = lax.dot_general(wuvT_ref[...], ckvn, _NT, preferred_element_type=F32).astype(BF16)


def _inproj(h, mod, wts, rope, *, tile):
    b, n, d = h.shape
    const2 = lambda bi, ti: (0, 0)
    tok = lambda w: pl.BlockSpec((None, tile, w), lambda bi, ti: (bi, ti, 0))
    tokT = lambda w: pl.BlockSpec((None, w, tile), lambda bi, ti: (bi, 0, ti))
    full = lambda a: pl.BlockSpec(a.shape, const2)
    in_specs = [tok(d), pl.BlockSpec((None, MOD_ROWS, d), lambda bi, ti: (bi, 0, 0))]
    in_specs += [full(wts[k]) for k in ('w_nat', 'w_T', 'w_uqT', 'w_ukvk', 'w_uvT', 'gq', 'gkv')]
    in_specs += [pl.BlockSpec((2, DIFF_HEAD_DIM // 2, tile), lambda bi, ti: (0, 0, ti)),
                 pl.BlockSpec((3, tile, HEAD_LANES), lambda bi, ti: (0, ti, 0)),
                 pl.BlockSpec((2, MLA_ROPE // 2, tile), lambda bi, ti: (0, 0, ti)),
                 pl.BlockSpec((3, tile, HEAD_LANES), lambda bi, ti: (0, ti, 0))]
    sds = jax.ShapeDtypeStruct
    out_shape = (sds((b, ATT_W, n), BF16), sds((b, n, ATT_W), BF16), sds((b, ATT_W, n), BF16),
                 sds((b, ATT_W, n), BF16), sds((b, n, ATT_W), BF16), sds((b, ATT_W, n), BF16),
                 sds((b, n, HYENA_PROJ), F32), sds((b, n, CONF_WIDTH), F32))
    out_specs = (tokT(ATT_W), tok(ATT_W), tokT(ATT_W), tokT(ATT_W), tok(ATT_W), tokT(ATT_W),
                 tok(HYENA_PROJ), tok(CONF_WIDTH))
    return pl.pallas_call(
        _inproj_kernel, grid=(b, n // tile), in_specs=in_specs, out_specs=out_specs, out_shape=out_shape,
        compiler_params=_params(2), name="inproj",
    )(h, mod, wts['w_nat'], wts['w_T'], wts['w_uqT'], wts['w_ukvk'], wts['w_uvT'], wts['gq'], wts['gkv'],
      rope['T_d'], rope['k_d'], rope['T_m'], rope['k_m'])


def _pad_heads(w, width):
    rows = w.shape[0]
    w = w.reshape(rows, MLA_HEADS, width)
    return jnp.pad(w, ((0, 0), (0, 0), (0, HEAD_LANES - width))).reshape(rows, ATT_W)


def _inproj_weights(p):
    d = p['w_in'].shape[0]
    dq, dk, dv, hy, cf, cq, ckv, kr = _split(p['w_in'][:, :sum(IN_SPLITS)], IN_SPLITS)
    krp = jnp.zeros((d, HEAD_LANES), F32).at[:, MLA_NOPE:MLA_NOPE + MLA_ROPE].set(kr)
    w_ukv = p['mla_w_ukv'].reshape(MLA_KV_RANK, MLA_HEADS, MLA_NOPE + MLA_V)
    return dict(
        w_nat=jnp.concatenate([dk, hy, cf, cq, ckv, krp], axis=1).astype(BF16),
        w_T=jnp.concatenate([dq * (DIFF_HEAD_DIM ** -0.5 * LOG2E), dv], axis=1).T.astype(BF16),
        w_uqT=_pad_heads(p['mla_w_uq'] * (MLA_SCALE * LOG2E), MLA_NOPE + MLA_ROPE).T.astype(BF16),
        w_ukvk=_pad_heads(w_ukv[:, :, :MLA_NOPE].reshape(MLA_KV_RANK, -1), MLA_NOPE).astype(BF16),
        w_uvT=_pad_heads(w_ukv[:, :, MLA_NOPE:].reshape(MLA_KV_RANK, -1), MLA_V).T.astype(BF16),
        gq=p['mla_q_norm_g'][None, :], gkv=p['mla_kv_norm_g'][None, :])


def _rope_tables(n_tok, rot_dim):
    rows = n_tok // GRID_W
    row = jnp.repeat(jnp.arange(rows), GRID_W).astype(F32)
    col = jnp.tile(jnp.arange(GRID_W), rows).astype(F32)
    nf = rot_dim // 4
    inv = ROPE_BASE ** (-jnp.arange(nf, dtype=F32) / nf)
    ang = jnp.concatenate([row[:, None] * inv, col[:, None] * inv], axis=-1)
    return jnp.cos(ang), jnp.sin(ang)


def _rope_operands(n_tok, identity):
    if identity:
        cos_d, sin_d = jnp.ones((n_tok, DIFF_HEAD_DIM // 2), F32), jnp.zeros((n_tok, DIFF_HEAD_DIM // 2), F32)
        cos_m, sin_m = jnp.ones((n_tok, MLA_ROPE // 2), F32), jnp.zeros((n_tok, MLA_ROPE // 2), F32)
    else:
        cos_d, sin_d = _rope_tables(n_tok, DIFF_HEAD_DIM)
        cos_m, sin_m = _rope_tables(n_tok, MLA_ROPE)
    z_d, z_m = jnp.zeros_like(sin_d), jnp.zeros_like(sin_m)
    two = lambda a, bb: jnp.tile(jnp.concatenate([a, bb], axis=1), (1, 2))
    lo, hi = jnp.zeros((n_tok, MLA_NOPE), F32), jnp.zeros((n_tok, HEAD_LANES - MLA_NOPE - MLA_ROPE), F32)
    mid = lambda a, bb: jnp.concatenate([lo, a, bb, hi], axis=1)
    return dict(T_d=jnp.stack([cos_d.T, sin_d.T]), T_m=jnp.stack([cos_m.T, sin_m.T]),
                k_d=jnp.stack([two(cos_d, cos_d), two(z_d, sin_d), two(-sin_d, z_d)]),
                k_m=jnp.stack([mid(cos_m, cos_m), mid(z_m, sin_m), mid(-sin_m, z_m)]))


def _merge_kernel(h_ref, mod_ref, a_ref, hy_ref, cf_ref, m_ref, wg_ref, wbd_ref, wbh_ref, wbc_ref, wbm_ref,
                  wo_ref, wrh_ref, wrl_ref, hn_ref, u2_ref, lg_ref):
    h = h_ref[...]
    d = h.shape[1]
    u = _modulated_norm(h, mod_ref[0:1, :], mod_ref[1:2, :]).astype(BF16)
    gates = jax.nn.sigmoid(jnp.dot(u, wg_ref[...], preferred_element_type=F32))
    dot = lambda x, w_ref: jnp.dot(x, w_ref[...], preferred_element_type=F32)
    acc = gates[:, :d] * dot(a_ref[...], wbd_ref)
    acc += gates[:, d:2 * d] * dot(hy_ref[...].astype(BF16), wbh_ref)
    acc += gates[:, 2 * d:3 * d] * dot(cf_ref[...].astype(BF16), wbc_ref)
    acc += gates[:, 3 * d:] * dot(m_ref[...], wbm_ref)
    hn = h + mod_ref[2:3, :] * dot(acc.astype(BF16), wo_ref)
    hn_ref[...] = hn
    u2 = _modulated_norm(hn, mod_ref[3:4, :], mod_ref[4:5, :])
    u2h = u2.astype(BF16)
    u2l = (u2 - u2h.astype(F32)).astype(BF16)
    u2_ref[...] = u2h
    lg_ref[...] = dot(u2h, wrh_ref) + (dot(u2l, wrh_ref) + dot(u2h, wrl_ref))


def _merge(h, mod, a, hyv, cfv, m, wts, *, tile):
    b, n, d = h.shape
    const2 = lambda bi, ti: (0, 0)
    tok = lambda w: pl.BlockSpec((None, tile, w), lambda bi, ti: (bi, ti, 0))
    names = ('w_gate', 'w_bd', 'w_bh', 'w_bc', 'w_bm', 'w_out', 'w_rh', 'w_rl')
    in_specs = [tok(d), pl.BlockSpec((None, MOD_ROWS, d), lambda bi, ti: (bi, 0, 0)),
                tok(ATT_W), tok(HYENA_WIDTH), tok(CONF_WIDTH), tok(ATT_W)]
    in_specs += [pl.BlockSpec(wts[k].shape, const2) for k in names]
    sds = jax.ShapeDtypeStruct
    return pl.pallas_call(
        _merge_kernel, grid=(b, n // tile), in_specs=in_specs,
        out_specs=(tok(d), tok(d), tok(HEAD_LANES)),
        out_shape=(sds((b, n, d), F32), sds((b, n, d), BF16), sds((b, n, HEAD_LANES), F32)),
        compiler_params=_params(2, 56 * 1024 * 1024), name="merge",
    )(h, mod, a, hyv, cfv, m, *[wts[k] for k in names])


def _merge_weights(p, lam_init):
    d = p['w_out'].shape[0]
    wb_d, wb_h, wb_c, wb_m = (w.T for w in _split(p['w_branch'].T, BRANCH_WIDTHS))
    wb_d = wb_d * (jnp.tile(p['diff_subln_g'], DIFF_HEADS) * (1.0 - lam_init))[:, None]
    wb_m = jnp.pad(wb_m.reshape(MLA_HEADS, MLA_V, d), ((0, 0), (0, HEAD_LANES - MLA_V), (0, 0))).reshape(ATT_W, d)
    w_r = jnp.pad(p['w_router'], ((0, 0), (0, HEAD_LANES - N_EXPERTS)))
    w_rh = w_r.astype(BF16)
    return dict(w_gate=p['w_in'][:, sum(IN_SPLITS):].astype(BF16), w_bd=wb_d.astype(BF16), w_bh=wb_h.astype(BF16),
                w_bc=wb_c.astype(BF16), w_bm=wb_m.astype(BF16), w_out=p['w_out'].astype(BF16),
                w_rh=w_rh, w_rl=(w_r - w_rh.astype(F32)).astype(BF16))


SUB_TOKENS = 256
GATHER_WINDOW = 272
COMBINE_WINDOW = 384
ROUTE_MIN_ROWS = 8


def _excl_scan(x, lane, row):
    inc = x
    s = 1
    while s < HEAD_LANES:
        inc = inc + jnp.where(lane >= s, pltpu.roll(inc, s, 2), 0.0)
        s *= 2
    tot = jnp.sum(x, axis=2, keepdims=True) + jnp.zeros_like(x)
    off = tot
    s = 1
    while s < x.shape[1]:
        off = off + jnp.where(row >= s, pltpu.roll(off, s, 1), 0.0)
        s *= 2
    return inc - x + (off - tot)


def _route_kernel(lg_ref, pos_ref, *, n_valid, cap):
    lg = lg_ref[...]
    shape = lg.shape
    lane = lax.broadcasted_iota(jnp.int32, shape, 2)
    row = lax.broadcasted_iota(jnp.int32, shape, 1)
    e = jnp.exp(lg - jnp.max(lg, axis=0, keepdims=True))
    aff = e / jnp.sum(e, axis=0, keepdims=True)
    bits = jnp.where(row * HEAD_LANES + lane < n_valid, pltpu.bitcast(aff, jnp.int32), -1)

    def count(mask):
        c = jnp.sum(jnp.where(mask, 1.0, 0.0), axis=2, keepdims=True)
        return jnp.sum(c, axis=1, keepdims=True)

    def step(i, thr):
        cand = thr | (jnp.int32(1) << (30 - i))
        return jnp.where(count(bits >= cand) >= cap, cand, thr)
    thr = lax.fori_loop(0, 31, step, jnp.zeros((shape[0], 1, 1), jnp.int32))
    gt = bits > thr
    eq = bits == thr
    need = cap - count(gt)
    tie_rank = _excl_scan(jnp.where(eq, 1.0, 0.0), lane, row)
    sel = gt | (eq & (tie_rank < need))
    pos = _excl_scan(jnp.where(sel, 1.0, 0.0), lane, row)
    pos_ref[...] = jnp.where(sel, pos.astype(jnp.int32), -1)


def _route(logits, cap):
    b, n, _ = logits.shape
    rows = max(ROUTE_MIN_ROWS, n // HEAD_LANES)
    lg = jnp.swapaxes(logits[..., :N_EXPERTS], 1, 2)
    lg = jnp.pad(lg, ((0, 0), (0, 0), (0, rows * HEAD_LANES - n))).reshape(b, N_EXPERTS, rows, HEAD_LANES)
    spec = pl.BlockSpec((None, N_EXPERTS, rows, HEAD_LANES), lambda bi: (bi, 0, 0, 0))
    pos = pl.pallas_call(
        functools.partial(_route_kernel, n_valid=n, cap=cap), grid=(b,), in_specs=[spec], out_specs=spec,
        out_shape=jax.ShapeDtypeStruct(lg.shape, jnp.int32), compiler_params=_params(1), name="route",
    )(lg)
    return pos.reshape(b, N_EXPERTS, rows * HEAD_LANES)[..., :n]


def _experts_kernel(base_ref, u_ref, pos_ref, win_ref, wout_ref, ye_ref, xe_ref, *, n_sub, cap):
    bi, ei, kb = pl.program_id(0), pl.program_id(1), pl.program_id(2)

    @pl.when(kb == 0)
    def _():
        xe_ref[...] = jnp.zeros(xe_ref.shape, F32)

    slot = lax.broadcasted_iota(jnp.int32, (GATHER_WINDOW, SUB_TOKENS), 0)
    for j in range(n_sub):
        base = pl.multiple_of(base_ref[bi, ei, kb * n_sub + j], 16)
        rel = pos_ref[:, j * SUB_TOKENS:(j + 1) * SUB_TOKENS] - base
        onehot = jnp.where(slot == rel, 1.0, 0.0).astype(BF16)
        rows = pl.ds(base, GATHER_WINDOW)
        xe_ref[rows, :] += jnp.dot(onehot, u_ref[j * SUB_TOKENS:(j + 1) * SUB_TOKENS, :],
                                   preferred_element_type=F32)

    @pl.when(kb == pl.num_programs(2) - 1)
    def _():
        f = wout_ref.shape[0]
        step = min(512, cap)
        for r0 in range(0, cap, step):
            x = xe_ref[r0:r0 + step, :].astype(BF16)
            hgu = jnp.dot(x, win_ref[...], preferred_element_type=F32)
            act = (jax.nn.silu(hgu[:, :f]) * hgu[:, f:]).astype(BF16)
            ye_ref[r0:r0 + step, :] = jnp.dot(act, wout_ref[...], preferred_element_type=F32).astype(BF16)
        ye_ref[cap:, :] = jnp.zeros((ye_ref.shape[0] - cap, ye_ref.shape[1]), BF16)


def _combine_kernel(base_ref, h_ref, mod_ref, lg_ref, posn_ref, ye_ref, hn_ref, *, n_sub):
    bi, kb, ei = pl.program_id(0), pl.program_id(1), pl.program_id(2)

    @pl.when(ei == 0)
    def _():
        hn_ref[...] = jnp.zeros(hn_ref.shape, F32)

    slot = lax.broadcasted_iota(jnp.int32, (SUB_TOKENS, COMBINE_WINDOW), 1)
    lane_e = lax.broadcasted_iota(jnp.int32, (SUB_TOKENS, N_EXPERTS), 1)
    for j in range(n_sub):
        tok = slice(j * SUB_TOKENS, (j + 1) * SUB_TOKENS)
        base = pl.multiple_of(base_ref[bi, ei, kb * n_sub + j], 16)
        lg = lg_ref[tok, :][:, :N_EXPERTS]
        ex = jnp.exp(lg - jnp.max(lg, axis=1, keepdims=True))
        aff = ex / jnp.sum(ex, axis=1, keepdims=True)
        mine = lane_e == ei
        gate = jnp.sum(jnp.where(mine, aff, 0.0), axis=1, keepdims=True)
        rel = jnp.sum(jnp.where(mine, posn_ref[tok, :], 0), axis=1, keepdims=True) - base
        onehot = jnp.where(slot == rel, 1.0, 0.0).astype(BF16)
        ye = ye_ref[pl.ds(base, COMBINE_WINDOW), :]
        hn_ref[tok, :] += gate * jnp.dot(onehot, ye, preferred_element_type=F32)

    @pl.when(ei == pl.num_programs(2) - 1)
    def _():
        hn_ref[...] = h_ref[...] + mod_ref[5:6, :] * hn_ref[...]


def _expert_choice_ffn(h, mod, u2, logits, w_exp_in, w_exp_out):
    b, n, d = u2.shape
    cap = max(1, EC_CAPACITY * n // N_EXPERTS)
    n_sub = min(4, n // SUB_TOKENS)
    big = n_sub * SUB_TOKENS
    n_big = n // big
    capp = cap + COMBINE_WINDOW
    pos = _route(logits, cap)
    cnt = jnp.sum((pos >= 0).reshape(b, N_EXPERTS, n // SUB_TOKENS, SUB_TOKENS), axis=-1)
    base = (jnp.cumsum(cnt, axis=-1) - cnt) // 16 * 16
    base = base.astype(jnp.int32)
    f = w_exp_out.shape[1]
    ye = pl.pallas_call(
        functools.partial(_experts_kernel, n_sub=n_sub, cap=cap),
        grid_spec=pltpu.PrefetchScalarGridSpec(
            num_scalar_prefetch=1, grid=(b, N_EXPERTS, n_big),
            in_specs=[pl.BlockSpec((None, big, d), lambda bi, ei, kb, base_r: (bi, kb, 0)),
                      pl.BlockSpec((None, None, 1, big), lambda bi, ei, kb, base_r: (bi, ei, 0, kb)),
                      pl.BlockSpec((None, d, 2 * f), lambda bi, ei, kb, base_r: (ei, 0, 0)),
                      pl.BlockSpec((None, f, d), lambda bi, ei, kb, base_r: (ei, 0, 0))],
            out_specs=pl.BlockSpec((None, None, capp, d), lambda bi, ei, kb, base_r: (bi, ei, 0, 0)),
            scratch_shapes=[pltpu.VMEM((capp, d), F32)]),
        out_shape=jax.ShapeDtypeStruct((b, N_EXPERTS, capp, d), BF16),
        compiler_params=_params(3, 56 * 1024 * 1024), name="experts",
    )(base, u2, pos.reshape(b, N_EXPERTS, 1, n), w_exp_in, w_exp_out)
    posn = jnp.swapaxes(pos, 1, 2)
    return pl.pallas_call(
        functools.partial(_combine_kernel, n_sub=n_sub),
        grid_spec=pltpu.PrefetchScalarGridSpec(
            num_scalar_prefetch=1, grid=(b, n_big, N_EXPERTS),
            in_specs=[pl.BlockSpec((None, big, d), lambda bi, kb, ei, base_r: (bi, kb, 0)),
                      pl.BlockSpec((None, MOD_ROWS, d), lambda bi, kb, ei, base_r: (bi, 0, 0)),
                      pl.BlockSpec((None, big, HEAD_LANES), lambda bi, kb, ei, base_r: (bi, kb, 0)),
                      pl.BlockSpec((None, big, N_EXPERTS), lambda bi, kb, ei, base_r: (bi, kb, 0)),
                      pl.BlockSpec((None, None, capp, d), lambda bi, kb, ei, base_r: (bi, ei, 0, 0))],
            out_specs=pl.BlockSpec((None, big, d), lambda bi, kb, ei, base_r: (bi, kb, 0))),
        out_shape=jax.ShapeDtypeStruct((b, n, d), F32),
        compiler_params=_params(3), name="combine",
    )(base, h, mod, logits, posn, ye)


def _split(z, sizes):
    out, start = [], 0
    for s in sizes:
        out.append(z[..., start:start + s])
        start += s
    return out


def _layer_norm(x, g, b):
    mu = jnp.mean(x, axis=-1, keepdims=True)
    var = jnp.mean(jnp.square(x - mu), axis=-1, keepdims=True)
    return (x - mu) * lax.rsqrt(var + EPS) * g + b


def _depthwise_conv(u, w):
    k, c = w.shape
    return lax.conv_general_dilated(u, w[:, None, :], window_strides=(1,), padding=[(k // 2, k // 2)],
                                    dimension_numbers=('NWC', 'WIO', 'NWC'), feature_group_count=c)


def _hyena_filters(n, p):
    t = jnp.linspace(0.0, 1.0, n, dtype=F32)[:, None]
    bands = (FILT_EMB - 1) // 2
    w = (2.0 * math.pi / n) * jnp.arange(n, dtype=F32)[:, None]
    f = jnp.linspace(1e-4, bands - 1, bands, dtype=F32)[None, :]
    z = jnp.concatenate([t, jnp.cos(f * w), -jnp.sin(f * w)], axis=-1)
    hid = jnp.sin(p['filt_freq'][0] * (z @ p['filt_w1'] + p['filt_b1']))
    hid = jnp.sin(p['filt_freq'][1] * (hid @ p['filt_w2'] + p['filt_b2']))
    h = (hid @ p['filt_w3']).reshape(n, HYENA_ORDER, 2, HYENA_WIDTH)
    deltas = jnp.abs(jnp.linspace(math.log(DECAY_TARGET) / SLOW_DECAY, math.log(DECAY_TARGET) / FAST_DECAY,
                                  HYENA_WIDTH, dtype=F32))
    h = h * jnp.exp(-t * deltas)[:, None, None, :]
    return h / jnp.sum(jnp.abs(h), axis=0, keepdims=True)


def _centred_long_conv(u, h_fwd, h_bwd):
    n = u.shape[1]
    k = jnp.concatenate([h_fwd, jnp.zeros_like(h_fwd[:1]), h_bwd[:-1][::-1]], axis=0)
    u_f = jnp.fft.rfft(u, n=2 * n, axis=1)
    k_f = jnp.fft.rfft(k, axis=0)
    return jnp.fft.irfft(u_f * k_f[None], n=2 * n, axis=1)[:, :n]


def _hyena_branch(z, p):
    n = z.shape[1]
    z = _depthwise_conv(z, p['hyena_short_w']) + p['hyena_short_b']
    x1, x2, v = _split(z, (HYENA_WIDTH, HYENA_WIDTH, HYENA_WIDTH))
    h = _hyena_filters(n, p)
    for o, gate in enumerate((x1, x2)):
        v = gate * (_centred_long_conv(v, h[:, o, 0], h[:, o, 1]) + p['hyena_skip'][o] * v)
    return v


def _conformer_branch(glu, p):
    u = _depthwise_conv(glu, p['conf_dw_w'])
    return jax.nn.silu(_layer_norm(u, p['conf_ln_g'], p['conf_ln_b']))


def _mod_rows(mod, norm_mix_g, norm_ffn_g, batch):
    sh1, sc1, g1, sh2, sc2, g2 = jnp.split(mod, 6, axis=-1)
    rows = jnp.stack([norm_mix_g * (1.0 + sc1), sh1, g1, norm_ffn_g * (1.0 + sc2), sh2, g2,
                      jnp.zeros_like(g1), jnp.zeros_like(g1)], axis=1)
    return jnp.broadcast_to(rows, (batch,) + rows.shape[1:])


def kernel(x, c, ctx, c_ctx, ada_w, ada_b, norm_mix_g, norm_ffn_g, w_in, diff_lambda, diff_subln_g, hyena_short_w, hyena_short_b, filt_w1, filt_b1, filt_freq, filt_w2, filt_b2, filt_w3, hyena_skip, conf_dw_w, conf_ln_g, conf_ln_b, mla_q_norm_g, mla_kv_norm_g, mla_w_uq, mla_w_ukv, w_branch, w_out, w_router, w_exp_in, w_exp_out, final_norm_g):
    depth = w_in.shape[0]
    batch, n_lat, d = x.shape
    n_ctx = ctx.shape[1]
    rope_lat = _rope_operands(n_lat, identity=False)
    rope_ctx = _rope_operands(n_ctx, identity=True)
    s_lat = jax.nn.silu(c)
    s_ctx = jax.nn.silu(c_ctx)[None]
    tile_lat, tile_ctx = min(512, n_lat), min(256, n_ctx)
    h_lat, h_ctx = x, ctx
    for l in range(depth):
        last = l == depth - 1
        p = dict(w_in=w_in[l], diff_subln_g=diff_subln_g[l], hyena_short_w=hyena_short_w[l],
                 hyena_short_b=hyena_short_b[l], filt_w1=filt_w1[l], filt_b1=filt_b1[l], filt_freq=filt_freq[l],
                 filt_w2=filt_w2[l], filt_b2=filt_b2[l], filt_w3=filt_w3[l], hyena_skip=hyena_skip[l],
                 conf_dw_w=conf_dw_w[l], conf_ln_g=conf_ln_g[l], conf_ln_b=conf_ln_b[l],
                 mla_q_norm_g=mla_q_norm_g[l], mla_kv_norm_g=mla_kv_norm_g[l], mla_w_uq=mla_w_uq[l],
                 mla_w_ukv=mla_w_ukv[l], w_branch=w_branch[l], w_out=w_out[l], w_router=w_router[l],
                 w_exp_in=w_exp_in[l], w_exp_out=w_exp_out[l])
        mod_lat = _mod_rows(s_lat @ ada_w[l] + ada_b[l], norm_mix_g[l], norm_ffn_g[l], batch)
        mod_ctx = _mod_rows(s_ctx @ ada_w[l] + ada_b[l], norm_mix_g[l], norm_ffn_g[l], batch)
        lam_init = 0.8 - 0.6 * math.exp(-0.3 * l)
        lq1, lk1, lq2, lk2 = diff_lambda[l].astype(F32)
        lam = jnp.reshape(jnp.exp(jnp.sum(lq1 * lk1)) - jnp.exp(jnp.sum(lq2 * lk2)) + lam_init, (1,))
        w_inp, w_mrg = _inproj_weights(p), _merge_weights(p, lam_init)

        qdT_l, kd_l, vdT_l, qmT_l, km_l, vmT_l, hy_l, glu_l = _inproj(h_lat, mod_lat, w_inp, rope_lat, tile=tile_lat)
        qdT_c, kd_c, vdT_c, qmT_c, km_c, vmT_c, hy_c, glu_c = _inproj(h_ctx, mod_ctx, w_inp, rope_ctx, tile=tile_ctx)
        a_lat = _flash_attention(lam, qdT_l, kd_c, vdT_c, kd_l, vdT_l, n_maps=2, tq=min(256, n_lat))
        m_lat = _flash_attention(lam, qmT_l, km_c, vmT_c, km_l, vmT_l, n_maps=1, tq=min(512, n_lat))
        h_lat, u2_lat, lg_lat = _merge(h_lat, mod_lat, a_lat, _hyena_branch(hy_l, p), _conformer_branch(glu_l, p),
                                       m_lat, w_mrg, tile=min(256, n_lat))
        w_ei, w_eo = p['w_exp_in'].astype(BF16), p['w_exp_out'].astype(BF16)
        h_lat = _expert_choice_ffn(h_lat, mod_lat, u2_lat, lg_lat, w_ei, w_eo)
        if not last:
            a_ctx = _flash_attention(lam, qdT_c, kd_c, vdT_c, None, None, n_maps=2, tq=n_ctx)
            m_ctx = _flash_attention(lam, qmT_c, km_c, vmT_c, None, None, n_maps=1, tq=n_ctx)
            h_ctx, u2_ctx, lg_ctx = _merge(h_ctx, mod_ctx, a_ctx, _hyena_branch(hy_c, p),
                                           _conformer_branch(glu_c, p), m_ctx, w_mrg, tile=tile_ctx)
            h_ctx = _expert_choice_ffn(h_ctx, mod_ctx, u2_ctx, lg_ctx, w_ei, w_eo)
    hf = h_lat
    return hf * lax.rsqrt(jnp.mean(hf * hf, axis=-1, keepdims=True) + EPS) * final_norm_g
```

```python
import functools
import math

import jax
import jax.numpy as jnp
from jax import lax
from jax.experimental import pallas as pl
from jax.experimental.pallas import tpu as pltpu

GRID_W = 64
ROPE_BASE = 10000.0
EPS = 1e-6

DIFF_HEADS = 4
DIFF_HEAD_DIM = 64
DIFF_V_DIM = 2 * DIFF_HEAD_DIM
HYENA_WIDTH = 256
HYENA_ORDER = 2
FILT_EMB = 33
DECAY_TARGET = 1e-2
FAST_DECAY = 0.3
SLOW_DECAY = 1.5
CONF_WIDTH = 256
MLA_HEADS = 4
MLA_Q_RANK = 256
MLA_KV_RANK = 128
MLA_NOPE = 64
MLA_ROPE = 32
MLA_V = 64
MLA_SCALE = (MLA_NOPE + MLA_ROPE) ** -0.5
N_BRANCH = 4
N_EXPERTS = 16
EC_CAPACITY = 2

DIFF_QK_W = DIFF_HEADS * 2 * DIFF_HEAD_DIM
DIFF_V_W = DIFF_HEADS * DIFF_V_DIM
HYENA_PROJ = (HYENA_ORDER + 1) * HYENA_WIDTH
CONF_PROJ = 2 * CONF_WIDTH
IN_SPLITS = (DIFF_QK_W, DIFF_QK_W, DIFF_V_W, HYENA_PROJ, CONF_PROJ, MLA_Q_RANK, MLA_KV_RANK, MLA_ROPE)
BRANCH_WIDTHS = (DIFF_V_W, HYENA_WIDTH, CONF_WIDTH, MLA_HEADS * MLA_V)

HEAD_LANES = 128
ATT_W = DIFF_HEADS * HEAD_LANES
LOG2E = 1.4426950408889634
VMEM_LIMIT_BYTES = 48 * 1024 * 1024
MOD_ROWS = 8

F32 = jnp.float32
BF16 = jnp.bfloat16
_NT = (((1,), (1,)), ((), ()))


def _params(n_axes, vmem=VMEM_LIMIT_BYTES):
    return pltpu.CompilerParams(dimension_semantics=("arbitrary",) * n_axes, vmem_limit_bytes=vmem)


def _flash_kernel(lam_ref, qT_ref, kc_ref, vcT_ref, *rest, n_maps, n_lat_chunks, tk):
    if n_lat_chunks:
        kl_ref, vlT_ref, o_ref, acc_ref, m_ref, l_ref, q2_ref, s_ref = rest
    else:
        o_ref, acc_ref, m_ref, l_ref, q2_ref = rest
    qT = qT_ref[...]
    tq = qT.shape[1]
    if n_maps == 2:
        row = lax.broadcasted_iota(jnp.int32, qT.shape, 0)
        zero = jnp.zeros_like(qT)
        q2_ref[:, :tq] = jnp.where(row < DIFF_HEAD_DIM, qT, zero)
        q2_ref[:, tq:] = jnp.where(row >= DIFF_HEAD_DIM, qT, zero)
    else:
        q2_ref[...] = qT
    m_ref[...] = jnp.full(m_ref.shape, -jnp.inf, F32)
    l_ref[...] = jnp.zeros(l_ref.shape, F32)
    acc_ref[...] = jnp.zeros(acc_ref.shape, F32)

    def scores(k):
        return jnp.dot(k, q2_ref[...], preferred_element_type=F32)

    def absorb(s, vT):
        m_prev = m_ref[...]
        m_new = jnp.maximum(m_prev, jnp.max(s, axis=0, keepdims=True))
        alpha = jnp.exp2(m_prev - m_new)
        p = jnp.exp2(s - m_new)
        l_ref[...] = alpha * l_ref[...] + jnp.sum(p, axis=0, keepdims=True)
        acc_ref[...] = alpha * acc_ref[...] + jnp.dot(vT, p.astype(BF16), preferred_element_type=F32)
        m_ref[...] = m_new

    def chunk(c):
        return pl.ds(c * tk if isinstance(c, int) else pl.multiple_of(c * tk, tk), tk)

    def keys(c):
        return kl_ref[chunk(c), :]

    def values_t(c):
        return vlT_ref[:, chunk(c)]

    absorb(scores(kc_ref[...]), vcT_ref[...])
    if n_lat_chunks:
        s_ref[0] = scores(keys(0))

        def pair(j, carry):
            c = 2 * j
            s_ref[1] = scores(keys(c + 1))
            absorb(s_ref[0], values_t(c))
            s_ref[0] = scores(keys(c + 2))
            absorb(s_ref[1], values_t(c + 1))
            return carry
        lax.fori_loop(0, n_lat_chunks // 2 - 1, pair, 0)
        s_ref[1] = scores(keys(n_lat_chunks - 1))
        absorb(s_ref[0], values_t(n_lat_chunks - 2))
        absorb(s_ref[1], values_t(n_lat_chunks - 1))
    o = acc_ref[...] / l_ref[...]
    if n_maps == 2:
        o = o[:, :tq] - lam_ref[0] * o[:, tq:]
        o = o * lax.rsqrt(jnp.mean(o * o, axis=0, keepdims=True) + EPS)
    o_ref[...] = o.T.astype(BF16)


def _flash_attention(lam, qT, kc, vcT, kl, vlT, *, n_maps, tq):
    b, _, s = qT.shape
    lc = kc.shape[1]
    r = n_maps * tq
    in_specs = [
        pl.BlockSpec(memory_space=pltpu.SMEM),
        pl.BlockSpec((None, HEAD_LANES, tq), lambda bi, hi, qi: (bi, hi, qi)),
        pl.BlockSpec((None, lc, HEAD_LANES), lambda bi, hi, qi: (bi, 0, hi)),
        pl.BlockSpec((None, HEAD_LANES, lc), lambda bi, hi, qi: (bi, hi, 0)),
    ]
    args = [lam, qT, kc, vcT]
    scratch = [pltpu.VMEM((HEAD_LANES, r), F32), pltpu.VMEM((1, r), F32), pltpu.VMEM((1, r), F32),
               pltpu.VMEM((HEAD_LANES, r), BF16)]
    n_lat_chunks, tk = 0, 0
    if kl is not None:
        sl = kl.shape[1]
        tk = _lat_chunk(sl)
        n_lat_chunks = sl // tk
        assert n_lat_chunks % 2 == 0 and n_lat_chunks * tk == sl
        scratch.append(pltpu.VMEM((2, tk, r), F32))
        in_specs += [
            pl.BlockSpec((None, sl, HEAD_LANES), lambda bi, hi, qi: (bi, 0, hi)),
            pl.BlockSpec((None, HEAD_LANES, sl), lambda bi, hi, qi: (bi, hi, 0)),
        ]
        args += [kl, vlT]
    return pl.pallas_call(
        functools.partial(_flash_kernel, n_maps=n_maps, n_lat_chunks=n_lat_chunks, tk=tk),
        grid=(b, DIFF_HEADS, s // tq),
        in_specs=in_specs,
        out_specs=pl.BlockSpec((None, tq, HEAD_LANES), lambda bi, hi, qi: (bi, qi, hi)),
        out_shape=jax.ShapeDtypeStruct((b, s, ATT_W), BF16),
        scratch_shapes=scratch,
        compiler_params=_params(3),
        name=f"flash_attention_{n_maps}map",
    )(*args)


def _lat_chunk(s):
    return min(1024, s // 2)


W_NAT_SPLITS = (DIFF_QK_W, HYENA_PROJ, CONF_PROJ, MLA_Q_RANK, MLA_KV_RANK, HEAD_LANES)


def _modulated_norm(h, a, shift):
    return h * lax.rsqrt(jnp.mean(h * h, axis=-1, keepdims=True) + EPS) * a + shift


def _rope_lanes(x, tab_ref, shift):
    return (x * tab_ref[0] + pltpu.roll(x, shift, 1) * tab_ref[1]
            + pltpu.roll(x, HEAD_LANES - shift, 1) * tab_ref[2])


def _inproj_kernel(h_ref, mod_ref, wnat_ref, wT_ref, wuqT_ref, wukvk_ref, wuvT_ref, gq_ref, gkv_ref,
                   ropeT_d_ref, rope_kd_ref, ropeT_m_ref, rope_km_ref,
                   qdT_ref, kd_ref, vdT_ref, qmT_ref, km_ref, vmT_ref, hy_ref, glu_ref):
    u = _modulated_norm(h_ref[...], mod_ref[0:1, :], mod_ref[1:2, :]).astype(BF16)
    z = jnp.dot(u, wnat_ref[...], preferred_element_type=F32)
    zT = lax.dot_general(wT_ref[...], u, _NT, preferred_element_type=F32)
    offs = [0]
    for w in W_NAT_SPLITS:
        offs.append(offs[-1] + w)
    dk, hy, cf, cq, ckv, krp = (z[:, offs[i]:offs[i + 1]] for i in range(len(W_NAT_SPLITS)))

    for hd in range(DIFF_HEADS):
        sl = slice(hd * HEAD_LANES, (hd + 1) * HEAD_LANES)
        kd_ref[:, sl] = _rope_lanes(dk[:, sl], rope_kd_ref, DIFF_HEAD_DIM // 2).astype(BF16)
    cos_d, sin_d = ropeT_d_ref[0], ropeT_d_ref[1]
    half = DIFF_HEAD_DIM // 2
    for g in range(2 * DIFF_HEADS):
        x1 = zT[g * DIFF_HEAD_DIM:g * DIFF_HEAD_DIM + half]
        x2 = zT[g * DIFF_HEAD_DIM + half:(g + 1) * DIFF_HEAD_DIM]
        qdT_ref[g * DIFF_HEAD_DIM:g * DIFF_HEAD_DIM + half, :] = (x1 * cos_d - x2 * sin_d).astype(BF16)
        qdT_ref[g * DIFF_HEAD_DIM + half:(g + 1) * DIFF_HEAD_DIM, :] = (x1 * sin_d + x2 * cos_d).astype(BF16)
    vdT_ref[...] = zT[DIFF_QK_W:].astype(BF16)

    hy_ref[...] = hy
    glu_ref[...] = cf[:, :CONF_WIDTH] * jax.nn.sigmoid(cf[:, CONF_WIDTH:])

    cqn = (cq * lax.rsqrt(jnp.mean(cq * cq, axis=-1, keepdims=True) + EPS) * gq_ref[...]).astype(BF16)
    ckvn = (ckv * lax.rsqrt(jnp.mean(ckv * ckv, axis=-1, keepdims=True) + EPS) * gkv_ref[...]).astype(BF16)
    qT = lax.dot_general(wuqT_ref[...], cqn, _NT, preferred_element_type=F32)
    cos_m, sin_m = ropeT_m_ref[0], ropeT_m_ref[1]
    hr = MLA_ROPE // 2
    for hd in range(MLA_HEADS):
        base = hd * HEAD_LANES
        r1 = base + MLA_NOPE
        x1, x2 = qT[r1:r1 + hr], qT[r1 + hr:r1 + 2 * hr]
        qmT_ref[base:r1, :] = qT[base:r1].astype(BF16)
        qmT_ref[r1:r1 + hr, :] = (x1 * cos_m - x2 * sin_m).astype(BF16)
        qmT_ref[r1 + hr:r1 + 2 * hr, :] = (x1 * sin_m + x2 * cos_m).astype(BF16)
        qmT_ref[r1 + 2 * hr:base + HEAD_LANES, :] = jnp.zeros((HEAD_LANES - MLA_NOPE - MLA_ROPE, qT.shape[1]), BF16)
    kn = jnp.dot(ckvn, wukvk_ref[...], preferred_element_type=F32)
    kr = _rope_lanes(krp, rope_km_ref, hr)
    for hd in range(MLA_HEADS):
        sl = slice(hd * HEAD_LANES, (hd + 1) * HEAD_LANES)
        km_ref[:, sl] = (kn[:, sl] + kr).astype(BF16)
    vmT_ref[...] = lax.dot_general(wuvT_ref[...], ckvn, _NT, preferred_element_type=F32).astype(BF16)


def _inproj(h, mod, wts, rope, *, tile):
    b, n, d = h.shape
    const2 = lambda bi, ti: (0, 0)
    tok = lambda w: pl.BlockSpec((None, tile, w), lambda bi, ti: (bi, ti, 0))
    tokT = lambda w: pl.BlockSpec((None, w, tile), lambda bi, ti: (bi, 0, ti))
    full = lambda a: pl.BlockSpec(a.shape, const2)
    in_specs = [tok(d), pl.BlockSpec((None, MOD_ROWS, d), lambda bi, ti: (bi, 0, 0))]
    in_specs += [full(wts[k]) for k in ('w_nat', 'w_T', 'w_uqT', 'w_ukvk', 'w_uvT', 'gq', 'gkv')]
    in_specs += [pl.BlockSpec((2, DIFF_HEAD_DIM // 2, tile), lambda bi, ti: (0, 0, ti)),
                 pl.BlockSpec((3, tile, HEAD_LANES), lambda bi, ti: (0, ti, 0)),
                 pl.BlockSpec((2, MLA_ROPE // 2, tile), lambda bi, ti: (0, 0, ti)),
                 pl.BlockSpec((3, tile, HEAD_LANES), lambda bi, ti: (0, ti, 0))]
    sds = jax.ShapeDtypeStruct
    out_shape = (sds((b, ATT_W, n), BF16), sds((b, n, ATT_W), BF16), sds((b, ATT_W, n), BF16),
                 sds((b, ATT_W, n), BF16), sds((b, n, ATT_W), BF16), sds((b, ATT_W, n), BF16),
                 sds((b, n, HYENA_PROJ), F32), sds((b, n, CONF_WIDTH), F32))
    out_specs = (tokT(ATT_W), tok(ATT_W), tokT(ATT_W), tokT(ATT_W), tok(ATT_W), tokT(ATT_W),
                 tok(HYENA_PROJ), tok(CONF_WIDTH))
    return pl.pallas_call(
        _inproj_kernel, grid=(b, n // tile), in_specs=in_specs, out_specs=out_specs, out_shape=out_shape,
        compiler_params=_params(2), name="inproj",
    )(h, mod, wts['w_nat'], wts['w_T'], wts['w_uqT'], wts['w_ukvk'], wts['w_uvT'], wts['gq'], wts['gkv'],
      rope['T_d'], rope['k_d'], rope['T_m'], rope['k_m'])


def _pad_heads(w, width):
    rows = w.shape[0]
    w = w.reshape(rows, MLA_HEADS, width)
    return jnp.pad(w, ((0, 0), (0, 0), (0, HEAD_LANES - width))).reshape(rows, ATT_W)


def _inproj_weights(p):
    d = p['w_in'].shape[0]
    dq, dk, dv, hy, cf, cq, ckv, kr = _split(p['w_in'][:, :sum(IN_SPLITS)], IN_SPLITS)
    krp = jnp.zeros((d, HEAD_LANES), F32).at[:, MLA_NOPE:MLA_NOPE + MLA_ROPE].set(kr)
    w_ukv = p['mla_w_ukv'].reshape(MLA_KV_RANK, MLA_HEADS, MLA_NOPE + MLA_V)
    return dict(
        w_nat=jnp.concatenate([dk, hy, cf, cq, ckv, krp], axis=1).astype(BF16),
        w_T=jnp.concatenate([dq * (DIFF_HEAD_DIM ** -0.5 * LOG2E), dv], axis=1).T.astype(BF16),
        w_uqT=_pad_heads(p['mla_w_uq'] * (MLA_SCALE * LOG2E), MLA_NOPE + MLA_ROPE).T.astype(BF16),
        w_ukvk=_pad_heads(w_ukv[:, :, :MLA_NOPE].reshape(MLA_KV_RANK, -1), MLA_NOPE).astype(BF16),
        w_uvT=_pad_heads(w_ukv[:, :, MLA_NOPE:].reshape(MLA_KV_RANK, -1), MLA_V).T.astype(BF16),
        gq=p['mla_q_norm_g'][None, :], gkv=p['mla_kv_norm_g'][None, :])


def _rope_tables(n_tok, rot_dim):
    rows = n_tok // GRID_W
    row = jnp.repeat(jnp.arange(rows), GRID_W).astype(F32)
    col = jnp.tile(jnp.arange(GRID_W), rows).astype(F32)
    nf = rot_dim // 4
    inv = ROPE_BASE ** (-jnp.arange(nf, dtype=F32) / nf)
    ang = jnp.concatenate([row[:, None] * inv, col[:, None] * inv], axis=-1)
    return jnp.cos(ang), jnp.sin(ang)


def _rope_operands(n_tok, identity):
    if identity:
        cos_d, sin_d = jnp.ones((n_tok, DIFF_HEAD_DIM // 2), F32), jnp.zeros((n_tok, DIFF_HEAD_DIM // 2), F32)
        cos_m, sin_m = jnp.ones((n_tok, MLA_ROPE // 2), F32), jnp.zeros((n_tok, MLA_ROPE // 2), F32)
    else:
        cos_d, sin_d = _rope_tables(n_tok, DIFF_HEAD_DIM)
        cos_m, sin_m = _rope_tables(n_tok, MLA_ROPE)
    z_d, z_m = jnp.zeros_like(sin_d), jnp.zeros_like(sin_m)
    two = lambda a, bb: jnp.tile(jnp.concatenate([a, bb], axis=1), (1, 2))
    lo, hi = jnp.zeros((n_tok, MLA_NOPE), F32), jnp.zeros((n_tok, HEAD_LANES - MLA_NOPE - MLA_ROPE), F32)
    mid = lambda a, bb: jnp.concatenate([lo, a, bb, hi], axis=1)
    return dict(T_d=jnp.stack([cos_d.T, sin_d.T]), T_m=jnp.stack([cos_m.T, sin_m.T]),
                k_d=jnp.stack([two(cos_d, cos_d), two(z_d, sin_d), two(-sin_d, z_d)]),
                k_m=jnp.stack([mid(cos_m, cos_m), mid(z_m, sin_m), mid(-sin_m, z_m)]))


def _merge_kernel(h_ref, mod_ref, a_ref, hy_ref, cf_ref, m_ref, wg_ref, wbd_ref, wbh_ref, wbc_ref, wbm_ref,
                  wo_ref, wrh_ref, wrl_ref, hn_ref, u2_ref, lg_ref):
    h = h_ref[...]
    d = h.shape[1]
    u = _modulated_norm(h, mod_ref[0:1, :], mod_ref[1:2, :]).astype(BF16)
    gates = jax.nn.sigmoid(jnp.dot(u, wg_ref[...], preferred_element_type=F32))
    dot = lambda x, w_ref: jnp.dot(x, w_ref[...], preferred_element_type=F32)
    acc = gates[:, :d] * dot(a_ref[...], wbd_ref)
    acc += gates[:, d:2 * d] * dot(hy_ref[...].astype(BF16), wbh_ref)
    acc += gates[:, 2 * d:3 * d] * dot(cf_ref[...].astype(BF16), wbc_ref)
    acc += gates[:, 3 * d:] * dot(m_ref[...], wbm_ref)
    hn = h + mod_ref[2:3, :] * dot(acc.astype(BF16), wo_ref)
    hn_ref[...] = hn
    u2 = _modulated_norm(hn, mod_ref[3:4, :], mod_ref[4:5, :])
    u2h = u2.astype(BF16)
    u2l = (u2 - u2h.astype(F32)).astype(BF16)
    u2_ref[...] = u2h
    lg_ref[...] = dot(u2h, wrh_ref) + (dot(u2l, wrh_ref) + dot(u2h, wrl_ref))


def _merge(h, mod, a, hyv, cfv, m, wts, *, tile):
    b, n, d = h.shape
    const2 = lambda bi, ti: (0, 0)
    tok = lambda w: pl.BlockSpec((None, tile, w), lambda bi, ti: (bi, ti, 0))
    names = ('w_gate', 'w_bd', 'w_bh', 'w_bc', 'w_bm', 'w_out', 'w_rh', 'w_rl')
    in_specs = [tok(d), pl.BlockSpec((None, MOD_ROWS, d), lambda bi, ti: (bi, 0, 0)),
                tok(ATT_W), tok(HYENA_WIDTH), tok(CONF_WIDTH), tok(ATT_W)]
    in_specs += [pl.BlockSpec(wts[k].shape, const2) for k in names]
    sds = jax.ShapeDtypeStruct
    return pl.pallas_call(
        _merge_kernel, grid=(b, n // tile), in_specs=in_specs,
        out_specs=(tok(d), tok(d), tok(HEAD_LANES)),
        out_shape=(sds((b, n, d), F32), sds((b, n, d), BF16), sds((b, n, HEAD_LANES), F32)),
        compiler_params=_params(2, 56 * 1024 * 1024), name="merge",
    )(h, mod, a, hyv, cfv, m, *[wts[k] for k in names])


def _merge_weights(p, lam_init):
    d = p['w_out'].shape[0]
    wb_d, wb_h, wb_c, wb_m = (w.T for w in _split(p['w_branch'].T, BRANCH_WIDTHS))
    wb_d = wb_d * (jnp.tile(p['diff_subln_g'], DIFF_HEADS) * (1.0 - lam_init))[:, None]
    wb_m = jnp.pad(wb_m.reshape(MLA_HEADS, MLA_V, d), ((0, 0), (0, HEAD_LANES - MLA_V), (0, 0))).reshape(ATT_W, d)
    w_r = jnp.pad(p['w_router'], ((0, 0), (0, HEAD_LANES - N_EXPERTS)))
    w_rh = w_r.astype(BF16)
    return dict(w_gate=p['w_in'][:, sum(IN_SPLITS):].astype(BF16), w_bd=wb_d.astype(BF16), w_bh=wb_h.astype(BF16),
                w_bc=wb_c.astype(BF16), w_bm=wb_m.astype(BF16), w_out=p['w_out'].astype(BF16),
                w_rh=w_rh, w_rl=(w_r - w_rh.astype(F32)).astype(BF16))


SUB_TOKENS = 256
GATHER_WINDOW = 272
COMBINE_WINDOW = 384
ROUTE_MIN_ROWS = 8


def _excl_scan(x, lane, row):
    inc = x
    s = 1
    while s < HEAD_LANES:
        inc = inc + jnp.where(lane >= s, pltpu.roll(inc, s, 2), 0.0)
        s *= 2
    tot = jnp.sum(x, axis=2, keepdims=True) + jnp.zeros_like(x)
    off = tot
    s = 1
    while s < x.shape[1]:
        off = off + jnp.where(row >= s, pltpu.roll(off, s, 1), 0.0)
        s *= 2
    return inc - x + (off - tot)


def _route_kernel(lg_ref, pos_ref, *, n_valid, cap):
    lg = lg_ref[...]
    shape = lg.shape
    lane = lax.broadcasted_iota(jnp.int32, shape, 2)
    row = lax.broadcasted_iota(jnp.int32, shape, 1)
    e = jnp.exp(lg - jnp.max(lg, axis=0, keepdims=True))
    aff = e / jnp.sum(e, axis=0, keepdims=True)
    bits = jnp.where(row * HEAD_LANES + lane < n_valid, pltpu.bitcast(aff, jnp.int32), -1)

    def count(mask):
        c = jnp.sum(jnp.where(mask, 1.0, 0.0), axis=2, keepdims=True)
        return jnp.sum(c, axis=1, keepdims=True)

    def step(i, thr):
        cand = thr | (jnp.int32(1) << (30 - i))
        return jnp.where(count(bits >= cand) >= cap, cand, thr)
    thr = lax.fori_loop(0, 31, step, jnp.zeros((shape[0], 1, 1), jnp.int32))
    gt = bits > thr
    eq = bits == thr
    need = cap - count(gt)
    tie_rank = _excl_scan(jnp.where(eq, 1.0, 0.0), lane, row)
    sel = gt | (eq & (tie_rank < need))
    pos = _excl_scan(jnp.where(sel, 1.0, 0.0), lane, row)
    pos_ref[...] = jnp.where(sel, pos.astype(jnp.int32), -1)


def _route(logits, cap):
    b, n, _ = logits.shape
    rows = max(ROUTE_MIN_ROWS, n // HEAD_LANES)
    lg = jnp.swapaxes(logits[..., :N_EXPERTS], 1, 2)
    lg = jnp.pad(lg, ((0, 0), (0, 0), (0, rows * HEAD_LANES - n))).reshape(b, N_EXPERTS, rows, HEAD_LANES)
    spec = pl.BlockSpec((None, N_EXPERTS, rows, HEAD_LANES), lambda bi: (bi, 0, 0, 0))
    pos = pl.pallas_call(
        functools.partial(_route_kernel, n_valid=n, cap=cap), grid=(b,), in_specs=[spec], out_specs=spec,
        out_shape=jax.ShapeDtypeStruct(lg.shape, jnp.int32), compiler_params=_params(1), name="route",
    )(lg)
    return pos.reshape(b, N_EXPERTS, rows * HEAD_LANES)[..., :n]


def _experts_kernel(base_ref, u_ref, pos_ref, win_ref, wout_ref, ye_ref, xe_ref, *, n_sub, cap):
    bi, ei, kb = pl.program_id(0), pl.program_id(1), pl.program_id(2)

    @pl.when(kb == 0)
    def _():
        xe_ref[...] = jnp.zeros(xe_ref.shape, F32)

    slot = lax.broadcasted_iota(jnp.int32, (GATHER_WINDOW, SUB_TOKENS), 0)
    for j in range(n_sub):
        base = pl.multiple_of(base_ref[bi, ei, kb * n_sub + j], 16)
        rel = pos_ref[:, j * SUB_TOKENS:(j + 1) * SUB_TOKENS] - base
        onehot = jnp.where(slot == rel, 1.0, 0.0).astype(BF16)
        rows = pl.ds(base, GATHER_WINDOW)
        xe_ref[rows, :] += jnp.dot(onehot, u_ref[j * SUB_TOKENS:(j + 1) * SUB_TOKENS, :],
                                   preferred_element_type=F32)

    @pl.when(kb == pl.num_programs(2) - 1)
    def _():
        f = wout_ref.shape[0]
        step = min(512, cap)
        for r0 in range(0, cap, step):
            x = xe_ref[r0:r0 + step, :].astype(BF16)
            hgu = jnp.dot(x, win_ref[...], preferred_element_type=F32)
            act = (jax.nn.silu(hgu[:, :f]) * hgu[:, f:]).astype(BF16)
            ye_ref[r0:r0 + step, :] = jnp.dot(act, wout_ref[...], preferred_element_type=F32).astype(BF16)
        ye_ref[cap:, :] = jnp.zeros((ye_ref.shape[0] - cap, ye_ref.shape[1]), BF16)


def _combine_kernel(base_ref, h_ref, mod_ref, lg_ref, posn_ref, ye_ref, hn_ref, *, n_sub, final_norm):
    bi, kb, ei = pl.program_id(0), pl.program_id(1), pl.program_id(2)

    @pl.when(ei == 0)
    def _():
        hn_ref[...] = jnp.zeros(hn_ref.shape, F32)

    slot = lax.broadcasted_iota(jnp.int32, (SUB_TOKENS, COMBINE_WINDOW), 1)
    lane_e = lax.broadcasted_iota(jnp.int32, (SUB_TOKENS, N_EXPERTS), 1)
    for j in range(n_sub):
        tok = slice(j * SUB_TOKENS, (j + 1) * SUB_TOKENS)
        base = pl.multiple_of(base_ref[bi, ei, kb * n_sub + j], 16)
        lg = lg_ref[tok, :][:, :N_EXPERTS]
        ex = jnp.exp(lg - jnp.max(lg, axis=1, keepdims=True))
        aff = ex / jnp.sum(ex, axis=1, keepdims=True)
        mine = lane_e == ei
        gate = jnp.sum(jnp.where(mine, aff, 0.0), axis=1, keepdims=True)
        rel = jnp.sum(jnp.where(mine, posn_ref[tok, :], 0), axis=1, keepdims=True) - base
        onehot = jnp.where(slot == rel, 1.0, 0.0).astype(BF16)
        ye = ye_ref[pl.ds(base, COMBINE_WINDOW), :]
        hn_ref[tok, :] += gate * jnp.dot(onehot, ye, preferred_element_type=F32)

    @pl.when(ei == pl.num_programs(2) - 1)
    def _():
        hn = h_ref[...] + mod_ref[5:6, :] * hn_ref[...]
        if final_norm:
            hn = hn * lax.rsqrt(jnp.mean(hn * hn, axis=-1, keepdims=True) + EPS) * mod_ref[6:7, :]
        hn_ref[...] = hn


def _expert_choice_ffn(h, mod, u2, logits, w_exp_in, w_exp_out, final_norm=False):
    b, n, d = u2.shape
    cap = max(1, EC_CAPACITY * n // N_EXPERTS)
    n_sub = min(4, n // SUB_TOKENS)
    big = n_sub * SUB_TOKENS
    n_big = n // big
    capp = cap + COMBINE_WINDOW
    pos = _route(logits, cap)
    cnt = jnp.sum((pos >= 0).reshape(b, N_EXPERTS, n // SUB_TOKENS, SUB_TOKENS), axis=-1)
    base = (jnp.cumsum(cnt, axis=-1) - cnt) // 16 * 16
    base = base.astype(jnp.int32)
    f = w_exp_out.shape[1]
    ye = pl.pallas_call(
        functools.partial(_experts_kernel, n_sub=n_sub, cap=cap),
        grid_spec=pltpu.PrefetchScalarGridSpec(
            num_scalar_prefetch=1, grid=(b, N_EXPERTS, n_big),
            in_specs=[pl.BlockSpec((None, big, d), lambda bi, ei, kb, base_r: (bi, kb, 0)),
                      pl.BlockSpec((None, None, 1, big), lambda bi, ei, kb, base_r: (bi, ei, 0, kb)),
                      pl.BlockSpec((None, d, 2 * f), lambda bi, ei, kb, base_r: (ei, 0, 0)),
                      pl.BlockSpec((None, f, d), lambda bi, ei, kb, base_r: (ei, 0, 0))],
            out_specs=pl.BlockSpec((None, None, capp, d), lambda bi, ei, kb, base_r: (bi, ei, 0, 0)),
            scratch_shapes=[pltpu.VMEM((capp, d), F32)]),
        out_shape=jax.ShapeDtypeStruct((b, N_EXPERTS, capp, d), BF16),
        compiler_params=_params(3, 56 * 1024 * 1024), name="experts",
    )(base, u2, pos.reshape(b, N_EXPERTS, 1, n), w_exp_in, w_exp_out)
    posn = jnp.swapaxes(pos, 1, 2)
    return pl.pallas_call(
        functools.partial(_combine_kernel, n_sub=n_sub, final_norm=final_norm),
        grid_spec=pltpu.PrefetchScalarGridSpec(
            num_scalar_prefetch=1, grid=(b, n_big, N_EXPERTS),
            in_specs=[pl.BlockSpec((None, big, d), lambda bi, kb, ei, base_r: (bi, kb, 0)),
                      pl.BlockSpec((None, MOD_ROWS, d), lambda bi, kb, ei, base_r: (bi, 0, 0)),
                      pl.BlockSpec((None, big, HEAD_LANES), lambda bi, kb, ei, base_r: (bi, kb, 0)),
                      pl.BlockSpec((None, big, N_EXPERTS), lambda bi, kb, ei, base_r: (bi, kb, 0)),
                      pl.BlockSpec((None, None, capp, d), lambda bi, kb, ei, base_r: (bi, ei, 0, 0))],
            out_specs=pl.BlockSpec((None, big, d), lambda bi, kb, ei, base_r: (bi, kb, 0))),
        out_shape=jax.ShapeDtypeStruct((b, n, d), F32),
        compiler_params=_params(3), name="combine",
    )(base, h, mod, logits, posn, ye)


def _split(z, sizes):
    out, start = [], 0
    for s in sizes:
        out.append(z[..., start:start + s])
        start += s
    return out


HALO = 16


def _fill_ext(ext_ref, x_ref, prev_ref, next_ref):
    ti, nt = pl.program_id(1), pl.num_programs(1)
    tt = x_ref.shape[0]
    ext_ref[0:HALO, :] = jnp.where(ti > 0, prev_ref[...], 0.0)
    ext_ref[HALO:HALO + tt, :] = x_ref[...]
    ext_ref[HALO + tt:, :] = jnp.where(ti < nt - 1, next_ref[...], 0.0)


def _taps(ext_ref, w_ref, tt):
    k = w_ref.shape[0]
    acc = None
    for j in range(k):
        start = HALO - k // 2 + j
        term = w_ref[j:j + 1, :] * ext_ref[start:start + tt, :]
        acc = term if acc is None else acc + term
    return acc


def _short_conv_kernel(x_ref, prev_ref, next_ref, w_ref, b_ref, x1_ref, x2_ref, v_ref, ext_ref):
    _fill_ext(ext_ref, x_ref, prev_ref, next_ref)
    y = _taps(ext_ref, w_ref, x_ref.shape[0]) + b_ref[...]
    x1_ref[...] = y[:, :HYENA_WIDTH]
    x2_ref[...] = y[:, HYENA_WIDTH:2 * HYENA_WIDTH]
    v_ref[...] = y[:, 2 * HYENA_WIDTH:]


def _conformer_kernel(x_ref, prev_ref, next_ref, w_ref, g_ref, b_ref, o_ref, ext_ref):
    _fill_ext(ext_ref, x_ref, prev_ref, next_ref)
    u = _taps(ext_ref, w_ref, x_ref.shape[0])
    mu = jnp.mean(u, axis=-1, keepdims=True)
    var = jnp.mean(jnp.square(u - mu), axis=-1, keepdims=True)
    y = (u - mu) * lax.rsqrt(var + EPS) * g_ref[...] + b_ref[...]
    o_ref[...] = y * jax.nn.sigmoid(y)


def _token_conv(body, x, consts, out_widths, name):
    b, n, w = x.shape
    tt = min(1024, n)
    per = tt // HALO
    last = n // HALO - 1
    in_specs = [pl.BlockSpec((None, tt, w), lambda bi, ti: (bi, ti, 0)),
                pl.BlockSpec((None, HALO, w), lambda bi, ti: (bi, jnp.maximum(ti * per - 1, 0), 0)),
                pl.BlockSpec((None, HALO, w), lambda bi, ti: (bi, jnp.minimum((ti + 1) * per, last), 0))]
    in_specs += [pl.BlockSpec(cst.shape, lambda bi, ti: (0, 0)) for cst in consts]
    outs = tuple(jax.ShapeDtypeStruct((b, n, ow), F32) for ow in out_widths)
    out_specs = tuple(pl.BlockSpec((None, tt, ow), lambda bi, ti: (bi, ti, 0)) for ow in out_widths)
    return pl.pallas_call(body, grid=(b, n // tt), in_specs=in_specs, out_specs=out_specs, out_shape=outs,
                          scratch_shapes=[pltpu.VMEM((tt + 2 * HALO, w), F32)],
                          compiler_params=_params(2), name=name)(x, x, x, *consts)


def _conformer_branch(glu, p):
    return _token_conv(_conformer_kernel, glu, (p['conf_dw_w'], p['conf_ln_g'][None, :], p['conf_ln_b'][None, :]),
                       (CONF_WIDTH,), "conformer")[0]


FILT_LANES = 128
DFT_SHORT = 256


def _split_bf16(x):
    hi = x.astype(BF16)
    return hi, (x - hi.astype(F32)).astype(BF16)


def _dot_split(ah, al, bh, bl):
    dot = lambda u, v: jnp.dot(u, v, preferred_element_type=F32)
    return dot(ah, bh) + (dot(al, bh) + dot(ah, bl))


def _filter_kernel(z_ref, w1h, w1l, b1, f1, w2h, w2l, b2, f2, w3h, w3l, dl_ref, h_ref, asum_ref):
    z = z_ref[...]
    hid = jnp.sin(f1[...] * (_dot_split(*_split_bf16(z), w1h[...], w1l[...]) + b1[...]))
    hid = jnp.sin(f2[...] * (_dot_split(*_split_bf16(hid), w2h[...], w2l[...]) + b2[...]))
    h = _dot_split(*_split_bf16(hid), w3h[...], w3l[...])
    h = h * jnp.exp(-z[:, 0:1] * dl_ref[...])
    h_ref[...] = h

    @pl.when(pl.program_id(0) == 0)
    def _():
        asum_ref[...] = jnp.zeros(asum_ref.shape, F32)
    asum_ref[...] += jnp.sum(jnp.abs(h), axis=0, keepdims=True)


def _normalise_kernel(h_ref, asum_ref, o_ref):
    o_ref[...] = h_ref[...] / asum_ref[...]


def _hyena_filters(n, p):
    t = jnp.linspace(0.0, 1.0, n, dtype=F32)[:, None]
    bands = (FILT_EMB - 1) // 2
    w = (2.0 * math.pi / n) * jnp.arange(n, dtype=F32)[:, None]
    f = jnp.linspace(1e-4, bands - 1, bands, dtype=F32)[None, :]
    z = jnp.concatenate([t, jnp.cos(f * w), -jnp.sin(f * w), jnp.zeros((n, FILT_LANES - FILT_EMB), F32)], axis=-1)
    hid = p['filt_w2'].shape[0]
    padc = lambda a: jnp.pad(a, ((0, 0), (0, FILT_LANES - a.shape[1])))
    padr = lambda a: jnp.pad(a, ((0, FILT_LANES - a.shape[0]), (0, 0)))
    w1, w2, w3 = padc(padr(p['filt_w1'])), padc(padr(p['filt_w2'])), padr(p['filt_w3'])
    b1, b2 = padc(p['filt_b1'][None, :]), padc(p['filt_b2'][None, :])
    f1, f2 = padc(p['filt_freq'][0][None, :]), padc(p['filt_freq'][1][None, :])
    deltas = jnp.abs(jnp.linspace(math.log(DECAY_TARGET) / SLOW_DECAY, math.log(DECAY_TARGET) / FAST_DECAY,
                                  HYENA_WIDTH, dtype=F32))
    width = w3.shape[1]
    dl = jnp.tile(deltas, width // HYENA_WIDTH)[None, :]
    consts = [*_split_bf16(w1), b1, f1, *_split_bf16(w2), b2, f2, *_split_bf16(w3), dl]
    tt = min(1024, n)
    cspec = lambda a: pl.BlockSpec(a.shape, lambda i: (0, 0))
    h_raw, asum = pl.pallas_call(
        _filter_kernel, grid=(n // tt,),
        in_specs=[pl.BlockSpec((tt, FILT_LANES), lambda i: (i, 0))] + [cspec(a) for a in consts],
        out_specs=(pl.BlockSpec((tt, width), lambda i: (i, 0)), pl.BlockSpec((1, width), lambda i: (0, 0))),
        out_shape=(jax.ShapeDtypeStruct((n, width), F32), jax.ShapeDtypeStruct((1, width), F32)),
        compiler_params=_params(1), name="hyena_filter_mlp")(z, *consts)
    return pl.pallas_call(
        _normalise_kernel, grid=(n // tt,),
        in_specs=[pl.BlockSpec((tt, width), lambda i: (i, 0)), cspec(asum)],
        out_specs=pl.BlockSpec((tt, width), lambda i: (i, 0)),
        out_shape=jax.ShapeDtypeStruct((n, width), F32), compiler_params=_params(1), name="hyena_filter_norm",
    )(h_raw, asum)


def _dft_tables(n):
    n2 = DFT_SHORT if n >= 4 * DFT_SHORT else n
    n1 = n // n2

    def cis(idx):
        ang = (-2.0 * math.pi / n) * idx.astype(F32)
        return jnp.cos(ang), jnp.sin(ang)
    k2 = jnp.arange(n2)
    fr, fi = cis((k2[:, None] * k2[None, :]) % n2 * n1)
    tabs = dict(n1=n1, n2=n2)
    tabs['f_hi'], tabs['f_lo'] = _split_bf16(jnp.stack([fr, fi]))
    k1 = jnp.arange(n1)
    tr, ti = cis(k1[:, None] * k2[None, :])
    tabs['tw'] = jnp.broadcast_to(jnp.stack([tr, ti], axis=1)[..., None], (n1, 2, n2, HEAD_LANES))
    if n1 > 1:
        gr, gi = cis((k1[:, None] * k1[None, :]) % n1 * n2)
        half = n1 // 2
        grh, gih = gr[:, :half], gi[:, :half]
        tabs['m_fwd'] = _split_bf16(jnp.block([[grh, -gih], [gih, grh]]))
        tabs['m_real'] = _split_bf16(jnp.concatenate([gr, gi], axis=0))
        tabs['m_inv'] = _split_bf16(jnp.block([[grh.T, gih.T], [-gih.T, grh.T]]))
    return tabs


def _rowmix_kernel(mh_ref, ml_ref, x_ref, o_ref):
    o_ref[...] = _dot_split(mh_ref[...], ml_ref[...], *_split_bf16(x_ref[...]))


def _rowmix(m, x):
    mh, ml = m
    rin, cols = x.shape
    ct = min(2048, cols)
    return pl.pallas_call(
        _rowmix_kernel, grid=(cols // ct,),
        in_specs=[pl.BlockSpec(mh.shape, lambda i: (0, 0)), pl.BlockSpec(ml.shape, lambda i: (0, 0)),
                  pl.BlockSpec((rin, ct), lambda i: (0, i))],
        out_specs=pl.BlockSpec((mh.shape[0], ct), lambda i: (0, i)),
        out_shape=jax.ShapeDtypeStruct((mh.shape[0], cols), F32), compiler_params=_params(1), name="dft_rowmix",
    )(mh, ml, x)


def _spectral_kernel(x_ref, tw_ref, fh_ref, fl_ref, k_ref, o_ref, *, conv):
    xr, xi = x_ref[0], x_ref[1]
    reps = xr.shape[1] // HEAD_LANES
    tr = jnp.concatenate([tw_ref[0]] * reps, axis=1)
    ti = jnp.concatenate([tw_ref[1]] * reps, axis=1)
    frh, fih, frl, fil = fh_ref[0], fh_ref[1], fl_ref[0], fl_ref[1]

    def dft(ar, ai, conj):
        arh, arl = _split_bf16(ar)
        aih, ail = _split_bf16(ai)
        rr, ii = _dot_split(frh, frl, arh, arl), _dot_split(fih, fil, aih, ail)
        ri, ir = _dot_split(frh, frl, aih, ail), _dot_split(fih, fil, arh, arl)
        return (rr + ii, ri - ir) if conj else (rr - ii, ri + ir)

    yr, yi = dft(xr * tr - xi * ti, xr * ti + xi * tr, False)
    if not conv:
        o_ref[0] = yr * k_ref[...]
        o_ref[1] = yi * k_ref[...]
        return
    kr, ki = k_ref[0], k_ref[1]
    cr, ci = dft(yr * kr - yi * ki, yr * ki + yi * kr, True)
    o_ref[0] = cr * tr + ci * ti
    o_ref[1] = ci * tr - cr * ti


def _spectral(x, k, tabs, conv):
    _, n1, n2, c = x.shape
    slab = pl.BlockSpec((2, None, n2, c), lambda i: (0, i, 0, 0))
    kspec = slab if conv else pl.BlockSpec(k.shape, lambda i: (0, 0))
    return pl.pallas_call(
        functools.partial(_spectral_kernel, conv=conv), grid=(n1,),
        in_specs=[slab, pl.BlockSpec((None, 2, n2, HEAD_LANES), lambda i: (i, 0, 0, 0)),
                  pl.BlockSpec(tabs['f_hi'].shape, lambda i: (0, 0, 0)),
                  pl.BlockSpec(tabs['f_lo'].shape, lambda i: (0, 0, 0)), kspec],
        out_specs=slab, out_shape=jax.ShapeDtypeStruct(x.shape, F32), compiler_params=_params(1),
        name="dft_spectral_conv" if conv else "dft_spectral_filter",
    )(x, tabs['tw'], tabs['f_hi'], tabs['f_lo'], k)


def _filter_spectrum(k, tabs):
    n, c = k.shape
    n1, n2 = tabs['n1'], tabs['n2']
    if n1 > 1:
        x = _rowmix(tabs['m_real'], k.reshape(n1, n2 * c)).reshape(2, n1, n2, c)
    else:
        x = jnp.stack([k, jnp.zeros_like(k)]).reshape(2, 1, n2, c)
    return _spectral(x, jnp.full((1, c), 1.0 / n, F32), tabs, conv=False)


def _long_conv(v, kf, tabs):
    b, n, c = v.shape
    assert b == 2
    n1, n2 = tabs['n1'], tabs['n2']
    if n1 > 1:
        x = _rowmix(tabs['m_fwd'], v.reshape(n1, n2 * c)).reshape(2, n1, n2, c)
        y = _spectral(x, kf, tabs, conv=True)
        return _rowmix(tabs['m_inv'], y.reshape(2 * n1, n2 * c)).reshape(2, n, c)
    x = jnp.concatenate([v, jnp.zeros_like(v)], axis=1).reshape(2, 1, n2, c)
    return _spectral(x, kf, tabs, conv=True).reshape(2, n2, c)[:, :n]


def _gate_kernel(g_ref, y_ref, v_ref, s_ref, o_ref):
    o_ref[...] = g_ref[...] * (y_ref[...] + s_ref[...] * v_ref[...])


def _hyena_gate(gate, y, v, skip):
    b, n, c = v.shape
    tt = min(2048, n)
    tok = pl.BlockSpec((None, tt, c), lambda bi, ti: (bi, ti, 0))
    return pl.pallas_call(_gate_kernel, grid=(b, n // tt),
                          in_specs=[tok, tok, tok, pl.BlockSpec((1, c), lambda bi, ti: (0, 0))], out_specs=tok,
                          out_shape=jax.ShapeDtypeStruct(v.shape, F32), compiler_params=_params(2),
                          name="hyena_gate")(gate, y, v, skip)


def _hyena_branch(hy, p, tabs):
    n = hy.shape[1]
    x1, x2, v = _token_conv(_short_conv_kernel, hy, (p['hyena_short_w'], p['hyena_short_b'][None, :]),
                            (HYENA_WIDTH,) * 3, "hyena_short_conv")
    h = _hyena_filters(n, p).reshape(n, HYENA_ORDER, 2, HYENA_WIDTH)
    for o, gate in enumerate((x1, x2)):
        k = jnp.concatenate([h[:, o, 0], jnp.zeros((1, HYENA_WIDTH), F32), h[:-1, o, 1][::-1]], axis=0)
        v = _hyena_gate(gate, _long_conv(v, _filter_spectrum(k, tabs), tabs), v, p['hyena_skip'][o][None, :])
    return v


def _adaln_kernel(c_ref, w_ref, b_ref, o_ref):
    s = c_ref[...]
    s = s * jax.nn.sigmoid(s)
    o_ref[...] = _dot_split(*_split_bf16(s), *_split_bf16(w_ref[...])) + b_ref[...]


def _adaln(cond, w, b):
    d, width = w.shape
    ct = width // 6
    return pl.pallas_call(
        _adaln_kernel, grid=(6,),
        in_specs=[pl.BlockSpec(cond.shape, lambda i: (0, 0)), pl.BlockSpec((d, ct), lambda i: (0, i)),
                  pl.BlockSpec((1, ct), lambda i: (0, i))],
        out_specs=pl.BlockSpec((cond.shape[0], ct), lambda i: (0, i)),
        out_shape=jax.ShapeDtypeStruct((cond.shape[0], width), F32), compiler_params=_params(1), name="adaln",
    )(cond, w, b[None, :])


def _mod_rows(mod, norm_mix_g, norm_ffn_g, final_g, batch):
    sh1, sc1, g1, sh2, sc2, g2 = jnp.split(mod, 6, axis=-1)
    rows = jnp.stack([norm_mix_g * (1.0 + sc1), sh1, g1, norm_ffn_g * (1.0 + sc2), sh2, g2,
                      jnp.broadcast_to(final_g, g1.shape), jnp.zeros_like(g1)], axis=1)
    return jnp.broadcast_to(rows, (batch,) + rows.shape[1:])


def kernel(x, c, ctx, c_ctx, ada_w, ada_b, norm_mix_g, norm_ffn_g, w_in, diff_lambda, diff_subln_g, hyena_short_w, hyena_short_b, filt_w1, filt_b1, filt_freq, filt_w2, filt_b2, filt_w3, hyena_skip, conf_dw_w, conf_ln_g, conf_ln_b, mla_q_norm_g, mla_kv_norm_g, mla_w_uq, mla_w_ukv, w_branch, w_out, w_router, w_exp_in, w_exp_out, final_norm_g):
    depth = w_in.shape[0]
    batch, n_lat, d = x.shape
    n_ctx = ctx.shape[1]
    rope_lat = _rope_operands(n_lat, identity=False)
    rope_ctx = _rope_operands(n_ctx, identity=True)
    dft_lat, dft_ctx = _dft_tables(2 * n_lat), _dft_tables(2 * n_ctx)
    cond = jnp.concatenate([c, c_ctx[None], jnp.zeros((MOD_ROWS - batch - 1, d), F32)], axis=0)
    tile_lat, tile_ctx = min(512, n_lat), min(256, n_ctx)
    h_lat, h_ctx = x, ctx
    for l in range(depth):
        last = l == depth - 1
        p = dict(w_in=w_in[l], diff_subln_g=diff_subln_g[l], hyena_short_w=hyena_short_w[l],
                 hyena_short_b=hyena_short_b[l], filt_w1=filt_w1[l], filt_b1=filt_b1[l], filt_freq=filt_freq[l],
                 filt_w2=filt_w2[l], filt_b2=filt_b2[l], filt_w3=filt_w3[l], hyena_skip=hyena_skip[l],
                 conf_dw_w=conf_dw_w[l], conf_ln_g=conf_ln_g[l], conf_ln_b=conf_ln_b[l],
                 mla_q_norm_g=mla_q_norm_g[l], mla_kv_norm_g=mla_kv_norm_g[l], mla_w_uq=mla_w_uq[l],
                 mla_w_ukv=mla_w_ukv[l], w_branch=w_branch[l], w_out=w_out[l], w_router=w_router[l],
                 w_exp_in=w_exp_in[l], w_exp_out=w_exp_out[l])
        ada = _adaln(cond, ada_w[l], ada_b[l])
        mod_lat = _mod_rows(ada[:batch], norm_mix_g[l], norm_ffn_g[l], final_norm_g, batch)
        mod_ctx = _mod_rows(ada[batch:batch + 1], norm_mix_g[l], norm_ffn_g[l], final_norm_g, batch)
        lam_init = 0.8 - 0.6 * math.exp(-0.3 * l)
        lq1, lk1, lq2, lk2 = diff_lambda[l].astype(F32)
        lam = jnp.reshape(jnp.exp(jnp.sum(lq1 * lk1)) - jnp.exp(jnp.sum(lq2 * lk2)) + lam_init, (1,))
        w_inp, w_mrg = _inproj_weights(p), _merge_weights(p, lam_init)

        qdT_l, kd_l, vdT_l, qmT_l, km_l, vmT_l, hy_l, glu_l = _inproj(h_lat, mod_lat, w_inp, rope_lat, tile=tile_lat)
        qdT_c, kd_c, vdT_c, qmT_c, km_c, vmT_c, hy_c, glu_c = _inproj(h_ctx, mod_ctx, w_inp, rope_ctx, tile=tile_ctx)
        a_lat = _flash_attention(lam, qdT_l, kd_c, vdT_c, kd_l, vdT_l, n_maps=2, tq=min(256, n_lat))
        m_lat = _flash_attention(lam, qmT_l, km_c, vmT_c, km_l, vmT_l, n_maps=1, tq=min(512, n_lat))
        h_lat, u2_lat, lg_lat = _merge(h_lat, mod_lat, a_lat, _hyena_branch(hy_l, p, dft_lat), _conformer_branch(glu_l, p),
                                       m_lat, w_mrg, tile=min(256, n_lat))
        w_ei, w_eo = p['w_exp_in'].astype(BF16), p['w_exp_out'].astype(BF16)
        h_lat = _expert_choice_ffn(h_lat, mod_lat, u2_lat, lg_lat, w_ei, w_eo, final_norm=last)
        if not last:
            a_ctx = _flash_attention(lam, qdT_c, kd_c, vdT_c, None, None, n_maps=2, tq=n_ctx)
            m_ctx = _flash_attention(lam, qmT_c, km_c, vmT_c, None, None, n_maps=1, tq=n_ctx)
            h_ctx, u2_ctx, lg_ctx = _merge(h_ctx, mod_ctx, a_ctx, _hyena_branch(hy_c, p, dft_ctx),
                                           _conformer_branch(glu_c, p), m_ctx, w_mrg, tile=tile_ctx)
            h_ctx = _expert_choice_ffn(h_ctx, mod_ctx, u2_ctx, lg_ctx, w_ei, w_eo)
    return h_lat
```

```python
import functools
import math

import jax
import jax.numpy as jnp
from jax import lax
from jax.experimental import pallas as pl
from jax.experimental.pallas import tpu as pltpu

GRID_W = 64
ROPE_BASE = 10000.0
EPS = 1e-6

DIFF_HEADS = 4
DIFF_HEAD_DIM = 64
DIFF_V_DIM = 2 * DIFF_HEAD_DIM
HYENA_WIDTH = 256
HYENA_ORDER = 2
FILT_EMB = 33
DECAY_TARGET = 1e-2
FAST_DECAY = 0.3
SLOW_DECAY = 1.5
CONF_WIDTH = 256
MLA_HEADS = 4
MLA_Q_RANK = 256
MLA_KV_RANK = 128
MLA_NOPE = 64
MLA_ROPE = 32
MLA_V = 64
MLA_SCALE = (MLA_NOPE + MLA_ROPE) ** -0.5
N_BRANCH = 4
N_EXPERTS = 16
EC_CAPACITY = 2

DIFF_QK_W = DIFF_HEADS * 2 * DIFF_HEAD_DIM
DIFF_V_W = DIFF_HEADS * DIFF_V_DIM
HYENA_PROJ = (HYENA_ORDER + 1) * HYENA_WIDTH
CONF_PROJ = 2 * CONF_WIDTH
IN_SPLITS = (DIFF_QK_W, DIFF_QK_W, DIFF_V_W, HYENA_PROJ, CONF_PROJ, MLA_Q_RANK, MLA_KV_RANK, MLA_ROPE)
BRANCH_WIDTHS = (DIFF_V_W, HYENA_WIDTH, CONF_WIDTH, MLA_HEADS * MLA_V)

HEAD_LANES = 128
ATT_W = DIFF_HEADS * HEAD_LANES
LOG2E = 1.4426950408889634
VMEM_LIMIT_BYTES = 48 * 1024 * 1024
MOD_ROWS = 8

F32 = jnp.float32
BF16 = jnp.bfloat16
_NT = (((1,), (1,)), ((), ()))


def _params(n_axes, vmem=VMEM_LIMIT_BYTES):
    return pltpu.CompilerParams(dimension_semantics=("arbitrary",) * n_axes, vmem_limit_bytes=vmem)


def _flash_kernel(lam_ref, qT_ref, kc_ref, vcT_ref, *rest, n_maps, n_lat_chunks, tk):
    if n_lat_chunks:
        kl_ref, vlT_ref, o_ref, acc_ref, m_ref, l_ref, q2_ref, s_ref = rest
    else:
        o_ref, acc_ref, m_ref, l_ref, q2_ref = rest
    qT = qT_ref[...]
    tq = qT.shape[1]
    if n_maps == 2:
        row = lax.broadcasted_iota(jnp.int32, qT.shape, 0)
        zero = jnp.zeros_like(qT)
        q2_ref[:, :tq] = jnp.where(row < DIFF_HEAD_DIM, qT, zero)
        q2_ref[:, tq:] = jnp.where(row >= DIFF_HEAD_DIM, qT, zero)
    else:
        q2_ref[...] = qT
    m_ref[...] = jnp.full(m_ref.shape, -jnp.inf, F32)
    l_ref[...] = jnp.zeros(l_ref.shape, F32)
    acc_ref[...] = jnp.zeros(acc_ref.shape, F32)

    def scores(k):
        return jnp.dot(k, q2_ref[...], preferred_element_type=F32)

    def absorb(s, vT):
        m_prev = m_ref[...]
        m_new = jnp.maximum(m_prev, jnp.max(s, axis=0, keepdims=True))
        alpha = jnp.exp2(m_prev - m_new)
        p = jnp.exp2(s - m_new)
        l_ref[...] = alpha * l_ref[...] + jnp.sum(p, axis=0, keepdims=True)
        acc_ref[...] = alpha * acc_ref[...] + jnp.dot(vT, p.astype(BF16), preferred_element_type=F32)
        m_ref[...] = m_new

    def chunk(c):
        return pl.ds(c * tk if isinstance(c, int) else pl.multiple_of(c * tk, tk), tk)

    def keys(c):
        return kl_ref[chunk(c), :]

    def values_t(c):
        return vlT_ref[:, chunk(c)]

    absorb(scores(kc_ref[...]), vcT_ref[...])
    if n_lat_chunks:
        s_ref[0] = scores(keys(0))

        def pair(j, carry):
            c = 2 * j
            s_ref[1] = scores(keys(c + 1))
            absorb(s_ref[0], values_t(c))
            s_ref[0] = scores(keys(c + 2))
            absorb(s_ref[1], values_t(c + 1))
            return carry
        lax.fori_loop(0, n_lat_chunks // 2 - 1, pair, 0)
        s_ref[1] = scores(keys(n_lat_chunks - 1))
        absorb(s_ref[0], values_t(n_lat_chunks - 2))
        absorb(s_ref[1], values_t(n_lat_chunks - 1))
    o = acc_ref[...] / l_ref[...]
    if n_maps == 2:
        o = o[:, :tq] - lam_ref[0] * o[:, tq:]
        o = o * lax.rsqrt(jnp.mean(o * o, axis=0, keepdims=True) + EPS)
    o_ref[...] = o.T.astype(BF16)


def _flash_attention(lam, qT, kc, vcT, kl, vlT, *, n_maps, tq):
    b, _, s = qT.shape
    lc = kc.shape[1]
    r = n_maps * tq
    in_specs = [
        pl.BlockSpec(memory_space=pltpu.SMEM),
        pl.BlockSpec((None, HEAD_LANES, tq), lambda bi, hi, qi: (bi, hi, qi)),
        pl.BlockSpec((None, lc, HEAD_LANES), lambda bi, hi, qi: (bi, 0, hi)),
        pl.BlockSpec((None, HEAD_LANES, lc), lambda bi, hi, qi: (bi, hi, 0)),
    ]
    args = [lam, qT, kc, vcT]
    scratch = [pltpu.VMEM((HEAD_LANES, r), F32), pltpu.VMEM((1, r), F32), pltpu.VMEM((1, r), F32),
               pltpu.VMEM((HEAD_LANES, r), BF16)]
    n_lat_chunks, tk = 0, 0
    if kl is not None:
        sl = kl.shape[1]
        tk = _lat_chunk(sl)
        n_lat_chunks = sl // tk
        assert n_lat_chunks % 2 == 0 and n_lat_chunks * tk == sl
        scratch.append(pltpu.VMEM((2, tk, r), F32))
        in_specs += [
            pl.BlockSpec((None, sl, HEAD_LANES), lambda bi, hi, qi: (bi, 0, hi)),
            pl.BlockSpec((None, HEAD_LANES, sl), lambda bi, hi, qi: (bi, hi, 0)),
        ]
        args += [kl, vlT]
    return pl.pallas_call(
        functools.partial(_flash_kernel, n_maps=n_maps, n_lat_chunks=n_lat_chunks, tk=tk),
        grid=(b, DIFF_HEADS, s // tq),
        in_specs=in_specs,
        out_specs=pl.BlockSpec((None, tq, HEAD_LANES), lambda bi, hi, qi: (bi, qi, hi)),
        out_shape=jax.ShapeDtypeStruct((b, s, ATT_W), BF16),
        scratch_shapes=scratch,
        compiler_params=_params(3),
        name=f"flash_attention_{n_maps}map",
    )(*args)


def _lat_chunk(s):
    return min(1024, s // 2)


W_NAT_SPLITS = (DIFF_QK_W, HYENA_PROJ, CONF_PROJ, MLA_Q_RANK, MLA_KV_RANK, HEAD_LANES)


def _modulated_norm(h, a, shift):
    return h * lax.rsqrt(jnp.mean(h * h, axis=-1, keepdims=True) + EPS) * a + shift


def _rope_lanes(x, tab_ref, shift):
    return (x * tab_ref[0] + pltpu.roll(x, shift, 1) * tab_ref[1]
            + pltpu.roll(x, HEAD_LANES - shift, 1) * tab_ref[2])


def _inproj_kernel(h_ref, mod_ref, wnat_ref, wT_ref, wuqT_ref, wukvk_ref, wuvT_ref, gq_ref, gkv_ref,
                   ropeT_d_ref, rope_kd_ref, ropeT_m_ref, rope_km_ref,
                   qdT_ref, kd_ref, vdT_ref, qmT_ref, km_ref, vmT_ref, hy_ref, glu_ref):
    u = _modulated_norm(h_ref[...], mod_ref[0:1, :], mod_ref[1:2, :]).astype(BF16)
    z = jnp.dot(u, wnat_ref[...], preferred_element_type=F32)
    zT = lax.dot_general(wT_ref[...], u, _NT, preferred_element_type=F32)
    offs = [0]
    for w in W_NAT_SPLITS:
        offs.append(offs[-1] + w)
    dk, hy, cf, cq, ckv, krp = (z[:, offs[i]:offs[i + 1]] for i in range(len(W_NAT_SPLITS)))

    for hd in range(DIFF_HEADS):
        sl = slice(hd * HEAD_LANES, (hd + 1) * HEAD_LANES)
        kd_ref[:, sl] = _rope_lanes(dk[:, sl], rope_kd_ref, DIFF_HEAD_DIM // 2).astype(BF16)
    cos_d, sin_d = ropeT_d_ref[0], ropeT_d_ref[1]
    half = DIFF_HEAD_DIM // 2
    for g in range(2 * DIFF_HEADS):
        x1 = zT[g * DIFF_HEAD_DIM:g * DIFF_HEAD_DIM + half]
        x2 = zT[g * DIFF_HEAD_DIM + half:(g + 1) * DIFF_HEAD_DIM]
        qdT_ref[g * DIFF_HEAD_DIM:g * DIFF_HEAD_DIM + half, :] = (x1 * cos_d - x2 * sin_d).astype(BF16)
        qdT_ref[g * DIFF_HEAD_DIM + half:(g + 1) * DIFF_HEAD_DIM, :] = (x1 * sin_d + x2 * cos_d).astype(BF16)
    vdT_ref[...] = zT[DIFF_QK_W:].astype(BF16)

    hy_ref[...] = hy
    glu_ref[...] = cf[:, :CONF_WIDTH] * jax.nn.sigmoid(cf[:, CONF_WIDTH:])

    cqn = (cq * lax.rsqrt(jnp.mean(cq * cq, axis=-1, keepdims=True) + EPS) * gq_ref[...]).astype(BF16)
    ckvn = (ckv * lax.rsqrt(jnp.mean(ckv * ckv, axis=-1, keepdims=True) + EPS) * gkv_ref[...]).astype(BF16)
    qT = lax.dot_general(wuqT_ref[...], cqn, _NT, preferred_element_type=F32)
    cos_m, sin_m = ropeT_m_ref[0], ropeT_m_ref[1]
    hr = MLA_ROPE // 2
    for hd in range(MLA_HEADS):
        base = hd * HEAD_LANES
        r1 = base + MLA_NOPE
        x1, x2 = qT[r1:r1 + hr], qT[r1 + hr:r1 + 2 * hr]
        qmT_ref[base:r1, :] = qT[base:r1].astype(BF16)
        qmT_ref[r1:r1 + hr, :] = (x1 * cos_m - x2 * sin_m).astype(BF16)
        qmT_ref[r1 + hr:r1 + 2 * hr, :] = (x1 * sin_m + x2 * cos_m).astype(BF16)
        qmT_ref[r1 + 2 * hr:base + HEAD_LANES, :] = jnp.zeros((HEAD_LANES - MLA_NOPE - MLA_ROPE, qT.shape[1]), BF16)
    kn = jnp.dot(ckvn, wukvk_ref[...], preferred_element_type=F32)
    kr = _rope_lanes(krp, rope_km_ref, hr)
    for hd in range(MLA_HEADS):
        sl = slice(hd * HEAD_LANES, (hd + 1) * HEAD_LANES)
        km_ref[:, sl] = (kn[:, sl] + kr).astype(BF16)
    vmT_ref[...] = lax.dot_general(wuvT_ref[...], ckvn, _NT, preferred_element_type=F32).astype(BF16)


def _inproj(h, mod, wts, rope, *, tile):
    b, n, d = h.shape
    const2 = lambda bi, ti: (0, 0)
    tok = lambda w: pl.BlockSpec((None, tile, w), lambda bi, ti: (bi, ti, 0))
    tokT = lambda w: pl.BlockSpec((None, w, tile), lambda bi, ti: (bi, 0, ti))
    full = lambda a: pl.BlockSpec(a.shape, const2)
    in_specs = [tok(d), pl.BlockSpec((None, MOD_ROWS, d), lambda bi, ti: (bi, 0, 0))]
    in_specs += [full(wts[k]) for k in ('w_nat', 'w_T', 'w_uqT', 'w_ukvk', 'w_uvT', 'gq', 'gkv')]
    in_specs += [pl.BlockSpec((2, DIFF_HEAD_DIM // 2, tile), lambda bi, ti: (0, 0, ti)),
                 pl.BlockSpec((3, tile, HEAD_LANES), lambda bi, ti: (0, ti, 0)),
                 pl.BlockSpec((2, MLA_ROPE // 2, tile), lambda bi, ti: (0, 0, ti)),
                 pl.BlockSpec((3, tile, HEAD_LANES), lambda bi, ti: (0, ti, 0))]
    sds = jax.ShapeDtypeStruct
    out_shape = (sds((b, ATT_W, n), BF16), sds((b, n, ATT_W), BF16), sds((b, ATT_W, n), BF16),
                 sds((b, ATT_W, n), BF16), sds((b, n, ATT_W), BF16), sds((b, ATT_W, n), BF16),
                 sds((b, n, HYENA_PROJ), F32), sds((b, n, CONF_WIDTH), F32))
    out_specs = (tokT(ATT_W), tok(ATT_W), tokT(ATT_W), tokT(ATT_W), tok(ATT_W), tokT(ATT_W),
                 tok(HYENA_PROJ), tok(CONF_WIDTH))
    return pl.pallas_call(
        _inproj_kernel, grid=(b, n // tile), in_specs=in_specs, out_specs=out_specs, out_shape=out_shape,
        compiler_params=_params(2), name="inproj",
    )(h, mod, wts['w_nat'], wts['w_T'], wts['w_uqT'], wts['w_ukvk'], wts['w_uvT'], wts['gq'], wts['gkv'],
      rope['T_d'], rope['k_d'], rope['T_m'], rope['k_m'])


def _pad_heads(w, width):
    rows = w.shape[0]
    w = w.reshape(rows, MLA_HEADS, width)
    return jnp.pad(w, ((0, 0), (0, 0), (0, HEAD_LANES - width))).reshape(rows, ATT_W)


def _inproj_weights(p):
    d = p['w_in'].shape[0]
    dq, dk, dv, hy, cf, cq, ckv, kr = _split(p['w_in'][:, :sum(IN_SPLITS)], IN_SPLITS)
    krp = jnp.zeros((d, HEAD_LANES), F32).at[:, MLA_NOPE:MLA_NOPE + MLA_ROPE].set(kr)
    w_ukv = p['mla_w_ukv'].reshape(MLA_KV_RANK, MLA_HEADS, MLA_NOPE + MLA_V)
    return dict(
        w_nat=jnp.concatenate([dk, hy, cf, cq, ckv, krp], axis=1).astype(BF16),
        w_T=jnp.concatenate([dq * (DIFF_HEAD_DIM ** -0.5 * LOG2E), dv], axis=1).T.astype(BF16),
        w_uqT=_pad_heads(p['mla_w_uq'] * (MLA_SCALE * LOG2E), MLA_NOPE + MLA_ROPE).T.astype(BF16),
        w_ukvk=_pad_heads(w_ukv[:, :, :MLA_NOPE].reshape(MLA_KV_RANK, -1), MLA_NOPE).astype(BF16),
        w_uvT=_pad_heads(w_ukv[:, :, MLA_NOPE:].reshape(MLA_KV_RANK, -1), MLA_V).T.astype(BF16),
        gq=p['mla_q_norm_g'][None, :], gkv=p['mla_kv_norm_g'][None, :])


def _rope_tables(n_tok, rot_dim):
    rows = n_tok // GRID_W
    row = jnp.repeat(jnp.arange(rows), GRID_W).astype(F32)
    col = jnp.tile(jnp.arange(GRID_W), rows).astype(F32)
    nf = rot_dim // 4
    inv = ROPE_BASE ** (-jnp.arange(nf, dtype=F32) / nf)
    ang = jnp.concatenate([row[:, None] * inv, col[:, None] * inv], axis=-1)
    return jnp.cos(ang), jnp.sin(ang)


def _rope_operands(n_tok, identity):
    if identity:
        cos_d, sin_d = jnp.ones((n_tok, DIFF_HEAD_DIM // 2), F32), jnp.zeros((n_tok, DIFF_HEAD_DIM // 2), F32)
        cos_m, sin_m = jnp.ones((n_tok, MLA_ROPE // 2), F32), jnp.zeros((n_tok, MLA_ROPE // 2), F32)
    else:
        cos_d, sin_d = _rope_tables(n_tok, DIFF_HEAD_DIM)
        cos_m, sin_m = _rope_tables(n_tok, MLA_ROPE)
    z_d, z_m = jnp.zeros_like(sin_d), jnp.zeros_like(sin_m)
    two = lambda a, bb: jnp.tile(jnp.concatenate([a, bb], axis=1), (1, 2))
    lo, hi = jnp.zeros((n_tok, MLA_NOPE), F32), jnp.zeros((n_tok, HEAD_LANES - MLA_NOPE - MLA_ROPE), F32)
    mid = lambda a, bb: jnp.concatenate([lo, a, bb, hi], axis=1)
    return dict(T_d=jnp.stack([cos_d.T, sin_d.T]), T_m=jnp.stack([cos_m.T, sin_m.T]),
                k_d=jnp.stack([two(cos_d, cos_d), two(z_d, sin_d), two(-sin_d, z_d)]),
                k_m=jnp.stack([mid(cos_m, cos_m), mid(z_m, sin_m), mid(-sin_m, z_m)]))


def _merge_kernel(h_ref, mod_ref, a_ref, hy_ref, cf_ref, m_ref, wg_ref, wbd_ref, wbh_ref, wbc_ref, wbm_ref,
                  wo_ref, wrh_ref, wrl_ref, hn_ref, u2_ref, lg_ref):
    h = h_ref[...]
    d = h.shape[1]
    u = _modulated_norm(h, mod_ref[0:1, :], mod_ref[1:2, :]).astype(BF16)
    gates = jax.nn.sigmoid(jnp.dot(u, wg_ref[...], preferred_element_type=F32))
    dot = lambda x, w_ref: jnp.dot(x, w_ref[...], preferred_element_type=F32)
    acc = gates[:, :d] * dot(a_ref[...], wbd_ref)
    acc += gates[:, d:2 * d] * dot(hy_ref[...].astype(BF16), wbh_ref)
    acc += gates[:, 2 * d:3 * d] * dot(cf_ref[...].astype(BF16), wbc_ref)
    acc += gates[:, 3 * d:] * dot(m_ref[...], wbm_ref)
    hn = h + mod_ref[2:3, :] * dot(acc.astype(BF16), wo_ref)
    hn_ref[...] = hn
    u2 = _modulated_norm(hn, mod_ref[3:4, :], mod_ref[4:5, :])
    u2h = u2.astype(BF16)
    u2l = (u2 - u2h.astype(F32)).astype(BF16)
    u2_ref[...] = u2h
    lg_ref[...] = dot(u2h, wrh_ref) + (dot(u2l, wrh_ref) + dot(u2h, wrl_ref))


def _merge(h, mod, a, hyv, cfv, m, wts, *, tile):
    b, n, d = h.shape
    const2 = lambda bi, ti: (0, 0)
    tok = lambda w: pl.BlockSpec((None, tile, w), lambda bi, ti: (bi, ti, 0))
    names = ('w_gate', 'w_bd', 'w_bh', 'w_bc', 'w_bm', 'w_out', 'w_rh', 'w_rl')
    in_specs = [tok(d), pl.BlockSpec((None, MOD_ROWS, d), lambda bi, ti: (bi, 0, 0)),
                tok(ATT_W), tok(HYENA_WIDTH), tok(CONF_WIDTH), tok(ATT_W)]
    in_specs += [pl.BlockSpec(wts[k].shape, const2) for k in names]
    sds = jax.ShapeDtypeStruct
    return pl.pallas_call(
        _merge_kernel, grid=(b, n // tile), in_specs=in_specs,
        out_specs=(tok(d), tok(d), tok(HEAD_LANES)),
        out_shape=(sds((b, n, d), F32), sds((b, n, d), BF16), sds((b, n, HEAD_LANES), F32)),
        compiler_params=_params(2, 56 * 1024 * 1024), name="merge",
    )(h, mod, a, hyv, cfv, m, *[wts[k] for k in names])


def _merge_weights(p, lam_init):
    d = p['w_out'].shape[0]
    wb_d, wb_h, wb_c, wb_m = (w.T for w in _split(p['w_branch'].T, BRANCH_WIDTHS))
    wb_d = wb_d * (jnp.tile(p['diff_subln_g'], DIFF_HEADS) * (1.0 - lam_init))[:, None]
    wb_m = jnp.pad(wb_m.reshape(MLA_HEADS, MLA_V, d), ((0, 0), (0, HEAD_LANES - MLA_V), (0, 0))).reshape(ATT_W, d)
    w_r = jnp.pad(p['w_router'], ((0, 0), (0, HEAD_LANES - N_EXPERTS)))
    w_rh = w_r.astype(BF16)
    return dict(w_gate=p['w_in'][:, sum(IN_SPLITS):].astype(BF16), w_bd=wb_d.astype(BF16), w_bh=wb_h.astype(BF16),
                w_bc=wb_c.astype(BF16), w_bm=wb_m.astype(BF16), w_out=p['w_out'].astype(BF16),
                w_rh=w_rh, w_rl=(w_r - w_rh.astype(F32)).astype(BF16))


SUB_TOKENS = 256
GATHER_WINDOW = 272
COMBINE_WINDOW = 384
ROUTE_MIN_ROWS = 8


def _excl_scan(x, lane, row):
    inc = x
    s = 1
    while s < HEAD_LANES:
        inc = inc + jnp.where(lane >= s, pltpu.roll(inc, s, 2), 0.0)
        s *= 2
    tot = jnp.sum(x, axis=2, keepdims=True) + jnp.zeros_like(x)
    off = tot
    s = 1
    while s < x.shape[1]:
        off = off + jnp.where(row >= s, pltpu.roll(off, s, 1), 0.0)
        s *= 2
    return inc - x + (off - tot)


def _route_kernel(lg_ref, pos_ref, *, n_valid, cap):
    lg = lg_ref[...]
    shape = lg.shape
    lane = lax.broadcasted_iota(jnp.int32, shape, 2)
    row = lax.broadcasted_iota(jnp.int32, shape, 1)
    e = jnp.exp(lg - jnp.max(lg, axis=0, keepdims=True))
    aff = e / jnp.sum(e, axis=0, keepdims=True)
    bits = jnp.where(row * HEAD_LANES + lane < n_valid, pltpu.bitcast(aff, jnp.int32), -1)

    def count(mask):
        c = jnp.sum(jnp.where(mask, 1.0, 0.0), axis=2, keepdims=True)
        return jnp.sum(c, axis=1, keepdims=True)

    def step(i, thr):
        cand = thr | (jnp.int32(1) << (30 - i))
        return jnp.where(count(bits >= cand) >= cap, cand, thr)
    thr = lax.fori_loop(0, 31, step, jnp.zeros((shape[0], 1, 1), jnp.int32))
    gt = bits > thr
    eq = bits == thr
    need = cap - count(gt)
    tie_rank = _excl_scan(jnp.where(eq, 1.0, 0.0), lane, row)
    sel = gt | (eq & (tie_rank < need))
    pos = _excl_scan(jnp.where(sel, 1.0, 0.0), lane, row)
    pos_ref[...] = jnp.where(sel, pos.astype(jnp.int32), -1)


def _route(logits, cap):
    b, n, _ = logits.shape
    rows = max(ROUTE_MIN_ROWS, n // HEAD_LANES)
    lg = jnp.swapaxes(logits[..., :N_EXPERTS], 1, 2)
    lg = jnp.pad(lg, ((0, 0), (0, 0), (0, rows * HEAD_LANES - n))).reshape(b, N_EXPERTS, rows, HEAD_LANES)
    spec = pl.BlockSpec((None, N_EXPERTS, rows, HEAD_LANES), lambda bi: (bi, 0, 0, 0))
    pos = pl.pallas_call(
        functools.partial(_route_kernel, n_valid=n, cap=cap), grid=(b,), in_specs=[spec], out_specs=spec,
        out_shape=jax.ShapeDtypeStruct(lg.shape, jnp.int32), compiler_params=_params(1), name="route",
    )(lg)
    return pos.reshape(b, N_EXPERTS, rows * HEAD_LANES)[..., :n]


def _experts_kernel(base_ref, u_ref, lg_ref, pos_ref, win_ref, wout_ref, ye_ref, xe_ref, *, n_sub, cap):
    bi, ei, kb = pl.program_id(0), pl.program_id(1), pl.program_id(2)
    d = u_ref.shape[1]

    @pl.when(kb == 0)
    def _():
        xe_ref[...] = jnp.zeros(xe_ref.shape, F32)

    slot = lax.broadcasted_iota(jnp.int32, (GATHER_WINDOW, SUB_TOKENS), 0)
    lane_e = lax.broadcasted_iota(jnp.int32, (SUB_TOKENS, N_EXPERTS), 1)
    for j in range(n_sub):
        tok = slice(j * SUB_TOKENS, (j + 1) * SUB_TOKENS)
        base = pl.multiple_of(base_ref[bi, ei, kb * n_sub + j], 16)
        rel = pos_ref[:, tok] - base
        onehot = jnp.where(slot == rel, 1.0, 0.0).astype(BF16)
        lg = lg_ref[tok, :][:, :N_EXPERTS]
        ex = jnp.exp(lg - jnp.max(lg, axis=1, keepdims=True))
        aff = ex / jnp.sum(ex, axis=1, keepdims=True)
        gate = jnp.sum(jnp.where(lane_e == ei, aff, 0.0), axis=1, keepdims=True)
        g_hi, g_lo = _split_bf16(jnp.broadcast_to(gate, (SUB_TOKENS, HEAD_LANES)))
        rows = pl.ds(base, GATHER_WINDOW)
        xe_ref[rows, :d] += jnp.dot(onehot, u_ref[tok, :], preferred_element_type=F32)
        xe_ref[rows, d:] += jnp.dot(onehot, jnp.concatenate([g_hi, g_lo], axis=1), preferred_element_type=F32)

    @pl.when(kb == pl.num_programs(2) - 1)
    def _():
        f = wout_ref.shape[0]
        step = min(512, cap)
        for r0 in range(0, cap, step):
            x = xe_ref[r0:r0 + step, :d].astype(BF16)
            gate = xe_ref[r0:r0 + step, d:d + HEAD_LANES] + xe_ref[r0:r0 + step, d + HEAD_LANES:]
            hgu = jnp.dot(x, win_ref[...], preferred_element_type=F32)
            act = (jax.nn.silu(hgu[:, :f]) * hgu[:, f:]).astype(BF16)
            y = jnp.dot(act, wout_ref[...], preferred_element_type=F32)
            ye_ref[r0:r0 + step, :] = (y * jnp.concatenate([gate] * (d // HEAD_LANES), axis=1)).astype(BF16)
        ye_ref[cap:, :] = jnp.zeros((ye_ref.shape[0] - cap, ye_ref.shape[1]), BF16)


def _combine_kernel(base_ref, h_ref, mod_ref, posn_ref, ye_ref, hn_ref, *, n_sub, final_norm):
    bi, kb, ei = pl.program_id(0), pl.program_id(1), pl.program_id(2)

    @pl.when(ei == 0)
    def _():
        hn_ref[...] = jnp.zeros(hn_ref.shape, F32)

    slot = lax.broadcasted_iota(jnp.int32, (SUB_TOKENS, COMBINE_WINDOW), 1)
    lane_e = lax.broadcasted_iota(jnp.int32, (SUB_TOKENS, N_EXPERTS), 1)
    for j in range(n_sub):
        tok = slice(j * SUB_TOKENS, (j + 1) * SUB_TOKENS)
        base = pl.multiple_of(base_ref[bi, ei, kb * n_sub + j], 16)
        rel = jnp.sum(jnp.where(lane_e == ei, posn_ref[tok, :], 0), axis=1, keepdims=True) - base
        onehot = jnp.where(slot == rel, 1.0, 0.0).astype(BF16)
        ye = ye_ref[pl.ds(base, COMBINE_WINDOW), :]
        hn_ref[tok, :] += jnp.dot(onehot, ye, preferred_element_type=F32)

    @pl.when(ei == pl.num_programs(2) - 1)
    def _():
        hn = h_ref[...] + mod_ref[5:6, :] * hn_ref[...]
        if final_norm:
            hn = hn * lax.rsqrt(jnp.mean(hn * hn, axis=-1, keepdims=True) + EPS) * mod_ref[6:7, :]
        hn_ref[...] = hn


def _expert_choice_ffn(h, mod, u2, logits, w_exp_in, w_exp_out, final_norm=False):
    b, n, d = u2.shape
    cap = max(1, EC_CAPACITY * n // N_EXPERTS)
    n_sub = min(4, n // SUB_TOKENS)
    big = n_sub * SUB_TOKENS
    n_big = n // big
    capp = cap + COMBINE_WINDOW
    pos = _route(logits, cap)
    cnt = jnp.sum((pos >= 0).reshape(b, N_EXPERTS, n // SUB_TOKENS, SUB_TOKENS), axis=-1)
    base = (jnp.cumsum(cnt, axis=-1) - cnt) // 16 * 16
    base = base.astype(jnp.int32)
    f = w_exp_out.shape[1]
    ye = pl.pallas_call(
        functools.partial(_experts_kernel, n_sub=n_sub, cap=cap),
        grid_spec=pltpu.PrefetchScalarGridSpec(
            num_scalar_prefetch=1, grid=(b, N_EXPERTS, n_big),
            in_specs=[pl.BlockSpec((None, big, d), lambda bi, ei, kb, base_r: (bi, kb, 0)),
                      pl.BlockSpec((None, big, HEAD_LANES), lambda bi, ei, kb, base_r: (bi, kb, 0)),
                      pl.BlockSpec((None, None, 1, big), lambda bi, ei, kb, base_r: (bi, ei, 0, kb)),
                      pl.BlockSpec((None, d, 2 * f), lambda bi, ei, kb, base_r: (ei, 0, 0)),
                      pl.BlockSpec((None, f, d), lambda bi, ei, kb, base_r: (ei, 0, 0))],
            out_specs=pl.BlockSpec((None, None, capp, d), lambda bi, ei, kb, base_r: (bi, ei, 0, 0)),
            scratch_shapes=[pltpu.VMEM((capp, d + 2 * HEAD_LANES), F32)]),
        out_shape=jax.ShapeDtypeStruct((b, N_EXPERTS, capp, d), BF16),
        compiler_params=_params(3, 56 * 1024 * 1024), name="experts",
    )(base, u2, logits, pos.reshape(b, N_EXPERTS, 1, n), w_exp_in, w_exp_out)
    posn = jnp.swapaxes(pos, 1, 2)
    return pl.pallas_call(
        functools.partial(_combine_kernel, n_sub=n_sub, final_norm=final_norm),
        grid_spec=pltpu.PrefetchScalarGridSpec(
            num_scalar_prefetch=1, grid=(b, n_big, N_EXPERTS),
            in_specs=[pl.BlockSpec((None, big, d), lambda bi, kb, ei, base_r: (bi, kb, 0)),
                      pl.BlockSpec((None, MOD_ROWS, d), lambda bi, kb, ei, base_r: (bi, 0, 0)),
                      pl.BlockSpec((None, big, N_EXPERTS), lambda bi, kb, ei, base_r: (bi, kb, 0)),
                      pl.BlockSpec((None, None, capp, d), lambda bi, kb, ei, base_r: (bi, ei, 0, 0))],
            out_specs=pl.BlockSpec((None, big, d), lambda bi, kb, ei, base_r: (bi, kb, 0))),
        out_shape=jax.ShapeDtypeStruct((b, n, d), F32),
        compiler_params=_params(3), name="combine",
    )(base, h, mod, posn, ye)


def _split(z, sizes):
    out, start = [], 0
    for s in sizes:
        out.append(z[..., start:start + s])
        start += s
    return out


HALO = 16


def _fill_ext(ext_ref, x_ref, prev_ref, next_ref):
    ti, nt = pl.program_id(1), pl.num_programs(1)
    tt = x_ref.shape[0]
    ext_ref[0:HALO, :] = jnp.where(ti > 0, prev_ref[...], 0.0)
    ext_ref[HALO:HALO + tt, :] = x_ref[...]
    ext_ref[HALO + tt:, :] = jnp.where(ti < nt - 1, next_ref[...], 0.0)


def _taps(ext_ref, w_ref, tt):
    k = w_ref.shape[0]
    acc = None
    for j in range(k):
        start = HALO - k // 2 + j
        term = w_ref[j:j + 1, :] * ext_ref[start:start + tt, :]
        acc = term if acc is None else acc + term
    return acc


def _short_conv_kernel(x_ref, prev_ref, next_ref, w_ref, b_ref, x1_ref, x2_ref, v_ref, ext_ref):
    _fill_ext(ext_ref, x_ref, prev_ref, next_ref)
    y = _taps(ext_ref, w_ref, x_ref.shape[0]) + b_ref[...]
    x1_ref[...] = y[:, :HYENA_WIDTH]
    x2_ref[...] = y[:, HYENA_WIDTH:2 * HYENA_WIDTH]
    v_ref[...] = y[:, 2 * HYENA_WIDTH:]


def _conformer_kernel(x_ref, prev_ref, next_ref, w_ref, g_ref, b_ref, o_ref, ext_ref):
    _fill_ext(ext_ref, x_ref, prev_ref, next_ref)
    u = _taps(ext_ref, w_ref, x_ref.shape[0])
    mu = jnp.mean(u, axis=-1, keepdims=True)
    var = jnp.mean(jnp.square(u - mu), axis=-1, keepdims=True)
    y = (u - mu) * lax.rsqrt(var + EPS) * g_ref[...] + b_ref[...]
    o_ref[...] = y * jax.nn.sigmoid(y)


def _token_conv(body, x, consts, out_widths, name):
    b, n, w = x.shape
    tt = min(1024, n)
    per = tt // HALO
    last = n // HALO - 1
    in_specs = [pl.BlockSpec((None, tt, w), lambda bi, ti: (bi, ti, 0)),
                pl.BlockSpec((None, HALO, w), lambda bi, ti: (bi, jnp.maximum(ti * per - 1, 0), 0)),
                pl.BlockSpec((None, HALO, w), lambda bi, ti: (bi, jnp.minimum((ti + 1) * per, last), 0))]
    in_specs += [pl.BlockSpec(cst.shape, lambda bi, ti: (0, 0)) for cst in consts]
    outs = tuple(jax.ShapeDtypeStruct((b, n, ow), F32) for ow in out_widths)
    out_specs = tuple(pl.BlockSpec((None, tt, ow), lambda bi, ti: (bi, ti, 0)) for ow in out_widths)
    return pl.pallas_call(body, grid=(b, n // tt), in_specs=in_specs, out_specs=out_specs, out_shape=outs,
                          scratch_shapes=[pltpu.VMEM((tt + 2 * HALO, w), F32)],
                          compiler_params=_params(2), name=name)(x, x, x, *consts)


def _conformer_branch(glu, p):
    return _token_conv(_conformer_kernel, glu, (p['conf_dw_w'], p['conf_ln_g'][None, :], p['conf_ln_b'][None, :]),
                       (CONF_WIDTH,), "conformer")[0]


FILT_LANES = 128
DFT_SHORT = 256


def _split_bf16(x):
    hi = x.astype(BF16)
    return hi, (x - hi.astype(F32)).astype(BF16)


def _dot_split(ah, al, bh, bl):
    dot = lambda u, v: jnp.dot(u, v, preferred_element_type=F32)
    return dot(ah, bh) + (dot(al, bh) + dot(ah, bl))


def _filter_kernel(z_ref, w1h, w1l, b1, f1, w2h, w2l, b2, f2, w3h, w3l, dl_ref, h_ref, asum_ref, *,
                   tiles_per_dir):
    z = z_ref[...]
    hid = jnp.sin(f1[...] * (_dot_split(*_split_bf16(z), w1h[...], w1l[...]) + b1[...]))
    hid = jnp.sin(f2[...] * (_dot_split(*_split_bf16(hid), w2h[...], w2l[...]) + b2[...]))
    h = _dot_split(*_split_bf16(hid), w3h[...], w3l[...])
    h = h * jnp.exp(-z[:, 0:1] * dl_ref[...])
    h_ref[...] = h

    @pl.when(pl.program_id(0) % tiles_per_dir == 0)
    def _():
        asum_ref[...] = jnp.zeros(asum_ref.shape, F32)
    asum_ref[...] += jnp.sum(jnp.abs(h), axis=0, keepdims=True)


def _normalise_kernel(h_ref, asum_ref, *o_refs, n):
    tt = h_ref.shape[0]
    row = pl.program_id(0) * tt + lax.broadcasted_iota(jnp.int32, h_ref.shape, 0)
    k = jnp.where(row == n, 0.0, h_ref[...] / asum_ref[...])
    for o, o_ref in enumerate(o_refs):
        o_ref[...] = k[:, o * HYENA_WIDTH:(o + 1) * HYENA_WIDTH]


def _hyena_taps(n, p):
    t = jnp.linspace(0.0, 1.0, n, dtype=F32)[:, None]
    bands = (FILT_EMB - 1) // 2
    w = (2.0 * math.pi / n) * jnp.arange(n, dtype=F32)[:, None]
    f = jnp.linspace(1e-4, bands - 1, bands, dtype=F32)[None, :]
    z = jnp.concatenate([t, jnp.cos(f * w), -jnp.sin(f * w), jnp.zeros((n, FILT_LANES - FILT_EMB), F32)], axis=-1)
    hid = p['filt_w2'].shape[0]
    padc = lambda a: jnp.pad(a, ((0, 0), (0, FILT_LANES - a.shape[1])))
    padr = lambda a: jnp.pad(a, ((0, FILT_LANES - a.shape[0]), (0, 0)))
    w1, w2, w3 = padc(padr(p['filt_w1'])), padc(padr(p['filt_w2'])), padr(p['filt_w3'])
    b1, b2 = padc(p['filt_b1'][None, :]), padc(p['filt_b2'][None, :])
    f1, f2 = padc(p['filt_freq'][0][None, :]), padc(p['filt_freq'][1][None, :])
    deltas = jnp.abs(jnp.linspace(math.log(DECAY_TARGET) / SLOW_DECAY, math.log(DECAY_TARGET) / FAST_DECAY,
                                  HYENA_WIDTH, dtype=F32))
    width = HYENA_ORDER * HYENA_WIDTH
    w3 = w3.reshape(FILT_LANES, HYENA_ORDER, 2, HYENA_WIDTH).transpose(2, 0, 1, 3).reshape(2, FILT_LANES, width)
    dl = jnp.tile(deltas, HYENA_ORDER)[None, :]
    z2 = jnp.concatenate([z, z[::-1]], axis=0)
    w3h, w3l = _split_bf16(w3)
    tt = min(1024, n)
    tiles_per_dir = n // tt
    cspec = lambda a: pl.BlockSpec(a.shape, lambda i: (0, 0))
    dirspec = lambda rows: pl.BlockSpec((None, rows, width), lambda i: (i // tiles_per_dir, 0, 0))
    tile = lambda w: pl.BlockSpec((tt, w), lambda i: (i, 0))
    small = [*_split_bf16(w1), b1, f1, *_split_bf16(w2), b2, f2]
    h_raw, asum = pl.pallas_call(
        functools.partial(_filter_kernel, tiles_per_dir=tiles_per_dir), grid=(2 * tiles_per_dir,),
        in_specs=[tile(FILT_LANES)] + [cspec(a) for a in small] + [dirspec(FILT_LANES), dirspec(FILT_LANES), cspec(dl)],
        out_specs=(tile(width), dirspec(1)),
        out_shape=(jax.ShapeDtypeStruct((2 * n, width), F32), jax.ShapeDtypeStruct((2, 1, width), F32)),
        compiler_params=_params(1), name="hyena_filter_mlp")(z2, *small, w3h, w3l, dl)
    return pl.pallas_call(
        functools.partial(_normalise_kernel, n=n), grid=(2 * tiles_per_dir,),
        in_specs=[tile(width), dirspec(1)],
        out_specs=tuple(tile(HYENA_WIDTH) for _ in range(HYENA_ORDER)),
        out_shape=tuple(jax.ShapeDtypeStruct((2 * n, HYENA_WIDTH), F32) for _ in range(HYENA_ORDER)),
        compiler_params=_params(1), name="hyena_filter_norm",
    )(h_raw, asum)


def _dft_tables(n):
    n2 = DFT_SHORT if n >= 4 * DFT_SHORT else n
    n1 = n // n2

    def cis(idx):
        ang = (-2.0 * math.pi / n) * idx.astype(F32)
        return jnp.cos(ang), jnp.sin(ang)
    k2 = jnp.arange(n2)
    fr, fi = cis((k2[:, None] * k2[None, :]) % n2 * n1)
    tabs = dict(n1=n1, n2=n2)
    tabs['f_hi'], tabs['f_lo'] = _split_bf16(jnp.stack([fr, fi]))
    k1 = jnp.arange(n1)
    tr, ti = cis(k1[:, None] * k2[None, :])
    tabs['tw'] = jnp.broadcast_to(jnp.stack([tr, ti], axis=1)[..., None], (n1, 2, n2, HEAD_LANES))
    if n1 > 1:
        gr, gi = cis((k1[:, None] * k1[None, :]) % n1 * n2)
        half = n1 // 2
        grh, gih = gr[:, :half], gi[:, :half]
        tabs['m_fwd'] = _split_bf16(jnp.block([[grh, -gih], [gih, grh]]))
        tabs['m_real'] = _split_bf16(jnp.concatenate([gr, gi], axis=0))
        tabs['m_inv'] = _split_bf16(jnp.block([[grh.T, gih.T], [-gih.T, grh.T]]))
    return tabs


def _rowmix_kernel(mh_ref, ml_ref, x_ref, o_ref):
    o_ref[...] = _dot_split(mh_ref[...], ml_ref[...], *_split_bf16(x_ref[...]))


def _rowmix(m, x):
    mh, ml = m
    rin, cols = x.shape
    ct = min(2048, cols)
    return pl.pallas_call(
        _rowmix_kernel, grid=(cols // ct,),
        in_specs=[pl.BlockSpec(mh.shape, lambda i: (0, 0)), pl.BlockSpec(ml.shape, lambda i: (0, 0)),
                  pl.BlockSpec((rin, ct), lambda i: (0, i))],
        out_specs=pl.BlockSpec((mh.shape[0], ct), lambda i: (0, i)),
        out_shape=jax.ShapeDtypeStruct((mh.shape[0], cols), F32), compiler_params=_params(1), name="dft_rowmix",
    )(mh, ml, x)


def _spectral_kernel(x_ref, tw_ref, fh_ref, fl_ref, k_ref, o_ref, *, conv):
    xr, xi = x_ref[0], x_ref[1]
    reps = xr.shape[1] // HEAD_LANES
    tr = jnp.concatenate([tw_ref[0]] * reps, axis=1)
    ti = jnp.concatenate([tw_ref[1]] * reps, axis=1)
    frh, fih, frl, fil = fh_ref[0], fh_ref[1], fl_ref[0], fl_ref[1]

    def dft(ar, ai, conj):
        arh, arl = _split_bf16(ar)
        aih, ail = _split_bf16(ai)
        rr, ii = _dot_split(frh, frl, arh, arl), _dot_split(fih, fil, aih, ail)
        ri, ir = _dot_split(frh, frl, aih, ail), _dot_split(fih, fil, arh, arl)
        return (rr + ii, ri - ir) if conj else (rr - ii, ri + ir)

    yr, yi = dft(xr * tr - xi * ti, xr * ti + xi * tr, False)
    if not conv:
        o_ref[0] = yr * k_ref[...]
        o_ref[1] = yi * k_ref[...]
        return
    kr, ki = k_ref[0], k_ref[1]
    cr, ci = dft(yr * kr - yi * ki, yr * ki + yi * kr, True)
    o_ref[0] = cr * tr + ci * ti
    o_ref[1] = ci * tr - cr * ti


def _spectral(x, k, tabs, conv):
    _, n1, n2, c = x.shape
    slab = pl.BlockSpec((2, None, n2, c), lambda i: (0, i, 0, 0))
    kspec = slab if conv else pl.BlockSpec(k.shape, lambda i: (0, 0))
    return pl.pallas_call(
        functools.partial(_spectral_kernel, conv=conv), grid=(n1,),
        in_specs=[slab, pl.BlockSpec((None, 2, n2, HEAD_LANES), lambda i: (i, 0, 0, 0)),
                  pl.BlockSpec(tabs['f_hi'].shape, lambda i: (0, 0, 0)),
                  pl.BlockSpec(tabs['f_lo'].shape, lambda i: (0, 0, 0)), kspec],
        out_specs=slab, out_shape=jax.ShapeDtypeStruct(x.shape, F32), compiler_params=_params(1),
        name="dft_spectral_conv" if conv else "dft_spectral_filter",
    )(x, tabs['tw'], tabs['f_hi'], tabs['f_lo'], k)


def _filter_spectrum(k, tabs):
    n, c = k.shape
    n1, n2 = tabs['n1'], tabs['n2']
    if n1 > 1:
        x = _rowmix(tabs['m_real'], k.reshape(n1, n2 * c)).reshape(2, n1, n2, c)
    else:
        x = jnp.stack([k, jnp.zeros_like(k)]).reshape(2, 1, n2, c)
    return _spectral(x, jnp.full((1, c), 1.0 / n, F32), tabs, conv=False)


def _long_conv(v, kf, tabs):
    b, n, c = v.shape
    assert b == 2
    n1, n2 = tabs['n1'], tabs['n2']
    if n1 > 1:
        x = _rowmix(tabs['m_fwd'], v.reshape(n1, n2 * c)).reshape(2, n1, n2, c)
        y = _spectral(x, kf, tabs, conv=True)
        return _rowmix(tabs['m_inv'], y.reshape(2 * n1, n2 * c)).reshape(2, n, c)
    x = jnp.concatenate([v, jnp.zeros_like(v)], axis=1).reshape(2, 1, n2, c)
    return _spectral(x, kf, tabs, conv=True).reshape(2, n2, c)[:, :n]


def _gate_kernel(g_ref, y_ref, v_ref, s_ref, o_ref):
    o_ref[...] = g_ref[...] * (y_ref[...] + s_ref[...] * v_ref[...])


def _hyena_gate(gate, y, v, skip):
    b, n, c = v.shape
    tt = min(2048, n)
    tok = pl.BlockSpec((None, tt, c), lambda bi, ti: (bi, ti, 0))
    return pl.pallas_call(_gate_kernel, grid=(b, n // tt),
                          in_specs=[tok, tok, tok, pl.BlockSpec((1, c), lambda bi, ti: (0, 0))], out_specs=tok,
                          out_shape=jax.ShapeDtypeStruct(v.shape, F32), compiler_params=_params(2),
                          name="hyena_gate")(gate, y, v, skip)


def _hyena_branch(hy, p, tabs):
    n = hy.shape[1]
    x1, x2, v = _token_conv(_short_conv_kernel, hy, (p['hyena_short_w'], p['hyena_short_b'][None, :]),
                            (HYENA_WIDTH,) * 3, "hyena_short_conv")
    taps = _hyena_taps(n, p)
    for o, gate in enumerate((x1, x2)):
        v = _hyena_gate(gate, _long_conv(v, _filter_spectrum(taps[o], tabs), tabs), v, p['hyena_skip'][o][None, :])
    return v


def _adaln_kernel(c_ref, w_ref, b_ref, o_ref):
    s = c_ref[...]
    s = s * jax.nn.sigmoid(s)
    o_ref[...] = _dot_split(*_split_bf16(s), *_split_bf16(w_ref[...])) + b_ref[...]


def _adaln(cond, w, b):
    d, width = w.shape
    ct = width // 6
    return pl.pallas_call(
        _adaln_kernel, grid=(6,),
        in_specs=[pl.BlockSpec(cond.shape, lambda i: (0, 0)), pl.BlockSpec((d, ct), lambda i: (0, i)),
                  pl.BlockSpec((1, ct), lambda i: (0, i))],
        out_specs=pl.BlockSpec((cond.shape[0], ct), lambda i: (0, i)),
        out_shape=jax.ShapeDtypeStruct((cond.shape[0], width), F32), compiler_params=_params(1), name="adaln",
    )(cond, w, b[None, :])


def _mod_rows(mod, norm_mix_g, norm_ffn_g, final_g, batch):
    sh1, sc1, g1, sh2, sc2, g2 = jnp.split(mod, 6, axis=-1)
    rows = jnp.stack([norm_mix_g * (1.0 + sc1), sh1, g1, norm_ffn_g * (1.0 + sc2), sh2, g2,
                      jnp.broadcast_to(final_g, g1.shape), jnp.zeros_like(g1)], axis=1)
    return jnp.broadcast_to(rows, (batch,) + rows.shape[1:])


def kernel(x, c, ctx, c_ctx, ada_w, ada_b, norm_mix_g, norm_ffn_g, w_in, diff_lambda, diff_subln_g, hyena_short_w, hyena_short_b, filt_w1, filt_b1, filt_freq, filt_w2, filt_b2, filt_w3, hyena_skip, conf_dw_w, conf_ln_g, conf_ln_b, mla_q_norm_g, mla_kv_norm_g, mla_w_uq, mla_w_ukv, w_branch, w_out, w_router, w_exp_in, w_exp_out, final_norm_g):
    depth = w_in.shape[0]
    batch, n_lat, d = x.shape
    n_ctx = ctx.shape[1]
    rope_lat = _rope_operands(n_lat, identity=False)
    rope_ctx = _rope_operands(n_ctx, identity=True)
    dft_lat, dft_ctx = _dft_tables(2 * n_lat), _dft_tables(2 * n_ctx)
    cond = jnp.concatenate([c, c_ctx[None], jnp.zeros((MOD_ROWS - batch - 1, d), F32)], axis=0)
    tile_lat, tile_ctx = min(512, n_lat), min(256, n_ctx)
    h_lat, h_ctx = x, ctx
    for l in range(depth):
        last = l == depth - 1
        p = dict(w_in=w_in[l], diff_subln_g=diff_subln_g[l], hyena_short_w=hyena_short_w[l],
                 hyena_short_b=hyena_short_b[l], filt_w1=filt_w1[l], filt_b1=filt_b1[l], filt_freq=filt_freq[l],
                 filt_w2=filt_w2[l], filt_b2=filt_b2[l], filt_w3=filt_w3[l], hyena_skip=hyena_skip[l],
                 conf_dw_w=conf_dw_w[l], conf_ln_g=conf_ln_g[l], conf_ln_b=conf_ln_b[l],
                 mla_q_norm_g=mla_q_norm_g[l], mla_kv_norm_g=mla_kv_norm_g[l], mla_w_uq=mla_w_uq[l],
                 mla_w_ukv=mla_w_ukv[l], w_branch=w_branch[l], w_out=w_out[l], w_router=w_router[l],
                 w_exp_in=w_exp_in[l], w_exp_out=w_exp_out[l])
        ada = _adaln(cond, ada_w[l], ada_b[l])
        mod_lat = _mod_rows(ada[:batch], norm_mix_g[l], norm_ffn_g[l], final_norm_g, batch)
        mod_ctx = _mod_rows(ada[batch:batch + 1], norm_mix_g[l], norm_ffn_g[l], final_norm_g, batch)
        lam_init = 0.8 - 0.6 * math.exp(-0.3 * l)
        lq1, lk1, lq2, lk2 = diff_lambda[l].astype(F32)
        lam = jnp.reshape(jnp.exp(jnp.sum(lq1 * lk1)) - jnp.exp(jnp.sum(lq2 * lk2)) + lam_init, (1,))
        w_inp, w_mrg = _inproj_weights(p), _merge_weights(p, lam_init)

        qdT_l, kd_l, vdT_l, qmT_l, km_l, vmT_l, hy_l, glu_l = _inproj(h_lat, mod_lat, w_inp, rope_lat, tile=tile_lat)
        qdT_c, kd_c, vdT_c, qmT_c, km_c, vmT_c, hy_c, glu_c = _inproj(h_ctx, mod_ctx, w_inp, rope_ctx, tile=tile_ctx)
        a_lat = _flash_attention(lam, qdT_l, kd_c, vdT_c, kd_l, vdT_l, n_maps=2, tq=min(256, n_lat))
        m_lat = _flash_attention(lam, qmT_l, km_c, vmT_c, km_l, vmT_l, n_maps=1, tq=min(512, n_lat))
        h_lat, u2_lat, lg_lat = _merge(h_lat, mod_lat, a_lat, _hyena_branch(hy_l, p, dft_lat), _conformer_branch(glu_l, p),
                                       m_lat, w_mrg, tile=min(256, n_lat))
        w_ei, w_eo = p['w_exp_in'].astype(BF16), p['w_exp_out'].astype(BF16)
        h_lat = _expert_choice_ffn(h_lat, mod_lat, u2_lat, lg_lat, w_ei, w_eo, final_norm=last)
        if not last:
            a_ctx = _flash_attention(lam, qdT_c, kd_c, vdT_c, None, None, n_maps=2, tq=n_ctx)
            m_ctx = _flash_attention(lam, qmT_c, km_c, vmT_c, None, None, n_maps=1, tq=n_ctx)
            h_ctx, u2_ctx, lg_ctx = _merge(h_ctx, mod_ctx, a_ctx, _hyena_branch(hy_c, p, dft_ctx),
                                           _conformer_branch(glu_c, p), m_ctx, w_mrg, tile=tile_ctx)
            h_ctx = _expert_choice_ffn(h_ctx, mod_ctx, u2_ctx, lg_ctx, w_ei, w_eo)
    return h_lat
```

```python
import functools
import math

import jax
import jax.numpy as jnp
from jax import lax
from jax.experimental import pallas as pl
from jax.experimental.pallas import tpu as pltpu

GRID_W = 64
ROPE_BASE = 10000.0
EPS = 1e-6

DIFF_HEADS = 4
DIFF_HEAD_DIM = 64
DIFF_V_DIM = 2 * DIFF_HEAD_DIM
HYENA_WIDTH = 256
HYENA_ORDER = 2
FILT_EMB = 33
DECAY_TARGET = 1e-2
FAST_DECAY = 0.3
SLOW_DECAY = 1.5
CONF_WIDTH = 256
MLA_HEADS = 4
MLA_Q_RANK = 256
MLA_KV_RANK = 128
MLA_NOPE = 64
MLA_ROPE = 32
MLA_V = 64
MLA_SCALE = (MLA_NOPE + MLA_ROPE) ** -0.5
N_BRANCH = 4
N_EXPERTS = 16
EC_CAPACITY = 2

DIFF_QK_W = DIFF_HEADS * 2 * DIFF_HEAD_DIM
DIFF_V_W = DIFF_HEADS * DIFF_V_DIM
HYENA_PROJ = (HYENA_ORDER + 1) * HYENA_WIDTH
CONF_PROJ = 2 * CONF_WIDTH
IN_SPLITS = (DIFF_QK_W, DIFF_QK_W, DIFF_V_W, HYENA_PROJ, CONF_PROJ, MLA_Q_RANK, MLA_KV_RANK, MLA_ROPE)
BRANCH_WIDTHS = (DIFF_V_W, HYENA_WIDTH, CONF_WIDTH, MLA_HEADS * MLA_V)

HEAD_LANES = 128
DIFF_V_PAD = 16
ATT_W = DIFF_HEADS * HEAD_LANES
LOG2E = 1.4426950408889634
VMEM_LIMIT_BYTES = 48 * 1024 * 1024
MOD_ROWS = 8

F32 = jnp.float32
BF16 = jnp.bfloat16
_NT = (((1,), (1,)), ((), ()))


def _params(n_axes, vmem=VMEM_LIMIT_BYTES):
    return pltpu.CompilerParams(dimension_semantics=("arbitrary",) * n_axes, vmem_limit_bytes=vmem)


def _flash_kernel(lam_ref, qT_ref, kc_ref, vcT_ref, *rest, n_maps, n_lat_chunks, tk, sum_row):
    if n_lat_chunks:
        kl_ref, vlT_ref, o_ref, acc_ref, m_ref, q2_ref, s_ref = rest
    else:
        o_ref, acc_ref, m_ref, q2_ref = rest
    qT = qT_ref[...]
    tq = qT.shape[1]
    if n_maps == 2:
        row = lax.broadcasted_iota(jnp.int32, qT.shape, 0)
        zero = jnp.zeros_like(qT)
        q2_ref[:, :tq] = jnp.where(row < DIFF_HEAD_DIM, qT, zero)
        q2_ref[:, tq:] = jnp.where(row >= DIFF_HEAD_DIM, qT, zero)
    else:
        q2_ref[...] = qT
    m_ref[...] = jnp.full(m_ref.shape, -jnp.inf, F32)
    acc_ref[...] = jnp.zeros(acc_ref.shape, F32)

    def scores(k):
        return jnp.dot(k, q2_ref[...], preferred_element_type=F32)

    def absorb(s, vT):
        m_prev = m_ref[...]
        m_new = jnp.maximum(m_prev, jnp.max(s, axis=0, keepdims=True))
        alpha = jnp.exp2(m_prev - m_new)
        p = jnp.exp2(s - m_new).astype(BF16)
        acc_ref[...] = alpha * acc_ref[...] + jnp.dot(vT, p, preferred_element_type=F32)
        m_ref[...] = m_new

    def chunk(c):
        return pl.ds(c * tk if isinstance(c, int) else pl.multiple_of(c * tk, tk), tk)

    def keys(c):
        return kl_ref[chunk(c), :]

    def values_t(c):
        return vlT_ref[:, chunk(c)]

    absorb(scores(kc_ref[...]), vcT_ref[...])
    if n_lat_chunks:
        s_ref[0] = scores(keys(0))

        def pair(j, carry):
            c = 2 * j
            s_ref[1] = scores(keys(c + 1))
            absorb(s_ref[0], values_t(c))
            s_ref[0] = scores(keys(c + 2))
            absorb(s_ref[1], values_t(c + 1))
            return carry
        lax.fori_loop(0, n_lat_chunks // 2 - 1, pair, 0)
        s_ref[1] = scores(keys(n_lat_chunks - 1))
        absorb(s_ref[0], values_t(n_lat_chunks - 2))
        absorb(s_ref[1], values_t(n_lat_chunks - 1))
    o = acc_ref[0:HEAD_LANES, :] / acc_ref[sum_row:sum_row + 1, :]
    if n_maps == 2:
        o = o[:, :tq] - lam_ref[0] * o[:, tq:]
        o = o * lax.rsqrt(jnp.mean(o * o, axis=0, keepdims=True) + EPS)
    o_ref[...] = o.T.astype(BF16)


def _flash_attention(lam, qT, kc, vcT, kl, vlT, *, n_maps, tq):
    b, _, s = qT.shape
    lc = kc.shape[1]
    mv = vcT.shape[2]
    sum_row = HEAD_LANES if n_maps == 2 else MLA_V
    r = n_maps * tq
    in_specs = [
        pl.BlockSpec(memory_space=pltpu.SMEM),
        pl.BlockSpec((None, HEAD_LANES, tq), lambda bi, hi, qi: (bi, hi, qi)),
        pl.BlockSpec((None, lc, HEAD_LANES), lambda bi, hi, qi: (bi, 0, hi)),
        pl.BlockSpec((None, None, mv, lc), lambda bi, hi, qi: (bi, hi, 0, 0)),
    ]
    args = [lam, qT, kc, vcT]
    scratch = [pltpu.VMEM((mv, r), F32), pltpu.VMEM((1, r), F32), pltpu.VMEM((HEAD_LANES, r), BF16)]
    n_lat_chunks, tk = 0, 0
    if kl is not None:
        sl = kl.shape[1]
        tk = _lat_chunk(sl)
        n_lat_chunks = sl // tk
        assert n_lat_chunks % 2 == 0 and n_lat_chunks * tk == sl
        scratch.append(pltpu.VMEM((2, tk, r), F32))
        in_specs += [
            pl.BlockSpec((None, sl, HEAD_LANES), lambda bi, hi, qi: (bi, 0, hi)),
            pl.BlockSpec((None, None, mv, sl), lambda bi, hi, qi: (bi, hi, 0, 0)),
        ]
        args += [kl, vlT]
    return pl.pallas_call(
        functools.partial(_flash_kernel, n_maps=n_maps, n_lat_chunks=n_lat_chunks, tk=tk, sum_row=sum_row),
        grid=(b, DIFF_HEADS, s // tq),
        in_specs=in_specs,
        out_specs=pl.BlockSpec((None, tq, HEAD_LANES), lambda bi, hi, qi: (bi, qi, hi)),
        out_shape=jax.ShapeDtypeStruct((b, s, ATT_W), BF16),
        scratch_shapes=scratch,
        compiler_params=_params(3),
        name=f"flash_attention_{n_maps}map",
    )(*args)


def _lat_chunk(s):
    return min(1024, s // 2)


W_NAT_SPLITS = (DIFF_QK_W, HYENA_PROJ, CONF_PROJ, MLA_Q_RANK, MLA_KV_RANK, HEAD_LANES)


def _modulated_norm(h, a, shift):
    return h * lax.rsqrt(jnp.mean(h * h, axis=-1, keepdims=True) + EPS) * a + shift


def _rope_lanes(x, tab_ref, shift):
    return (x * tab_ref[0] + pltpu.roll(x, shift, 1) * tab_ref[1]
            + pltpu.roll(x, HEAD_LANES - shift, 1) * tab_ref[2])


def _inproj_kernel(h_ref, mod_ref, wnat_ref, wT_ref, wuqT_ref, wukvk_ref, wuvT_ref, gq_ref, gkv_ref,
                   ropeT_d_ref, rope_kd_ref, ropeT_m_ref, rope_km_ref,
                   qdT_ref, kd_ref, vdT_ref, qmT_ref, km_ref, vmT_ref, hy_ref, glu_ref):
    u = _modulated_norm(h_ref[...], mod_ref[0:1, :], mod_ref[1:2, :]).astype(BF16)
    z = jnp.dot(u, wnat_ref[...], preferred_element_type=F32)
    zT = lax.dot_general(wT_ref[...], u, _NT, preferred_element_type=F32)
    offs = [0]
    for w in W_NAT_SPLITS:
        offs.append(offs[-1] + w)
    dk, hy, cf, cq, ckv, krp = (z[:, offs[i]:offs[i + 1]] for i in range(len(W_NAT_SPLITS)))

    for hd in range(DIFF_HEADS):
        sl = slice(hd * HEAD_LANES, (hd + 1) * HEAD_LANES)
        kd_ref[:, sl] = _rope_lanes(dk[:, sl], rope_kd_ref, DIFF_HEAD_DIM // 2).astype(BF16)
    cos_d, sin_d = ropeT_d_ref[0], ropeT_d_ref[1]
    half = DIFF_HEAD_DIM // 2
    for g in range(2 * DIFF_HEADS):
        x1 = zT[g * DIFF_HEAD_DIM:g * DIFF_HEAD_DIM + half]
        x2 = zT[g * DIFF_HEAD_DIM + half:(g + 1) * DIFF_HEAD_DIM]
        qdT_ref[g * DIFF_HEAD_DIM:g * DIFF_HEAD_DIM + half, :] = (x1 * cos_d - x2 * sin_d).astype(BF16)
        qdT_ref[g * DIFF_HEAD_DIM + half:(g + 1) * DIFF_HEAD_DIM, :] = (x1 * sin_d + x2 * cos_d).astype(BF16)
    tail = jnp.where(lax.broadcasted_iota(jnp.int32, (DIFF_V_PAD, zT.shape[1]), 0) == 0, 1.0, 0.0).astype(BF16)
    for hd in range(DIFF_HEADS):
        r0 = DIFF_QK_W + hd * DIFF_V_DIM
        vdT_ref[hd, 0:DIFF_V_DIM, :] = zT[r0:r0 + DIFF_V_DIM].astype(BF16)
        vdT_ref[hd, DIFF_V_DIM:, :] = tail

    hy_ref[...] = hy
    glu_ref[...] = cf[:, :CONF_WIDTH] * jax.nn.sigmoid(cf[:, CONF_WIDTH:])

    cqn = (cq * lax.rsqrt(jnp.mean(cq * cq, axis=-1, keepdims=True) + EPS) * gq_ref[...]).astype(BF16)
    ckvn = (ckv * lax.rsqrt(jnp.mean(ckv * ckv, axis=-1, keepdims=True) + EPS) * gkv_ref[...]).astype(BF16)
    qT = lax.dot_general(wuqT_ref[...], cqn, _NT, preferred_element_type=F32)
    cos_m, sin_m = ropeT_m_ref[0], ropeT_m_ref[1]
    hr = MLA_ROPE // 2
    for hd in range(MLA_HEADS):
        base = hd * HEAD_LANES
        r1 = base + MLA_NOPE
        x1, x2 = qT[r1:r1 + hr], qT[r1 + hr:r1 + 2 * hr]
        qmT_ref[base:r1, :] = qT[base:r1].astype(BF16)
        qmT_ref[r1:r1 + hr, :] = (x1 * cos_m - x2 * sin_m).astype(BF16)
        qmT_ref[r1 + hr:r1 + 2 * hr, :] = (x1 * sin_m + x2 * cos_m).astype(BF16)
        qmT_ref[r1 + 2 * hr:base + HEAD_LANES, :] = jnp.zeros((HEAD_LANES - MLA_NOPE - MLA_ROPE, qT.shape[1]), BF16)
    kn = jnp.dot(ckvn, wukvk_ref[...], preferred_element_type=F32)
    kr = _rope_lanes(krp, rope_km_ref, hr)
    for hd in range(MLA_HEADS):
        sl = slice(hd * HEAD_LANES, (hd + 1) * HEAD_LANES)
        km_ref[:, sl] = (kn[:, sl] + kr).astype(BF16)
    vT = lax.dot_general(wuvT_ref[...], ckvn, _NT, preferred_element_type=F32)
    ones_row = lax.broadcasted_iota(jnp.int32, vT.shape, 0) % HEAD_LANES == MLA_V
    vT = jnp.where(ones_row, 1.0, vT).astype(BF16)
    for hd in range(MLA_HEADS):
        vmT_ref[hd] = vT[hd * HEAD_LANES:(hd + 1) * HEAD_LANES]


def _inproj(h, mod, wts, rope, *, tile):
    b, n, d = h.shape
    const2 = lambda bi, ti: (0, 0)
    tok = lambda w: pl.BlockSpec((None, tile, w), lambda bi, ti: (bi, ti, 0))
    tokT = lambda w: pl.BlockSpec((None, w, tile), lambda bi, ti: (bi, 0, ti))
    full = lambda a: pl.BlockSpec(a.shape, const2)
    in_specs = [tok(d), pl.BlockSpec((None, MOD_ROWS, d), lambda bi, ti: (bi, 0, 0))]
    in_specs += [full(wts[k]) for k in ('w_nat', 'w_T', 'w_uqT', 'w_ukvk', 'w_uvT', 'gq', 'gkv')]
    in_specs += [pl.BlockSpec((2, DIFF_HEAD_DIM // 2, tile), lambda bi, ti: (0, 0, ti)),
                 pl.BlockSpec((3, tile, HEAD_LANES), lambda bi, ti: (0, ti, 0)),
                 pl.BlockSpec((2, MLA_ROPE // 2, tile), lambda bi, ti: (0, 0, ti)),
                 pl.BlockSpec((3, tile, HEAD_LANES), lambda bi, ti: (0, ti, 0))]
    sds = jax.ShapeDtypeStruct
    vrows_d, vrows_m = DIFF_V_DIM + DIFF_V_PAD, HEAD_LANES
    headsT = lambda rows: pl.BlockSpec((None, DIFF_HEADS, rows, tile), lambda bi, ti: (bi, 0, 0, ti))
    out_shape = (sds((b, ATT_W, n), BF16), sds((b, n, ATT_W), BF16), sds((b, DIFF_HEADS, vrows_d, n), BF16),
                 sds((b, ATT_W, n), BF16), sds((b, n, ATT_W), BF16), sds((b, MLA_HEADS, vrows_m, n), BF16),
                 sds((b, n, HYENA_PROJ), F32), sds((b, n, CONF_WIDTH), F32))
    out_specs = (tokT(ATT_W), tok(ATT_W), headsT(vrows_d), tokT(ATT_W), tok(ATT_W), headsT(vrows_m),
                 tok(HYENA_PROJ), tok(CONF_WIDTH))
    return pl.pallas_call(
        _inproj_kernel, grid=(b, n // tile), in_specs=in_specs, out_specs=out_specs, out_shape=out_shape,
        compiler_params=_params(2), name="inproj",
    )(h, mod, wts['w_nat'], wts['w_T'], wts['w_uqT'], wts['w_ukvk'], wts['w_uvT'], wts['gq'], wts['gkv'],
      rope['T_d'], rope['k_d'], rope['T_m'], rope['k_m'])


def _pad_heads(w, width):
    rows = w.shape[0]
    w = w.reshape(rows, MLA_HEADS, width)
    return jnp.pad(w, ((0, 0), (0, 0), (0, HEAD_LANES - width))).reshape(rows, ATT_W)


def _inproj_weights(p):
    d = p['w_in'].shape[0]
    dq, dk, dv, hy, cf, cq, ckv, kr = _split(p['w_in'][:, :sum(IN_SPLITS)], IN_SPLITS)
    krp = jnp.zeros((d, HEAD_LANES), F32).at[:, MLA_NOPE:MLA_NOPE + MLA_ROPE].set(kr)
    w_ukv = p['mla_w_ukv'].reshape(MLA_KV_RANK, MLA_HEADS, MLA_NOPE + MLA_V)
    return dict(
        w_nat=jnp.concatenate([dk, hy, cf, cq, ckv, krp], axis=1).astype(BF16),
        w_T=jnp.concatenate([dq * (DIFF_HEAD_DIM ** -0.5 * LOG2E), dv], axis=1).T.astype(BF16),
        w_uqT=_pad_heads(p['mla_w_uq'] * (MLA_SCALE * LOG2E), MLA_NOPE + MLA_ROPE).T.astype(BF16),
        w_ukvk=_pad_heads(w_ukv[:, :, :MLA_NOPE].reshape(MLA_KV_RANK, -1), MLA_NOPE).astype(BF16),
        w_uvT=_pad_heads(w_ukv[:, :, MLA_NOPE:].reshape(MLA_KV_RANK, -1), MLA_V).T.astype(BF16),
        gq=p['mla_q_norm_g'][None, :], gkv=p['mla_kv_norm_g'][None, :])


def _rope_tables(n_tok, rot_dim):
    rows = n_tok // GRID_W
    row = jnp.repeat(jnp.arange(rows), GRID_W).astype(F32)
    col = jnp.tile(jnp.arange(GRID_W), rows).astype(F32)
    nf = rot_dim // 4
    inv = ROPE_BASE ** (-jnp.arange(nf, dtype=F32) / nf)
    ang = jnp.concatenate([row[:, None] * inv, col[:, None] * inv], axis=-1)
    return jnp.cos(ang), jnp.sin(ang)


def _rope_operands(n_tok, identity):
    if identity:
        cos_d, sin_d = jnp.ones((n_tok, DIFF_HEAD_DIM // 2), F32), jnp.zeros((n_tok, DIFF_HEAD_DIM // 2), F32)
        cos_m, sin_m = jnp.ones((n_tok, MLA_ROPE // 2), F32), jnp.zeros((n_tok, MLA_ROPE // 2), F32)
    else:
        cos_d, sin_d = _rope_tables(n_tok, DIFF_HEAD_DIM)
        cos_m, sin_m = _rope_tables(n_tok, MLA_ROPE)
    z_d, z_m = jnp.zeros_like(sin_d), jnp.zeros_like(sin_m)
    two = lambda a, bb: jnp.tile(jnp.concatenate([a, bb], axis=1), (1, 2))
    lo, hi = jnp.zeros((n_tok, MLA_NOPE), F32), jnp.zeros((n_tok, HEAD_LANES - MLA_NOPE - MLA_ROPE), F32)
    mid = lambda a, bb: jnp.concatenate([lo, a, bb, hi], axis=1)
    return dict(T_d=jnp.stack([cos_d.T, sin_d.T]), T_m=jnp.stack([cos_m.T, sin_m.T]),
                k_d=jnp.stack([two(cos_d, cos_d), two(z_d, sin_d), two(-sin_d, z_d)]),
                k_m=jnp.stack([mid(cos_m, cos_m), mid(z_m, sin_m), mid(-sin_m, z_m)]))


def _merge_kernel(h_ref, mod_ref, a_ref, hy_ref, cf_ref, m_ref, wg_ref, wbd_ref, wbh_ref, wbc_ref, wbm_ref,
                  wo_ref, wrh_ref, wrl_ref, hn_ref, u2_ref, lg_ref):
    h = h_ref[...]
    d = h.shape[1]
    u = _modulated_norm(h, mod_ref[0:1, :], mod_ref[1:2, :]).astype(BF16)
    gates = jax.nn.sigmoid(jnp.dot(u, wg_ref[...], preferred_element_type=F32))
    dot = lambda x, w_ref: jnp.dot(x, w_ref[...], preferred_element_type=F32)
    acc = gates[:, :d] * dot(a_ref[...], wbd_ref)
    acc += gates[:, d:2 * d] * dot(hy_ref[...].astype(BF16), wbh_ref)
    acc += gates[:, 2 * d:3 * d] * dot(cf_ref[...].astype(BF16), wbc_ref)
    acc += gates[:, 3 * d:] * dot(m_ref[...], wbm_ref)
    hn = h + mod_ref[2:3, :] * dot(acc.astype(BF16), wo_ref)
    hn_ref[...] = hn
    u2 = _modulated_norm(hn, mod_ref[3:4, :], mod_ref[4:5, :])
    u2h = u2.astype(BF16)
    u2l = (u2 - u2h.astype(F32)).astype(BF16)
    u2_ref[...] = u2h
    lg_ref[...] = dot(u2h, wrh_ref) + (dot(u2l, wrh_ref) + dot(u2h, wrl_ref))


def _merge(h, mod, a, hyv, cfv, m, wts, *, tile):
    b, n, d = h.shape
    const2 = lambda bi, ti: (0, 0)
    tok = lambda w: pl.BlockSpec((None, tile, w), lambda bi, ti: (bi, ti, 0))
    names = ('w_gate', 'w_bd', 'w_bh', 'w_bc', 'w_bm', 'w_out', 'w_rh', 'w_rl')
    in_specs = [tok(d), pl.BlockSpec((None, MOD_ROWS, d), lambda bi, ti: (bi, 0, 0)),
                tok(ATT_W), tok(HYENA_WIDTH), tok(CONF_WIDTH), tok(ATT_W)]
    in_specs += [pl.BlockSpec(wts[k].shape, const2) for k in names]
    sds = jax.ShapeDtypeStruct
    return pl.pallas_call(
        _merge_kernel, grid=(b, n // tile), in_specs=in_specs,
        out_specs=(tok(d), tok(d), tok(HEAD_LANES)),
        out_shape=(sds((b, n, d), F32), sds((b, n, d), BF16), sds((b, n, HEAD_LANES), F32)),
        compiler_params=_params(2, 56 * 1024 * 1024), name="merge",
    )(h, mod, a, hyv, cfv, m, *[wts[k] for k in names])


def _merge_weights(p, lam_init):
    d = p['w_out'].shape[0]
    wb_d, wb_h, wb_c, wb_m = (w.T for w in _split(p['w_branch'].T, BRANCH_WIDTHS))
    wb_d = wb_d * (jnp.tile(p['diff_subln_g'], DIFF_HEADS) * (1.0 - lam_init))[:, None]
    wb_m = jnp.pad(wb_m.reshape(MLA_HEADS, MLA_V, d), ((0, 0), (0, HEAD_LANES - MLA_V), (0, 0))).reshape(ATT_W, d)
    w_r = jnp.pad(p['w_router'], ((0, 0), (0, HEAD_LANES - N_EXPERTS)))
    w_rh = w_r.astype(BF16)
    return dict(w_gate=p['w_in'][:, sum(IN_SPLITS):].astype(BF16), w_bd=wb_d.astype(BF16), w_bh=wb_h.astype(BF16),
                w_bc=wb_c.astype(BF16), w_bm=wb_m.astype(BF16), w_out=p['w_out'].astype(BF16),
                w_rh=w_rh, w_rl=(w_r - w_rh.astype(F32)).astype(BF16))


SUB_TOKENS = 256
GATHER_WINDOW = 272
COMBINE_WINDOW = 384
ROUTE_MIN_ROWS = 8


def _excl_scan(x, lane, row):
    inc = x
    s = 1
    while s < HEAD_LANES:
        inc = inc + jnp.where(lane >= s, pltpu.roll(inc, s, 2), 0.0)
        s *= 2
    tot = jnp.sum(x, axis=2, keepdims=True) + jnp.zeros_like(x)
    off = tot
    s = 1
    while s < x.shape[1]:
        off = off + jnp.where(row >= s, pltpu.roll(off, s, 1), 0.0)
        s *= 2
    return inc - x + (off - tot)


def _route_kernel(lg_ref, pos_ref, aff_ref, *, n_valid, cap):
    lg = lg_ref[...]
    shape = lg.shape
    lane = lax.broadcasted_iota(jnp.int32, shape, 2)
    row = lax.broadcasted_iota(jnp.int32, shape, 1)
    e = jnp.exp(lg - jnp.max(lg, axis=0, keepdims=True))
    aff = e / jnp.sum(e, axis=0, keepdims=True)
    bits = jnp.where(row * HEAD_LANES + lane < n_valid, pltpu.bitcast(aff, jnp.int32), -1)

    def count(mask):
        c = jnp.sum(jnp.where(mask, 1.0, 0.0), axis=2, keepdims=True)
        return jnp.sum(c, axis=1, keepdims=True)

    def step(i, thr):
        cand = thr | (jnp.int32(1) << (30 - i))
        return jnp.where(count(bits >= cand) >= cap, cand, thr)
    thr = lax.fori_loop(0, 31, step, jnp.zeros((shape[0], 1, 1), jnp.int32))
    gt = bits > thr
    eq = bits == thr
    need = cap - count(gt)
    tie_rank = _excl_scan(jnp.where(eq, 1.0, 0.0), lane, row)
    sel = gt | (eq & (tie_rank < need))
    pos = _excl_scan(jnp.where(sel, 1.0, 0.0), lane, row)
    pos_ref[...] = jnp.where(sel, pos.astype(jnp.int32), -1)
    aff_ref[...] = aff


def _route(logits, cap):
    b, n, _ = logits.shape
    rows = max(ROUTE_MIN_ROWS, n // HEAD_LANES)
    lg = jnp.swapaxes(logits[..., :N_EXPERTS], 1, 2)
    lg = jnp.pad(lg, ((0, 0), (0, 0), (0, rows * HEAD_LANES - n))).reshape(b, N_EXPERTS, rows, HEAD_LANES)
    spec = pl.BlockSpec((None, N_EXPERTS, rows, HEAD_LANES), lambda bi: (bi, 0, 0, 0))
    pos, aff = pl.pallas_call(
        functools.partial(_route_kernel, n_valid=n, cap=cap), grid=(b,), in_specs=[spec], out_specs=(spec, spec),
        out_shape=(jax.ShapeDtypeStruct(lg.shape, jnp.int32), jax.ShapeDtypeStruct(lg.shape, F32)),
        compiler_params=_params(1), name="route",
    )(lg)
    flat = lambda a: a.reshape(b, N_EXPERTS, rows * HEAD_LANES)[..., :n]
    return flat(pos), flat(aff)


def _experts_kernel(base_ref, u_ref, mod_ref, aff_ref, pos_ref, win_ref, wout_ref, ye_ref, xe_ref, *,
                    n_sub, cap):
    bi, ei, kb = pl.program_id(0), pl.program_id(1), pl.program_id(2)
    d = u_ref.shape[1]

    @pl.when(kb == 0)
    def _():
        xe_ref[...] = jnp.zeros(xe_ref.shape, F32)

    slot = lax.broadcasted_iota(jnp.int32, (GATHER_WINDOW, SUB_TOKENS), 0)
    ones = jnp.ones((SUB_TOKENS, HEAD_LANES), BF16)
    for j in range(n_sub):
        tok = slice(j * SUB_TOKENS, (j + 1) * SUB_TOKENS)
        base = pl.multiple_of(base_ref[bi, ei, kb * n_sub + j], 16)
        match = slot == (pos_ref[:, tok] - base)
        onehot = jnp.where(match, 1.0, 0.0).astype(BF16)
        g = aff_ref[:, tok]
        g_hi = g.astype(BF16).astype(F32)
        sel_hi = jnp.where(match, g_hi, 0.0).astype(BF16)
        sel_lo = jnp.where(match, g - g_hi, 0.0).astype(BF16)
        rows = pl.ds(base, GATHER_WINDOW)
        xe_ref[rows, :d] += jnp.dot(onehot, u_ref[tok, :], preferred_element_type=F32)
        xe_ref[rows, d:d + HEAD_LANES] += jnp.dot(sel_hi, ones, preferred_element_type=F32)
        xe_ref[rows, d + HEAD_LANES:] += jnp.dot(sel_lo, ones, preferred_element_type=F32)

    @pl.when(kb == pl.num_programs(2) - 1)
    def _():
        f = wout_ref.shape[0]
        step = min(512, cap)
        for r0 in range(0, cap, step):
            x = xe_ref[r0:r0 + step, :d].astype(BF16)
            gate = xe_ref[r0:r0 + step, d:d + HEAD_LANES] + xe_ref[r0:r0 + step, d + HEAD_LANES:]
            hgu = jnp.dot(x, win_ref[...], preferred_element_type=F32)
            act = (jax.nn.silu(hgu[:, :f]) * hgu[:, f:]).astype(BF16)
            y = jnp.dot(act, wout_ref[...], preferred_element_type=F32)
            scale = jnp.concatenate([gate] * (d // HEAD_LANES), axis=1) * mod_ref[5:6, :]
            ye_ref[r0:r0 + step, :] = (y * scale).astype(BF16)
        ye_ref[cap:, :] = jnp.zeros((ye_ref.shape[0] - cap, ye_ref.shape[1]), BF16)


def _combine_kernel(base_ref, h_ref, mod_ref, posn_ref, ye_ref, hn_ref, *, n_sub, final_norm):
    bi, kb, ei = pl.program_id(0), pl.program_id(1), pl.program_id(2)

    @pl.when(ei == 0)
    def _():
        hn_ref[...] = h_ref[...]

    slot = lax.broadcasted_iota(jnp.int32, (SUB_TOKENS, COMBINE_WINDOW), 1)
    lane_e = lax.broadcasted_iota(jnp.int32, (SUB_TOKENS, N_EXPERTS), 1)
    for j in range(n_sub):
        tok = slice(j * SUB_TOKENS, (j + 1) * SUB_TOKENS)
        base = pl.multiple_of(base_ref[bi, ei, kb * n_sub + j], 16)
        rel = jnp.sum(jnp.where(lane_e == ei, posn_ref[tok, :], 0), axis=1, keepdims=True) - base
        onehot = jnp.where(slot == rel, 1.0, 0.0).astype(BF16)
        ye = ye_ref[pl.ds(base, COMBINE_WINDOW), :]
        hn_ref[tok, :] += jnp.dot(onehot, ye, preferred_element_type=F32)

    if final_norm:
        @pl.when(ei == pl.num_programs(2) - 1)
        def _():
            hn = hn_ref[...]
            hn_ref[...] = hn * lax.rsqrt(jnp.mean(hn * hn, axis=-1, keepdims=True) + EPS) * mod_ref[6:7, :]


def _expert_choice_ffn(h, mod, u2, logits, w_exp_in, w_exp_out, final_norm=False):
    b, n, d = u2.shape
    cap = max(1, EC_CAPACITY * n // N_EXPERTS)
    n_sub, n_sub_c = min(4, n // SUB_TOKENS), min(8, n // SUB_TOKENS)
    big, big_c = n_sub * SUB_TOKENS, n_sub_c * SUB_TOKENS
    n_big, n_big_c = n // big, n // big_c
    capp = cap + COMBINE_WINDOW
    pos, aff = _route(logits, cap)
    cnt = jnp.sum((pos >= 0).reshape(b, N_EXPERTS, n // SUB_TOKENS, SUB_TOKENS), axis=-1)
    base = (jnp.cumsum(cnt, axis=-1) - cnt) // 16 * 16
    base = base.astype(jnp.int32)
    f = w_exp_out.shape[1]
    ye = pl.pallas_call(
        functools.partial(_experts_kernel, n_sub=n_sub, cap=cap),
        grid_spec=pltpu.PrefetchScalarGridSpec(
            num_scalar_prefetch=1, grid=(b, N_EXPERTS, n_big),
            in_specs=[pl.BlockSpec((None, big, d), lambda bi, ei, kb, base_r: (bi, kb, 0)),
                      pl.BlockSpec((None, MOD_ROWS, d), lambda bi, ei, kb, base_r: (bi, 0, 0)),
                      pl.BlockSpec((None, None, 1, big), lambda bi, ei, kb, base_r: (bi, ei, 0, kb)),
                      pl.BlockSpec((None, None, 1, big), lambda bi, ei, kb, base_r: (bi, ei, 0, kb)),
                      pl.BlockSpec((None, d, 2 * f), lambda bi, ei, kb, base_r: (ei, 0, 0)),
                      pl.BlockSpec((None, f, d), lambda bi, ei, kb, base_r: (ei, 0, 0))],
            out_specs=pl.BlockSpec((None, None, capp, d), lambda bi, ei, kb, base_r: (bi, ei, 0, 0)),
            scratch_shapes=[pltpu.VMEM((capp, d + 2 * HEAD_LANES), F32)]),
        out_shape=jax.ShapeDtypeStruct((b, N_EXPERTS, capp, d), BF16),
        compiler_params=_params(3, 56 * 1024 * 1024), name="experts",
    )(base, u2, mod, aff.reshape(b, N_EXPERTS, 1, n), pos.reshape(b, N_EXPERTS, 1, n), w_exp_in, w_exp_out)
    posn = jnp.swapaxes(pos, 1, 2)
    return pl.pallas_call(
        functools.partial(_combine_kernel, n_sub=n_sub_c, final_norm=final_norm),
        grid_spec=pltpu.PrefetchScalarGridSpec(
            num_scalar_prefetch=1, grid=(b, n_big_c, N_EXPERTS),
            in_specs=[pl.BlockSpec((None, big_c, d), lambda bi, kb, ei, base_r: (bi, kb, 0)),
                      pl.BlockSpec((None, MOD_ROWS, d), lambda bi, kb, ei, base_r: (bi, 0, 0)),
                      pl.BlockSpec((None, big_c, N_EXPERTS), lambda bi, kb, ei, base_r: (bi, kb, 0)),
                      pl.BlockSpec((None, None, capp, d), lambda bi, kb, ei, base_r: (bi, ei, 0, 0))],
            out_specs=pl.BlockSpec((None, big_c, d), lambda bi, kb, ei, base_r: (bi, kb, 0))),
        out_shape=jax.ShapeDtypeStruct((b, n, d), F32),
        compiler_params=_params(3, 56 * 1024 * 1024), name="combine",
    )(base, h, mod, posn, ye)


def _split(z, sizes):
    out, start = [], 0
    for s in sizes:
        out.append(z[..., start:start + s])
        start += s
    return out


HALO = 16


def _fill_ext(ext_ref, x_ref, prev_ref, next_ref):
    ti, nt = pl.program_id(1), pl.num_programs(1)
    tt = x_ref.shape[0]
    ext_ref[0:HALO, :] = jnp.where(ti > 0, prev_ref[...], 0.0)
    ext_ref[HALO:HALO + tt, :] = x_ref[...]
    ext_ref[HALO + tt:, :] = jnp.where(ti < nt - 1, next_ref[...], 0.0)


def _taps(ext_ref, w_ref, tt):
    k = w_ref.shape[0]
    acc = None
    for j in range(k):
        start = HALO - k // 2 + j
        term = w_ref[j:j + 1, :] * ext_ref[start:start + tt, :]
        acc = term if acc is None else acc + term
    return acc


def _short_conv_kernel(x_ref, prev_ref, next_ref, w_ref, b_ref, x1_ref, x2_ref, v_ref, ext_ref):
    _fill_ext(ext_ref, x_ref, prev_ref, next_ref)
    y = _taps(ext_ref, w_ref, x_ref.shape[0]) + b_ref[...]
    x1_ref[...] = y[:, :HYENA_WIDTH]
    x2_ref[...] = y[:, HYENA_WIDTH:2 * HYENA_WIDTH]
    v_ref[...] = y[:, 2 * HYENA_WIDTH:]


def _conformer_kernel(x_ref, prev_ref, next_ref, w_ref, g_ref, b_ref, o_ref, ext_ref):
    _fill_ext(ext_ref, x_ref, prev_ref, next_ref)
    u = _taps(ext_ref, w_ref, x_ref.shape[0])
    mu = jnp.mean(u, axis=-1, keepdims=True)
    var = jnp.mean(jnp.square(u - mu), axis=-1, keepdims=True)
    y = (u - mu) * lax.rsqrt(var + EPS) * g_ref[...] + b_ref[...]
    o_ref[...] = y * jax.nn.sigmoid(y)


def _token_conv(body, x, consts, out_widths, name):
    b, n, w = x.shape
    tt = min(1024, n)
    per = tt // HALO
    last = n // HALO - 1
    in_specs = [pl.BlockSpec((None, tt, w), lambda bi, ti: (bi, ti, 0)),
                pl.BlockSpec((None, HALO, w), lambda bi, ti: (bi, jnp.maximum(ti * per - 1, 0), 0)),
                pl.BlockSpec((None, HALO, w), lambda bi, ti: (bi, jnp.minimum((ti + 1) * per, last), 0))]
    in_specs += [pl.BlockSpec(cst.shape, lambda bi, ti: (0, 0)) for cst in consts]
    outs = tuple(jax.ShapeDtypeStruct((b, n, ow), F32) for ow in out_widths)
    out_specs = tuple(pl.BlockSpec((None, tt, ow), lambda bi, ti: (bi, ti, 0)) for ow in out_widths)
    return pl.pallas_call(body, grid=(b, n // tt), in_specs=in_specs, out_specs=out_specs, out_shape=outs,
                          scratch_shapes=[pltpu.VMEM((tt + 2 * HALO, w), F32)],
                          compiler_params=_params(2), name=name)(x, x, x, *consts)


def _conformer_branch(glu, p):
    return _token_conv(_conformer_kernel, glu, (p['conf_dw_w'], p['conf_ln_g'][None, :], p['conf_ln_b'][None, :]),
                       (CONF_WIDTH,), "conformer")[0]


FILT_LANES = 128
DFT_SHORT = 256


def _split_bf16(x):
    hi = x.astype(BF16)
    return hi, (x - hi.astype(F32)).astype(BF16)


def _dot_split(ah, al, bh, bl):
    dot = lambda u, v: jnp.dot(u, v, preferred_element_type=F32)
    return dot(ah, bh) + (dot(al, bh) + dot(ah, bl))


def _filter_kernel(z_ref, w1h, w1l, b1, f1, w2h, w2l, b2, f2, w3h, w3l, dl_ref, h_ref, asum_ref, *,
                   tiles_per_dir):
    z = z_ref[...]
    hid = jnp.sin(f1[...] * (_dot_split(*_split_bf16(z), w1h[...], w1l[...]) + b1[...]))
    hid = jnp.sin(f2[...] * (_dot_split(*_split_bf16(hid), w2h[...], w2l[...]) + b2[...]))
    h = _dot_split(*_split_bf16(hid), w3h[...], w3l[...])
    h = h * jnp.exp(-z[:, 0:1] * dl_ref[...])
    h_ref[...] = h

    @pl.when(pl.program_id(0) % tiles_per_dir == 0)
    def _():
        asum_ref[...] = jnp.zeros(asum_ref.shape, F32)
    asum_ref[...] += jnp.sum(jnp.abs(h), axis=0, keepdims=True)


def _normalise_kernel(h_ref, asum_ref, *o_refs, n):
    tt = h_ref.shape[0]
    row = pl.program_id(0) * tt + lax.broadcasted_iota(jnp.int32, h_ref.shape, 0)
    k = jnp.where(row == n, 0.0, h_ref[...] / asum_ref[...])
    for o, o_ref in enumerate(o_refs):
        o_ref[...] = k[:, o * HYENA_WIDTH:(o + 1) * HYENA_WIDTH]


def _hyena_taps(n, p):
    t = jnp.linspace(0.0, 1.0, n, dtype=F32)[:, None]
    bands = (FILT_EMB - 1) // 2
    w = (2.0 * math.pi / n) * jnp.arange(n, dtype=F32)[:, None]
    f = jnp.linspace(1e-4, bands - 1, bands, dtype=F32)[None, :]
    z = jnp.concatenate([t, jnp.cos(f * w), -jnp.sin(f * w), jnp.zeros((n, FILT_LANES - FILT_EMB), F32)], axis=-1)
    hid = p['filt_w2'].shape[0]
    padc = lambda a: jnp.pad(a, ((0, 0), (0, FILT_LANES - a.shape[1])))
    padr = lambda a: jnp.pad(a, ((0, FILT_LANES - a.shape[0]), (0, 0)))
    w1, w2, w3 = padc(padr(p['filt_w1'])), padc(padr(p['filt_w2'])), padr(p['filt_w3'])
    b1, b2 = padc(p['filt_b1'][None, :]), padc(p['filt_b2'][None, :])
    f1, f2 = padc(p['filt_freq'][0][None, :]), padc(p['filt_freq'][1][None, :])
    deltas = jnp.abs(jnp.linspace(math.log(DECAY_TARGET) / SLOW_DECAY, math.log(DECAY_TARGET) / FAST_DECAY,
                                  HYENA_WIDTH, dtype=F32))
    width = HYENA_ORDER * HYENA_WIDTH
    w3 = w3.reshape(FILT_LANES, HYENA_ORDER, 2, HYENA_WIDTH).transpose(2, 0, 1, 3).reshape(2, FILT_LANES, width)
    dl = jnp.tile(deltas, HYENA_ORDER)[None, :]
    z2 = jnp.concatenate([z, z[::-1]], axis=0)
    w3h, w3l = _split_bf16(w3)
    tt = min(1024, n)
    tiles_per_dir = n // tt
    cspec = lambda a: pl.BlockSpec(a.shape, lambda i: (0, 0))
    dirspec = lambda rows: pl.BlockSpec((None, rows, width), lambda i: (i // tiles_per_dir, 0, 0))
    tile = lambda w: pl.BlockSpec((tt, w), lambda i: (i, 0))
    small = [*_split_bf16(w1), b1, f1, *_split_bf16(w2), b2, f2]
    h_raw, asum = pl.pallas_call(
        functools.partial(_filter_kernel, tiles_per_dir=tiles_per_dir), grid=(2 * tiles_per_dir,),
        in_specs=[tile(FILT_LANES)] + [cspec(a) for a in small] + [dirspec(FILT_LANES), dirspec(FILT_LANES), cspec(dl)],
        out_specs=(tile(width), dirspec(1)),
        out_shape=(jax.ShapeDtypeStruct((2 * n, width), F32), jax.ShapeDtypeStruct((2, 1, width), F32)),
        compiler_params=_params(1), name="hyena_filter_mlp")(z2, *small, w3h, w3l, dl)
    return pl.pallas_call(
        functools.partial(_normalise_kernel, n=n), grid=(2 * tiles_per_dir,),
        in_specs=[tile(width), dirspec(1)],
        out_specs=tuple(tile(HYENA_WIDTH) for _ in range(HYENA_ORDER)),
        out_shape=tuple(jax.ShapeDtypeStruct((2 * n, HYENA_WIDTH), F32) for _ in range(HYENA_ORDER)),
        compiler_params=_params(1), name="hyena_filter_norm",
    )(h_raw, asum)


def _dft_tables(n):
    n2 = DFT_SHORT if n >= 4 * DFT_SHORT else n
    n1 = n // n2

    def cis(idx):
        ang = (-2.0 * math.pi / n) * idx.astype(F32)
        return jnp.cos(ang), jnp.sin(ang)
    k2 = jnp.arange(n2)
    fr, fi = cis((k2[:, None] * k2[None, :]) % n2 * n1)
    tabs = dict(n1=n1, n2=n2)
    tabs['f_hi'], tabs['f_lo'] = _split_bf16(jnp.stack([fr, fi]))
    k1 = jnp.arange(n1)
    tr, ti = cis(k1[:, None] * k2[None, :])
    tabs['tw'] = jnp.broadcast_to(jnp.stack([tr, ti], axis=1)[..., None], (n1, 2, n2, HEAD_LANES))
    if n1 > 1:
        gr, gi = cis((k1[:, None] * k1[None, :]) % n1 * n2)
        half = n1 // 2
        grh, gih = gr[:, :half], gi[:, :half]
        tabs['m_fwd'] = _split_bf16(jnp.block([[grh, -gih], [gih, grh]]))
        tabs['m_real'] = _split_bf16(jnp.concatenate([gr, gi], axis=0))
        tabs['m_inv'] = _split_bf16(jnp.block([[grh.T, gih.T], [-gih.T, grh.T]]))
    return tabs


def _rowmix_kernel(mh_ref, ml_ref, x_ref, o_ref):
    o_ref[...] = _dot_split(mh_ref[...], ml_ref[...], *_split_bf16(x_ref[...]))


def _rowmix(m, x):
    mh, ml = m
    rin, cols = x.shape
    ct = min(2048, cols)
    return pl.pallas_call(
        _rowmix_kernel, grid=(cols // ct,),
        in_specs=[pl.BlockSpec(mh.shape, lambda i: (0, 0)), pl.BlockSpec(ml.shape, lambda i: (0, 0)),
                  pl.BlockSpec((rin, ct), lambda i: (0, i))],
        out_specs=pl.BlockSpec((mh.shape[0], ct), lambda i: (0, i)),
        out_shape=jax.ShapeDtypeStruct((mh.shape[0], cols), F32), compiler_params=_params(1), name="dft_rowmix",
    )(mh, ml, x)


def _spectral_kernel(x_ref, tw_ref, fh_ref, fl_ref, k_ref, o_ref, *, conv):
    xr, xi = x_ref[0], x_ref[1]
    reps = xr.shape[1] // HEAD_LANES
    tr = jnp.concatenate([tw_ref[0]] * reps, axis=1)
    ti = jnp.concatenate([tw_ref[1]] * reps, axis=1)
    frh, fih, frl, fil = fh_ref[0], fh_ref[1], fl_ref[0], fl_ref[1]

    def dft(ar, ai, conj):
        arh, arl = _split_bf16(ar)
        aih, ail = _split_bf16(ai)
        rr, ii = _dot_split(frh, frl, arh, arl), _dot_split(fih, fil, aih, ail)
        ri, ir = _dot_split(frh, frl, aih, ail), _dot_split(fih, fil, arh, arl)
        return (rr + ii, ri - ir) if conj else (rr - ii, ri + ir)

    yr, yi = dft(xr * tr - xi * ti, xr * ti + xi * tr, False)
    if not conv:
        o_ref[0] = yr * k_ref[...]
        o_ref[1] = yi * k_ref[...]
        return
    kr, ki = k_ref[0], k_ref[1]
    cr, ci = dft(yr * kr - yi * ki, yr * ki + yi * kr, True)
    o_ref[0] = cr * tr + ci * ti
    o_ref[1] = ci * tr - cr * ti


def _spectral(x, k, tabs, conv):
    _, n1, n2, c = x.shape
    slab = pl.BlockSpec((2, None, n2, c), lambda i: (0, i, 0, 0))
    kspec = slab if conv else pl.BlockSpec(k.shape, lambda i: (0, 0))
    return pl.pallas_call(
        functools.partial(_spectral_kernel, conv=conv), grid=(n1,),
        in_specs=[slab, pl.BlockSpec((None, 2, n2, HEAD_LANES), lambda i: (i, 0, 0, 0)),
                  pl.BlockSpec(tabs['f_hi'].shape, lambda i: (0, 0, 0)),
                  pl.BlockSpec(tabs['f_lo'].shape, lambda i: (0, 0, 0)), kspec],
        out_specs=slab, out_shape=jax.ShapeDtypeStruct(x.shape, F32), compiler_params=_params(1),
        name="dft_spectral_conv" if conv else "dft_spectral_filter",
    )(x, tabs['tw'], tabs['f_hi'], tabs['f_lo'], k)


def _filter_spectrum(k, tabs):
    n, c = k.shape
    n1, n2 = tabs['n1'], tabs['n2']
    if n1 > 1:
        x = _rowmix(tabs['m_real'], k.reshape(n1, n2 * c)).reshape(2, n1, n2, c)
    else:
        x = jnp.stack([k, jnp.zeros_like(k)]).reshape(2, 1, n2, c)
    return _spectral(x, jnp.full((1, c), 1.0 / n, F32), tabs, conv=False)


def _long_conv(v, kf, tabs):
    b, n, c = v.shape
    assert b == 2
    n1, n2 = tabs['n1'], tabs['n2']
    if n1 > 1:
        x = _rowmix(tabs['m_fwd'], v.reshape(n1, n2 * c)).reshape(2, n1, n2, c)
        y = _spectral(x, kf, tabs, conv=True)
        return _rowmix(tabs['m_inv'], y.reshape(2 * n1, n2 * c)).reshape(2, n, c)
    x = jnp.concatenate([v, jnp.zeros_like(v)], axis=1).reshape(2, 1, n2, c)
    return _spectral(x, kf, tabs, conv=True).reshape(2, n2, c)[:, :n]


def _gate_kernel(g_ref, y_ref, v_ref, s_ref, o_ref):
    o_ref[...] = g_ref[...] * (y_ref[...] + s_ref[...] * v_ref[...])


def _hyena_gate(gate, y, v, skip):
    b, n, c = v.shape
    tt = min(2048, n)
    tok = pl.BlockSpec((None, tt, c), lambda bi, ti: (bi, ti, 0))
    return pl.pallas_call(_gate_kernel, grid=(b, n // tt),
                          in_specs=[tok, tok, tok, pl.BlockSpec((1, c), lambda bi, ti: (0, 0))], out_specs=tok,
                          out_shape=jax.ShapeDtypeStruct(v.shape, F32), compiler_params=_params(2),
                          name="hyena_gate")(gate, y, v, skip)


def _hyena_branch(hy, p, tabs):
    n = hy.shape[1]
    x1, x2, v = _token_conv(_short_conv_kernel, hy, (p['hyena_short_w'], p['hyena_short_b'][None, :]),
                            (HYENA_WIDTH,) * 3, "hyena_short_conv")
    taps = _hyena_taps(n, p)
    for o, gate in enumerate((x1, x2)):
        v = _hyena_gate(gate, _long_conv(v, _filter_spectrum(taps[o], tabs), tabs), v, p['hyena_skip'][o][None, :])
    return v


def _adaln_kernel(c_ref, w_ref, b_ref, o_ref):
    s = c_ref[...]
    s = s * jax.nn.sigmoid(s)
    o_ref[...] = _dot_split(*_split_bf16(s), *_split_bf16(w_ref[...])) + b_ref[...]


def _adaln(cond, w, b):
    d, width = w.shape
    ct = width // 6
    return pl.pallas_call(
        _adaln_kernel, grid=(6,),
        in_specs=[pl.BlockSpec(cond.shape, lambda i: (0, 0)), pl.BlockSpec((d, ct), lambda i: (0, i)),
                  pl.BlockSpec((1, ct), lambda i: (0, i))],
        out_specs=pl.BlockSpec((cond.shape[0], ct), lambda i: (0, i)),
        out_shape=jax.ShapeDtypeStruct((cond.shape[0], width), F32), compiler_params=_params(1), name="adaln",
    )(cond, w, b[None, :])


def _mod_rows(mod, norm_mix_g, norm_ffn_g, final_g, batch):
    sh1, sc1, g1, sh2, sc2, g2 = jnp.split(mod, 6, axis=-1)
    rows = jnp.stack([norm_mix_g * (1.0 + sc1), sh1, g1, norm_ffn_g * (1.0 + sc2), sh2, g2,
                      jnp.broadcast_to(final_g, g1.shape), jnp.zeros_like(g1)], axis=1)
    return jnp.broadcast_to(rows, (batch,) + rows.shape[1:])


def kernel(x, c, ctx, c_ctx, ada_w, ada_b, norm_mix_g, norm_ffn_g, w_in, diff_lambda, diff_subln_g, hyena_short_w, hyena_short_b, filt_w1, filt_b1, filt_freq, filt_w2, filt_b2, filt_w3, hyena_skip, conf_dw_w, conf_ln_g, conf_ln_b, mla_q_norm_g, mla_kv_norm_g, mla_w_uq, mla_w_ukv, w_branch, w_out, w_router, w_exp_in, w_exp_out, final_norm_g):
    depth = w_in.shape[0]
    batch, n_lat, d = x.shape
    n_ctx = ctx.shape[1]
    rope_lat = _rope_operands(n_lat, identity=False)
    rope_ctx = _rope_operands(n_ctx, identity=True)
    dft_lat, dft_ctx = _dft_tables(2 * n_lat), _dft_tables(2 * n_ctx)
    cond = jnp.concatenate([c, c_ctx[None], jnp.zeros((MOD_ROWS - batch - 1, d), F32)], axis=0)
    tile_lat, tile_ctx = min(512, n_lat), min(256, n_ctx)
    h_lat, h_ctx = x, ctx
    for l in range(depth):
        last = l == depth - 1
        p = dict(w_in=w_in[l], diff_subln_g=diff_subln_g[l], hyena_short_w=hyena_short_w[l],
                 hyena_short_b=hyena_short_b[l], filt_w1=filt_w1[l], filt_b1=filt_b1[l], filt_freq=filt_freq[l],
                 filt_w2=filt_w2[l], filt_b2=filt_b2[l], filt_w3=filt_w3[l], hyena_skip=hyena_skip[l],
                 conf_dw_w=conf_dw_w[l], conf_ln_g=conf_ln_g[l], conf_ln_b=conf_ln_b[l],
                 mla_q_norm_g=mla_q_norm_g[l], mla_kv_norm_g=mla_kv_norm_g[l], mla_w_uq=mla_w_uq[l],
                 mla_w_ukv=mla_w_ukv[l], w_branch=w_branch[l], w_out=w_out[l], w_router=w_router[l],
                 w_exp_in=w_exp_in[l], w_exp_out=w_exp_out[l])
        ada = _adaln(cond, ada_w[l], ada_b[l])
        mod_lat = _mod_rows(ada[:batch], norm_mix_g[l], norm_ffn_g[l], final_norm_g, batch)
        mod_ctx = _mod_rows(ada[batch:batch + 1], norm_mix_g[l], norm_ffn_g[l], final_norm_g, batch)
        lam_init = 0.8 - 0.6 * math.exp(-0.3 * l)
        lq1, lk1, lq2, lk2 = diff_lambda[l].astype(F32)
        lam = jnp.reshape(jnp.exp(jnp.sum(lq1 * lk1)) - jnp.exp(jnp.sum(lq2 * lk2)) + lam_init, (1,))
        w_inp, w_mrg = _inproj_weights(p), _merge_weights(p, lam_init)

        qdT_l, kd_l, vdT_l, qmT_l, km_l, vmT_l, hy_l, glu_l = _inproj(h_lat, mod_lat, w_inp, rope_lat, tile=tile_lat)
        qdT_c, kd_c, vdT_c, qmT_c, km_c, vmT_c, hy_c, glu_c = _inproj(h_ctx, mod_ctx, w_inp, rope_ctx, tile=tile_ctx)
        a_lat = _flash_attention(lam, qdT_l, kd_c, vdT_c, kd_l, vdT_l, n_maps=2, tq=min(256, n_lat))
        m_lat = _flash_attention(lam, qmT_l, km_c, vmT_c, km_l, vmT_l, n_maps=1, tq=min(512, n_lat))
        h_lat, u2_lat, lg_lat = _merge(h_lat, mod_lat, a_lat, _hyena_branch(hy_l, p, dft_lat), _conformer_branch(glu_l, p),
                                       m_lat, w_mrg, tile=min(256, n_lat))
        w_ei, w_eo = p['w_exp_in'].astype(BF16), p['w_exp_out'].astype(BF16)
        h_lat = _expert_choice_ffn(h_lat, mod_lat, u2_lat, lg_lat, w_ei, w_eo, final_norm=last)
        if not last:
            a_ctx = _flash_attention(lam, qdT_c, kd_c, vdT_c, None, None, n_maps=2, tq=n_ctx)
            m_ctx = _flash_attention(lam, qmT_c, km_c, vmT_c, None, None, n_maps=1, tq=n_ctx)
            h_ctx, u2_ctx, lg_ctx = _merge(h_ctx, mod_ctx, a_ctx, _hyena_branch(hy_c, p, dft_ctx),
                                           _conformer_branch(glu_c, p), m_ctx, w_mrg, tile=tile_ctx)
            h_ctx = _expert_choice_ffn(h_ctx, mod_ctx, u2_ctx, lg_ctx, w_ei, w_eo)
    return h_lat
```

```python
import functools
import math

import jax
import jax.numpy as jnp
from jax import lax
from jax.experimental import pallas as pl
from jax.experimental.pallas import tpu as pltpu

GRID_W = 64
ROPE_BASE = 10000.0
EPS = 1e-6

DIFF_HEADS = 4
DIFF_HEAD_DIM = 64
DIFF_V_DIM = 2 * DIFF_HEAD_DIM
HYENA_WIDTH = 256
HYENA_ORDER = 2
FILT_EMB = 33
DECAY_TARGET = 1e-2
FAST_DECAY = 0.3
SLOW_DECAY = 1.5
CONF_WIDTH = 256
MLA_HEADS = 4
MLA_Q_RANK = 256
MLA_KV_RANK = 128
MLA_NOPE = 64
MLA_ROPE = 32
MLA_V = 64
MLA_SCALE = (MLA_NOPE + MLA_ROPE) ** -0.5
N_BRANCH = 4
N_EXPERTS = 16
EC_CAPACITY = 2

DIFF_QK_W = DIFF_HEADS * 2 * DIFF_HEAD_DIM
DIFF_V_W = DIFF_HEADS * DIFF_V_DIM
HYENA_PROJ = (HYENA_ORDER + 1) * HYENA_WIDTH
CONF_PROJ = 2 * CONF_WIDTH
IN_SPLITS = (DIFF_QK_W, DIFF_QK_W, DIFF_V_W, HYENA_PROJ, CONF_PROJ, MLA_Q_RANK, MLA_KV_RANK, MLA_ROPE)
BRANCH_WIDTHS = (DIFF_V_W, HYENA_WIDTH, CONF_WIDTH, MLA_HEADS * MLA_V)

HEAD_LANES = 128
DIFF_V_PAD = 16
ATT_W = DIFF_HEADS * HEAD_LANES
LOG2E = 1.4426950408889634
VMEM_LIMIT_BYTES = 48 * 1024 * 1024
MOD_ROWS = 8

F32 = jnp.float32
BF16 = jnp.bfloat16
_NT = (((1,), (1,)), ((), ()))


def _params(n_axes, vmem=VMEM_LIMIT_BYTES):
    return pltpu.CompilerParams(dimension_semantics=("arbitrary",) * n_axes, vmem_limit_bytes=vmem)


def _flash_kernel(lam_ref, qT_ref, kc_ref, vcT_ref, *rest, n_maps, n_lat_chunks, tk, sum_row):
    if n_lat_chunks:
        kl_ref, vlT_ref, o_ref, acc_ref, m_ref, q2_ref, s_ref = rest
    else:
        o_ref, acc_ref, m_ref, q2_ref = rest
    qT = qT_ref[...]
    tq = qT.shape[1]
    if n_maps == 2:
        row = lax.broadcasted_iota(jnp.int32, qT.shape, 0)
        zero = jnp.zeros_like(qT)
        q2_ref[:, :tq] = jnp.where(row < DIFF_HEAD_DIM, qT, zero)
        q2_ref[:, tq:] = jnp.where(row >= DIFF_HEAD_DIM, qT, zero)
    else:
        q2_ref[...] = qT
    m_ref[...] = jnp.full(m_ref.shape, -jnp.inf, F32)
    acc_ref[...] = jnp.zeros(acc_ref.shape, F32)

    def scores(k):
        return jnp.dot(k, q2_ref[...], preferred_element_type=F32)

    def absorb(s, vT):
        m_prev = m_ref[...]
        m_new = jnp.maximum(m_prev, jnp.max(s, axis=0, keepdims=True))
        alpha = jnp.exp2(m_prev - m_new)
        p = jnp.exp2(s - m_new).astype(BF16)
        acc_ref[...] = alpha * acc_ref[...] + jnp.dot(vT, p, preferred_element_type=F32)
        m_ref[...] = m_new

    def chunk(c):
        return pl.ds(c * tk if isinstance(c, int) else pl.multiple_of(c * tk, tk), tk)

    def keys(c):
        return kl_ref[chunk(c), :]

    def values_t(c):
        return vlT_ref[:, chunk(c)]

    absorb(scores(kc_ref[...]), vcT_ref[...])
    if n_lat_chunks:
        s_ref[0] = scores(keys(0))

        def pair(j, carry):
            c = 2 * j
            s_ref[1] = scores(keys(c + 1))
            absorb(s_ref[0], values_t(c))
            s_ref[0] = scores(keys(c + 2))
            absorb(s_ref[1], values_t(c + 1))
            return carry
        lax.fori_loop(0, n_lat_chunks // 2 - 1, pair, 0)
        s_ref[1] = scores(keys(n_lat_chunks - 1))
        absorb(s_ref[0], values_t(n_lat_chunks - 2))
        absorb(s_ref[1], values_t(n_lat_chunks - 1))
    o = acc_ref[0:HEAD_LANES, :] / acc_ref[sum_row:sum_row + 1, :]
    if n_maps == 2:
        o = o[:, :tq] - lam_ref[0] * o[:, tq:]
        o = o * lax.rsqrt(jnp.mean(o * o, axis=0, keepdims=True) + EPS)
    o_ref[...] = o.T.astype(BF16)


def _flash_stream_kernel(lam_ref, qT_ref, kc_ref, vcT_ref, kl_ref, vlT_ref, o_ref, acc_ref, m_ref, q2_ref, sc_ref,
                         s_ref, *, n_maps, n_lat_chunks, tk, tq, sum_row):
    n_q = qT_ref.shape[1] // tq

    def load_queries(qi):
        qT = qT_ref[:, pl.ds(pl.multiple_of(qi * tq, tq), tq)]
        if n_maps == 2:
            row = lax.broadcasted_iota(jnp.int32, qT.shape, 0)
            zero = jnp.zeros_like(qT)
            q2_ref[:, :tq] = jnp.where(row < DIFF_HEAD_DIM, qT, zero)
            q2_ref[:, tq:] = jnp.where(row >= DIFF_HEAD_DIM, qT, zero)
        else:
            q2_ref[...] = qT

    def scores(k):
        return jnp.dot(k, q2_ref[...], preferred_element_type=F32)

    def absorb(s, vT):
        m_prev = m_ref[...]
        m_new = jnp.maximum(m_prev, jnp.max(s, axis=0, keepdims=True))
        alpha = jnp.exp2(m_prev - m_new)
        p = jnp.exp2(s - m_new).astype(BF16)
        acc_ref[...] = alpha * acc_ref[...] + jnp.dot(vT, p, preferred_element_type=F32)
        m_ref[...] = m_new

    load_queries(0)
    sc_ref[...] = scores(kc_ref[...])

    def query_block(qi, carry):
        m_ref[...] = jnp.full(m_ref.shape, -jnp.inf, F32)
        acc_ref[...] = jnp.zeros(acc_ref.shape, F32)
        def chunk(c):
            return pl.ds(c * tk if isinstance(c, int) else pl.multiple_of(c * tk, tk), tk)

        s_ref[0] = scores(kl_ref[chunk(0), :])
        absorb(sc_ref[...], vcT_ref[...])

        def pair(j, inner):
            c = 2 * j
            s_ref[1] = scores(kl_ref[chunk(c + 1), :])
            absorb(s_ref[0], vlT_ref[:, chunk(c)])
            s_ref[0] = scores(kl_ref[chunk(c + 2), :])
            absorb(s_ref[1], vlT_ref[:, chunk(c + 1)])
            return inner
        lax.fori_loop(0, n_lat_chunks // 2 - 1, pair, 0)
        s_ref[1] = scores(kl_ref[chunk(n_lat_chunks - 1), :])
        absorb(s_ref[0], vlT_ref[:, chunk(n_lat_chunks - 2)])
        load_queries(jnp.minimum(qi + 1, n_q - 1))
        sc_ref[...] = scores(kc_ref[...])
        absorb(s_ref[1], vlT_ref[:, chunk(n_lat_chunks - 1)])
        o = acc_ref[0:HEAD_LANES, :] / acc_ref[sum_row:sum_row + 1, :]
        if n_maps == 2:
            o = o[:, :tq] - lam_ref[0] * o[:, tq:]
            o = o * lax.rsqrt(jnp.mean(o * o, axis=0, keepdims=True) + EPS)
        o_ref[pl.ds(pl.multiple_of(qi * tq, tq), tq), :] = o.T.astype(BF16)
        return carry
    lax.fori_loop(0, n_q, query_block, 0)


def _flash_attention(lam, qT, kc, vcT, kl, vlT, *, n_maps, tq):
    b, _, s = qT.shape
    lc = kc.shape[1]
    mv = vcT.shape[2]
    sum_row = HEAD_LANES if n_maps == 2 else MLA_V
    r = n_maps * tq
    if kl is not None:
        sl = kl.shape[1]
        tk = _lat_chunk(sl)
        n_lat_chunks = sl // tk
        assert n_lat_chunks * tk == sl and s % tq == 0
        return pl.pallas_call(
            functools.partial(_flash_stream_kernel, n_maps=n_maps, n_lat_chunks=n_lat_chunks, tk=tk, tq=tq,
                              sum_row=sum_row),
            grid=(b, DIFF_HEADS),
            in_specs=[pl.BlockSpec(memory_space=pltpu.SMEM),
                      pl.BlockSpec((None, HEAD_LANES, s), lambda bi, hi: (bi, hi, 0)),
                      pl.BlockSpec((None, lc, HEAD_LANES), lambda bi, hi: (bi, 0, hi)),
                      pl.BlockSpec((None, None, mv, lc), lambda bi, hi: (bi, hi, 0, 0)),
                      pl.BlockSpec((None, sl, HEAD_LANES), lambda bi, hi: (bi, 0, hi)),
                      pl.BlockSpec((None, None, mv, sl), lambda bi, hi: (bi, hi, 0, 0))],
            out_specs=pl.BlockSpec((None, s, HEAD_LANES), lambda bi, hi: (bi, 0, hi)),
            out_shape=jax.ShapeDtypeStruct((b, s, ATT_W), BF16),
            scratch_shapes=[pltpu.VMEM((mv, r), F32), pltpu.VMEM((1, r), F32), pltpu.VMEM((HEAD_LANES, r), BF16),
                            pltpu.VMEM((lc, r), F32), pltpu.VMEM((2, tk, r), F32)],
            compiler_params=_params(2, 56 * 1024 * 1024),
            name=f"flash_attention_{n_maps}map",
        )(lam, qT, kc, vcT, kl, vlT)
    in_specs = [
        pl.BlockSpec(memory_space=pltpu.SMEM),
        pl.BlockSpec((None, HEAD_LANES, tq), lambda bi, hi, qi: (bi, hi, qi)),
        pl.BlockSpec((None, lc, HEAD_LANES), lambda bi, hi, qi: (bi, 0, hi)),
        pl.BlockSpec((None, None, mv, lc), lambda bi, hi, qi: (bi, hi, 0, 0)),
    ]
    args = [lam, qT, kc, vcT]
    scratch = [pltpu.VMEM((mv, r), F32), pltpu.VMEM((1, r), F32), pltpu.VMEM((HEAD_LANES, r), BF16)]
    n_lat_chunks, tk = 0, 0
    if kl is not None:
        sl = kl.shape[1]
        tk = _lat_chunk(sl)
        n_lat_chunks = sl // tk
        assert n_lat_chunks % 2 == 0 and n_lat_chunks * tk == sl
        scratch.append(pltpu.VMEM((2, tk, r), F32))
        in_specs += [
            pl.BlockSpec((None, sl, HEAD_LANES), lambda bi, hi, qi: (bi, 0, hi)),
            pl.BlockSpec((None, None, mv, sl), lambda bi, hi, qi: (bi, hi, 0, 0)),
        ]
        args += [kl, vlT]
    return pl.pallas_call(
        functools.partial(_flash_kernel, n_maps=n_maps, n_lat_chunks=n_lat_chunks, tk=tk, sum_row=sum_row),
        grid=(b, DIFF_HEADS, s // tq),
        in_specs=in_specs,
        out_specs=pl.BlockSpec((None, tq, HEAD_LANES), lambda bi, hi, qi: (bi, qi, hi)),
        out_shape=jax.ShapeDtypeStruct((b, s, ATT_W), BF16),
        scratch_shapes=scratch,
        compiler_params=_params(3),
        name=f"flash_attention_{n_maps}map",
    )(*args)


def _lat_chunk(s):
    return min(1024, s // 2)


W_NAT_SPLITS = (DIFF_QK_W, HYENA_PROJ, CONF_PROJ, MLA_Q_RANK, MLA_KV_RANK, HEAD_LANES)


def _modulated_norm(h, a, shift):
    return h * lax.rsqrt(jnp.mean(h * h, axis=-1, keepdims=True) + EPS) * a + shift


def _rope_lanes(x, tab_ref, shift):
    return (x * tab_ref[0] + pltpu.roll(x, shift, 1) * tab_ref[1]
            + pltpu.roll(x, HEAD_LANES - shift, 1) * tab_ref[2])


def _inproj_kernel(h_ref, mod_ref, wnat_ref, wT_ref, wuqT_ref, wukvk_ref, wuvT_ref, gq_ref, gkv_ref,
                   ropeT_d_ref, rope_kd_ref, ropeT_m_ref, rope_km_ref,
                   qdT_ref, kd_ref, vdT_ref, qmT_ref, km_ref, vmT_ref, hy_ref, glu_ref):
    u = _modulated_norm(h_ref[...], mod_ref[0:1, :], mod_ref[1:2, :]).astype(BF16)
    z = jnp.dot(u, wnat_ref[...], preferred_element_type=F32)
    zT = lax.dot_general(wT_ref[...], u, _NT, preferred_element_type=F32)
    offs = [0]
    for w in W_NAT_SPLITS:
        offs.append(offs[-1] + w)
    dk, hy, cf, cq, ckv, krp = (z[:, offs[i]:offs[i + 1]] for i in range(len(W_NAT_SPLITS)))

    for hd in range(DIFF_HEADS):
        sl = slice(hd * HEAD_LANES, (hd + 1) * HEAD_LANES)
        kd_ref[:, sl] = _rope_lanes(dk[:, sl], rope_kd_ref, DIFF_HEAD_DIM // 2).astype(BF16)
    cos_d, sin_d = ropeT_d_ref[0], ropeT_d_ref[1]
    half = DIFF_HEAD_DIM // 2
    for g in range(2 * DIFF_HEADS):
        x1 = zT[g * DIFF_HEAD_DIM:g * DIFF_HEAD_DIM + half]
        x2 = zT[g * DIFF_HEAD_DIM + half:(g + 1) * DIFF_HEAD_DIM]
        qdT_ref[g * DIFF_HEAD_DIM:g * DIFF_HEAD_DIM + half, :] = (x1 * cos_d - x2 * sin_d).astype(BF16)
        qdT_ref[g * DIFF_HEAD_DIM + half:(g + 1) * DIFF_HEAD_DIM, :] = (x1 * sin_d + x2 * cos_d).astype(BF16)
    tail = jnp.where(lax.broadcasted_iota(jnp.int32, (DIFF_V_PAD, zT.shape[1]), 0) == 0, 1.0, 0.0).astype(BF16)
    for hd in range(DIFF_HEADS):
        r0 = DIFF_QK_W + hd * DIFF_V_DIM
        vdT_ref[hd, 0:DIFF_V_DIM, :] = zT[r0:r0 + DIFF_V_DIM].astype(BF16)
        vdT_ref[hd, DIFF_V_DIM:, :] = tail

    hy_ref[...] = hy
    glu_ref[...] = cf[:, :CONF_WIDTH] * jax.nn.sigmoid(cf[:, CONF_WIDTH:])

    cqn = (cq * lax.rsqrt(jnp.mean(cq * cq, axis=-1, keepdims=True) + EPS) * gq_ref[...]).astype(BF16)
    ckvn = (ckv * lax.rsqrt(jnp.mean(ckv * ckv, axis=-1, keepdims=True) + EPS) * gkv_ref[...]).astype(BF16)
    qT = lax.dot_general(wuqT_ref[...], cqn, _NT, preferred_element_type=F32)
    cos_m, sin_m = ropeT_m_ref[0], ropeT_m_ref[1]
    hr = MLA_ROPE // 2
    for hd in range(MLA_HEADS):
        base = hd * HEAD_LANES
        r1 = base + MLA_NOPE
        x1, x2 = qT[r1:r1 + hr], qT[r1 + hr:r1 + 2 * hr]
        qmT_ref[base:r1, :] = qT[base:r1].astype(BF16)
        qmT_ref[r1:r1 + hr, :] = (x1 * cos_m - x2 * sin_m).astype(BF16)
        qmT_ref[r1 + hr:r1 + 2 * hr, :] = (x1 * sin_m + x2 * cos_m).astype(BF16)
        qmT_ref[r1 + 2 * hr:base + HEAD_LANES, :] = jnp.zeros((HEAD_LANES - MLA_NOPE - MLA_ROPE, qT.shape[1]), BF16)
    kn = jnp.dot(ckvn, wukvk_ref[...], preferred_element_type=F32)
    kr = _rope_lanes(krp, rope_km_ref, hr)
    for hd in range(MLA_HEADS):
        sl = slice(hd * HEAD_LANES, (hd + 1) * HEAD_LANES)
        km_ref[:, sl] = (kn[:, sl] + kr).astype(BF16)
    vT = lax.dot_general(wuvT_ref[...], ckvn, _NT, preferred_element_type=F32)
    ones_row = lax.broadcasted_iota(jnp.int32, vT.shape, 0) % HEAD_LANES == MLA_V
    vT = jnp.where(ones_row, 1.0, vT).astype(BF16)
    for hd in range(MLA_HEADS):
        vmT_ref[hd] = vT[hd * HEAD_LANES:(hd + 1) * HEAD_LANES]


def _inproj(h, mod, wts, rope, *, tile):
    b, n, d = h.shape
    const2 = lambda bi, ti: (0, 0)
    tok = lambda w: pl.BlockSpec((None, tile, w), lambda bi, ti: (bi, ti, 0))
    tokT = lambda w: pl.BlockSpec((None, w, tile), lambda bi, ti: (bi, 0, ti))
    full = lambda a: pl.BlockSpec(a.shape, const2)
    in_specs = [tok(d), pl.BlockSpec((None, MOD_ROWS, d), lambda bi, ti: (bi, 0, 0))]
    in_specs += [full(wts[k]) for k in ('w_nat', 'w_T', 'w_uqT', 'w_ukvk', 'w_uvT', 'gq', 'gkv')]
    in_specs += [pl.BlockSpec((2, DIFF_HEAD_DIM // 2, tile), lambda bi, ti: (0, 0, ti)),
                 pl.BlockSpec((3, tile, HEAD_LANES), lambda bi, ti: (0, ti, 0)),
                 pl.BlockSpec((2, MLA_ROPE // 2, tile), lambda bi, ti: (0, 0, ti)),
                 pl.BlockSpec((3, tile, HEAD_LANES), lambda bi, ti: (0, ti, 0))]
    sds = jax.ShapeDtypeStruct
    vrows_d, vrows_m = DIFF_V_DIM + DIFF_V_PAD, HEAD_LANES
    headsT = lambda rows: pl.BlockSpec((None, DIFF_HEADS, rows, tile), lambda bi, ti: (bi, 0, 0, ti))
    out_shape = (sds((b, ATT_W, n), BF16), sds((b, n, ATT_W), BF16), sds((b, DIFF_HEADS, vrows_d, n), BF16),
                 sds((b, ATT_W, n), BF16), sds((b, n, ATT_W), BF16), sds((b, MLA_HEADS, vrows_m, n), BF16),
                 sds((b, n, HYENA_PROJ), F32), sds((b, n, CONF_WIDTH), F32))
    out_specs = (tokT(ATT_W), tok(ATT_W), headsT(vrows_d), tokT(ATT_W), tok(ATT_W), headsT(vrows_m),
                 tok(HYENA_PROJ), tok(CONF_WIDTH))
    return pl.pallas_call(
        _inproj_kernel, grid=(b, n // tile), in_specs=in_specs, out_specs=out_specs, out_shape=out_shape,
        compiler_params=_params(2), name="inproj",
    )(h, mod, wts['w_nat'], wts['w_T'], wts['w_uqT'], wts['w_ukvk'], wts['w_uvT'], wts['gq'], wts['gkv'],
      rope['T_d'], rope['k_d'], rope['T_m'], rope['k_m'])


def _pad_heads(w, width):
    rows = w.shape[0]
    w = w.reshape(rows, MLA_HEADS, width)
    return jnp.pad(w, ((0, 0), (0, 0), (0, HEAD_LANES - width))).reshape(rows, ATT_W)


def _inproj_weights(p):
    d = p['w_in'].shape[0]
    dq, dk, dv, hy, cf, cq, ckv, kr = _split(p['w_in'][:, :sum(IN_SPLITS)], IN_SPLITS)
    krp = jnp.zeros((d, HEAD_LANES), F32).at[:, MLA_NOPE:MLA_NOPE + MLA_ROPE].set(kr)
    w_ukv = p['mla_w_ukv'].reshape(MLA_KV_RANK, MLA_HEADS, MLA_NOPE + MLA_V)
    return dict(
        w_nat=jnp.concatenate([dk, hy, cf, cq, ckv, krp], axis=1).astype(BF16),
        w_T=jnp.concatenate([dq * (DIFF_HEAD_DIM ** -0.5 * LOG2E), dv], axis=1).T.astype(BF16),
        w_uqT=_pad_heads(p['mla_w_uq'] * (MLA_SCALE * LOG2E), MLA_NOPE + MLA_ROPE).T.astype(BF16),
        w_ukvk=_pad_heads(w_ukv[:, :, :MLA_NOPE].reshape(MLA_KV_RANK, -1), MLA_NOPE).astype(BF16),
        w_uvT=_pad_heads(w_ukv[:, :, MLA_NOPE:].reshape(MLA_KV_RANK, -1), MLA_V).T.astype(BF16),
        gq=p['mla_q_norm_g'][None, :], gkv=p['mla_kv_norm_g'][None, :])


def _rope_tables(n_tok, rot_dim):
    rows = n_tok // GRID_W
    row = jnp.repeat(jnp.arange(rows), GRID_W).astype(F32)
    col = jnp.tile(jnp.arange(GRID_W), rows).astype(F32)
    nf = rot_dim // 4
    inv = ROPE_BASE ** (-jnp.arange(nf, dtype=F32) / nf)
    ang = jnp.concatenate([row[:, None] * inv, col[:, None] * inv], axis=-1)
    return jnp.cos(ang), jnp.sin(ang)


def _rope_operands(n_tok, identity):
    if identity:
        cos_d, sin_d = jnp.ones((n_tok, DIFF_HEAD_DIM // 2), F32), jnp.zeros((n_tok, DIFF_HEAD_DIM // 2), F32)
        cos_m, sin_m = jnp.ones((n_tok, MLA_ROPE // 2), F32), jnp.zeros((n_tok, MLA_ROPE // 2), F32)
    else:
        cos_d, sin_d = _rope_tables(n_tok, DIFF_HEAD_DIM)
        cos_m, sin_m = _rope_tables(n_tok, MLA_ROPE)
    z_d, z_m = jnp.zeros_like(sin_d), jnp.zeros_like(sin_m)
    two = lambda a, bb: jnp.tile(jnp.concatenate([a, bb], axis=1), (1, 2))
    lo, hi = jnp.zeros((n_tok, MLA_NOPE), F32), jnp.zeros((n_tok, HEAD_LANES - MLA_NOPE - MLA_ROPE), F32)
    mid = lambda a, bb: jnp.concatenate([lo, a, bb, hi], axis=1)
    return dict(T_d=jnp.stack([cos_d.T, sin_d.T]), T_m=jnp.stack([cos_m.T, sin_m.T]),
                k_d=jnp.stack([two(cos_d, cos_d), two(z_d, sin_d), two(-sin_d, z_d)]),
                k_m=jnp.stack([mid(cos_m, cos_m), mid(z_m, sin_m), mid(-sin_m, z_m)]))


def _merge_kernel(h_ref, mod_ref, a_ref, hy_ref, cf_ref, m_ref, wg_ref, wbd_ref, wbh_ref, wbc_ref, wbm_ref,
                  wo_ref, wrh_ref, wrl_ref, hn_ref, u2_ref, lg_ref):
    h = h_ref[...]
    d = h.shape[1]
    u = _modulated_norm(h, mod_ref[0:1, :], mod_ref[1:2, :]).astype(BF16)
    gates = jax.nn.sigmoid(jnp.dot(u, wg_ref[...], preferred_element_type=F32))
    dot = lambda x, w_ref: jnp.dot(x, w_ref[...], preferred_element_type=F32)
    acc = gates[:, :d] * dot(a_ref[...], wbd_ref)
    acc += gates[:, d:2 * d] * dot(hy_ref[...].astype(BF16), wbh_ref)
    acc += gates[:, 2 * d:3 * d] * dot(cf_ref[...].astype(BF16), wbc_ref)
    acc += gates[:, 3 * d:] * dot(m_ref[...], wbm_ref)
    hn = h + mod_ref[2:3, :] * dot(acc.astype(BF16), wo_ref)
    hn_ref[...] = hn
    u2 = _modulated_norm(hn, mod_ref[3:4, :], mod_ref[4:5, :])
    u2h = u2.astype(BF16)
    u2l = (u2 - u2h.astype(F32)).astype(BF16)
    u2_ref[...] = u2h
    lg_ref[...] = dot(u2h, wrh_ref) + (dot(u2l, wrh_ref) + dot(u2h, wrl_ref))


def _merge(h, mod, a, hyv, cfv, m, wts, *, tile):
    b, n, d = h.shape
    const2 = lambda bi, ti: (0, 0)
    tok = lambda w: pl.BlockSpec((None, tile, w), lambda bi, ti: (bi, ti, 0))
    names = ('w_gate', 'w_bd', 'w_bh', 'w_bc', 'w_bm', 'w_out', 'w_rh', 'w_rl')
    in_specs = [tok(d), pl.BlockSpec((None, MOD_ROWS, d), lambda bi, ti: (bi, 0, 0)),
                tok(ATT_W), tok(HYENA_WIDTH), tok(CONF_WIDTH), tok(ATT_W)]
    in_specs += [pl.BlockSpec(wts[k].shape, const2) for k in names]
    sds = jax.ShapeDtypeStruct
    return pl.pallas_call(
        _merge_kernel, grid=(b, n // tile), in_specs=in_specs,
        out_specs=(tok(d), tok(d), tok(HEAD_LANES)),
        out_shape=(sds((b, n, d), F32), sds((b, n, d), BF16), sds((b, n, HEAD_LANES), F32)),
        compiler_params=_params(2, 56 * 1024 * 1024), name="merge",
    )(h, mod, a, hyv, cfv, m, *[wts[k] for k in names])


def _merge_weights(p, lam_init):
    d = p['w_out'].shape[0]
    wb_d, wb_h, wb_c, wb_m = (w.T for w in _split(p['w_branch'].T, BRANCH_WIDTHS))
    wb_d = wb_d * (jnp.tile(p['diff_subln_g'], DIFF_HEADS) * (1.0 - lam_init))[:, None]
    wb_m = jnp.pad(wb_m.reshape(MLA_HEADS, MLA_V, d), ((0, 0), (0, HEAD_LANES - MLA_V), (0, 0))).reshape(ATT_W, d)
    w_r = jnp.pad(p['w_router'], ((0, 0), (0, HEAD_LANES - N_EXPERTS)))
    w_rh = w_r.astype(BF16)
    return dict(w_gate=p['w_in'][:, sum(IN_SPLITS):].astype(BF16), w_bd=wb_d.astype(BF16), w_bh=wb_h.astype(BF16),
                w_bc=wb_c.astype(BF16), w_bm=wb_m.astype(BF16), w_out=p['w_out'].astype(BF16),
                w_rh=w_rh, w_rl=(w_r - w_rh.astype(F32)).astype(BF16))


SUB_TOKENS = 256
GATHER_WINDOW = 272
COMBINE_WINDOW = 384
ROUTE_MIN_ROWS = 8


def _excl_scan(x, lane, row):
    inc = x
    s = 1
    while s < HEAD_LANES:
        inc = inc + jnp.where(lane >= s, pltpu.roll(inc, s, 2), 0.0)
        s *= 2
    tot = jnp.sum(x, axis=2, keepdims=True) + jnp.zeros_like(x)
    off = tot
    s = 1
    while s < x.shape[1]:
        off = off + jnp.where(row >= s, pltpu.roll(off, s, 1), 0.0)
        s *= 2
    return inc - x + (off - tot)


def _route_kernel(lg_ref, pos_ref, aff_ref, *, n_valid, cap):
    lg = lg_ref[...]
    shape = lg.shape
    lane = lax.broadcasted_iota(jnp.int32, shape, 2)
    row = lax.broadcasted_iota(jnp.int32, shape, 1)
    e = jnp.exp(lg - jnp.max(lg, axis=0, keepdims=True))
    aff = e / jnp.sum(e, axis=0, keepdims=True)
    bits = jnp.where(row * HEAD_LANES + lane < n_valid, pltpu.bitcast(aff, jnp.int32), -1)

    def count(mask):
        c = jnp.sum(jnp.where(mask, 1.0, 0.0), axis=2, keepdims=True)
        return jnp.sum(c, axis=1, keepdims=True)

    def step(i, thr):
        cand = thr | (jnp.int32(1) << (30 - i))
        return jnp.where(count(bits >= cand) >= cap, cand, thr)
    thr = lax.fori_loop(0, 31, step, jnp.zeros((shape[0], 1, 1), jnp.int32))
    gt = bits > thr
    eq = bits == thr
    need = cap - count(gt)
    tie_rank = _excl_scan(jnp.where(eq, 1.0, 0.0), lane, row)
    sel = gt | (eq & (tie_rank < need))
    pos = _excl_scan(jnp.where(sel, 1.0, 0.0), lane, row)
    pos_ref[...] = jnp.where(sel, pos.astype(jnp.int32), -1)
    aff_ref[...] = aff


def _route(logits, cap):
    b, n, _ = logits.shape
    rows = max(ROUTE_MIN_ROWS, n // HEAD_LANES)
    lg = jnp.swapaxes(logits[..., :N_EXPERTS], 1, 2)
    lg = jnp.pad(lg, ((0, 0), (0, 0), (0, rows * HEAD_LANES - n))).reshape(b, N_EXPERTS, rows, HEAD_LANES)
    spec = pl.BlockSpec((None, N_EXPERTS, rows, HEAD_LANES), lambda bi: (bi, 0, 0, 0))
    pos, aff = pl.pallas_call(
        functools.partial(_route_kernel, n_valid=n, cap=cap), grid=(b,), in_specs=[spec], out_specs=(spec, spec),
        out_shape=(jax.ShapeDtypeStruct(lg.shape, jnp.int32), jax.ShapeDtypeStruct(lg.shape, F32)),
        compiler_params=_params(1), name="route",
    )(lg)
    flat = lambda a: a.reshape(b, N_EXPERTS, rows * HEAD_LANES)[..., :n]
    return flat(pos), flat(aff)


def _experts_kernel(base_ref, u_ref, mod_ref, aff_ref, pos_ref, win_ref, wout_ref, ye_ref, xe_ref, *,
                    n_sub, cap):
    bi, ei, kb = pl.program_id(0), pl.program_id(1), pl.program_id(2)
    d = u_ref.shape[1]

    @pl.when(kb == 0)
    def _():
        xe_ref[...] = jnp.zeros(xe_ref.shape, F32)

    slot = lax.broadcasted_iota(jnp.int32, (GATHER_WINDOW, SUB_TOKENS), 0)
    ones = jnp.ones((SUB_TOKENS, HEAD_LANES), BF16)
    for j in range(n_sub):
        tok = slice(j * SUB_TOKENS, (j + 1) * SUB_TOKENS)
        base = pl.multiple_of(base_ref[bi, ei, kb * n_sub + j], 16)
        match = slot == (pos_ref[:, tok] - base)
        onehot = jnp.where(match, 1.0, 0.0).astype(BF16)
        g = aff_ref[:, tok]
        g_hi = g.astype(BF16).astype(F32)
        sel_hi = jnp.where(match, g_hi, 0.0).astype(BF16)
        sel_lo = jnp.where(match, g - g_hi, 0.0).astype(BF16)
        rows = pl.ds(base, GATHER_WINDOW)
        xe_ref[rows, :d] += jnp.dot(onehot, u_ref[tok, :], preferred_element_type=F32)
        xe_ref[rows, d:d + HEAD_LANES] += jnp.dot(sel_hi, ones, preferred_element_type=F32)
        xe_ref[rows, d + HEAD_LANES:] += jnp.dot(sel_lo, ones, preferred_element_type=F32)

    @pl.when(kb == pl.num_programs(2) - 1)
    def _():
        f = wout_ref.shape[0]
        step = min(512, cap)
        for r0 in range(0, cap, step):
            x = xe_ref[r0:r0 + step, :d].astype(BF16)
            gate = xe_ref[r0:r0 + step, d:d + HEAD_LANES] + xe_ref[r0:r0 + step, d + HEAD_LANES:]
            hgu = jnp.dot(x, win_ref[...], preferred_element_type=F32)
            act = (jax.nn.silu(hgu[:, :f]) * hgu[:, f:]).astype(BF16)
            y = jnp.dot(act, wout_ref[...], preferred_element_type=F32)
            scale = jnp.concatenate([gate] * (d // HEAD_LANES), axis=1) * mod_ref[5:6, :]
            ye_ref[r0:r0 + step, :] = (y * scale).astype(BF16)
        ye_ref[cap:, :] = jnp.zeros((ye_ref.shape[0] - cap, ye_ref.shape[1]), BF16)


def _combine_kernel(base_ref, h_ref, mod_ref, posn_ref, ye_ref, hn_ref, *, n_sub, final_norm):
    bi, kb, ei = pl.program_id(0), pl.program_id(1), pl.program_id(2)

    @pl.when(ei == 0)
    def _():
        hn_ref[...] = h_ref[...]

    slot = lax.broadcasted_iota(jnp.int32, (SUB_TOKENS, COMBINE_WINDOW), 1)
    lane_e = lax.broadcasted_iota(jnp.int32, (SUB_TOKENS, N_EXPERTS), 1)
    for j in range(n_sub):
        tok = slice(j * SUB_TOKENS, (j + 1) * SUB_TOKENS)
        base = pl.multiple_of(base_ref[bi, ei, kb * n_sub + j], 16)
        rel = jnp.sum(jnp.where(lane_e == ei, posn_ref[tok, :], 0), axis=1, keepdims=True) - base
        onehot = jnp.where(slot == rel, 1.0, 0.0).astype(BF16)
        ye = ye_ref[pl.ds(base, COMBINE_WINDOW), :]
        hn_ref[tok, :] += jnp.dot(onehot, ye, preferred_element_type=F32)

    if final_norm:
        @pl.when(ei == pl.num_programs(2) - 1)
        def _():
            hn = hn_ref[...]
            hn_ref[...] = hn * lax.rsqrt(jnp.mean(hn * hn, axis=-1, keepdims=True) + EPS) * mod_ref[6:7, :]


def _expert_choice_ffn(h, mod, u2, logits, w_exp_in, w_exp_out, final_norm=False):
    b, n, d = u2.shape
    cap = max(1, EC_CAPACITY * n // N_EXPERTS)
    n_sub, n_sub_c = min(8, n // SUB_TOKENS), min(8, n // SUB_TOKENS)
    big, big_c = n_sub * SUB_TOKENS, n_sub_c * SUB_TOKENS
    n_big, n_big_c = n // big, n // big_c
    capp = cap + COMBINE_WINDOW
    pos, aff = _route(logits, cap)
    cnt = jnp.sum((pos >= 0).reshape(b, N_EXPERTS, n // SUB_TOKENS, SUB_TOKENS), axis=-1)
    base = (jnp.cumsum(cnt, axis=-1) - cnt) // 16 * 16
    base = base.astype(jnp.int32)
    f = w_exp_out.shape[1]
    ye = pl.pallas_call(
        functools.partial(_experts_kernel, n_sub=n_sub, cap=cap),
        grid_spec=pltpu.PrefetchScalarGridSpec(
            num_scalar_prefetch=1, grid=(b, N_EXPERTS, n_big),
            in_specs=[pl.BlockSpec((None, big, d), lambda bi, ei, kb, base_r: (bi, kb, 0)),
                      pl.BlockSpec((None, MOD_ROWS, d), lambda bi, ei, kb, base_r: (bi, 0, 0)),
                      pl.BlockSpec((None, None, 1, big), lambda bi, ei, kb, base_r: (bi, ei, 0, kb)),
                      pl.BlockSpec((None, None, 1, big), lambda bi, ei, kb, base_r: (bi, ei, 0, kb)),
                      pl.BlockSpec((None, d, 2 * f), lambda bi, ei, kb, base_r: (ei, 0, 0)),
                      pl.BlockSpec((None, f, d), lambda bi, ei, kb, base_r: (ei, 0, 0))],
            out_specs=pl.BlockSpec((None, None, capp, d), lambda bi, ei, kb, base_r: (bi, ei, 0, 0)),
            scratch_shapes=[pltpu.VMEM((capp, d + 2 * HEAD_LANES), F32)]),
        out_shape=jax.ShapeDtypeStruct((b, N_EXPERTS, capp, d), BF16),
        compiler_params=_params(3, 56 * 1024 * 1024), name="experts",
    )(base, u2, mod, aff.reshape(b, N_EXPERTS, 1, n), pos.reshape(b, N_EXPERTS, 1, n), w_exp_in, w_exp_out)
    posn = jnp.swapaxes(pos, 1, 2)
    return pl.pallas_call(
        functools.partial(_combine_kernel, n_sub=n_sub_c, final_norm=final_norm),
        grid_spec=pltpu.PrefetchScalarGridSpec(
            num_scalar_prefetch=1, grid=(b, n_big_c, N_EXPERTS),
            in_specs=[pl.BlockSpec((None, big_c, d), lambda bi, kb, ei, base_r: (bi, kb, 0)),
                      pl.BlockSpec((None, MOD_ROWS, d), lambda bi, kb, ei, base_r: (bi, 0, 0)),
                      pl.BlockSpec((None, big_c, N_EXPERTS), lambda bi, kb, ei, base_r: (bi, kb, 0)),
                      pl.BlockSpec((None, None, capp, d), lambda bi, kb, ei, base_r: (bi, ei, 0, 0))],
            out_specs=pl.BlockSpec((None, big_c, d), lambda bi, kb, ei, base_r: (bi, kb, 0))),
        out_shape=jax.ShapeDtypeStruct((b, n, d), F32),
        compiler_params=_params(3, 56 * 1024 * 1024), name="combine",
    )(base, h, mod, posn, ye)


def _split(z, sizes):
    out, start = [], 0
    for s in sizes:
        out.append(z[..., start:start + s])
        start += s
    return out


HALO = 16


def _fill_ext(ext_ref, x_ref, prev_ref, next_ref):
    ti, nt = pl.program_id(1), pl.num_programs(1)
    tt = x_ref.shape[0]
    ext_ref[0:HALO, :] = jnp.where(ti > 0, prev_ref[...], 0.0)
    ext_ref[HALO:HALO + tt, :] = x_ref[...]
    ext_ref[HALO + tt:, :] = jnp.where(ti < nt - 1, next_ref[...], 0.0)


def _taps(ext_ref, w_ref, tt):
    k = w_ref.shape[0]
    acc = None
    for j in range(k):
        start = HALO - k // 2 + j
        term = w_ref[j:j + 1, :] * ext_ref[start:start + tt, :]
        acc = term if acc is None else acc + term
    return acc


def _short_conv_kernel(x_ref, prev_ref, next_ref, w_ref, b_ref, x1_ref, x2_ref, v_ref, ext_ref):
    _fill_ext(ext_ref, x_ref, prev_ref, next_ref)
    y = _taps(ext_ref, w_ref, x_ref.shape[0]) + b_ref[...]
    x1_ref[...] = y[:, :HYENA_WIDTH]
    x2_ref[...] = y[:, HYENA_WIDTH:2 * HYENA_WIDTH]
    v_ref[...] = y[:, 2 * HYENA_WIDTH:]


def _conformer_kernel(x_ref, prev_ref, next_ref, w_ref, g_ref, b_ref, o_ref, ext_ref):
    _fill_ext(ext_ref, x_ref, prev_ref, next_ref)
    u = _taps(ext_ref, w_ref, x_ref.shape[0])
    mu = jnp.mean(u, axis=-1, keepdims=True)
    var = jnp.mean(jnp.square(u - mu), axis=-1, keepdims=True)
    y = (u - mu) * lax.rsqrt(var + EPS) * g_ref[...] + b_ref[...]
    o_ref[...] = y * jax.nn.sigmoid(y)


def _token_conv(body, x, consts, out_widths, name):
    b, n, w = x.shape
    tt = min(1024, n)
    per = tt // HALO
    last = n // HALO - 1
    in_specs = [pl.BlockSpec((None, tt, w), lambda bi, ti: (bi, ti, 0)),
                pl.BlockSpec((None, HALO, w), lambda bi, ti: (bi, jnp.maximum(ti * per - 1, 0), 0)),
                pl.BlockSpec((None, HALO, w), lambda bi, ti: (bi, jnp.minimum((ti + 1) * per, last), 0))]
    in_specs += [pl.BlockSpec(cst.shape, lambda bi, ti: (0, 0)) for cst in consts]
    outs = tuple(jax.ShapeDtypeStruct((b, n, ow), F32) for ow in out_widths)
    out_specs = tuple(pl.BlockSpec((None, tt, ow), lambda bi, ti: (bi, ti, 0)) for ow in out_widths)
    return pl.pallas_call(body, grid=(b, n // tt), in_specs=in_specs, out_specs=out_specs, out_shape=outs,
                          scratch_shapes=[pltpu.VMEM((tt + 2 * HALO, w), F32)],
                          compiler_params=_params(2), name=name)(x, x, x, *consts)


def _conformer_branch(glu, p):
    return _token_conv(_conformer_kernel, glu, (p['conf_dw_w'], p['conf_ln_g'][None, :], p['conf_ln_b'][None, :]),
                       (CONF_WIDTH,), "conformer")[0]


FILT_LANES = 128
DFT_SHORT = 256


def _split_bf16(x):
    hi = x.astype(BF16)
    return hi, (x - hi.astype(F32)).astype(BF16)


def _dot_split(ah, al, bh, bl):
    dot = lambda u, v: jnp.dot(u, v, preferred_element_type=F32)
    return dot(ah, bh) + (dot(al, bh) + dot(ah, bl))


def _dot_const(mh, ml, x):
    xb = x.astype(BF16)
    return jnp.dot(mh, xb, preferred_element_type=F32) + jnp.dot(ml, xb, preferred_element_type=F32)


def _filter_kernel(z_ref, w1h, w1l, b1, f1, w2h, w2l, b2, f2, w3h, w3l, dl_ref, h_ref, asum_ref, *,
                   tiles_per_dir):
    z = z_ref[...]
    hid = jnp.sin(f1[...] * (_dot_split(*_split_bf16(z), w1h[...], w1l[...]) + b1[...]))
    hid = jnp.sin(f2[...] * (_dot_split(*_split_bf16(hid), w2h[...], w2l[...]) + b2[...]))
    h = _dot_split(*_split_bf16(hid), w3h[...], w3l[...])
    h = h * jnp.exp(-z[:, 0:1] * dl_ref[...])
    h_ref[...] = h

    @pl.when(pl.program_id(0) % tiles_per_dir == 0)
    def _():
        asum_ref[...] = jnp.zeros(asum_ref.shape, F32)
    asum_ref[...] += jnp.sum(jnp.abs(h), axis=0, keepdims=True)


def _normalise_kernel(h_ref, asum_ref, *o_refs, n):
    tt = h_ref.shape[0]
    row = pl.program_id(0) * tt + lax.broadcasted_iota(jnp.int32, h_ref.shape, 0)
    k = jnp.where(row == n, 0.0, h_ref[...] / asum_ref[...])
    for o, o_ref in enumerate(o_refs):
        o_ref[...] = k[:, o * HYENA_WIDTH:(o + 1) * HYENA_WIDTH]


def _hyena_taps(n, p):
    t = jnp.linspace(0.0, 1.0, n, dtype=F32)[:, None]
    bands = (FILT_EMB - 1) // 2
    w = (2.0 * math.pi / n) * jnp.arange(n, dtype=F32)[:, None]
    f = jnp.linspace(1e-4, bands - 1, bands, dtype=F32)[None, :]
    z = jnp.concatenate([t, jnp.cos(f * w), -jnp.sin(f * w), jnp.zeros((n, FILT_LANES - FILT_EMB), F32)], axis=-1)
    hid = p['filt_w2'].shape[0]
    padc = lambda a: jnp.pad(a, ((0, 0), (0, FILT_LANES - a.shape[1])))
    padr = lambda a: jnp.pad(a, ((0, FILT_LANES - a.shape[0]), (0, 0)))
    w1, w2, w3 = padc(padr(p['filt_w1'])), padc(padr(p['filt_w2'])), padr(p['filt_w3'])
    b1, b2 = padc(p['filt_b1'][None, :]), padc(p['filt_b2'][None, :])
    f1, f2 = padc(p['filt_freq'][0][None, :]), padc(p['filt_freq'][1][None, :])
    deltas = jnp.abs(jnp.linspace(math.log(DECAY_TARGET) / SLOW_DECAY, math.log(DECAY_TARGET) / FAST_DECAY,
                                  HYENA_WIDTH, dtype=F32))
    width = HYENA_ORDER * HYENA_WIDTH
    w3 = w3.reshape(FILT_LANES, HYENA_ORDER, 2, HYENA_WIDTH).transpose(2, 0, 1, 3).reshape(2, FILT_LANES, width)
    dl = jnp.tile(deltas, HYENA_ORDER)[None, :]
    z2 = jnp.concatenate([z, z[::-1]], axis=0)
    w3h, w3l = _split_bf16(w3)
    tt = min(1024, n)
    tiles_per_dir = n // tt
    cspec = lambda a: pl.BlockSpec(a.shape, lambda i: (0, 0))
    dirspec = lambda rows: pl.BlockSpec((None, rows, width), lambda i: (i // tiles_per_dir, 0, 0))
    tile = lambda w: pl.BlockSpec((tt, w), lambda i: (i, 0))
    small = [*_split_bf16(w1), b1, f1, *_split_bf16(w2), b2, f2]
    h_raw, asum = pl.pallas_call(
        functools.partial(_filter_kernel, tiles_per_dir=tiles_per_dir), grid=(2 * tiles_per_dir,),
        in_specs=[tile(FILT_LANES)] + [cspec(a) for a in small] + [dirspec(FILT_LANES), dirspec(FILT_LANES), cspec(dl)],
        out_specs=(tile(width), dirspec(1)),
        out_shape=(jax.ShapeDtypeStruct((2 * n, width), F32), jax.ShapeDtypeStruct((2, 1, width), F32)),
        compiler_params=_params(1), name="hyena_filter_mlp")(z2, *small, w3h, w3l, dl)
    return pl.pallas_call(
        functools.partial(_normalise_kernel, n=n), grid=(2 * tiles_per_dir,),
        in_specs=[tile(width), dirspec(1)],
        out_specs=tuple(tile(HYENA_WIDTH) for _ in range(HYENA_ORDER)),
        out_shape=tuple(jax.ShapeDtypeStruct((2 * n, HYENA_WIDTH), F32) for _ in range(HYENA_ORDER)),
        compiler_params=_params(1), name="hyena_filter_norm",
    )(h_raw, asum)


def _dft_tables(n):
    n2 = DFT_SHORT if n >= 4 * DFT_SHORT else n
    n1 = n // n2

    def cis(idx):
        ang = (-2.0 * math.pi / n) * idx.astype(F32)
        return jnp.cos(ang), jnp.sin(ang)
    k2 = jnp.arange(n2)
    fr, fi = cis((k2[:, None] * k2[None, :]) % n2 * n1)
    tabs = dict(n1=n1, n2=n2)
    tabs['f_hi'], tabs['f_lo'] = _split_bf16(jnp.stack([fr, fi]))
    k1 = jnp.arange(n1)
    tr, ti = cis(k1[:, None] * k2[None, :])
    tabs['tw'] = jnp.broadcast_to(jnp.stack([tr, ti], axis=1)[..., None], (n1, 2, n2, HEAD_LANES))
    if n1 > 1:
        gr, gi = cis((k1[:, None] * k1[None, :]) % n1 * n2)
        half = n1 // 2
        grh, gih = gr[:, :half], gi[:, :half]
        tabs['m_fwd'] = _split_bf16(jnp.block([[grh, -gih], [gih, grh]]))
        tabs['m_real'] = _split_bf16(jnp.concatenate([gr, gi], axis=0))
        tabs['m_inv'] = _split_bf16(jnp.block([[grh.T, gih.T], [-gih.T, grh.T]]))
    return tabs


def _rowmix_kernel(mh_ref, ml_ref, x_ref, o_ref):
    o_ref[...] = _dot_const(mh_ref[...], ml_ref[...], x_ref[...])


def _rowmix(m, x):
    mh, ml = m
    rin, cols = x.shape
    ct = min(2048, cols)
    return pl.pallas_call(
        _rowmix_kernel, grid=(cols // ct,),
        in_specs=[pl.BlockSpec(mh.shape, lambda i: (0, 0)), pl.BlockSpec(ml.shape, lambda i: (0, 0)),
                  pl.BlockSpec((rin, ct), lambda i: (0, i))],
        out_specs=pl.BlockSpec((mh.shape[0], ct), lambda i: (0, i)),
        out_shape=jax.ShapeDtypeStruct((mh.shape[0], cols), F32), compiler_params=_params(1), name="dft_rowmix",
    )(mh, ml, x)


def _spectral_kernel(x_ref, tw_ref, fh_ref, fl_ref, k_ref, o_ref, *, conv):
    xr, xi = x_ref[0], x_ref[1]
    reps = xr.shape[1] // HEAD_LANES
    tr = jnp.concatenate([tw_ref[0]] * reps, axis=1)
    ti = jnp.concatenate([tw_ref[1]] * reps, axis=1)
    frh, fih, frl, fil = fh_ref[0], fh_ref[1], fl_ref[0], fl_ref[1]

    def dft(ar, ai, conj):
        rr, ii = _dot_const(frh, frl, ar), _dot_const(fih, fil, ai)
        ri, ir = _dot_const(frh, frl, ai), _dot_const(fih, fil, ar)
        return (rr + ii, ri - ir) if conj else (rr - ii, ri + ir)

    yr, yi = dft(xr * tr - xi * ti, xr * ti + xi * tr, False)
    if not conv:
        o_ref[0] = yr * k_ref[...]
        o_ref[1] = yi * k_ref[...]
        return
    kr, ki = k_ref[0], k_ref[1]
    cr, ci = dft(yr * kr - yi * ki, yr * ki + yi * kr, True)
    o_ref[0] = cr * tr + ci * ti
    o_ref[1] = ci * tr - cr * ti


def _spectral(x, k, tabs, conv):
    _, n1, n2, c = x.shape
    slab = pl.BlockSpec((2, None, n2, c), lambda i: (0, i, 0, 0))
    kspec = slab if conv else pl.BlockSpec(k.shape, lambda i: (0, 0))
    return pl.pallas_call(
        functools.partial(_spectral_kernel, conv=conv), grid=(n1,),
        in_specs=[slab, pl.BlockSpec((None, 2, n2, HEAD_LANES), lambda i: (i, 0, 0, 0)),
                  pl.BlockSpec(tabs['f_hi'].shape, lambda i: (0, 0, 0)),
                  pl.BlockSpec(tabs['f_lo'].shape, lambda i: (0, 0, 0)), kspec],
        out_specs=slab, out_shape=jax.ShapeDtypeStruct(x.shape, F32), compiler_params=_params(1),
        name="dft_spectral_conv" if conv else "dft_spectral_filter",
    )(x, tabs['tw'], tabs['f_hi'], tabs['f_lo'], k)


def _filter_spectrum(k, tabs):
    n, c = k.shape
    n1, n2 = tabs['n1'], tabs['n2']
    if n1 > 1:
        x = _rowmix(tabs['m_real'], k.reshape(n1, n2 * c)).reshape(2, n1, n2, c)
    else:
        x = jnp.stack([k, jnp.zeros_like(k)]).reshape(2, 1, n2, c)
    return _spectral(x, jnp.full((1, c), 1.0 / n, F32), tabs, conv=False)


def _long_conv(v, kf, tabs):
    b, n, c = v.shape
    assert b == 2
    n1, n2 = tabs['n1'], tabs['n2']
    if n1 > 1:
        x = _rowmix(tabs['m_fwd'], v.reshape(n1, n2 * c)).reshape(2, n1, n2, c)
        y = _spectral(x, kf, tabs, conv=True)
        return _rowmix(tabs['m_inv'], y.reshape(2 * n1, n2 * c)).reshape(2, n, c)
    x = jnp.concatenate([v, jnp.zeros_like(v)], axis=1).reshape(2, 1, n2, c)
    return _spectral(x, kf, tabs, conv=True).reshape(2, n2, c)[:, :n]


def _gate_kernel(g_ref, y_ref, v_ref, s_ref, o_ref):
    o_ref[...] = g_ref[...] * (y_ref[...] + s_ref[...] * v_ref[...])


def _hyena_gate(gate, y, v, skip):
    b, n, c = v.shape
    tt = min(2048, n)
    tok = pl.BlockSpec((None, tt, c), lambda bi, ti: (bi, ti, 0))
    return pl.pallas_call(_gate_kernel, grid=(b, n // tt),
                          in_specs=[tok, tok, tok, pl.BlockSpec((1, c), lambda bi, ti: (0, 0))], out_specs=tok,
                          out_shape=jax.ShapeDtypeStruct(v.shape, F32), compiler_params=_params(2),
                          name="hyena_gate")(gate, y, v, skip)


def _hyena_branch(hy, p, tabs):
    n = hy.shape[1]
    x1, x2, v = _token_conv(_short_conv_kernel, hy, (p['hyena_short_w'], p['hyena_short_b'][None, :]),
                            (HYENA_WIDTH,) * 3, "hyena_short_conv")
    taps = _hyena_taps(n, p)
    for o, gate in enumerate((x1, x2)):
        v = _hyena_gate(gate, _long_conv(v, _filter_spectrum(taps[o], tabs), tabs), v, p['hyena_skip'][o][None, :])
    return v


def _adaln_kernel(c_ref, w_ref, b_ref, o_ref):
    s = c_ref[...]
    s = s * jax.nn.sigmoid(s)
    o_ref[...] = _dot_split(*_split_bf16(s), *_split_bf16(w_ref[...])) + b_ref[...]


def _adaln(cond, w, b):
    d, width = w.shape
    ct = width // 6
    return pl.pallas_call(
        _adaln_kernel, grid=(6,),
        in_specs=[pl.BlockSpec(cond.shape, lambda i: (0, 0)), pl.BlockSpec((d, ct), lambda i: (0, i)),
                  pl.BlockSpec((1, ct), lambda i: (0, i))],
        out_specs=pl.BlockSpec((cond.shape[0], ct), lambda i: (0, i)),
        out_shape=jax.ShapeDtypeStruct((cond.shape[0], width), F32), compiler_params=_params(1), name="adaln",
    )(cond, w, b[None, :])


def _mod_rows(mod, norm_mix_g, norm_ffn_g, final_g, batch):
    sh1, sc1, g1, sh2, sc2, g2 = jnp.split(mod, 6, axis=-1)
    rows = jnp.stack([norm_mix_g * (1.0 + sc1), sh1, g1, norm_ffn_g * (1.0 + sc2), sh2, g2,
                      jnp.broadcast_to(final_g, g1.shape), jnp.zeros_like(g1)], axis=1)
    return jnp.broadcast_to(rows, (batch,) + rows.shape[1:])


def kernel(x, c, ctx, c_ctx, ada_w, ada_b, norm_mix_g, norm_ffn_g, w_in, diff_lambda, diff_subln_g, hyena_short_w, hyena_short_b, filt_w1, filt_b1, filt_freq, filt_w2, filt_b2, filt_w3, hyena_skip, conf_dw_w, conf_ln_g, conf_ln_b, mla_q_norm_g, mla_kv_norm_g, mla_w_uq, mla_w_ukv, w_branch, w_out, w_router, w_exp_in, w_exp_out, final_norm_g):
    depth = w_in.shape[0]
    batch, n_lat, d = x.shape
    n_ctx = ctx.shape[1]
    rope_lat = _rope_operands(n_lat, identity=False)
    rope_ctx = _rope_operands(n_ctx, identity=True)
    dft_lat, dft_ctx = _dft_tables(2 * n_lat), _dft_tables(2 * n_ctx)
    cond = jnp.concatenate([c, c_ctx[None], jnp.zeros((MOD_ROWS - batch - 1, d), F32)], axis=0)
    tile_lat, tile_ctx = min(512, n_lat), min(256, n_ctx)
    h_lat, h_ctx = x, ctx
    for l in range(depth):
        last = l == depth - 1
        p = dict(w_in=w_in[l], diff_subln_g=diff_subln_g[l], hyena_short_w=hyena_short_w[l],
                 hyena_short_b=hyena_short_b[l], filt_w1=filt_w1[l], filt_b1=filt_b1[l], filt_freq=filt_freq[l],
                 filt_w2=filt_w2[l], filt_b2=filt_b2[l], filt_w3=filt_w3[l], hyena_skip=hyena_skip[l],
                 conf_dw_w=conf_dw_w[l], conf_ln_g=conf_ln_g[l], conf_ln_b=conf_ln_b[l],
                 mla_q_norm_g=mla_q_norm_g[l], mla_kv_norm_g=mla_kv_norm_g[l], mla_w_uq=mla_w_uq[l],
                 mla_w_ukv=mla_w_ukv[l], w_branch=w_branch[l], w_out=w_out[l], w_router=w_router[l],
                 w_exp_in=w_exp_in[l], w_exp_out=w_exp_out[l])
        ada = _adaln(cond, ada_w[l], ada_b[l])
        mod_lat = _mod_rows(ada[:batch], norm_mix_g[l], norm_ffn_g[l], final_norm_g, batch)
        mod_ctx = _mod_rows(ada[batch:batch + 1], norm_mix_g[l], norm_ffn_g[l], final_norm_g, batch)
        lam_init = 0.8 - 0.6 * math.exp(-0.3 * l)
        lq1, lk1, lq2, lk2 = diff_lambda[l].astype(F32)
        lam = jnp.reshape(jnp.exp(jnp.sum(lq1 * lk1)) - jnp.exp(jnp.sum(lq2 * lk2)) + lam_init, (1,))
        w_inp, w_mrg = _inproj_weights(p), _merge_weights(p, lam_init)

        qdT_l, kd_l, vdT_l, qmT_l, km_l, vmT_l, hy_l, glu_l = _inproj(h_lat, mod_lat, w_inp, rope_lat, tile=tile_lat)
        qdT_c, kd_c, vdT_c, qmT_c, km_c, vmT_c, hy_c, glu_c = _inproj(h_ctx, mod_ctx, w_inp, rope_ctx, tile=tile_ctx)
        a_lat = _flash_attention(lam, qdT_l, kd_c, vdT_c, kd_l, vdT_l, n_maps=2, tq=min(256, n_lat))
        m_lat = _flash_attention(lam, qmT_l, km_c, vmT_c, km_l, vmT_l, n_maps=1, tq=min(512, n_lat))
        h_lat, u2_lat, lg_lat = _merge(h_lat, mod_lat, a_lat, _hyena_branch(hy_l, p, dft_lat), _conformer_branch(glu_l, p),
                                       m_lat, w_mrg, tile=min(256, n_lat))
        w_ei, w_eo = p['w_exp_in'].astype(BF16), p['w_exp_out'].astype(BF16)
        h_lat = _expert_choice_ffn(h_lat, mod_lat, u2_lat, lg_lat, w_ei, w_eo, final_norm=last)
        if not last:
            a_ctx = _flash_attention(lam, qdT_c, kd_c, vdT_c, None, None, n_maps=2, tq=n_ctx)
            m_ctx = _flash_attention(lam, qmT_c, km_c, vmT_c, None, None, n_maps=1, tq=n_ctx)
            h_ctx, u2_ctx, lg_ctx = _merge(h_ctx, mod_ctx, a_ctx, _hyena_branch(hy_c, p, dft_ctx),
                                           _conformer_branch(glu_c, p), m_ctx, w_mrg, tile=tile_ctx)
            h_ctx = _expert_choice_ffn(h_ctx, mod_ctx, u2_ctx, lg_ctx, w_ei, w_eo)
    return h_lat
```

```python
import functools
import math

import jax
import jax.numpy as jnp
from jax import lax
from jax.experimental import pallas as pl
from jax.experimental.pallas import tpu as pltpu

GRID_W = 64
ROPE_BASE = 10000.0
EPS = 1e-6

DIFF_HEADS = 4
DIFF_HEAD_DIM = 64
DIFF_V_DIM = 2 * DIFF_HEAD_DIM
HYENA_WIDTH = 256
HYENA_ORDER = 2
FILT_EMB = 33
DECAY_TARGET = 1e-2
FAST_DECAY = 0.3
SLOW_DECAY = 1.5
CONF_WIDTH = 256
MLA_HEADS = 4
MLA_Q_RANK = 256
MLA_KV_RANK = 128
MLA_NOPE = 64
MLA_ROPE = 32
MLA_V = 64
MLA_SCALE = (MLA_NOPE + MLA_ROPE) ** -0.5
N_BRANCH = 4
N_EXPERTS = 16
EC_CAPACITY = 2

DIFF_QK_W = DIFF_HEADS * 2 * DIFF_HEAD_DIM
DIFF_V_W = DIFF_HEADS * DIFF_V_DIM
HYENA_PROJ = (HYENA_ORDER + 1) * HYENA_WIDTH
CONF_PROJ = 2 * CONF_WIDTH
IN_SPLITS = (DIFF_QK_W, DIFF_QK_W, DIFF_V_W, HYENA_PROJ, CONF_PROJ, MLA_Q_RANK, MLA_KV_RANK, MLA_ROPE)
BRANCH_WIDTHS = (DIFF_V_W, HYENA_WIDTH, CONF_WIDTH, MLA_HEADS * MLA_V)

HEAD_LANES = 128
DIFF_V_PAD = 16
ATT_W = DIFF_HEADS * HEAD_LANES
LOG2E = 1.4426950408889634
VMEM_LIMIT_BYTES = 48 * 1024 * 1024
MOD_ROWS = 8

F32 = jnp.float32
BF16 = jnp.bfloat16
_NT = (((1,), (1,)), ((), ()))


def _params(n_axes, vmem=VMEM_LIMIT_BYTES):
    return pltpu.CompilerParams(dimension_semantics=("arbitrary",) * n_axes, vmem_limit_bytes=vmem)


def _flash_kernel(lam_ref, qT_ref, kc_ref, vcT_ref, *rest, n_maps, n_lat_chunks, tk, sum_row):
    if n_lat_chunks:
        kl_ref, vlT_ref, o_ref, acc_ref, m_ref, q2_ref, s_ref = rest
    else:
        o_ref, acc_ref, m_ref, q2_ref = rest
    qT = qT_ref[...]
    tq = qT.shape[1]
    if n_maps == 2:
        row = lax.broadcasted_iota(jnp.int32, qT.shape, 0)
        zero = jnp.zeros_like(qT)
        q2_ref[:, :tq] = jnp.where(row < DIFF_HEAD_DIM, qT, zero)
        q2_ref[:, tq:] = jnp.where(row >= DIFF_HEAD_DIM, qT, zero)
    else:
        q2_ref[...] = qT
    m_ref[...] = jnp.full(m_ref.shape, -jnp.inf, F32)
    acc_ref[...] = jnp.zeros(acc_ref.shape, F32)

    def scores(k):
        return jnp.dot(k, q2_ref[...], preferred_element_type=F32)

    def absorb(s, vT):
        m_prev = m_ref[...]
        m_new = jnp.maximum(m_prev, jnp.max(s, axis=0, keepdims=True))
        alpha = jnp.exp2(m_prev - m_new)
        p = jnp.exp2(s - m_new).astype(BF16)
        acc_ref[...] = alpha * acc_ref[...] + jnp.dot(vT, p, preferred_element_type=F32)
        m_ref[...] = m_new

    def chunk(c):
        return pl.ds(c * tk if isinstance(c, int) else pl.multiple_of(c * tk, tk), tk)

    def keys(c):
        return kl_ref[chunk(c), :]

    def values_t(c):
        return vlT_ref[:, chunk(c)]

    absorb(scores(kc_ref[...]), vcT_ref[...])
    if n_lat_chunks:
        s_ref[0] = scores(keys(0))

        def pair(j, carry):
            c = 2 * j
            s_ref[1] = scores(keys(c + 1))
            absorb(s_ref[0], values_t(c))
            s_ref[0] = scores(keys(c + 2))
            absorb(s_ref[1], values_t(c + 1))
            return carry
        lax.fori_loop(0, n_lat_chunks // 2 - 1, pair, 0)
        s_ref[1] = scores(keys(n_lat_chunks - 1))
        absorb(s_ref[0], values_t(n_lat_chunks - 2))
        absorb(s_ref[1], values_t(n_lat_chunks - 1))
    o = acc_ref[0:HEAD_LANES, :] / acc_ref[sum_row:sum_row + 1, :]
    if n_maps == 2:
        o = o[:, :tq] - lam_ref[0] * o[:, tq:]
        o = o * lax.rsqrt(jnp.mean(o * o, axis=0, keepdims=True) + EPS)
    o_ref[...] = o.T.astype(BF16)


def _flash_stream_kernel(lam_ref, qT_ref, kc_ref, vcT_ref, kl_ref, vlT_ref, o_ref, acc_ref, m_ref, q2_ref, sc_ref,
                         s_ref, *, n_maps, n_lat_chunks, tk, tq, sum_row):
    n_q = qT_ref.shape[1] // tq

    def load_queries(qi):
        qT = qT_ref[:, pl.ds(pl.multiple_of(qi * tq, tq), tq)]
        if n_maps == 2:
            row = lax.broadcasted_iota(jnp.int32, qT.shape, 0)
            zero = jnp.zeros_like(qT)
            q2_ref[:, :tq] = jnp.where(row < DIFF_HEAD_DIM, qT, zero)
            q2_ref[:, tq:] = jnp.where(row >= DIFF_HEAD_DIM, qT, zero)
        else:
            q2_ref[...] = qT

    def scores(k):
        return jnp.dot(k, q2_ref[...], preferred_element_type=F32)

    def absorb(s, vT):
        m_prev = m_ref[...]
        m_new = jnp.maximum(m_prev, jnp.max(s, axis=0, keepdims=True))
        alpha = jnp.exp2(m_prev - m_new)
        p = jnp.exp2(s - m_new).astype(BF16)
        acc_ref[...] = alpha * acc_ref[...] + jnp.dot(vT, p, preferred_element_type=F32)
        m_ref[...] = m_new

    load_queries(0)
    sc_ref[...] = scores(kc_ref[...])

    def query_block(qi, carry):
        m_ref[...] = jnp.full(m_ref.shape, -jnp.inf, F32)
        acc_ref[...] = jnp.zeros(acc_ref.shape, F32)
        def chunk(c):
            return pl.ds(c * tk if isinstance(c, int) else pl.multiple_of(c * tk, tk), tk)

        s_ref[0] = scores(kl_ref[chunk(0), :])
        absorb(sc_ref[...], vcT_ref[...])

        def pair(j, inner):
            c = 2 * j
            s_ref[1] = scores(kl_ref[chunk(c + 1), :])
            absorb(s_ref[0], vlT_ref[:, chunk(c)])
            s_ref[0] = scores(kl_ref[chunk(c + 2), :])
            absorb(s_ref[1], vlT_ref[:, chunk(c + 1)])
            return inner
        lax.fori_loop(0, n_lat_chunks // 2 - 1, pair, 0)
        s_ref[1] = scores(kl_ref[chunk(n_lat_chunks - 1), :])
        absorb(s_ref[0], vlT_ref[:, chunk(n_lat_chunks - 2)])
        load_queries(jnp.minimum(qi + 1, n_q - 1))
        sc_ref[...] = scores(kc_ref[...])
        absorb(s_ref[1], vlT_ref[:, chunk(n_lat_chunks - 1)])
        o = acc_ref[0:HEAD_LANES, :] / acc_ref[sum_row:sum_row + 1, :]
        if n_maps == 2:
            o = o[:, :tq] - lam_ref[0] * o[:, tq:]
            o = o * lax.rsqrt(jnp.mean(o * o, axis=0, keepdims=True) + EPS)
        o_ref[pl.ds(pl.multiple_of(qi * tq, tq), tq), :] = o.T.astype(BF16)
        return carry
    lax.fori_loop(0, n_q, query_block, 0)


def _flash_attention(lam, qT, kc, vcT, kl, vlT, *, n_maps, tq):
    b, _, s = qT.shape
    lc = kc.shape[1]
    mv = vcT.shape[2]
    sum_row = HEAD_LANES if n_maps == 2 else MLA_V
    r = n_maps * tq
    if kl is not None:
        sl = kl.shape[1]
        tk = _lat_chunk(sl)
        n_lat_chunks = sl // tk
        assert n_lat_chunks * tk == sl and s % tq == 0
        return pl.pallas_call(
            functools.partial(_flash_stream_kernel, n_maps=n_maps, n_lat_chunks=n_lat_chunks, tk=tk, tq=tq,
                              sum_row=sum_row),
            grid=(b, DIFF_HEADS),
            in_specs=[pl.BlockSpec(memory_space=pltpu.SMEM),
                      pl.BlockSpec((None, HEAD_LANES, s), lambda bi, hi: (bi, hi, 0)),
                      pl.BlockSpec((None, lc, HEAD_LANES), lambda bi, hi: (bi, 0, hi)),
                      pl.BlockSpec((None, None, mv, lc), lambda bi, hi: (bi, hi, 0, 0)),
                      pl.BlockSpec((None, sl, HEAD_LANES), lambda bi, hi: (bi, 0, hi)),
                      pl.BlockSpec((None, None, mv, sl), lambda bi, hi: (bi, hi, 0, 0))],
            out_specs=pl.BlockSpec((None, s, HEAD_LANES), lambda bi, hi: (bi, 0, hi)),
            out_shape=jax.ShapeDtypeStruct((b, s, ATT_W), BF16),
            scratch_shapes=[pltpu.VMEM((mv, r), F32), pltpu.VMEM((1, r), F32), pltpu.VMEM((HEAD_LANES, r), BF16),
                            pltpu.VMEM((lc, r), F32), pltpu.VMEM((2, tk, r), F32)],
            compiler_params=_params(2, 56 * 1024 * 1024),
            name=f"flash_attention_{n_maps}map",
        )(lam, qT, kc, vcT, kl, vlT)
    in_specs = [
        pl.BlockSpec(memory_space=pltpu.SMEM),
        pl.BlockSpec((None, HEAD_LANES, tq), lambda bi, hi, qi: (bi, hi, qi)),
        pl.BlockSpec((None, lc, HEAD_LANES), lambda bi, hi, qi: (bi, 0, hi)),
        pl.BlockSpec((None, None, mv, lc), lambda bi, hi, qi: (bi, hi, 0, 0)),
    ]
    args = [lam, qT, kc, vcT]
    scratch = [pltpu.VMEM((mv, r), F32), pltpu.VMEM((1, r), F32), pltpu.VMEM((HEAD_LANES, r), BF16)]
    n_lat_chunks, tk = 0, 0
    if kl is not None:
        sl = kl.shape[1]
        tk = _lat_chunk(sl)
        n_lat_chunks = sl // tk
        assert n_lat_chunks % 2 == 0 and n_lat_chunks * tk == sl
        scratch.append(pltpu.VMEM((2, tk, r), F32))
        in_specs += [
            pl.BlockSpec((None, sl, HEAD_LANES), lambda bi, hi, qi: (bi, 0, hi)),
            pl.BlockSpec((None, None, mv, sl), lambda bi, hi, qi: (bi, hi, 0, 0)),
        ]
        args += [kl, vlT]
    return pl.pallas_call(
        functools.partial(_flash_kernel, n_maps=n_maps, n_lat_chunks=n_lat_chunks, tk=tk, sum_row=sum_row),
        grid=(b, DIFF_HEADS, s // tq),
        in_specs=in_specs,
        out_specs=pl.BlockSpec((None, tq, HEAD_LANES), lambda bi, hi, qi: (bi, qi, hi)),
        out_shape=jax.ShapeDtypeStruct((b, s, ATT_W), BF16),
        scratch_shapes=scratch,
        compiler_params=_params(3),
        name=f"flash_attention_{n_maps}map",
    )(*args)


def _lat_chunk(s):
    return min(1024, s // 2)


W_NAT_SPLITS = (DIFF_QK_W, HYENA_PROJ, CONF_PROJ, MLA_Q_RANK, MLA_KV_RANK, HEAD_LANES)


def _modulated_norm(h, a, shift):
    return h * lax.rsqrt(jnp.mean(h * h, axis=-1, keepdims=True) + EPS) * a + shift


def _rope_lanes(x, tab_ref, shift):
    return (x * tab_ref[0] + pltpu.roll(x, shift, 1) * tab_ref[1]
            + pltpu.roll(x, HEAD_LANES - shift, 1) * tab_ref[2])


def _inproj_kernel(h_ref, mod_ref, wnat_ref, wT_ref, wuqT_ref, wukvk_ref, wuvT_ref, gq_ref, gkv_ref,
                   ropeT_d_ref, rope_kd_ref, ropeT_m_ref, rope_km_ref,
                   qdT_ref, kd_ref, vdT_ref, qmT_ref, km_ref, vmT_ref, hy_ref, glu_ref):
    u = _modulated_norm(h_ref[...], mod_ref[0:1, :], mod_ref[1:2, :]).astype(BF16)
    z = jnp.dot(u, wnat_ref[...], preferred_element_type=F32)
    zT = lax.dot_general(wT_ref[...], u, _NT, preferred_element_type=F32)
    offs = [0]
    for w in W_NAT_SPLITS:
        offs.append(offs[-1] + w)
    dk, hy, cf, cq, ckv, krp = (z[:, offs[i]:offs[i + 1]] for i in range(len(W_NAT_SPLITS)))

    for hd in range(DIFF_HEADS):
        sl = slice(hd * HEAD_LANES, (hd + 1) * HEAD_LANES)
        kd_ref[:, sl] = _rope_lanes(dk[:, sl], rope_kd_ref, DIFF_HEAD_DIM // 2).astype(BF16)
    cos_d, sin_d = ropeT_d_ref[0], ropeT_d_ref[1]
    half = DIFF_HEAD_DIM // 2
    for g in range(2 * DIFF_HEADS):
        x1 = zT[g * DIFF_HEAD_DIM:g * DIFF_HEAD_DIM + half]
        x2 = zT[g * DIFF_HEAD_DIM + half:(g + 1) * DIFF_HEAD_DIM]
        qdT_ref[g * DIFF_HEAD_DIM:g * DIFF_HEAD_DIM + half, :] = (x1 * cos_d - x2 * sin_d).astype(BF16)
        qdT_ref[g * DIFF_HEAD_DIM + half:(g + 1) * DIFF_HEAD_DIM, :] = (x1 * sin_d + x2 * cos_d).astype(BF16)
    tail = jnp.where(lax.broadcasted_iota(jnp.int32, (DIFF_V_PAD, zT.shape[1]), 0) == 0, 1.0, 0.0).astype(BF16)
    for hd in range(DIFF_HEADS):
        r0 = DIFF_QK_W + hd * DIFF_V_DIM
        vdT_ref[hd, 0:DIFF_V_DIM, :] = zT[r0:r0 + DIFF_V_DIM].astype(BF16)
        vdT_ref[hd, DIFF_V_DIM:, :] = tail

    hy_ref[...] = hy
    glu_ref[...] = cf[:, :CONF_WIDTH] * jax.nn.sigmoid(cf[:, CONF_WIDTH:])

    cqn = (cq * lax.rsqrt(jnp.mean(cq * cq, axis=-1, keepdims=True) + EPS) * gq_ref[...]).astype(BF16)
    ckvn = (ckv * lax.rsqrt(jnp.mean(ckv * ckv, axis=-1, keepdims=True) + EPS) * gkv_ref[...]).astype(BF16)
    qT = lax.dot_general(wuqT_ref[...], cqn, _NT, preferred_element_type=F32)
    cos_m, sin_m = ropeT_m_ref[0], ropeT_m_ref[1]
    hr = MLA_ROPE // 2
    for hd in range(MLA_HEADS):
        base = hd * HEAD_LANES
        r1 = base + MLA_NOPE
        x1, x2 = qT[r1:r1 + hr], qT[r1 + hr:r1 + 2 * hr]
        qmT_ref[base:r1, :] = qT[base:r1].astype(BF16)
        qmT_ref[r1:r1 + hr, :] = (x1 * cos_m - x2 * sin_m).astype(BF16)
        qmT_ref[r1 + hr:r1 + 2 * hr, :] = (x1 * sin_m + x2 * cos_m).astype(BF16)
        qmT_ref[r1 + 2 * hr:base + HEAD_LANES, :] = jnp.zeros((HEAD_LANES - MLA_NOPE - MLA_ROPE, qT.shape[1]), BF16)
    kn = jnp.dot(ckvn, wukvk_ref[...], preferred_element_type=F32)
    kr = _rope_lanes(krp, rope_km_ref, hr)
    for hd in range(MLA_HEADS):
        sl = slice(hd * HEAD_LANES, (hd + 1) * HEAD_LANES)
        km_ref[:, sl] = (kn[:, sl] + kr).astype(BF16)
    vT = lax.dot_general(wuvT_ref[...], ckvn, _NT, preferred_element_type=F32)
    ones_row = lax.broadcasted_iota(jnp.int32, vT.shape, 0) % HEAD_LANES == MLA_V
    vT = jnp.where(ones_row, 1.0, vT).astype(BF16)
    for hd in range(MLA_HEADS):
        vmT_ref[hd] = vT[hd * HEAD_LANES:(hd + 1) * HEAD_LANES]


def _inproj(h, mod, wts, rope, *, tile):
    b, n, d = h.shape
    const2 = lambda bi, ti: (0, 0)
    tok = lambda w: pl.BlockSpec((None, tile, w), lambda bi, ti: (bi, ti, 0))
    tokT = lambda w: pl.BlockSpec((None, w, tile), lambda bi, ti: (bi, 0, ti))
    full = lambda a: pl.BlockSpec(a.shape, const2)
    in_specs = [tok(d), pl.BlockSpec((None, MOD_ROWS, d), lambda bi, ti: (bi, 0, 0))]
    in_specs += [full(wts[k]) for k in ('w_nat', 'w_T', 'w_uqT', 'w_ukvk', 'w_uvT', 'gq', 'gkv')]
    in_specs += [pl.BlockSpec((2, DIFF_HEAD_DIM // 2, tile), lambda bi, ti: (0, 0, ti)),
                 pl.BlockSpec((3, tile, HEAD_LANES), lambda bi, ti: (0, ti, 0)),
                 pl.BlockSpec((2, MLA_ROPE // 2, tile), lambda bi, ti: (0, 0, ti)),
                 pl.BlockSpec((3, tile, HEAD_LANES), lambda bi, ti: (0, ti, 0))]
    sds = jax.ShapeDtypeStruct
    vrows_d, vrows_m = DIFF_V_DIM + DIFF_V_PAD, HEAD_LANES
    headsT = lambda rows: pl.BlockSpec((None, DIFF_HEADS, rows, tile), lambda bi, ti: (bi, 0, 0, ti))
    out_shape = (sds((b, ATT_W, n), BF16), sds((b, n, ATT_W), BF16), sds((b, DIFF_HEADS, vrows_d, n), BF16),
                 sds((b, ATT_W, n), BF16), sds((b, n, ATT_W), BF16), sds((b, MLA_HEADS, vrows_m, n), BF16),
                 sds((b, n, HYENA_PROJ), F32), sds((b, n, CONF_WIDTH), F32))
    out_specs = (tokT(ATT_W), tok(ATT_W), headsT(vrows_d), tokT(ATT_W), tok(ATT_W), headsT(vrows_m),
                 tok(HYENA_PROJ), tok(CONF_WIDTH))
    return pl.pallas_call(
        _inproj_kernel, grid=(b, n // tile), in_specs=in_specs, out_specs=out_specs, out_shape=out_shape,
        compiler_params=_params(2), name="inproj",
    )(h, mod, wts['w_nat'], wts['w_T'], wts['w_uqT'], wts['w_ukvk'], wts['w_uvT'], wts['gq'], wts['gkv'],
      rope['T_d'], rope['k_d'], rope['T_m'], rope['k_m'])


def _pad_heads(w, width):
    rows = w.shape[0]
    w = w.reshape(rows, MLA_HEADS, width)
    return jnp.pad(w, ((0, 0), (0, 0), (0, HEAD_LANES - width))).reshape(rows, ATT_W)


def _inproj_weights(p):
    d = p['w_in'].shape[0]
    dq, dk, dv, hy, cf, cq, ckv, kr = _split(p['w_in'][:, :sum(IN_SPLITS)], IN_SPLITS)
    krp = jnp.zeros((d, HEAD_LANES), F32).at[:, MLA_NOPE:MLA_NOPE + MLA_ROPE].set(kr)
    w_ukv = p['mla_w_ukv'].reshape(MLA_KV_RANK, MLA_HEADS, MLA_NOPE + MLA_V)
    return dict(
        w_nat=jnp.concatenate([dk, hy, cf, cq, ckv, krp], axis=1).astype(BF16),
        w_T=jnp.concatenate([dq * (DIFF_HEAD_DIM ** -0.5 * LOG2E), dv], axis=1).T.astype(BF16),
        w_uqT=_pad_heads(p['mla_w_uq'] * (MLA_SCALE * LOG2E), MLA_NOPE + MLA_ROPE).T.astype(BF16),
        w_ukvk=_pad_heads(w_ukv[:, :, :MLA_NOPE].reshape(MLA_KV_RANK, -1), MLA_NOPE).astype(BF16),
        w_uvT=_pad_heads(w_ukv[:, :, MLA_NOPE:].reshape(MLA_KV_RANK, -1), MLA_V).T.astype(BF16),
        gq=p['mla_q_norm_g'][None, :], gkv=p['mla_kv_norm_g'][None, :])


def _rope_tables(n_tok, rot_dim):
    rows = n_tok // GRID_W
    row = jnp.repeat(jnp.arange(rows), GRID_W).astype(F32)
    col = jnp.tile(jnp.arange(GRID_W), rows).astype(F32)
    nf = rot_dim // 4
    inv = ROPE_BASE ** (-jnp.arange(nf, dtype=F32) / nf)
    ang = jnp.concatenate([row[:, None] * inv, col[:, None] * inv], axis=-1)
    return jnp.cos(ang), jnp.sin(ang)


def _rope_operands(n_tok, identity):
    if identity:
        cos_d, sin_d = jnp.ones((n_tok, DIFF_HEAD_DIM // 2), F32), jnp.zeros((n_tok, DIFF_HEAD_DIM // 2), F32)
        cos_m, sin_m = jnp.ones((n_tok, MLA_ROPE // 2), F32), jnp.zeros((n_tok, MLA_ROPE // 2), F32)
    else:
        cos_d, sin_d = _rope_tables(n_tok, DIFF_HEAD_DIM)
        cos_m, sin_m = _rope_tables(n_tok, MLA_ROPE)
    z_d, z_m = jnp.zeros_like(sin_d), jnp.zeros_like(sin_m)
    two = lambda a, bb: jnp.tile(jnp.concatenate([a, bb], axis=1), (1, 2))
    lo, hi = jnp.zeros((n_tok, MLA_NOPE), F32), jnp.zeros((n_tok, HEAD_LANES - MLA_NOPE - MLA_ROPE), F32)
    mid = lambda a, bb: jnp.concatenate([lo, a, bb, hi], axis=1)
    return dict(T_d=jnp.stack([cos_d.T, sin_d.T]), T_m=jnp.stack([cos_m.T, sin_m.T]),
                k_d=jnp.stack([two(cos_d, cos_d), two(z_d, sin_d), two(-sin_d, z_d)]),
                k_m=jnp.stack([mid(cos_m, cos_m), mid(z_m, sin_m), mid(-sin_m, z_m)]))


def _merge_kernel(h_ref, mod_ref, a_ref, hy_ref, cf_ref, m_ref, wg_ref, wbd_ref, wbh_ref, wbc_ref, wbm_ref,
                  wo_ref, wrh_ref, wrl_ref, hn_ref, u2_ref, lg_ref):
    h = h_ref[...]
    d = h.shape[1]
    u = _modulated_norm(h, mod_ref[0:1, :], mod_ref[1:2, :]).astype(BF16)
    gates = jax.nn.sigmoid(jnp.dot(u, wg_ref[...], preferred_element_type=F32))
    dot = lambda x, w_ref: jnp.dot(x, w_ref[...], preferred_element_type=F32)
    acc = gates[:, :d] * dot(a_ref[...], wbd_ref)
    acc += gates[:, d:2 * d] * dot(hy_ref[...].astype(BF16), wbh_ref)
    acc += gates[:, 2 * d:3 * d] * dot(cf_ref[...].astype(BF16), wbc_ref)
    acc += gates[:, 3 * d:] * dot(m_ref[...], wbm_ref)
    hn = h + mod_ref[2:3, :] * dot(acc.astype(BF16), wo_ref)
    hn_ref[...] = hn
    u2 = _modulated_norm(hn, mod_ref[3:4, :], mod_ref[4:5, :])
    u2h = u2.astype(BF16)
    u2l = (u2 - u2h.astype(F32)).astype(BF16)
    u2_ref[...] = u2h
    lg_ref[...] = dot(u2h, wrh_ref) + (dot(u2l, wrh_ref) + dot(u2h, wrl_ref))


def _merge(h, mod, a, hyv, cfv, m, wts, *, tile):
    b, n, d = h.shape
    const2 = lambda bi, ti: (0, 0)
    tok = lambda w: pl.BlockSpec((None, tile, w), lambda bi, ti: (bi, ti, 0))
    names = ('w_gate', 'w_bd', 'w_bh', 'w_bc', 'w_bm', 'w_out', 'w_rh', 'w_rl')
    in_specs = [tok(d), pl.BlockSpec((None, MOD_ROWS, d), lambda bi, ti: (bi, 0, 0)),
                tok(ATT_W), tok(HYENA_WIDTH), tok(CONF_WIDTH), tok(ATT_W)]
    in_specs += [pl.BlockSpec(wts[k].shape, const2, pipeline_mode=pl.Buffered(1)) for k in names]
    sds = jax.ShapeDtypeStruct
    return pl.pallas_call(
        _merge_kernel, grid=(b, n // tile), in_specs=in_specs,
        out_specs=(tok(d), tok(d), tok(HEAD_LANES)),
        out_shape=(sds((b, n, d), F32), sds((b, n, d), BF16), sds((b, n, HEAD_LANES), F32)),
        compiler_params=_params(2, 56 * 1024 * 1024), name="merge",
    )(h, mod, a, hyv, cfv, m, *[wts[k] for k in names])


def _merge_weights(p, lam_init):
    d = p['w_out'].shape[0]
    wb_d, wb_h, wb_c, wb_m = (w.T for w in _split(p['w_branch'].T, BRANCH_WIDTHS))
    wb_d = wb_d * (jnp.tile(p['diff_subln_g'], DIFF_HEADS) * (1.0 - lam_init))[:, None]
    wb_m = jnp.pad(wb_m.reshape(MLA_HEADS, MLA_V, d), ((0, 0), (0, HEAD_LANES - MLA_V), (0, 0))).reshape(ATT_W, d)
    w_r = jnp.pad(p['w_router'], ((0, 0), (0, HEAD_LANES - N_EXPERTS)))
    w_rh = w_r.astype(BF16)
    return dict(w_gate=p['w_in'][:, sum(IN_SPLITS):].astype(BF16), w_bd=wb_d.astype(BF16), w_bh=wb_h.astype(BF16),
                w_bc=wb_c.astype(BF16), w_bm=wb_m.astype(BF16), w_out=p['w_out'].astype(BF16),
                w_rh=w_rh, w_rl=(w_r - w_rh.astype(F32)).astype(BF16))


SUB_TOKENS = 256
GATHER_WINDOW = 272
COMBINE_WINDOW = 384
ROUTE_MIN_ROWS = 8


def _excl_scan(x, lane, row):
    inc = x
    s = 1
    while s < HEAD_LANES:
        inc = inc + jnp.where(lane >= s, pltpu.roll(inc, s, 2), 0.0)
        s *= 2
    tot = jnp.sum(x, axis=2, keepdims=True) + jnp.zeros_like(x)
    off = tot
    s = 1
    while s < x.shape[1]:
        off = off + jnp.where(row >= s, pltpu.roll(off, s, 1), 0.0)
        s *= 2
    return inc - x + (off - tot)


def _route_kernel(lg_ref, pos_ref, aff_ref, *, n_valid, cap):
    lg = lg_ref[...]
    shape = lg.shape
    lane = lax.broadcasted_iota(jnp.int32, shape, 2)
    row = lax.broadcasted_iota(jnp.int32, shape, 1)
    e = jnp.exp(lg - jnp.max(lg, axis=0, keepdims=True))
    aff = e / jnp.sum(e, axis=0, keepdims=True)
    bits = jnp.where(row * HEAD_LANES + lane < n_valid, pltpu.bitcast(aff, jnp.int32), -1)

    def count(mask):
        c = jnp.sum(jnp.where(mask, 1.0, 0.0), axis=2, keepdims=True)
        return jnp.sum(c, axis=1, keepdims=True)

    def step(i, thr):
        cand = thr | (jnp.int32(1) << (30 - i))
        return jnp.where(count(bits >= cand) >= cap, cand, thr)
    thr = lax.fori_loop(0, 31, step, jnp.zeros((shape[0], 1, 1), jnp.int32))
    gt = bits > thr
    eq = bits == thr
    need = cap - count(gt)
    tie_rank = _excl_scan(jnp.where(eq, 1.0, 0.0), lane, row)
    sel = gt | (eq & (tie_rank < need))
    pos = _excl_scan(jnp.where(sel, 1.0, 0.0), lane, row)
    pos_ref[...] = jnp.where(sel, pos.astype(jnp.int32), -1)
    aff_ref[...] = aff


def _route(logits, cap):
    b, n, _ = logits.shape
    rows = max(ROUTE_MIN_ROWS, n // HEAD_LANES)
    lg = jnp.swapaxes(logits[..., :N_EXPERTS], 1, 2)
    lg = jnp.pad(lg, ((0, 0), (0, 0), (0, rows * HEAD_LANES - n))).reshape(b, N_EXPERTS, rows, HEAD_LANES)
    spec = pl.BlockSpec((None, N_EXPERTS, rows, HEAD_LANES), lambda bi: (bi, 0, 0, 0))
    pos, aff = pl.pallas_call(
        functools.partial(_route_kernel, n_valid=n, cap=cap), grid=(b,), in_specs=[spec], out_specs=(spec, spec),
        out_shape=(jax.ShapeDtypeStruct(lg.shape, jnp.int32), jax.ShapeDtypeStruct(lg.shape, F32)),
        compiler_params=_params(1), name="route",
    )(lg)
    flat = lambda a: a.reshape(b, N_EXPERTS, rows * HEAD_LANES)[..., :n]
    return flat(pos), flat(aff)


def _experts_kernel(base_ref, u_ref, mod_ref, aff_ref, pos_ref, win_ref, wout_ref, ye_ref, xe_ref, *,
                    n_sub, cap):
    bi, ei, kb = pl.program_id(0), pl.program_id(1), pl.program_id(2)
    d = u_ref.shape[1]

    @pl.when(kb == 0)
    def _():
        xe_ref[...] = jnp.zeros(xe_ref.shape, F32)

    slot = lax.broadcasted_iota(jnp.int32, (GATHER_WINDOW, SUB_TOKENS), 0)
    ones = jnp.ones((SUB_TOKENS, HEAD_LANES), BF16)
    for j in range(n_sub):
        tok = slice(j * SUB_TOKENS, (j + 1) * SUB_TOKENS)
        base = pl.multiple_of(base_ref[bi, ei, kb * n_sub + j], 16)
        match = slot == (pos_ref[:, tok] - base)
        onehot = jnp.where(match, 1.0, 0.0).astype(BF16)
        g = aff_ref[:, tok]
        g_hi = g.astype(BF16).astype(F32)
        sel_hi = jnp.where(match, g_hi, 0.0).astype(BF16)
        sel_lo = jnp.where(match, g - g_hi, 0.0).astype(BF16)
        rows = pl.ds(base, GATHER_WINDOW)
        xe_ref[rows, :d] += jnp.dot(onehot, u_ref[tok, :], preferred_element_type=F32)
        xe_ref[rows, d:d + HEAD_LANES] += jnp.dot(sel_hi, ones, preferred_element_type=F32)
        xe_ref[rows, d + HEAD_LANES:] += jnp.dot(sel_lo, ones, preferred_element_type=F32)

    @pl.when(kb == pl.num_programs(2) - 1)
    def _():
        f = wout_ref.shape[0]
        step = min(512, cap)
        for r0 in range(0, cap, step):
            x = xe_ref[r0:r0 + step, :d].astype(BF16)
            gate = xe_ref[r0:r0 + step, d:d + HEAD_LANES] + xe_ref[r0:r0 + step, d + HEAD_LANES:]
            hgu = jnp.dot(x, win_ref[...], preferred_element_type=F32)
            act = (jax.nn.silu(hgu[:, :f]) * hgu[:, f:]).astype(BF16)
            y = jnp.dot(act, wout_ref[...], preferred_element_type=F32)
            scale = jnp.concatenate([gate] * (d // HEAD_LANES), axis=1) * mod_ref[5:6, :]
            ye_ref[r0:r0 + step, :] = (y * scale).astype(BF16)
        ye_ref[cap:, :] = jnp.zeros((ye_ref.shape[0] - cap, ye_ref.shape[1]), BF16)


def _combine_kernel(base_ref, h_ref, mod_ref, posn_ref, ye_ref, hn_ref, *, n_sub, final_norm):
    bi, kb, ei = pl.program_id(0), pl.program_id(1), pl.program_id(2)

    @pl.when(ei == 0)
    def _():
        hn_ref[...] = h_ref[...]

    slot = lax.broadcasted_iota(jnp.int32, (SUB_TOKENS, COMBINE_WINDOW), 1)
    lane_e = lax.broadcasted_iota(jnp.int32, (SUB_TOKENS, N_EXPERTS), 1)
    for j in range(n_sub):
        tok = slice(j * SUB_TOKENS, (j + 1) * SUB_TOKENS)
        base = pl.multiple_of(base_ref[bi, ei, kb * n_sub + j], 16)
        rel = jnp.sum(jnp.where(lane_e == ei, posn_ref[tok, :], 0), axis=1, keepdims=True) - base
        onehot = jnp.where(slot == rel, 1.0, 0.0).astype(BF16)
        ye = ye_ref[pl.ds(base, COMBINE_WINDOW), :]
        hn_ref[tok, :] += jnp.dot(onehot, ye, preferred_element_type=F32)

    if final_norm:
        @pl.when(ei == pl.num_programs(2) - 1)
        def _():
            hn = hn_ref[...]
            hn_ref[...] = hn * lax.rsqrt(jnp.mean(hn * hn, axis=-1, keepdims=True) + EPS) * mod_ref[6:7, :]


def _expert_choice_ffn(h, mod, u2, logits, w_exp_in, w_exp_out, final_norm=False):
    b, n, d = u2.shape
    cap = max(1, EC_CAPACITY * n // N_EXPERTS)
    n_sub, n_sub_c = min(8, n // SUB_TOKENS), min(8, n // SUB_TOKENS)
    big, big_c = n_sub * SUB_TOKENS, n_sub_c * SUB_TOKENS
    n_big, n_big_c = n // big, n // big_c
    capp = cap + COMBINE_WINDOW
    pos, aff = _route(logits, cap)
    cnt = jnp.sum((pos >= 0).reshape(b, N_EXPERTS, n // SUB_TOKENS, SUB_TOKENS), axis=-1)
    base = (jnp.cumsum(cnt, axis=-1) - cnt) // 16 * 16
    base = base.astype(jnp.int32)
    f = w_exp_out.shape[1]
    ye = pl.pallas_call(
        functools.partial(_experts_kernel, n_sub=n_sub, cap=cap),
        grid_spec=pltpu.PrefetchScalarGridSpec(
            num_scalar_prefetch=1, grid=(b, N_EXPERTS, n_big),
            in_specs=[pl.BlockSpec((None, big, d), lambda bi, ei, kb, base_r: (bi, kb, 0)),
                      pl.BlockSpec((None, MOD_ROWS, d), lambda bi, ei, kb, base_r: (bi, 0, 0)),
                      pl.BlockSpec((None, None, 1, big), lambda bi, ei, kb, base_r: (bi, ei, 0, kb)),
                      pl.BlockSpec((None, None, 1, big), lambda bi, ei, kb, base_r: (bi, ei, 0, kb)),
                      pl.BlockSpec((None, d, 2 * f), lambda bi, ei, kb, base_r: (ei, 0, 0)),
                      pl.BlockSpec((None, f, d), lambda bi, ei, kb, base_r: (ei, 0, 0))],
            out_specs=pl.BlockSpec((None, None, capp, d), lambda bi, ei, kb, base_r: (bi, ei, 0, 0)),
            scratch_shapes=[pltpu.VMEM((capp, d + 2 * HEAD_LANES), F32)]),
        out_shape=jax.ShapeDtypeStruct((b, N_EXPERTS, capp, d), BF16),
        compiler_params=_params(3, 56 * 1024 * 1024), name="experts",
    )(base, u2, mod, aff.reshape(b, N_EXPERTS, 1, n), pos.reshape(b, N_EXPERTS, 1, n), w_exp_in, w_exp_out)
    posn = jnp.swapaxes(pos, 1, 2)
    return pl.pallas_call(
        functools.partial(_combine_kernel, n_sub=n_sub_c, final_norm=final_norm),
        grid_spec=pltpu.PrefetchScalarGridSpec(
            num_scalar_prefetch=1, grid=(b, n_big_c, N_EXPERTS),
            in_specs=[pl.BlockSpec((None, big_c, d), lambda bi, kb, ei, base_r: (bi, kb, 0)),
                      pl.BlockSpec((None, MOD_ROWS, d), lambda bi, kb, ei, base_r: (bi, 0, 0)),
                      pl.BlockSpec((None, big_c, N_EXPERTS), lambda bi, kb, ei, base_r: (bi, kb, 0)),
                      pl.BlockSpec((None, None, capp, d), lambda bi, kb, ei, base_r: (bi, ei, 0, 0))],
            out_specs=pl.BlockSpec((None, big_c, d), lambda bi, kb, ei, base_r: (bi, kb, 0))),
        out_shape=jax.ShapeDtypeStruct((b, n, d), F32),
        compiler_params=_params(3, 56 * 1024 * 1024), name="combine",
    )(base, h, mod, posn, ye)


def _split(z, sizes):
    out, start = [], 0
    for s in sizes:
        out.append(z[..., start:start + s])
        start += s
    return out


HALO = 16


def _fill_ext(ext_ref, x_ref, prev_ref, next_ref):
    ti, nt = pl.program_id(1), pl.num_programs(1)
    tt = x_ref.shape[0]
    ext_ref[0:HALO, :] = jnp.where(ti > 0, prev_ref[...], 0.0)
    ext_ref[HALO:HALO + tt, :] = x_ref[...]
    ext_ref[HALO + tt:, :] = jnp.where(ti < nt - 1, next_ref[...], 0.0)


def _taps(ext_ref, w_ref, tt):
    k = w_ref.shape[0]
    acc = None
    for j in range(k):
        start = HALO - k // 2 + j
        term = w_ref[j:j + 1, :] * ext_ref[start:start + tt, :]
        acc = term if acc is None else acc + term
    return acc


def _short_conv_kernel(x_ref, prev_ref, next_ref, w_ref, b_ref, x1_ref, x2_ref, v_ref, ext_ref):
    _fill_ext(ext_ref, x_ref, prev_ref, next_ref)
    y = _taps(ext_ref, w_ref, x_ref.shape[0]) + b_ref[...]
    x1_ref[...] = y[:, :HYENA_WIDTH]
    x2_ref[...] = y[:, HYENA_WIDTH:2 * HYENA_WIDTH]
    v_ref[...] = y[:, 2 * HYENA_WIDTH:]


def _conformer_kernel(x_ref, prev_ref, next_ref, w_ref, g_ref, b_ref, o_ref, ext_ref):
    _fill_ext(ext_ref, x_ref, prev_ref, next_ref)
    u = _taps(ext_ref, w_ref, x_ref.shape[0])
    mu = jnp.mean(u, axis=-1, keepdims=True)
    var = jnp.mean(jnp.square(u - mu), axis=-1, keepdims=True)
    y = (u - mu) * lax.rsqrt(var + EPS) * g_ref[...] + b_ref[...]
    o_ref[...] = y * jax.nn.sigmoid(y)


def _token_conv(body, x, consts, out_widths, name):
    b, n, w = x.shape
    tt = min(1024, n)
    per = tt // HALO
    last = n // HALO - 1
    in_specs = [pl.BlockSpec((None, tt, w), lambda bi, ti: (bi, ti, 0)),
                pl.BlockSpec((None, HALO, w), lambda bi, ti: (bi, jnp.maximum(ti * per - 1, 0), 0)),
                pl.BlockSpec((None, HALO, w), lambda bi, ti: (bi, jnp.minimum((ti + 1) * per, last), 0))]
    in_specs += [pl.BlockSpec(cst.shape, lambda bi, ti: (0, 0)) for cst in consts]
    outs = tuple(jax.ShapeDtypeStruct((b, n, ow), F32) for ow in out_widths)
    out_specs = tuple(pl.BlockSpec((None, tt, ow), lambda bi, ti: (bi, ti, 0)) for ow in out_widths)
    return pl.pallas_call(body, grid=(b, n // tt), in_specs=in_specs, out_specs=out_specs, out_shape=outs,
                          scratch_shapes=[pltpu.VMEM((tt + 2 * HALO, w), F32)],
                          compiler_params=_params(2), name=name)(x, x, x, *consts)


def _conformer_branch(glu, p):
    return _token_conv(_conformer_kernel, glu, (p['conf_dw_w'], p['conf_ln_g'][None, :], p['conf_ln_b'][None, :]),
                       (CONF_WIDTH,), "conformer")[0]


FILT_LANES = 128
DFT_SHORT = 256


def _split_bf16(x):
    hi = x.astype(BF16)
    return hi, (x - hi.astype(F32)).astype(BF16)


def _dot_split(ah, al, bh, bl):
    dot = lambda u, v: jnp.dot(u, v, preferred_element_type=F32)
    return dot(ah, bh) + (dot(al, bh) + dot(ah, bl))


def _dot_const(mh, ml, x):
    xb = x.astype(BF16)
    return jnp.dot(mh, xb, preferred_element_type=F32) + jnp.dot(ml, xb, preferred_element_type=F32)


def _filter_kernel(z_ref, w1h, w1l, b1, f1, w2h, w2l, b2, f2, w3h, w3l, dl_ref, h_ref, asum_ref, *,
                   tiles_per_dir):
    z = z_ref[...]
    hid = jnp.sin(f1[...] * (_dot_split(*_split_bf16(z), w1h[...], w1l[...]) + b1[...]))
    hid = jnp.sin(f2[...] * (_dot_split(*_split_bf16(hid), w2h[...], w2l[...]) + b2[...]))
    h = _dot_split(*_split_bf16(hid), w3h[...], w3l[...])
    h = h * jnp.exp(-z[:, 0:1] * dl_ref[...])
    h_ref[...] = h

    @pl.when(pl.program_id(0) % tiles_per_dir == 0)
    def _():
        asum_ref[...] = jnp.zeros(asum_ref.shape, F32)
    asum_ref[...] += jnp.sum(jnp.abs(h), axis=0, keepdims=True)


def _normalise_kernel(h_ref, asum_ref, *o_refs, n):
    tt = h_ref.shape[0]
    row = pl.program_id(0) * tt + lax.broadcasted_iota(jnp.int32, h_ref.shape, 0)
    k = jnp.where(row == n, 0.0, h_ref[...] / asum_ref[...])
    for o, o_ref in enumerate(o_refs):
        o_ref[...] = k[:, o * HYENA_WIDTH:(o + 1) * HYENA_WIDTH]


def _hyena_taps(n, p):
    t = jnp.linspace(0.0, 1.0, n, dtype=F32)[:, None]
    bands = (FILT_EMB - 1) // 2
    w = (2.0 * math.pi / n) * jnp.arange(n, dtype=F32)[:, None]
    f = jnp.linspace(1e-4, bands - 1, bands, dtype=F32)[None, :]
    t2, w2pos = jnp.concatenate([t, t[::-1]], axis=0), jnp.concatenate([w, w[::-1]], axis=0)
    z2 = jnp.concatenate([t2, jnp.cos(f * w2pos), -jnp.sin(f * w2pos),
                          jnp.zeros((2 * n, FILT_LANES - FILT_EMB), F32)], axis=-1)
    padc = lambda a: jnp.pad(a, ((0, 0), (0, FILT_LANES - a.shape[1])))
    padr = lambda a: jnp.pad(a, ((0, FILT_LANES - a.shape[0]), (0, 0)))
    w1, w2, w3 = padc(padr(p['filt_w1'])), padc(padr(p['filt_w2'])), padr(p['filt_w3'])
    b1, b2 = padc(p['filt_b1'][None, :]), padc(p['filt_b2'][None, :])
    f1, f2 = padc(p['filt_freq'][0][None, :]), padc(p['filt_freq'][1][None, :])
    deltas = jnp.abs(jnp.linspace(math.log(DECAY_TARGET) / SLOW_DECAY, math.log(DECAY_TARGET) / FAST_DECAY,
                                  HYENA_WIDTH, dtype=F32))
    width = HYENA_ORDER * HYENA_WIDTH
    w3 = w3.reshape(FILT_LANES, HYENA_ORDER, 2, HYENA_WIDTH).transpose(2, 0, 1, 3).reshape(2, FILT_LANES, width)
    dl = jnp.tile(deltas, HYENA_ORDER)[None, :]
    w3h, w3l = _split_bf16(w3)
    tt = min(1024, n)
    tiles_per_dir = n // tt
    cspec = lambda a: pl.BlockSpec(a.shape, lambda i: (0, 0))
    dirspec = lambda rows: pl.BlockSpec((None, rows, width), lambda i: (i // tiles_per_dir, 0, 0))
    tile = lambda w: pl.BlockSpec((tt, w), lambda i: (i, 0))
    small = [*_split_bf16(w1), b1, f1, *_split_bf16(w2), b2, f2]
    h_raw, asum = pl.pallas_call(
        functools.partial(_filter_kernel, tiles_per_dir=tiles_per_dir), grid=(2 * tiles_per_dir,),
        in_specs=[tile(FILT_LANES)] + [cspec(a) for a in small] + [dirspec(FILT_LANES), dirspec(FILT_LANES), cspec(dl)],
        out_specs=(tile(width), dirspec(1)),
        out_shape=(jax.ShapeDtypeStruct((2 * n, width), F32), jax.ShapeDtypeStruct((2, 1, width), F32)),
        compiler_params=_params(1), name="hyena_filter_mlp")(z2, *small, w3h, w3l, dl)
    return pl.pallas_call(
        functools.partial(_normalise_kernel, n=n), grid=(2 * tiles_per_dir,),
        in_specs=[tile(width), dirspec(1)],
        out_specs=tuple(tile(HYENA_WIDTH) for _ in range(HYENA_ORDER)),
        out_shape=tuple(jax.ShapeDtypeStruct((2 * n, HYENA_WIDTH), F32) for _ in range(HYENA_ORDER)),
        compiler_params=_params(1), name="hyena_filter_norm",
    )(h_raw, asum)


def _dft_tables(n):
    n2 = DFT_SHORT if n >= 4 * DFT_SHORT else n
    n1 = n // n2

    def cis(idx):
        ang = (-2.0 * math.pi / n) * idx.astype(F32)
        return jnp.cos(ang), jnp.sin(ang)
    k2 = jnp.arange(n2)
    fr, fi = cis((k2[:, None] * k2[None, :]) % n2 * n1)
    tabs = dict(n1=n1, n2=n2)
    tabs['f_hi'], tabs['f_lo'] = _split_bf16(jnp.stack([fr, fi]))
    k1 = jnp.arange(n1)
    tr, ti = cis(k1[:, None] * k2[None, :])
    tabs['tw'] = jnp.broadcast_to(jnp.stack([tr, ti], axis=1)[..., None], (n1, 2, n2, HEAD_LANES))
    if n1 > 1:
        gr, gi = cis((k1[:, None] * k1[None, :]) % n1 * n2)
        half = n1 // 2
        grh, gih = gr[:, :half], gi[:, :half]
        tabs['m_fwd'] = _split_bf16(jnp.block([[grh, -gih], [gih, grh]]))
        tabs['m_real'] = _split_bf16(jnp.concatenate([gr, gi], axis=0))
        tabs['m_inv'] = _split_bf16(jnp.block([[grh.T, gih.T], [-gih.T, grh.T]]))
    return tabs


def _rowmix_kernel(mh_ref, ml_ref, x_ref, o_ref):
    o_ref[...] = _dot_const(mh_ref[...], ml_ref[...], x_ref[...])


def _rowmix(m, x):
    mh, ml = m
    rin, cols = x.shape
    ct = min(2048, cols)
    return pl.pallas_call(
        _rowmix_kernel, grid=(cols // ct,),
        in_specs=[pl.BlockSpec(mh.shape, lambda i: (0, 0)), pl.BlockSpec(ml.shape, lambda i: (0, 0)),
                  pl.BlockSpec((rin, ct), lambda i: (0, i))],
        out_specs=pl.BlockSpec((mh.shape[0], ct), lambda i: (0, i)),
        out_shape=jax.ShapeDtypeStruct((mh.shape[0], cols), F32), compiler_params=_params(1), name="dft_rowmix",
    )(mh, ml, x)


def _spectral_kernel(x_ref, tw_ref, fh_ref, fl_ref, k_ref, o_ref, *, conv):
    xr, xi = x_ref[0], x_ref[1]
    reps = xr.shape[1] // HEAD_LANES
    tr = jnp.concatenate([tw_ref[0]] * reps, axis=1)
    ti = jnp.concatenate([tw_ref[1]] * reps, axis=1)
    frh, fih, frl, fil = fh_ref[0], fh_ref[1], fl_ref[0], fl_ref[1]

    def dft(ar, ai, conj):
        rr, ii = _dot_const(frh, frl, ar), _dot_const(fih, fil, ai)
        ri, ir = _dot_const(frh, frl, ai), _dot_const(fih, fil, ar)
        return (rr + ii, ri - ir) if conj else (rr - ii, ri + ir)

    yr, yi = dft(xr * tr - xi * ti, xr * ti + xi * tr, False)
    if not conv:
        o_ref[0] = yr * k_ref[...]
        o_ref[1] = yi * k_ref[...]
        return
    kr, ki = k_ref[0], k_ref[1]
    cr, ci = dft(yr * kr - yi * ki, yr * ki + yi * kr, True)
    o_ref[0] = cr * tr + ci * ti
    o_ref[1] = ci * tr - cr * ti


def _spectral(x, k, tabs, conv):
    _, n1, n2, c = x.shape
    slab = pl.BlockSpec((2, None, n2, c), lambda i: (0, i, 0, 0))
    kspec = slab if conv else pl.BlockSpec(k.shape, lambda i: (0, 0))
    return pl.pallas_call(
        functools.partial(_spectral_kernel, conv=conv), grid=(n1,),
        in_specs=[slab, pl.BlockSpec((None, 2, n2, HEAD_LANES), lambda i: (i, 0, 0, 0)),
                  pl.BlockSpec(tabs['f_hi'].shape, lambda i: (0, 0, 0)),
                  pl.BlockSpec(tabs['f_lo'].shape, lambda i: (0, 0, 0)), kspec],
        out_specs=slab, out_shape=jax.ShapeDtypeStruct(x.shape, F32), compiler_params=_params(1),
        name="dft_spectral_conv" if conv else "dft_spectral_filter",
    )(x, tabs['tw'], tabs['f_hi'], tabs['f_lo'], k)


def _filter_spectrum(k, tabs):
    n, c = k.shape
    n1, n2 = tabs['n1'], tabs['n2']
    if n1 > 1:
        x = _rowmix(tabs['m_real'], k.reshape(n1, n2 * c)).reshape(2, n1, n2, c)
    else:
        x = jnp.stack([k, jnp.zeros_like(k)]).reshape(2, 1, n2, c)
    return _spectral(x, jnp.full((1, c), 1.0 / n, F32), tabs, conv=False)


def _rowmix_gate_kernel(mh_ref, ml_ref, x_ref, g_ref, v_ref, s_ref, o_ref):
    o_ref[...] = g_ref[...] * (_dot_const(mh_ref[...], ml_ref[...], x_ref[...]) + s_ref[...] * v_ref[...])


def _gated_long_conv(gate, v, kf, skip, tabs):
    b, n, c = v.shape
    assert b == 2
    n1, n2 = tabs['n1'], tabs['n2']
    if n1 == 1:
        x = jnp.concatenate([v, jnp.zeros_like(v)], axis=1).reshape(2, 1, n2, c)
        y = _spectral(x, kf, tabs, conv=True).reshape(2, n2, c)[:, :n]
        return _hyena_gate(gate, y, v, skip)
    x = _rowmix(tabs['m_fwd'], v.reshape(n1, n2 * c)).reshape(2, n1, n2, c)
    y = _spectral(x, kf, tabs, conv=True).reshape(2 * n1, n2 * c)
    mh, ml = tabs['m_inv']
    cols = n2 * c
    ct = min(2048, cols)
    tile = lambda rows: pl.BlockSpec((rows, ct), lambda i: (0, i))
    return pl.pallas_call(
        _rowmix_gate_kernel, grid=(cols // ct,),
        in_specs=[pl.BlockSpec(mh.shape, lambda i: (0, 0)), pl.BlockSpec(ml.shape, lambda i: (0, 0)),
                  tile(2 * n1), tile(n1), tile(n1), pl.BlockSpec((1, ct), lambda i: (0, 0))],
        out_specs=tile(n1), out_shape=jax.ShapeDtypeStruct((n1, cols), F32), compiler_params=_params(1),
        name="dft_rowmix_gate",
    )(mh, ml, y, gate.reshape(n1, cols), v.reshape(n1, cols), jnp.tile(skip, (1, ct // c))).reshape(2, n, c)


def _gate_kernel(g_ref, y_ref, v_ref, s_ref, o_ref):
    o_ref[...] = g_ref[...] * (y_ref[...] + s_ref[...] * v_ref[...])


def _hyena_gate(gate, y, v, skip):
    b, n, c = v.shape
    tt = min(2048, n)
    tok = pl.BlockSpec((None, tt, c), lambda bi, ti: (bi, ti, 0))
    return pl.pallas_call(_gate_kernel, grid=(b, n // tt),
                          in_specs=[tok, tok, tok, pl.BlockSpec((1, c), lambda bi, ti: (0, 0))], out_specs=tok,
                          out_shape=jax.ShapeDtypeStruct(v.shape, F32), compiler_params=_params(2),
                          name="hyena_gate")(gate, y, v, skip)


def _hyena_branch(hy, p, tabs):
    n = hy.shape[1]
    x1, x2, v = _token_conv(_short_conv_kernel, hy, (p['hyena_short_w'], p['hyena_short_b'][None, :]),
                            (HYENA_WIDTH,) * 3, "hyena_short_conv")
    taps = _hyena_taps(n, p)
    for o, gate in enumerate((x1, x2)):
        v = _gated_long_conv(gate, v, _filter_spectrum(taps[o], tabs), p['hyena_skip'][o][None, :], tabs)
    return v


def _adaln_kernel(c_ref, w_ref, b_ref, o_ref):
    s = c_ref[...]
    s = s * jax.nn.sigmoid(s)
    o_ref[...] = _dot_split(*_split_bf16(s), *_split_bf16(w_ref[...])) + b_ref[...]


def _adaln(cond, w, b):
    d, width = w.shape
    ct = width // 6
    return pl.pallas_call(
        _adaln_kernel, grid=(6,),
        in_specs=[pl.BlockSpec(cond.shape, lambda i: (0, 0)), pl.BlockSpec((d, ct), lambda i: (0, i)),
                  pl.BlockSpec((1, ct), lambda i: (0, i))],
        out_specs=pl.BlockSpec((cond.shape[0], ct), lambda i: (0, i)),
        out_shape=jax.ShapeDtypeStruct((cond.shape[0], width), F32), compiler_params=_params(1), name="adaln",
    )(cond, w, b[None, :])


def _mod_rows(mod, norm_mix_g, norm_ffn_g, final_g, batch):
    sh1, sc1, g1, sh2, sc2, g2 = jnp.split(mod, 6, axis=-1)
    rows = jnp.stack([norm_mix_g * (1.0 + sc1), sh1, g1, norm_ffn_g * (1.0 + sc2), sh2, g2,
                      jnp.broadcast_to(final_g, g1.shape), jnp.zeros_like(g1)], axis=1)
    return jnp.broadcast_to(rows, (batch,) + rows.shape[1:])


def kernel(x, c, ctx, c_ctx, ada_w, ada_b, norm_mix_g, norm_ffn_g, w_in, diff_lambda, diff_subln_g, hyena_short_w, hyena_short_b, filt_w1, filt_b1, filt_freq, filt_w2, filt_b2, filt_w3, hyena_skip, conf_dw_w, conf_ln_g, conf_ln_b, mla_q_norm_g, mla_kv_norm_g, mla_w_uq, mla_w_ukv, w_branch, w_out, w_router, w_exp_in, w_exp_out, final_norm_g):
    depth = w_in.shape[0]
    batch, n_lat, d = x.shape
    n_ctx = ctx.shape[1]
    rope_lat = _rope_operands(n_lat, identity=False)
    rope_ctx = _rope_operands(n_ctx, identity=True)
    dft_lat, dft_ctx = _dft_tables(2 * n_lat), _dft_tables(2 * n_ctx)
    cond = jnp.concatenate([c, c_ctx[None], jnp.zeros((MOD_ROWS - batch - 1, d), F32)], axis=0)
    tile_lat, tile_ctx = min(512, n_lat), min(256, n_ctx)
    h_lat, h_ctx = x, ctx
    for l in range(depth):
        last = l == depth - 1
        p = dict(w_in=w_in[l], diff_subln_g=diff_subln_g[l], hyena_short_w=hyena_short_w[l],
                 hyena_short_b=hyena_short_b[l], filt_w1=filt_w1[l], filt_b1=filt_b1[l], filt_freq=filt_freq[l],
                 filt_w2=filt_w2[l], filt_b2=filt_b2[l], filt_w3=filt_w3[l], hyena_skip=hyena_skip[l],
                 conf_dw_w=conf_dw_w[l], conf_ln_g=conf_ln_g[l], conf_ln_b=conf_ln_b[l],
                 mla_q_norm_g=mla_q_norm_g[l], mla_kv_norm_g=mla_kv_norm_g[l], mla_w_uq=mla_w_uq[l],
                 mla_w_ukv=mla_w_ukv[l], w_branch=w_branch[l], w_out=w_out[l], w_router=w_router[l],
                 w_exp_in=w_exp_in[l], w_exp_out=w_exp_out[l])
        ada = _adaln(cond, ada_w[l], ada_b[l])
        mod_lat = _mod_rows(ada[:batch], norm_mix_g[l], norm_ffn_g[l], final_norm_g, batch)
        mod_ctx = _mod_rows(ada[batch:batch + 1], norm_mix_g[l], norm_ffn_g[l], final_norm_g, batch)
        lam_init = 0.8 - 0.6 * math.exp(-0.3 * l)
        lq1, lk1, lq2, lk2 = diff_lambda[l].astype(F32)
        lam = jnp.reshape(jnp.exp(jnp.sum(lq1 * lk1)) - jnp.exp(jnp.sum(lq2 * lk2)) + lam_init, (1,))
        w_inp, w_mrg = _inproj_weights(p), _merge_weights(p, lam_init)

        qdT_l, kd_l, vdT_l, qmT_l, km_l, vmT_l, hy_l, glu_l = _inproj(h_lat, mod_lat, w_inp, rope_lat, tile=tile_lat)
        qdT_c, kd_c, vdT_c, qmT_c, km_c, vmT_c, hy_c, glu_c = _inproj(h_ctx, mod_ctx, w_inp, rope_ctx, tile=tile_ctx)
        a_lat = _flash_attention(lam, qdT_l, kd_c, vdT_c, kd_l, vdT_l, n_maps=2, tq=min(512, n_lat))
        m_lat = _flash_attention(lam, qmT_l, km_c, vmT_c, km_l, vmT_l, n_maps=1, tq=min(1024, n_lat))
        h_lat, u2_lat, lg_lat = _merge(h_lat, mod_lat, a_lat, _hyena_branch(hy_l, p, dft_lat), _conformer_branch(glu_l, p),
                                       m_lat, w_mrg, tile=min(256, n_lat))
        w_ei, w_eo = p['w_exp_in'].astype(BF16), p['w_exp_out'].astype(BF16)
        h_lat = _expert_choice_ffn(h_lat, mod_lat, u2_lat, lg_lat, w_ei, w_eo, final_norm=last)
        if not last:
            a_ctx = _flash_attention(lam, qdT_c, kd_c, vdT_c, None, None, n_maps=2, tq=n_ctx)
            m_ctx = _flash_attention(lam, qmT_c, km_c, vmT_c, None, None, n_maps=1, tq=n_ctx)
            h_ctx, u2_ctx, lg_ctx = _merge(h_ctx, mod_ctx, a_ctx, _hyena_branch(hy_c, p, dft_ctx),
                                           _conformer_branch(glu_c, p), m_ctx, w_mrg, tile=tile_ctx)
            h_ctx = _expert_choice_ffn(h_ctx, mod_ctx, u2_ctx, lg_ctx, w_ei, w_eo)
    return h_lat
```

```python
import functools
import math

import jax
import jax.numpy as jnp
from jax import lax
from jax.experimental import pallas as pl
from jax.experimental.pallas import tpu as pltpu

GRID_W = 64
ROPE_BASE = 10000.0
EPS = 1e-6

DIFF_HEADS = 4
DIFF_HEAD_DIM = 64
DIFF_V_DIM = 2 * DIFF_HEAD_DIM
HYENA_WIDTH = 256
HYENA_ORDER = 2
FILT_EMB = 33
DECAY_TARGET = 1e-2
FAST_DECAY = 0.3
SLOW_DECAY = 1.5
CONF_WIDTH = 256
MLA_HEADS = 4
MLA_Q_RANK = 256
MLA_KV_RANK = 128
MLA_NOPE = 64
MLA_ROPE = 32
MLA_V = 64
MLA_SCALE = (MLA_NOPE + MLA_ROPE) ** -0.5
N_BRANCH = 4
N_EXPERTS = 16
EC_CAPACITY = 2

DIFF_QK_W = DIFF_HEADS * 2 * DIFF_HEAD_DIM
DIFF_V_W = DIFF_HEADS * DIFF_V_DIM
HYENA_PROJ = (HYENA_ORDER + 1) * HYENA_WIDTH
CONF_PROJ = 2 * CONF_WIDTH
IN_SPLITS = (DIFF_QK_W, DIFF_QK_W, DIFF_V_W, HYENA_PROJ, CONF_PROJ, MLA_Q_RANK, MLA_KV_RANK, MLA_ROPE)
BRANCH_WIDTHS = (DIFF_V_W, HYENA_WIDTH, CONF_WIDTH, MLA_HEADS * MLA_V)

HEAD_LANES = 128
DIFF_V_PAD = 16
ATT_W = DIFF_HEADS * HEAD_LANES
LOG2E = 1.4426950408889634
VMEM_LIMIT_BYTES = 48 * 1024 * 1024
MOD_ROWS = 8

F32 = jnp.float32
BF16 = jnp.bfloat16
_NT = (((1,), (1,)), ((), ()))


def _params(n_axes, vmem=VMEM_LIMIT_BYTES):
    return pltpu.CompilerParams(dimension_semantics=("arbitrary",) * n_axes, vmem_limit_bytes=vmem)


def _flash_kernel(lam_ref, qT_ref, kc_ref, vcT_ref, *rest, n_maps, n_lat_chunks, tk, sum_row):
    if n_lat_chunks:
        kl_ref, vlT_ref, o_ref, acc_ref, m_ref, q2_ref, s_ref = rest
    else:
        o_ref, acc_ref, m_ref, q2_ref = rest
    qT = qT_ref[...]
    tq = qT.shape[1]
    if n_maps == 2:
        row = lax.broadcasted_iota(jnp.int32, qT.shape, 0)
        zero = jnp.zeros_like(qT)
        q2_ref[:, :tq] = jnp.where(row < DIFF_HEAD_DIM, qT, zero)
        q2_ref[:, tq:] = jnp.where(row >= DIFF_HEAD_DIM, qT, zero)
    else:
        q2_ref[...] = qT
    m_ref[...] = jnp.full(m_ref.shape, -jnp.inf, F32)
    acc_ref[...] = jnp.zeros(acc_ref.shape, F32)

    def scores(k):
        return jnp.dot(k, q2_ref[...], preferred_element_type=F32)

    def absorb(s, vT):
        m_prev = m_ref[...]
        m_new = jnp.maximum(m_prev, jnp.max(s, axis=0, keepdims=True))
        alpha = jnp.exp2(m_prev - m_new)
        p = jnp.exp2(s - m_new).astype(BF16)
        acc_ref[...] = alpha * acc_ref[...] + jnp.dot(vT, p, preferred_element_type=F32)
        m_ref[...] = m_new

    def chunk(c):
        return pl.ds(c * tk if isinstance(c, int) else pl.multiple_of(c * tk, tk), tk)

    def keys(c):
        return kl_ref[chunk(c), :]

    def values_t(c):
        return vlT_ref[:, chunk(c)]

    absorb(scores(kc_ref[...]), vcT_ref[...])
    if n_lat_chunks:
        s_ref[0] = scores(keys(0))

        def pair(j, carry):
            c = 2 * j
            s_ref[1] = scores(keys(c + 1))
            absorb(s_ref[0], values_t(c))
            s_ref[0] = scores(keys(c + 2))
            absorb(s_ref[1], values_t(c + 1))
            return carry
        lax.fori_loop(0, n_lat_chunks // 2 - 1, pair, 0)
        s_ref[1] = scores(keys(n_lat_chunks - 1))
        absorb(s_ref[0], values_t(n_lat_chunks - 2))
        absorb(s_ref[1], values_t(n_lat_chunks - 1))
    o = acc_ref[0:HEAD_LANES, :] / acc_ref[sum_row:sum_row + 1, :]
    if n_maps == 2:
        o = o[:, :tq] - lam_ref[0] * o[:, tq:]
        o = o * lax.rsqrt(jnp.mean(o * o, axis=0, keepdims=True) + EPS)
    o_ref[...] = o.T.astype(BF16)


def _flash_stream_kernel(lam_ref, qT_ref, kc_ref, vcT_ref, kl_ref, vlT_ref, o_ref, acc_ref, m_ref, q2_ref, sc_ref,
                         s_ref, *, n_maps, n_lat_chunks, tk, tq, sum_row):
    n_q = qT_ref.shape[1] // tq

    def load_queries(qi):
        qT = qT_ref[:, pl.ds(pl.multiple_of(qi * tq, tq), tq)]
        if n_maps == 2:
            row = lax.broadcasted_iota(jnp.int32, qT.shape, 0)
            zero = jnp.zeros_like(qT)
            q2_ref[:, :tq] = jnp.where(row < DIFF_HEAD_DIM, qT, zero)
            q2_ref[:, tq:] = jnp.where(row >= DIFF_HEAD_DIM, qT, zero)
        else:
            q2_ref[...] = qT

    def scores(k):
        return jnp.dot(k, q2_ref[...], preferred_element_type=F32)

    def absorb(s, vT):
        m_prev = m_ref[...]
        m_new = jnp.maximum(m_prev, jnp.max(s, axis=0, keepdims=True))
        alpha = jnp.exp2(m_prev - m_new)
        p = jnp.exp2(s - m_new).astype(BF16)
        acc_ref[...] = alpha * acc_ref[...] + jnp.dot(vT, p, preferred_element_type=F32)
        m_ref[...] = m_new

    load_queries(0)
    sc_ref[...] = scores(kc_ref[...])

    def query_block(qi, carry):
        m_ref[...] = jnp.full(m_ref.shape, -jnp.inf, F32)
        acc_ref[...] = jnp.zeros(acc_ref.shape, F32)
        def chunk(c):
            return pl.ds(c * tk if isinstance(c, int) else pl.multiple_of(c * tk, tk), tk)

        s_ref[0] = scores(kl_ref[chunk(0), :])
        absorb(sc_ref[...], vcT_ref[...])

        def pair(j, inner):
            c = 2 * j
            s_ref[1] = scores(kl_ref[chunk(c + 1), :])
            absorb(s_ref[0], vlT_ref[:, chunk(c)])
            s_ref[0] = scores(kl_ref[chunk(c + 2), :])
            absorb(s_ref[1], vlT_ref[:, chunk(c + 1)])
            return inner
        lax.fori_loop(0, n_lat_chunks // 2 - 1, pair, 0)
        s_ref[1] = scores(kl_ref[chunk(n_lat_chunks - 1), :])
        absorb(s_ref[0], vlT_ref[:, chunk(n_lat_chunks - 2)])
        load_queries(jnp.minimum(qi + 1, n_q - 1))
        sc_ref[...] = scores(kc_ref[...])
        absorb(s_ref[1], vlT_ref[:, chunk(n_lat_chunks - 1)])
        o = acc_ref[0:HEAD_LANES, :] / acc_ref[sum_row:sum_row + 1, :]
        if n_maps == 2:
            o = o[:, :tq] - lam_ref[0] * o[:, tq:]
            o = o * lax.rsqrt(jnp.mean(o * o, axis=0, keepdims=True) + EPS)
        o_ref[pl.ds(pl.multiple_of(qi * tq, tq), tq), :] = o.T.astype(BF16)
        return carry
    lax.fori_loop(0, n_q, query_block, 0)


def _flash_attention(lam, qT, kc, vcT, kl, vlT, *, n_maps, tq):
    b, _, s = qT.shape
    lc = kc.shape[1]
    mv = vcT.shape[2]
    sum_row = HEAD_LANES if n_maps == 2 else MLA_V
    r = n_maps * tq
    if kl is not None:
        sl = kl.shape[1]
        tk = _lat_chunk(sl)
        n_lat_chunks = sl // tk
        assert n_lat_chunks * tk == sl and s % tq == 0
        return pl.pallas_call(
            functools.partial(_flash_stream_kernel, n_maps=n_maps, n_lat_chunks=n_lat_chunks, tk=tk, tq=tq,
                              sum_row=sum_row),
            grid=(b, DIFF_HEADS),
            in_specs=[pl.BlockSpec(memory_space=pltpu.SMEM),
                      pl.BlockSpec((None, HEAD_LANES, s), lambda bi, hi: (bi, hi, 0)),
                      pl.BlockSpec((None, lc, HEAD_LANES), lambda bi, hi: (bi, 0, hi)),
                      pl.BlockSpec((None, None, mv, lc), lambda bi, hi: (bi, hi, 0, 0)),
                      pl.BlockSpec((None, sl, HEAD_LANES), lambda bi, hi: (bi, 0, hi)),
                      pl.BlockSpec((None, None, mv, sl), lambda bi, hi: (bi, hi, 0, 0))],
            out_specs=pl.BlockSpec((None, s, HEAD_LANES), lambda bi, hi: (bi, 0, hi)),
            out_shape=jax.ShapeDtypeStruct((b, s, ATT_W), BF16),
            scratch_shapes=[pltpu.VMEM((mv, r), F32), pltpu.VMEM((1, r), F32), pltpu.VMEM((HEAD_LANES, r), BF16),
                            pltpu.VMEM((lc, r), F32), pltpu.VMEM((2, tk, r), F32)],
            compiler_params=_params(2, 56 * 1024 * 1024),
            name=f"flash_attention_{n_maps}map",
        )(lam, qT, kc, vcT, kl, vlT)
    in_specs = [
        pl.BlockSpec(memory_space=pltpu.SMEM),
        pl.BlockSpec((None, HEAD_LANES, tq), lambda bi, hi, qi: (bi, hi, qi)),
        pl.BlockSpec((None, lc, HEAD_LANES), lambda bi, hi, qi: (bi, 0, hi)),
        pl.BlockSpec((None, None, mv, lc), lambda bi, hi, qi: (bi, hi, 0, 0)),
    ]
    args = [lam, qT, kc, vcT]
    scratch = [pltpu.VMEM((mv, r), F32), pltpu.VMEM((1, r), F32), pltpu.VMEM((HEAD_LANES, r), BF16)]
    n_lat_chunks, tk = 0, 0
    if kl is not None:
        sl = kl.shape[1]
        tk = _lat_chunk(sl)
        n_lat_chunks = sl // tk
        assert n_lat_chunks % 2 == 0 and n_lat_chunks * tk == sl
        scratch.append(pltpu.VMEM((2, tk, r), F32))
        in_specs += [
            pl.BlockSpec((None, sl, HEAD_LANES), lambda bi, hi, qi: (bi, 0, hi)),
            pl.BlockSpec((None, None, mv, sl), lambda bi, hi, qi: (bi, hi, 0, 0)),
        ]
        args += [kl, vlT]
    return pl.pallas_call(
        functools.partial(_flash_kernel, n_maps=n_maps, n_lat_chunks=n_lat_chunks, tk=tk, sum_row=sum_row),
        grid=(b, DIFF_HEADS, s // tq),
        in_specs=in_specs,
        out_specs=pl.BlockSpec((None, tq, HEAD_LANES), lambda bi, hi, qi: (bi, qi, hi)),
        out_shape=jax.ShapeDtypeStruct((b, s, ATT_W), BF16),
        scratch_shapes=scratch,
        compiler_params=_params(3),
        name=f"flash_attention_{n_maps}map",
    )(*args)


def _lat_chunk(s):
    return min(1024, s // 2)


W_NAT_SPLITS = (DIFF_QK_W, HYENA_PROJ, CONF_PROJ, MLA_Q_RANK, MLA_KV_RANK, HEAD_LANES)


def _modulated_norm(h, a, shift):
    return h * lax.rsqrt(jnp.mean(h * h, axis=-1, keepdims=True) + EPS) * a + shift


def _rope_lanes(x, tab_ref, shift):
    return (x * tab_ref[0] + pltpu.roll(x, shift, 1) * tab_ref[1]
            + pltpu.roll(x, HEAD_LANES - shift, 1) * tab_ref[2])


def _inproj_kernel(h_ref, mod_ref, wnat_ref, wT_ref, wuqT_ref, wukvk_ref, wuvT_ref, gq_ref, gkv_ref,
                   ropeT_d_ref, rope_kd_ref, ropeT_m_ref, rope_km_ref,
                   qdT_ref, kd_ref, vdT_ref, qmT_ref, km_ref, vmT_ref, hy_ref, glu_ref):
    u = _modulated_norm(h_ref[...], mod_ref[0:1, :], mod_ref[1:2, :]).astype(BF16)
    z = jnp.dot(u, wnat_ref[...], preferred_element_type=F32)
    zT = lax.dot_general(wT_ref[...], u, _NT, preferred_element_type=F32)
    offs = [0]
    for w in W_NAT_SPLITS:
        offs.append(offs[-1] + w)
    dk, hy, cf, cq, ckv, krp = (z[:, offs[i]:offs[i + 1]] for i in range(len(W_NAT_SPLITS)))

    for hd in range(DIFF_HEADS):
        sl = slice(hd * HEAD_LANES, (hd + 1) * HEAD_LANES)
        kd_ref[:, sl] = _rope_lanes(dk[:, sl], rope_kd_ref, DIFF_HEAD_DIM // 2).astype(BF16)
    cos_d, sin_d = ropeT_d_ref[0], ropeT_d_ref[1]
    half = DIFF_HEAD_DIM // 2
    for g in range(2 * DIFF_HEADS):
        x1 = zT[g * DIFF_HEAD_DIM:g * DIFF_HEAD_DIM + half]
        x2 = zT[g * DIFF_HEAD_DIM + half:(g + 1) * DIFF_HEAD_DIM]
        qdT_ref[g * DIFF_HEAD_DIM:g * DIFF_HEAD_DIM + half, :] = (x1 * cos_d - x2 * sin_d).astype(BF16)
        qdT_ref[g * DIFF_HEAD_DIM + half:(g + 1) * DIFF_HEAD_DIM, :] = (x1 * sin_d + x2 * cos_d).astype(BF16)
    tail = jnp.where(lax.broadcasted_iota(jnp.int32, (DIFF_V_PAD, zT.shape[1]), 0) == 0, 1.0, 0.0).astype(BF16)
    for hd in range(DIFF_HEADS):
        r0 = DIFF_QK_W + hd * DIFF_V_DIM
        vdT_ref[hd, 0:DIFF_V_DIM, :] = zT[r0:r0 + DIFF_V_DIM].astype(BF16)
        vdT_ref[hd, DIFF_V_DIM:, :] = tail

    hy_ref[...] = hy
    glu_ref[...] = cf[:, :CONF_WIDTH] * jax.nn.sigmoid(cf[:, CONF_WIDTH:])

    cqn = (cq * lax.rsqrt(jnp.mean(cq * cq, axis=-1, keepdims=True) + EPS) * gq_ref[...]).astype(BF16)
    ckvn = (ckv * lax.rsqrt(jnp.mean(ckv * ckv, axis=-1, keepdims=True) + EPS) * gkv_ref[...]).astype(BF16)
    qT = lax.dot_general(wuqT_ref[...], cqn, _NT, preferred_element_type=F32)
    cos_m, sin_m = ropeT_m_ref[0], ropeT_m_ref[1]
    hr = MLA_ROPE // 2
    for hd in range(MLA_HEADS):
        base = hd * HEAD_LANES
        r1 = base + MLA_NOPE
        x1, x2 = qT[r1:r1 + hr], qT[r1 + hr:r1 + 2 * hr]
        qmT_ref[base:r1, :] = qT[base:r1].astype(BF16)
        qmT_ref[r1:r1 + hr, :] = (x1 * cos_m - x2 * sin_m).astype(BF16)
        qmT_ref[r1 + hr:r1 + 2 * hr, :] = (x1 * sin_m + x2 * cos_m).astype(BF16)
        qmT_ref[r1 + 2 * hr:base + HEAD_LANES, :] = jnp.zeros((HEAD_LANES - MLA_NOPE - MLA_ROPE, qT.shape[1]), BF16)
    kn = jnp.dot(ckvn, wukvk_ref[...], preferred_element_type=F32)
    kr = _rope_lanes(krp, rope_km_ref, hr)
    for hd in range(MLA_HEADS):
        sl = slice(hd * HEAD_LANES, (hd + 1) * HEAD_LANES)
        km_ref[:, sl] = (kn[:, sl] + kr).astype(BF16)
    vT = lax.dot_general(wuvT_ref[...], ckvn, _NT, preferred_element_type=F32)
    ones_row = lax.broadcasted_iota(jnp.int32, vT.shape, 0) % HEAD_LANES == MLA_V
    vT = jnp.where(ones_row, 1.0, vT).astype(BF16)
    for hd in range(MLA_HEADS):
        vmT_ref[hd] = vT[hd * HEAD_LANES:(hd + 1) * HEAD_LANES]


def _inproj(h, mod, wts, rope, *, tile):
    b, n, d = h.shape
    const2 = lambda bi, ti: (0, 0)
    tok = lambda w: pl.BlockSpec((None, tile, w), lambda bi, ti: (bi, ti, 0))
    tokT = lambda w: pl.BlockSpec((None, w, tile), lambda bi, ti: (bi, 0, ti))
    full = lambda a: pl.BlockSpec(a.shape, const2)
    in_specs = [tok(d), pl.BlockSpec((None, MOD_ROWS, d), lambda bi, ti: (bi, 0, 0))]
    in_specs += [full(wts[k]) for k in ('w_nat', 'w_T', 'w_uqT', 'w_ukvk', 'w_uvT', 'gq', 'gkv')]
    in_specs += [pl.BlockSpec((2, DIFF_HEAD_DIM // 2, tile), lambda bi, ti: (0, 0, ti)),
                 pl.BlockSpec((3, tile, HEAD_LANES), lambda bi, ti: (0, ti, 0)),
                 pl.BlockSpec((2, MLA_ROPE // 2, tile), lambda bi, ti: (0, 0, ti)),
                 pl.BlockSpec((3, tile, HEAD_LANES), lambda bi, ti: (0, ti, 0))]
    sds = jax.ShapeDtypeStruct
    vrows_d, vrows_m = DIFF_V_DIM + DIFF_V_PAD, HEAD_LANES
    headsT = lambda rows: pl.BlockSpec((None, DIFF_HEADS, rows, tile), lambda bi, ti: (bi, 0, 0, ti))
    out_shape = (sds((b, ATT_W, n), BF16), sds((b, n, ATT_W), BF16), sds((b, DIFF_HEADS, vrows_d, n), BF16),
                 sds((b, ATT_W, n), BF16), sds((b, n, ATT_W), BF16), sds((b, MLA_HEADS, vrows_m, n), BF16),
                 sds((b, n, HYENA_PROJ), F32), sds((b, n, CONF_WIDTH), F32))
    out_specs = (tokT(ATT_W), tok(ATT_W), headsT(vrows_d), tokT(ATT_W), tok(ATT_W), headsT(vrows_m),
                 tok(HYENA_PROJ), tok(CONF_WIDTH))
    return pl.pallas_call(
        _inproj_kernel, grid=(b, n // tile), in_specs=in_specs, out_specs=out_specs, out_shape=out_shape,
        compiler_params=_params(2), name="inproj",
    )(h, mod, wts['w_nat'], wts['w_T'], wts['w_uqT'], wts['w_ukvk'], wts['w_uvT'], wts['gq'], wts['gkv'],
      rope['T_d'], rope['k_d'], rope['T_m'], rope['k_m'])


def _pad_heads(w, width):
    rows = w.shape[0]
    w = w.reshape(rows, MLA_HEADS, width)
    return jnp.pad(w, ((0, 0), (0, 0), (0, HEAD_LANES - width))).reshape(rows, ATT_W)


def _inproj_weights(p):
    d = p['w_in'].shape[0]
    dq, dk, dv, hy, cf, cq, ckv, kr = _split(p['w_in'][:, :sum(IN_SPLITS)], IN_SPLITS)
    krp = jnp.zeros((d, HEAD_LANES), F32).at[:, MLA_NOPE:MLA_NOPE + MLA_ROPE].set(kr)
    w_ukv = p['mla_w_ukv'].reshape(MLA_KV_RANK, MLA_HEADS, MLA_NOPE + MLA_V)
    return dict(
        w_nat=jnp.concatenate([dk, hy, cf, cq, ckv, krp], axis=1).astype(BF16),
        w_T=jnp.concatenate([dq * (DIFF_HEAD_DIM ** -0.5 * LOG2E), dv], axis=1).T.astype(BF16),
        w_uqT=_pad_heads(p['mla_w_uq'] * (MLA_SCALE * LOG2E), MLA_NOPE + MLA_ROPE).T.astype(BF16),
        w_ukvk=_pad_heads(w_ukv[:, :, :MLA_NOPE].reshape(MLA_KV_RANK, -1), MLA_NOPE).astype(BF16),
        w_uvT=_pad_heads(w_ukv[:, :, MLA_NOPE:].reshape(MLA_KV_RANK, -1), MLA_V).T.astype(BF16),
        gq=p['mla_q_norm_g'][None, :], gkv=p['mla_kv_norm_g'][None, :])


def _rope_tables(n_tok, rot_dim):
    rows = n_tok // GRID_W
    row = jnp.repeat(jnp.arange(rows), GRID_W).astype(F32)
    col = jnp.tile(jnp.arange(GRID_W), rows).astype(F32)
    nf = rot_dim // 4
    inv = ROPE_BASE ** (-jnp.arange(nf, dtype=F32) / nf)
    ang = jnp.concatenate([row[:, None] * inv, col[:, None] * inv], axis=-1)
    return jnp.cos(ang), jnp.sin(ang)


def _rope_operands(n_tok, identity):
    if identity:
        cos_d, sin_d = jnp.ones((n_tok, DIFF_HEAD_DIM // 2), F32), jnp.zeros((n_tok, DIFF_HEAD_DIM // 2), F32)
        cos_m, sin_m = jnp.ones((n_tok, MLA_ROPE // 2), F32), jnp.zeros((n_tok, MLA_ROPE // 2), F32)
    else:
        cos_d, sin_d = _rope_tables(n_tok, DIFF_HEAD_DIM)
        cos_m, sin_m = _rope_tables(n_tok, MLA_ROPE)
    z_d, z_m = jnp.zeros_like(sin_d), jnp.zeros_like(sin_m)
    two = lambda a, bb: jnp.tile(jnp.concatenate([a, bb], axis=1), (1, 2))
    lo, hi = jnp.zeros((n_tok, MLA_NOPE), F32), jnp.zeros((n_tok, HEAD_LANES - MLA_NOPE - MLA_ROPE), F32)
    mid = lambda a, bb: jnp.concatenate([lo, a, bb, hi], axis=1)
    return dict(T_d=jnp.stack([cos_d.T, sin_d.T]), T_m=jnp.stack([cos_m.T, sin_m.T]),
                k_d=jnp.stack([two(cos_d, cos_d), two(z_d, sin_d), two(-sin_d, z_d)]),
                k_m=jnp.stack([mid(cos_m, cos_m), mid(z_m, sin_m), mid(-sin_m, z_m)]))


def _merge_kernel(h_ref, mod_ref, a_ref, hy_ref, cf_ref, m_ref, wg_ref, wbd_ref, wbh_ref, wbc_ref, wbm_ref,
                  wo_ref, wrh_ref, wrl_ref, hn_ref, u2_ref, lg_ref):
    h = h_ref[...]
    d = h.shape[1]
    u = _modulated_norm(h, mod_ref[0:1, :], mod_ref[1:2, :]).astype(BF16)
    gates = jax.nn.sigmoid(jnp.dot(u, wg_ref[...], preferred_element_type=F32))
    dot = lambda x, w_ref: jnp.dot(x, w_ref[...], preferred_element_type=F32)
    acc = gates[:, :d] * dot(a_ref[...], wbd_ref)
    acc += gates[:, d:2 * d] * dot(hy_ref[...].astype(BF16), wbh_ref)
    acc += gates[:, 2 * d:3 * d] * dot(cf_ref[...].astype(BF16), wbc_ref)
    acc += gates[:, 3 * d:] * dot(m_ref[...], wbm_ref)
    hn = h + mod_ref[2:3, :] * dot(acc.astype(BF16), wo_ref)
    hn_ref[...] = hn
    u2 = _modulated_norm(hn, mod_ref[3:4, :], mod_ref[4:5, :])
    u2h = u2.astype(BF16)
    u2l = (u2 - u2h.astype(F32)).astype(BF16)
    u2_ref[...] = u2h
    lg_ref[...] = dot(u2h, wrh_ref) + (dot(u2l, wrh_ref) + dot(u2h, wrl_ref))


def _merge(h, mod, a, hyv, cfv, m, wts, *, tile):
    b, n, d = h.shape
    const2 = lambda bi, ti: (0, 0)
    tok = lambda w: pl.BlockSpec((None, tile, w), lambda bi, ti: (bi, ti, 0))
    names = ('w_gate', 'w_bd', 'w_bh', 'w_bc', 'w_bm', 'w_out', 'w_rh', 'w_rl')
    in_specs = [tok(d), pl.BlockSpec((None, MOD_ROWS, d), lambda bi, ti: (bi, 0, 0)),
                tok(ATT_W), tok(HYENA_WIDTH), tok(CONF_WIDTH), tok(ATT_W)]
    in_specs += [pl.BlockSpec(wts[k].shape, const2, pipeline_mode=pl.Buffered(1)) for k in names]
    sds = jax.ShapeDtypeStruct
    return pl.pallas_call(
        _merge_kernel, grid=(b, n // tile), in_specs=in_specs,
        out_specs=(tok(d), tok(d), tok(HEAD_LANES)),
        out_shape=(sds((b, n, d), F32), sds((b, n, d), BF16), sds((b, n, HEAD_LANES), F32)),
        compiler_params=_params(2, 56 * 1024 * 1024), name="merge",
    )(h, mod, a, hyv, cfv, m, *[wts[k] for k in names])


def _merge_weights(p, lam_init):
    d = p['w_out'].shape[0]
    wb_d, wb_h, wb_c, wb_m = (w.T for w in _split(p['w_branch'].T, BRANCH_WIDTHS))
    wb_d = wb_d * (jnp.tile(p['diff_subln_g'], DIFF_HEADS) * (1.0 - lam_init))[:, None]
    wb_m = jnp.pad(wb_m.reshape(MLA_HEADS, MLA_V, d), ((0, 0), (0, HEAD_LANES - MLA_V), (0, 0))).reshape(ATT_W, d)
    w_r = jnp.pad(p['w_router'], ((0, 0), (0, HEAD_LANES - N_EXPERTS)))
    w_rh = w_r.astype(BF16)
    return dict(w_gate=p['w_in'][:, sum(IN_SPLITS):].astype(BF16), w_bd=wb_d.astype(BF16), w_bh=wb_h.astype(BF16),
                w_bc=wb_c.astype(BF16), w_bm=wb_m.astype(BF16), w_out=p['w_out'].astype(BF16),
                w_rh=w_rh, w_rl=(w_r - w_rh.astype(F32)).astype(BF16))


SUB_TOKENS = 256
GATHER_WINDOW = 272
COMBINE_TOKENS = 128
COMBINE_WINDOW = 256
ROUTE_MIN_ROWS = 8


def _excl_scan(x, lane, row):
    inc = x
    s = 1
    while s < HEAD_LANES:
        inc = inc + jnp.where(lane >= s, pltpu.roll(inc, s, 2), 0.0)
        s *= 2
    tot = jnp.sum(x, axis=2, keepdims=True) + jnp.zeros_like(x)
    off = tot
    s = 1
    while s < x.shape[1]:
        off = off + jnp.where(row >= s, pltpu.roll(off, s, 1), 0.0)
        s *= 2
    return inc - x + (off - tot)


def _route_kernel(lg_ref, pos_ref, aff_ref, *, n_valid, cap):
    lg = lg_ref[...]
    shape = lg.shape
    lane = lax.broadcasted_iota(jnp.int32, shape, 2)
    row = lax.broadcasted_iota(jnp.int32, shape, 1)
    e = jnp.exp(lg - jnp.max(lg, axis=0, keepdims=True))
    aff = e / jnp.sum(e, axis=0, keepdims=True)
    bits = jnp.where(row * HEAD_LANES + lane < n_valid, pltpu.bitcast(aff, jnp.int32), -1)

    def count(mask):
        c = jnp.sum(jnp.where(mask, 1.0, 0.0), axis=2, keepdims=True)
        return jnp.sum(c, axis=1, keepdims=True)

    def step(i, thr):
        cand = thr | (jnp.int32(1) << (30 - i))
        return jnp.where(count(bits >= cand) >= cap, cand, thr)
    thr = lax.fori_loop(0, 31, step, jnp.zeros((shape[0], 1, 1), jnp.int32))
    gt = bits > thr
    eq = bits == thr
    need = cap - count(gt)
    tie_rank = _excl_scan(jnp.where(eq, 1.0, 0.0), lane, row)
    sel = gt | (eq & (tie_rank < need))
    pos = _excl_scan(jnp.where(sel, 1.0, 0.0), lane, row)
    pos_ref[...] = jnp.where(sel, pos.astype(jnp.int32), -1)
    aff_ref[...] = aff


def _route(logits, cap):
    b, n, _ = logits.shape
    rows = max(ROUTE_MIN_ROWS, n // HEAD_LANES)
    lg = jnp.swapaxes(logits[..., :N_EXPERTS], 1, 2)
    lg = jnp.pad(lg, ((0, 0), (0, 0), (0, rows * HEAD_LANES - n))).reshape(b, N_EXPERTS, rows, HEAD_LANES)
    spec = pl.BlockSpec((None, N_EXPERTS, rows, HEAD_LANES), lambda bi: (bi, 0, 0, 0))
    pos, aff = pl.pallas_call(
        functools.partial(_route_kernel, n_valid=n, cap=cap), grid=(b,), in_specs=[spec], out_specs=(spec, spec),
        out_shape=(jax.ShapeDtypeStruct(lg.shape, jnp.int32), jax.ShapeDtypeStruct(lg.shape, F32)),
        compiler_params=_params(1), name="route",
    )(lg)
    flat = lambda a: a.reshape(b, N_EXPERTS, rows * HEAD_LANES)[..., :n]
    return flat(pos), flat(aff)


def _experts_kernel(base_ref, u_ref, mod_ref, aff_ref, pos_ref, win_ref, wout_ref, ye_ref, xe_ref, *,
                    n_sub, cap):
    bi, ei, kb = pl.program_id(0), pl.program_id(1), pl.program_id(2)
    d = u_ref.shape[1]

    @pl.when(kb == 0)
    def _():
        xe_ref[...] = jnp.zeros(xe_ref.shape, F32)

    slot = lax.broadcasted_iota(jnp.int32, (GATHER_WINDOW, SUB_TOKENS), 0)
    ones = jnp.ones((SUB_TOKENS, HEAD_LANES), BF16)
    for j in range(n_sub):
        tok = slice(j * SUB_TOKENS, (j + 1) * SUB_TOKENS)
        base = pl.multiple_of(base_ref[bi, ei, kb * n_sub + j], 16)
        match = slot == (pos_ref[:, tok] - base)
        onehot = jnp.where(match, 1.0, 0.0).astype(BF16)
        g = aff_ref[:, tok]
        g_hi = g.astype(BF16).astype(F32)
        sel_hi = jnp.where(match, g_hi, 0.0).astype(BF16)
        sel_lo = jnp.where(match, g - g_hi, 0.0).astype(BF16)
        rows = pl.ds(base, GATHER_WINDOW)
        xe_ref[rows, :d] += jnp.dot(onehot, u_ref[tok, :], preferred_element_type=F32)
        xe_ref[rows, d:d + HEAD_LANES] += jnp.dot(sel_hi, ones, preferred_element_type=F32)
        xe_ref[rows, d + HEAD_LANES:] += jnp.dot(sel_lo, ones, preferred_element_type=F32)

    @pl.when(kb == pl.num_programs(2) - 1)
    def _():
        f = wout_ref.shape[0]
        step = min(512, cap)
        for r0 in range(0, cap, step):
            x = xe_ref[r0:r0 + step, :d].astype(BF16)
            gate = xe_ref[r0:r0 + step, d:d + HEAD_LANES] + xe_ref[r0:r0 + step, d + HEAD_LANES:]
            hgu = jnp.dot(x, win_ref[...], preferred_element_type=F32)
            act = (jax.nn.silu(hgu[:, :f]) * hgu[:, f:]).astype(BF16)
            y = jnp.dot(act, wout_ref[...], preferred_element_type=F32)
            scale = jnp.concatenate([gate] * (d // HEAD_LANES), axis=1) * mod_ref[5:6, :]
            ye_ref[r0:r0 + step, :] = (y * scale).astype(BF16)
        ye_ref[cap:, :] = jnp.zeros((ye_ref.shape[0] - cap, ye_ref.shape[1]), BF16)


def _combine_kernel(base_ref, h_ref, mod_ref, posn_ref, ye_ref, hn_ref, *, n_sub, final_norm):
    bi, kb, ei = pl.program_id(0), pl.program_id(1), pl.program_id(2)

    @pl.when(ei == 0)
    def _():
        hn_ref[...] = h_ref[...]

    slot = lax.broadcasted_iota(jnp.int32, (COMBINE_TOKENS, COMBINE_WINDOW), 1)
    lane_e = lax.broadcasted_iota(jnp.int32, (COMBINE_TOKENS, N_EXPERTS), 1)
    for j in range(n_sub):
        tok = slice(j * COMBINE_TOKENS, (j + 1) * COMBINE_TOKENS)
        base = pl.multiple_of(base_ref[bi, ei, kb * n_sub + j], 16)
        rel = jnp.sum(jnp.where(lane_e == ei, posn_ref[tok, :], 0), axis=1, keepdims=True) - base
        onehot = jnp.where(slot == rel, 1.0, 0.0).astype(BF16)
        ye = ye_ref[pl.ds(base, COMBINE_WINDOW), :]
        hn_ref[tok, :] += jnp.dot(onehot, ye, preferred_element_type=F32)

    if final_norm:
        @pl.when(ei == pl.num_programs(2) - 1)
        def _():
            hn = hn_ref[...]
            hn_ref[...] = hn * lax.rsqrt(jnp.mean(hn * hn, axis=-1, keepdims=True) + EPS) * mod_ref[6:7, :]


def _expert_choice_ffn(h, mod, u2, logits, w_exp_in, w_exp_out, final_norm=False):
    b, n, d = u2.shape
    cap = max(1, EC_CAPACITY * n // N_EXPERTS)
    n_sub, n_sub_c = min(8, n // SUB_TOKENS), min(16, n // COMBINE_TOKENS)
    big, big_c = n_sub * SUB_TOKENS, n_sub_c * COMBINE_TOKENS
    n_big, n_big_c = n // big, n // big_c
    capp = cap + GATHER_WINDOW
    pos, aff = _route(logits, cap)

    def window_starts(sub):
        cnt = jnp.sum((pos >= 0).reshape(b, N_EXPERTS, n // sub, sub), axis=-1)
        return ((jnp.cumsum(cnt, axis=-1) - cnt) // 16 * 16).astype(jnp.int32)
    base, base_c = window_starts(SUB_TOKENS), window_starts(COMBINE_TOKENS)
    f = w_exp_out.shape[1]
    ye = pl.pallas_call(
        functools.partial(_experts_kernel, n_sub=n_sub, cap=cap),
        grid_spec=pltpu.PrefetchScalarGridSpec(
            num_scalar_prefetch=1, grid=(b, N_EXPERTS, n_big),
            in_specs=[pl.BlockSpec((None, big, d), lambda bi, ei, kb, base_r: (bi, kb, 0)),
                      pl.BlockSpec((None, MOD_ROWS, d), lambda bi, ei, kb, base_r: (bi, 0, 0)),
                      pl.BlockSpec((None, None, 1, big), lambda bi, ei, kb, base_r: (bi, ei, 0, kb)),
                      pl.BlockSpec((None, None, 1, big), lambda bi, ei, kb, base_r: (bi, ei, 0, kb)),
                      pl.BlockSpec((None, d, 2 * f), lambda bi, ei, kb, base_r: (ei, 0, 0)),
                      pl.BlockSpec((None, f, d), lambda bi, ei, kb, base_r: (ei, 0, 0))],
            out_specs=pl.BlockSpec((None, None, capp, d), lambda bi, ei, kb, base_r: (bi, ei, 0, 0)),
            scratch_shapes=[pltpu.VMEM((capp, d + 2 * HEAD_LANES), F32)]),
        out_shape=jax.ShapeDtypeStruct((b, N_EXPERTS, capp, d), BF16),
        compiler_params=_params(3, 56 * 1024 * 1024), name="experts",
    )(base, u2, mod, aff.reshape(b, N_EXPERTS, 1, n), pos.reshape(b, N_EXPERTS, 1, n), w_exp_in, w_exp_out)
    posn = jnp.swapaxes(pos, 1, 2)
    return pl.pallas_call(
        functools.partial(_combine_kernel, n_sub=n_sub_c, final_norm=final_norm),
        grid_spec=pltpu.PrefetchScalarGridSpec(
            num_scalar_prefetch=1, grid=(b, n_big_c, N_EXPERTS),
            in_specs=[pl.BlockSpec((None, big_c, d), lambda bi, kb, ei, base_r: (bi, kb, 0)),
                      pl.BlockSpec((None, MOD_ROWS, d), lambda bi, kb, ei, base_r: (bi, 0, 0)),
                      pl.BlockSpec((None, big_c, N_EXPERTS), lambda bi, kb, ei, base_r: (bi, kb, 0)),
                      pl.BlockSpec((None, None, capp, d), lambda bi, kb, ei, base_r: (bi, ei, 0, 0))],
            out_specs=pl.BlockSpec((None, big_c, d), lambda bi, kb, ei, base_r: (bi, kb, 0))),
        out_shape=jax.ShapeDtypeStruct((b, n, d), F32),
        compiler_params=_params(3, 56 * 1024 * 1024), name="combine",
    )(base_c, h, mod, posn, ye)


def _split(z, sizes):
    out, start = [], 0
    for s in sizes:
        out.append(z[..., start:start + s])
        start += s
    return out


HALO = 16


def _fill_ext(ext_ref, x_ref, prev_ref, next_ref):
    ti, nt = pl.program_id(1), pl.num_programs(1)
    tt = x_ref.shape[0]
    ext_ref[0:HALO, :] = jnp.where(ti > 0, prev_ref[...], 0.0)
    ext_ref[HALO:HALO + tt, :] = x_ref[...]
    ext_ref[HALO + tt:, :] = jnp.where(ti < nt - 1, next_ref[...], 0.0)


def _taps(ext_ref, w_ref, tt):
    k = w_ref.shape[0]
    acc = None
    for j in range(k):
        start = HALO - k // 2 + j
        term = w_ref[j:j + 1, :] * ext_ref[start:start + tt, :]
        acc = term if acc is None else acc + term
    return acc


def _short_conv_kernel(x_ref, prev_ref, next_ref, w_ref, b_ref, x1_ref, x2_ref, v_ref, ext_ref):
    _fill_ext(ext_ref, x_ref, prev_ref, next_ref)
    y = _taps(ext_ref, w_ref, x_ref.shape[0]) + b_ref[...]
    x1_ref[...] = y[:, :HYENA_WIDTH]
    x2_ref[...] = y[:, HYENA_WIDTH:2 * HYENA_WIDTH]
    v_ref[...] = y[:, 2 * HYENA_WIDTH:]


def _conformer_kernel(x_ref, prev_ref, next_ref, w_ref, g_ref, b_ref, o_ref, ext_ref):
    _fill_ext(ext_ref, x_ref, prev_ref, next_ref)
    u = _taps(ext_ref, w_ref, x_ref.shape[0])
    mu = jnp.mean(u, axis=-1, keepdims=True)
    var = jnp.mean(jnp.square(u - mu), axis=-1, keepdims=True)
    y = (u - mu) * lax.rsqrt(var + EPS) * g_ref[...] + b_ref[...]
    o_ref[...] = y * jax.nn.sigmoid(y)


def _token_conv(body, x, consts, out_widths, name):
    b, n, w = x.shape
    tt = min(1024, n)
    per = tt // HALO
    last = n // HALO - 1
    in_specs = [pl.BlockSpec((None, tt, w), lambda bi, ti: (bi, ti, 0)),
                pl.BlockSpec((None, HALO, w), lambda bi, ti: (bi, jnp.maximum(ti * per - 1, 0), 0)),
                pl.BlockSpec((None, HALO, w), lambda bi, ti: (bi, jnp.minimum((ti + 1) * per, last), 0))]
    in_specs += [pl.BlockSpec(cst.shape, lambda bi, ti: (0, 0)) for cst in consts]
    outs = tuple(jax.ShapeDtypeStruct((b, n, ow), F32) for ow in out_widths)
    out_specs = tuple(pl.BlockSpec((None, tt, ow), lambda bi, ti: (bi, ti, 0)) for ow in out_widths)
    return pl.pallas_call(body, grid=(b, n // tt), in_specs=in_specs, out_specs=out_specs, out_shape=outs,
                          scratch_shapes=[pltpu.VMEM((tt + 2 * HALO, w), F32)],
                          compiler_params=_params(2), name=name)(x, x, x, *consts)


def _conformer_branch(glu, p):
    return _token_conv(_conformer_kernel, glu, (p['conf_dw_w'], p['conf_ln_g'][None, :], p['conf_ln_b'][None, :]),
                       (CONF_WIDTH,), "conformer")[0]


FILT_LANES = 128
DFT_SHORT = 256


def _split_bf16(x):
    hi = x.astype(BF16)
    return hi, (x - hi.astype(F32)).astype(BF16)


def _dot_split(ah, al, bh, bl):
    dot = lambda u, v: jnp.dot(u, v, preferred_element_type=F32)
    return dot(ah, bh) + (dot(al, bh) + dot(ah, bl))


def _dot_const(mh, ml, x):
    xb = x.astype(BF16)
    return jnp.dot(mh, xb, preferred_element_type=F32) + jnp.dot(ml, xb, preferred_element_type=F32)


def _filter_kernel(z_ref, w1h, w1l, b1, f1, w2h, w2l, b2, f2, w3h, w3l, dl_ref, h_ref, asum_ref, *,
                   tiles_per_dir):
    z = z_ref[...]
    hid = jnp.sin(f1[...] * (_dot_split(*_split_bf16(z), w1h[...], w1l[...]) + b1[...]))
    hid = jnp.sin(f2[...] * (_dot_split(*_split_bf16(hid), w2h[...], w2l[...]) + b2[...]))
    h = _dot_split(*_split_bf16(hid), w3h[...], w3l[...])
    h = h * jnp.exp(-z[:, 0:1] * dl_ref[...])
    h_ref[...] = h

    @pl.when(pl.program_id(0) % tiles_per_dir == 0)
    def _():
        asum_ref[...] = jnp.zeros(asum_ref.shape, F32)
    asum_ref[...] += jnp.sum(jnp.abs(h), axis=0, keepdims=True)


def _normalise_kernel(h_ref, asum_ref, *o_refs, n):
    tt = h_ref.shape[0]
    row = pl.program_id(0) * tt + lax.broadcasted_iota(jnp.int32, h_ref.shape, 0)
    k = jnp.where(row == n, 0.0, h_ref[...] / asum_ref[...])
    for o, o_ref in enumerate(o_refs):
        o_ref[...] = k[:, o * HYENA_WIDTH:(o + 1) * HYENA_WIDTH]


def _hyena_taps(n, p):
    t = jnp.linspace(0.0, 1.0, n, dtype=F32)[:, None]
    bands = (FILT_EMB - 1) // 2
    w = (2.0 * math.pi / n) * jnp.arange(n, dtype=F32)[:, None]
    f = jnp.linspace(1e-4, bands - 1, bands, dtype=F32)[None, :]
    t2, w2pos = jnp.concatenate([t, t[::-1]], axis=0), jnp.concatenate([w, w[::-1]], axis=0)
    z2 = jnp.concatenate([t2, jnp.cos(f * w2pos), -jnp.sin(f * w2pos),
                          jnp.zeros((2 * n, FILT_LANES - FILT_EMB), F32)], axis=-1)
    padc = lambda a: jnp.pad(a, ((0, 0), (0, FILT_LANES - a.shape[1])))
    padr = lambda a: jnp.pad(a, ((0, FILT_LANES - a.shape[0]), (0, 0)))
    w1, w2, w3 = padc(padr(p['filt_w1'])), padc(padr(p['filt_w2'])), padr(p['filt_w3'])
    b1, b2 = padc(p['filt_b1'][None, :]), padc(p['filt_b2'][None, :])
    f1, f2 = padc(p['filt_freq'][0][None, :]), padc(p['filt_freq'][1][None, :])
    deltas = jnp.abs(jnp.linspace(math.log(DECAY_TARGET) / SLOW_DECAY, math.log(DECAY_TARGET) / FAST_DECAY,
                                  HYENA_WIDTH, dtype=F32))
    width = HYENA_ORDER * HYENA_WIDTH
    w3 = w3.reshape(FILT_LANES, HYENA_ORDER, 2, HYENA_WIDTH).transpose(2, 0, 1, 3).reshape(2, FILT_LANES, width)
    dl = jnp.tile(deltas, HYENA_ORDER)[None, :]
    w3h, w3l = _split_bf16(w3)
    tt = min(1024, n)
    tiles_per_dir = n // tt
    cspec = lambda a: pl.BlockSpec(a.shape, lambda i: (0, 0))
    dirspec = lambda rows: pl.BlockSpec((None, rows, width), lambda i: (i // tiles_per_dir, 0, 0))
    tile = lambda w: pl.BlockSpec((tt, w), lambda i: (i, 0))
    small = [*_split_bf16(w1), b1, f1, *_split_bf16(w2), b2, f2]
    h_raw, asum = pl.pallas_call(
        functools.partial(_filter_kernel, tiles_per_dir=tiles_per_dir), grid=(2 * tiles_per_dir,),
        in_specs=[tile(FILT_LANES)] + [cspec(a) for a in small] + [dirspec(FILT_LANES), dirspec(FILT_LANES), cspec(dl)],
        out_specs=(tile(width), dirspec(1)),
        out_shape=(jax.ShapeDtypeStruct((2 * n, width), F32), jax.ShapeDtypeStruct((2, 1, width), F32)),
        compiler_params=_params(1), name="hyena_filter_mlp")(z2, *small, w3h, w3l, dl)
    return pl.pallas_call(
        functools.partial(_normalise_kernel, n=n), grid=(2 * tiles_per_dir,),
        in_specs=[tile(width), dirspec(1)],
        out_specs=tuple(tile(HYENA_WIDTH) for _ in range(HYENA_ORDER)),
        out_shape=tuple(jax.ShapeDtypeStruct((2 * n, HYENA_WIDTH), F32) for _ in range(HYENA_ORDER)),
        compiler_params=_params(1), name="hyena_filter_norm",
    )(h_raw, asum)


def _dft_tables(n):
    n2 = DFT_SHORT if n >= 4 * DFT_SHORT else n
    n1 = n // n2

    def cis(idx):
        ang = (-2.0 * math.pi / n) * idx.astype(F32)
        return jnp.cos(ang), jnp.sin(ang)
    k2 = jnp.arange(n2)
    fr, fi = cis((k2[:, None] * k2[None, :]) % n2 * n1)
    tabs = dict(n1=n1, n2=n2)
    tabs['f_hi'], tabs['f_lo'] = _split_bf16(jnp.stack([fr, fi]))
    k1 = jnp.arange(n1)
    tr, ti = cis(k1[:, None] * k2[None, :])
    tabs['tw'] = jnp.broadcast_to(jnp.stack([tr, ti], axis=1)[..., None], (n1, 2, n2, HEAD_LANES))
    if n1 > 1:
        gr, gi = cis((k1[:, None] * k1[None, :]) % n1 * n2)
        half = n1 // 2
        grh, gih = gr[:, :half], gi[:, :half]
        tabs['m_fwd'] = _split_bf16(jnp.block([[grh, -gih], [gih, grh]]))
        tabs['m_real'] = _split_bf16(jnp.concatenate([gr, gi], axis=0))
        tabs['m_inv'] = _split_bf16(jnp.block([[grh.T, gih.T], [-gih.T, grh.T]]))
    return tabs


def _rowmix_kernel(mh_ref, ml_ref, x_ref, o_ref):
    o_ref[...] = _dot_const(mh_ref[...], ml_ref[...], x_ref[...])


def _rowmix(m, x):
    mh, ml = m
    rin, cols = x.shape
    ct = min(2048, cols)
    return pl.pallas_call(
        _rowmix_kernel, grid=(cols // ct,),
        in_specs=[pl.BlockSpec(mh.shape, lambda i: (0, 0)), pl.BlockSpec(ml.shape, lambda i: (0, 0)),
                  pl.BlockSpec((rin, ct), lambda i: (0, i))],
        out_specs=pl.BlockSpec((mh.shape[0], ct), lambda i: (0, i)),
        out_shape=jax.ShapeDtypeStruct((mh.shape[0], cols), F32), compiler_params=_params(1), name="dft_rowmix",
    )(mh, ml, x)


def _spectral_kernel(x_ref, tw_ref, fh_ref, fl_ref, k_ref, o_ref, *, conv):
    xr, xi = x_ref[0], x_ref[1]
    reps = xr.shape[1] // HEAD_LANES
    tr = jnp.concatenate([tw_ref[0]] * reps, axis=1)
    ti = jnp.concatenate([tw_ref[1]] * reps, axis=1)
    frh, fih, frl, fil = fh_ref[0], fh_ref[1], fl_ref[0], fl_ref[1]

    def dft(ar, ai, conj):
        rr, ii = _dot_const(frh, frl, ar), _dot_const(fih, fil, ai)
        ri, ir = _dot_const(frh, frl, ai), _dot_const(fih, fil, ar)
        return (rr + ii, ri - ir) if conj else (rr - ii, ri + ir)

    yr, yi = dft(xr * tr - xi * ti, xr * ti + xi * tr, False)
    if not conv:
        o_ref[0] = yr * k_ref[...]
        o_ref[1] = yi * k_ref[...]
        return
    kr, ki = k_ref[0], k_ref[1]
    cr, ci = dft(yr * kr - yi * ki, yr * ki + yi * kr, True)
    o_ref[0] = cr * tr + ci * ti
    o_ref[1] = ci * tr - cr * ti


def _spectral(x, k, tabs, conv):
    _, n1, n2, c = x.shape
    slab = pl.BlockSpec((2, None, n2, c), lambda i: (0, i, 0, 0))
    kspec = slab if conv else pl.BlockSpec(k.shape, lambda i: (0, 0))
    return pl.pallas_call(
        functools.partial(_spectral_kernel, conv=conv), grid=(n1,),
        in_specs=[slab, pl.BlockSpec((None, 2, n2, HEAD_LANES), lambda i: (i, 0, 0, 0)),
                  pl.BlockSpec(tabs['f_hi'].shape, lambda i: (0, 0, 0)),
                  pl.BlockSpec(tabs['f_lo'].shape, lambda i: (0, 0, 0)), kspec],
        out_specs=slab, out_shape=jax.ShapeDtypeStruct(x.shape, F32), compiler_params=_params(1),
        name="dft_spectral_conv" if conv else "dft_spectral_filter",
    )(x, tabs['tw'], tabs['f_hi'], tabs['f_lo'], k)


def _filter_spectrum(k, tabs):
    n, c = k.shape
    n1, n2 = tabs['n1'], tabs['n2']
    if n1 > 1:
        x = _rowmix(tabs['m_real'], k.reshape(n1, n2 * c)).reshape(2, n1, n2, c)
    else:
        x = jnp.stack([k, jnp.zeros_like(k)]).reshape(2, 1, n2, c)
    return _spectral(x, jnp.full((1, c), 1.0 / n, F32), tabs, conv=False)


def _rowmix_gate_kernel(mh_ref, ml_ref, x_ref, g_ref, v_ref, s_ref, o_ref):
    o_ref[...] = g_ref[...] * (_dot_const(mh_ref[...], ml_ref[...], x_ref[...]) + s_ref[...] * v_ref[...])


def _gated_long_conv(gate, v, kf, skip, tabs):
    b, n, c = v.shape
    assert b == 2
    n1, n2 = tabs['n1'], tabs['n2']
    if n1 == 1:
        x = jnp.concatenate([v, jnp.zeros_like(v)], axis=1).reshape(2, 1, n2, c)
        y = _spectral(x, kf, tabs, conv=True).reshape(2, n2, c)[:, :n]
        return _hyena_gate(gate, y, v, skip)
    x = _rowmix(tabs['m_fwd'], v.reshape(n1, n2 * c)).reshape(2, n1, n2, c)
    y = _spectral(x, kf, tabs, conv=True).reshape(2 * n1, n2 * c)
    mh, ml = tabs['m_inv']
    cols = n2 * c
    ct = min(2048, cols)
    tile = lambda rows: pl.BlockSpec((rows, ct), lambda i: (0, i))
    return pl.pallas_call(
        _rowmix_gate_kernel, grid=(cols // ct,),
        in_specs=[pl.BlockSpec(mh.shape, lambda i: (0, 0)), pl.BlockSpec(ml.shape, lambda i: (0, 0)),
                  tile(2 * n1), tile(n1), tile(n1), pl.BlockSpec((1, ct), lambda i: (0, 0))],
        out_specs=tile(n1), out_shape=jax.ShapeDtypeStruct((n1, cols), F32), compiler_params=_params(1),
        name="dft_rowmix_gate",
    )(mh, ml, y, gate.reshape(n1, cols), v.reshape(n1, cols), jnp.tile(skip, (1, ct // c))).reshape(2, n, c)


def _gate_kernel(g_ref, y_ref, v_ref, s_ref, o_ref):
    o_ref[...] = g_ref[...] * (y_ref[...] + s_ref[...] * v_ref[...])


def _hyena_gate(gate, y, v, skip):
    b, n, c = v.shape
    tt = min(2048, n)
    tok = pl.BlockSpec((None, tt, c), lambda bi, ti: (bi, ti, 0))
    return pl.pallas_call(_gate_kernel, grid=(b, n // tt),
                          in_specs=[tok, tok, tok, pl.BlockSpec((1, c), lambda bi, ti: (0, 0))], out_specs=tok,
                          out_shape=jax.ShapeDtypeStruct(v.shape, F32), compiler_params=_params(2),
                          name="hyena_gate")(gate, y, v, skip)


def _hyena_branch(hy, p, tabs):
    n = hy.shape[1]
    x1, x2, v = _token_conv(_short_conv_kernel, hy, (p['hyena_short_w'], p['hyena_short_b'][None, :]),
                            (HYENA_WIDTH,) * 3, "hyena_short_conv")
    taps = _hyena_taps(n, p)
    for o, gate in enumerate((x1, x2)):
        v = _gated_long_conv(gate, v, _filter_spectrum(taps[o], tabs), p['hyena_skip'][o][None, :], tabs)
    return v


def _adaln_kernel(c_ref, w_ref, b_ref, o_ref):
    s = c_ref[...]
    s = s * jax.nn.sigmoid(s)
    o_ref[...] = _dot_split(*_split_bf16(s), *_split_bf16(w_ref[...])) + b_ref[...]


def _adaln(cond, w, b):
    d, width = w.shape
    ct = width // 6
    return pl.pallas_call(
        _adaln_kernel, grid=(6,),
        in_specs=[pl.BlockSpec(cond.shape, lambda i: (0, 0)), pl.BlockSpec((d, ct), lambda i: (0, i)),
                  pl.BlockSpec((1, ct), lambda i: (0, i))],
        out_specs=pl.BlockSpec((cond.shape[0], ct), lambda i: (0, i)),
        out_shape=jax.ShapeDtypeStruct((cond.shape[0], width), F32), compiler_params=_params(1), name="adaln",
    )(cond, w, b[None, :])


def _mod_rows(mod, norm_mix_g, norm_ffn_g, final_g, batch):
    sh1, sc1, g1, sh2, sc2, g2 = jnp.split(mod, 6, axis=-1)
    rows = jnp.stack([norm_mix_g * (1.0 + sc1), sh1, g1, norm_ffn_g * (1.0 + sc2), sh2, g2,
                      jnp.broadcast_to(final_g, g1.shape), jnp.zeros_like(g1)], axis=1)
    return jnp.broadcast_to(rows, (batch,) + rows.shape[1:])


def kernel(x, c, ctx, c_ctx, ada_w, ada_b, norm_mix_g, norm_ffn_g, w_in, diff_lambda, diff_subln_g, hyena_short_w, hyena_short_b, filt_w1, filt_b1, filt_freq, filt_w2, filt_b2, filt_w3, hyena_skip, conf_dw_w, conf_ln_g, conf_ln_b, mla_q_norm_g, mla_kv_norm_g, mla_w_uq, mla_w_ukv, w_branch, w_out, w_router, w_exp_in, w_exp_out, final_norm_g):
    depth = w_in.shape[0]
    batch, n_lat, d = x.shape
    n_ctx = ctx.shape[1]
    rope_lat = _rope_operands(n_lat, identity=False)
    rope_ctx = _rope_operands(n_ctx, identity=True)
    dft_lat, dft_ctx = _dft_tables(2 * n_lat), _dft_tables(2 * n_ctx)
    cond = jnp.concatenate([c, c_ctx[None], jnp.zeros((MOD_ROWS - batch - 1, d), F32)], axis=0)
    tile_lat, tile_ctx = min(512, n_lat), min(256, n_ctx)
    h_lat, h_ctx = x, ctx
    for l in range(depth):
        last = l == depth - 1
        p = dict(w_in=w_in[l], diff_subln_g=diff_subln_g[l], hyena_short_w=hyena_short_w[l],
                 hyena_short_b=hyena_short_b[l], filt_w1=filt_w1[l], filt_b1=filt_b1[l], filt_freq=filt_freq[l],
                 filt_w2=filt_w2[l], filt_b2=filt_b2[l], filt_w3=filt_w3[l], hyena_skip=hyena_skip[l],
                 conf_dw_w=conf_dw_w[l], conf_ln_g=conf_ln_g[l], conf_ln_b=conf_ln_b[l],
                 mla_q_norm_g=mla_q_norm_g[l], mla_kv_norm_g=mla_kv_norm_g[l], mla_w_uq=mla_w_uq[l],
                 mla_w_ukv=mla_w_ukv[l], w_branch=w_branch[l], w_out=w_out[l], w_router=w_router[l],
                 w_exp_in=w_exp_in[l], w_exp_out=w_exp_out[l])
        ada = _adaln(cond, ada_w[l], ada_b[l])
        mod_lat = _mod_rows(ada[:batch], norm_mix_g[l], norm_ffn_g[l], final_norm_g, batch)
        mod_ctx = _mod_rows(ada[batch:batch + 1], norm_mix_g[l], norm_ffn_g[l], final_norm_g, batch)
        lam_init = 0.8 - 0.6 * math.exp(-0.3 * l)
        lq1, lk1, lq2, lk2 = diff_lambda[l].astype(F32)
        lam = jnp.reshape(jnp.exp(jnp.sum(lq1 * lk1)) - jnp.exp(jnp.sum(lq2 * lk2)) + lam_init, (1,))
        w_inp, w_mrg = _inproj_weights(p), _merge_weights(p, lam_init)

        qdT_l, kd_l, vdT_l, qmT_l, km_l, vmT_l, hy_l, glu_l = _inproj(h_lat, mod_lat, w_inp, rope_lat, tile=tile_lat)
        qdT_c, kd_c, vdT_c, qmT_c, km_c, vmT_c, hy_c, glu_c = _inproj(h_ctx, mod_ctx, w_inp, rope_ctx, tile=tile_ctx)
        a_lat = _flash_attention(lam, qdT_l, kd_c, vdT_c, kd_l, vdT_l, n_maps=2, tq=min(512, n_lat))
        m_lat = _flash_attention(lam, qmT_l, km_c, vmT_c, km_l, vmT_l, n_maps=1, tq=min(1024, n_lat))
        h_lat, u2_lat, lg_lat = _merge(h_lat, mod_lat, a_lat, _hyena_branch(hy_l, p, dft_lat), _conformer_branch(glu_l, p),
                                       m_lat, w_mrg, tile=min(256, n_lat))
        w_ei, w_eo = p['w_exp_in'].astype(BF16), p['w_exp_out'].astype(BF16)
        h_lat = _expert_choice_ffn(h_lat, mod_lat, u2_lat, lg_lat, w_ei, w_eo, final_norm=last)
        if not last:
            a_ctx = _flash_attention(lam, qdT_c, kd_c, vdT_c, None, None, n_maps=2, tq=n_ctx)
            m_ctx = _flash_attention(lam, qmT_c, km_c, vmT_c, None, None, n_maps=1, tq=n_ctx)
            h_ctx, u2_ctx, lg_ctx = _merge(h_ctx, mod_ctx, a_ctx, _hyena_branch(hy_c, p, dft_ctx),
                                           _conformer_branch(glu_c, p), m_ctx, w_mrg, tile=tile_ctx)
            h_ctx = _expert_choice_ffn(h_ctx, mod_ctx, u2_ctx, lg_ctx, w_ei, w_eo)
    return h_lat
```

```python
import functools
import math

import jax
import jax.numpy as jnp
from jax import lax
from jax.experimental import pallas as pl
from jax.experimental.pallas import tpu as pltpu

GRID_W = 64
ROPE_BASE = 10000.0
EPS = 1e-6

DIFF_HEADS = 4
DIFF_HEAD_DIM = 64
DIFF_V_DIM = 2 * DIFF_HEAD_DIM
HYENA_WIDTH = 256
HYENA_ORDER = 2
FILT_EMB = 33
DECAY_TARGET = 1e-2
FAST_DECAY = 0.3
SLOW_DECAY = 1.5
CONF_WIDTH = 256
MLA_HEADS = 4
MLA_Q_RANK = 256
MLA_KV_RANK = 128
MLA_NOPE = 64
MLA_ROPE = 32
MLA_V = 64
MLA_SCALE = (MLA_NOPE + MLA_ROPE) ** -0.5
N_BRANCH = 4
N_EXPERTS = 16
EC_CAPACITY = 2

DIFF_QK_W = DIFF_HEADS * 2 * DIFF_HEAD_DIM
DIFF_V_W = DIFF_HEADS * DIFF_V_DIM
HYENA_PROJ = (HYENA_ORDER + 1) * HYENA_WIDTH
CONF_PROJ = 2 * CONF_WIDTH
IN_SPLITS = (DIFF_QK_W, DIFF_QK_W, DIFF_V_W, HYENA_PROJ, CONF_PROJ, MLA_Q_RANK, MLA_KV_RANK, MLA_ROPE)
BRANCH_WIDTHS = (DIFF_V_W, HYENA_WIDTH, CONF_WIDTH, MLA_HEADS * MLA_V)

HEAD_LANES = 128
DIFF_V_PAD = 16
ATT_W = DIFF_HEADS * HEAD_LANES
LOG2E = 1.4426950408889634
VMEM_LIMIT_BYTES = 48 * 1024 * 1024
MOD_ROWS = 8

F32 = jnp.float32
BF16 = jnp.bfloat16
_NT = (((1,), (1,)), ((), ()))


def _params(n_axes, vmem=VMEM_LIMIT_BYTES):
    return pltpu.CompilerParams(dimension_semantics=("arbitrary",) * n_axes, vmem_limit_bytes=vmem)


def _flash_kernel(lam_ref, qT_ref, kc_ref, vcT_ref, *rest, n_maps, n_lat_chunks, tk, sum_row):
    if n_lat_chunks:
        kl_ref, vlT_ref, o_ref, acc_ref, m_ref, q2_ref, s_ref = rest
    else:
        o_ref, acc_ref, m_ref, q2_ref = rest
    qT = qT_ref[...]
    tq = qT.shape[1]
    if n_maps == 2:
        row = lax.broadcasted_iota(jnp.int32, qT.shape, 0)
        zero = jnp.zeros_like(qT)
        q2_ref[:, :tq] = jnp.where(row < DIFF_HEAD_DIM, qT, zero)
        q2_ref[:, tq:] = jnp.where(row >= DIFF_HEAD_DIM, qT, zero)
    else:
        q2_ref[...] = qT
    m_ref[...] = jnp.full(m_ref.shape, -jnp.inf, F32)
    acc_ref[...] = jnp.zeros(acc_ref.shape, F32)

    def scores(k):
        return jnp.dot(k, q2_ref[...], preferred_element_type=F32)

    def absorb(s, vT):
        m_prev = m_ref[...]
        m_new = jnp.maximum(m_prev, jnp.max(s, axis=0, keepdims=True))
        alpha = jnp.exp2(m_prev - m_new)
        p = jnp.exp2(s - m_new).astype(BF16)
        acc_ref[...] = alpha * acc_ref[...] + jnp.dot(vT, p, preferred_element_type=F32)
        m_ref[...] = m_new

    def chunk(c):
        return pl.ds(c * tk if isinstance(c, int) else pl.multiple_of(c * tk, tk), tk)

    def keys(c):
        return kl_ref[chunk(c), :]

    def values_t(c):
        return vlT_ref[:, chunk(c)]

    absorb(scores(kc_ref[...]), vcT_ref[...])
    if n_lat_chunks:
        s_ref[0] = scores(keys(0))

        def pair(j, carry):
            c = 2 * j
            s_ref[1] = scores(keys(c + 1))
            absorb(s_ref[0], values_t(c))
            s_ref[0] = scores(keys(c + 2))
            absorb(s_ref[1], values_t(c + 1))
            return carry
        lax.fori_loop(0, n_lat_chunks // 2 - 1, pair, 0)
        s_ref[1] = scores(keys(n_lat_chunks - 1))
        absorb(s_ref[0], values_t(n_lat_chunks - 2))
        absorb(s_ref[1], values_t(n_lat_chunks - 1))
    o = acc_ref[0:HEAD_LANES, :] / acc_ref[sum_row:sum_row + 1, :]
    if n_maps == 2:
        o = o[:, :tq] - lam_ref[0] * o[:, tq:]
        o = o * lax.rsqrt(jnp.mean(o * o, axis=0, keepdims=True) + EPS)
    o_ref[...] = o.T.astype(BF16)


def _flash_stream_kernel(lam_ref, qT_ref, kc_ref, vcT_ref, kl_ref, vlT_ref, o_ref, acc_ref, m_ref, q2_ref, sc_ref,
                         s_ref, *, n_maps, n_lat_chunks, tk, tq, sum_row):
    n_q = qT_ref.shape[1] // tq

    def load_queries(qi):
        qT = qT_ref[:, pl.ds(pl.multiple_of(qi * tq, tq), tq)]
        if n_maps == 2:
            row = lax.broadcasted_iota(jnp.int32, qT.shape, 0)
            zero = jnp.zeros_like(qT)
            q2_ref[:, :tq] = jnp.where(row < DIFF_HEAD_DIM, qT, zero)
            q2_ref[:, tq:] = jnp.where(row >= DIFF_HEAD_DIM, qT, zero)
        else:
            q2_ref[...] = qT

    def scores(k):
        return jnp.dot(k, q2_ref[...], preferred_element_type=F32)

    def absorb(s, vT):
        m_prev = m_ref[...]
        m_new = jnp.maximum(m_prev, jnp.max(s, axis=0, keepdims=True))
        alpha = jnp.exp2(m_prev - m_new)
        p = jnp.exp2(s - m_new).astype(BF16)
        acc_ref[...] = alpha * acc_ref[...] + jnp.dot(vT, p, preferred_element_type=F32)
        m_ref[...] = m_new

    load_queries(0)
    sc_ref[...] = scores(kc_ref[...])

    def query_block(qi, carry):
        m_ref[...] = jnp.full(m_ref.shape, -jnp.inf, F32)
        acc_ref[...] = jnp.zeros(acc_ref.shape, F32)
        def chunk(c):
            return pl.ds(c * tk if isinstance(c, int) else pl.multiple_of(c * tk, tk), tk)

        s_ref[0] = scores(kl_ref[chunk(0), :])
        absorb(sc_ref[...], vcT_ref[...])

        def pair(j, inner):
            c = 2 * j
            s_ref[1] = scores(kl_ref[chunk(c + 1), :])
            absorb(s_ref[0], vlT_ref[:, chunk(c)])
            s_ref[0] = scores(kl_ref[chunk(c + 2), :])
            absorb(s_ref[1], vlT_ref[:, chunk(c + 1)])
            return inner
        lax.fori_loop(0, n_lat_chunks // 2 - 1, pair, 0)
        s_ref[1] = scores(kl_ref[chunk(n_lat_chunks - 1), :])
        absorb(s_ref[0], vlT_ref[:, chunk(n_lat_chunks - 2)])
        load_queries(jnp.minimum(qi + 1, n_q - 1))
        sc_ref[...] = scores(kc_ref[...])
        absorb(s_ref[1], vlT_ref[:, chunk(n_lat_chunks - 1)])
        o = acc_ref[0:HEAD_LANES, :] / acc_ref[sum_row:sum_row + 1, :]
        if n_maps == 2:
            o = o[:, :tq] - lam_ref[0] * o[:, tq:]
            o = o * lax.rsqrt(jnp.mean(o * o, axis=0, keepdims=True) + EPS)
        o_ref[pl.ds(pl.multiple_of(qi * tq, tq), tq), :] = o.T.astype(BF16)
        return carry
    lax.fori_loop(0, n_q, query_block, 0)


def _flash_attention(lam, qT, kc, vcT, kl, vlT, *, n_maps, tq):
    b, _, s = qT.shape
    lc = kc.shape[1]
    mv = vcT.shape[2]
    sum_row = HEAD_LANES if n_maps == 2 else MLA_V
    r = n_maps * tq
    if kl is not None:
        sl = kl.shape[1]
        tk = _lat_chunk(sl)
        n_lat_chunks = sl // tk
        assert n_lat_chunks * tk == sl and s % tq == 0
        return pl.pallas_call(
            functools.partial(_flash_stream_kernel, n_maps=n_maps, n_lat_chunks=n_lat_chunks, tk=tk, tq=tq,
                              sum_row=sum_row),
            grid=(b, DIFF_HEADS),
            in_specs=[pl.BlockSpec(memory_space=pltpu.SMEM),
                      pl.BlockSpec((None, HEAD_LANES, s), lambda bi, hi: (bi, hi, 0)),
                      pl.BlockSpec((None, lc, HEAD_LANES), lambda bi, hi: (bi, 0, hi)),
                      pl.BlockSpec((None, None, mv, lc), lambda bi, hi: (bi, hi, 0, 0)),
                      pl.BlockSpec((None, sl, HEAD_LANES), lambda bi, hi: (bi, 0, hi)),
                      pl.BlockSpec((None, None, mv, sl), lambda bi, hi: (bi, hi, 0, 0))],
            out_specs=pl.BlockSpec((None, s, HEAD_LANES), lambda bi, hi: (bi, 0, hi)),
            out_shape=jax.ShapeDtypeStruct((b, s, ATT_W), BF16),
            scratch_shapes=[pltpu.VMEM((mv, r), F32), pltpu.VMEM((1, r), F32), pltpu.VMEM((HEAD_LANES, r), BF16),
                            pltpu.VMEM((lc, r), F32), pltpu.VMEM((2, tk, r), F32)],
            compiler_params=_params(2, 56 * 1024 * 1024),
            name=f"flash_attention_{n_maps}map",
        )(lam, qT, kc, vcT, kl, vlT)
    in_specs = [
        pl.BlockSpec(memory_space=pltpu.SMEM),
        pl.BlockSpec((None, HEAD_LANES, tq), lambda bi, hi, qi: (bi, hi, qi)),
        pl.BlockSpec((None, lc, HEAD_LANES), lambda bi, hi, qi: (bi, 0, hi)),
        pl.BlockSpec((None, None, mv, lc), lambda bi, hi, qi: (bi, hi, 0, 0)),
    ]
    args = [lam, qT, kc, vcT]
    scratch = [pltpu.VMEM((mv, r), F32), pltpu.VMEM((1, r), F32), pltpu.VMEM((HEAD_LANES, r), BF16)]
    n_lat_chunks, tk = 0, 0
    if kl is not None:
        sl = kl.shape[1]
        tk = _lat_chunk(sl)
        n_lat_chunks = sl // tk
        assert n_lat_chunks % 2 == 0 and n_lat_chunks * tk == sl
        scratch.append(pltpu.VMEM((2, tk, r), F32))
        in_specs += [
            pl.BlockSpec((None, sl, HEAD_LANES), lambda bi, hi, qi: (bi, 0, hi)),
            pl.BlockSpec((None, None, mv, sl), lambda bi, hi, qi: (bi, hi, 0, 0)),
        ]
        args += [kl, vlT]
    return pl.pallas_call(
        functools.partial(_flash_kernel, n_maps=n_maps, n_lat_chunks=n_lat_chunks, tk=tk, sum_row=sum_row),
        grid=(b, DIFF_HEADS, s // tq),
        in_specs=in_specs,
        out_specs=pl.BlockSpec((None, tq, HEAD_LANES), lambda bi, hi, qi: (bi, qi, hi)),
        out_shape=jax.ShapeDtypeStruct((b, s, ATT_W), BF16),
        scratch_shapes=scratch,
        compiler_params=_params(3),
        name=f"flash_attention_{n_maps}map",
    )(*args)


def _lat_chunk(s):
    return min(1024, s // 2)


W_NAT_SPLITS = (DIFF_QK_W, HYENA_PROJ, CONF_PROJ, MLA_Q_RANK, MLA_KV_RANK, HEAD_LANES)


def _modulated_norm(h, a, shift):
    return h * lax.rsqrt(jnp.mean(h * h, axis=-1, keepdims=True) + EPS) * a + shift


def _rope_lanes(x, tab_ref, shift):
    return (x * tab_ref[0] + pltpu.roll(x, shift, 1) * tab_ref[1]
            + pltpu.roll(x, HEAD_LANES - shift, 1) * tab_ref[2])


def _inproj_kernel(h_ref, mod_ref, wnat_ref, wT_ref, wuqT_ref, wukvk_ref, wuvT_ref, gq_ref, gkv_ref,
                   ropeT_d_ref, rope_kd_ref, ropeT_m_ref, rope_km_ref,
                   qdT_ref, kd_ref, vdT_ref, qmT_ref, km_ref, vmT_ref, hy_ref, glu_ref):
    u = _modulated_norm(h_ref[...], mod_ref[0:1, :], mod_ref[1:2, :]).astype(BF16)
    z = jnp.dot(u, wnat_ref[...], preferred_element_type=F32)
    zT = lax.dot_general(wT_ref[...], u, _NT, preferred_element_type=F32)
    offs = [0]
    for w in W_NAT_SPLITS:
        offs.append(offs[-1] + w)
    dk, hy, cf, cq, ckv, krp = (z[:, offs[i]:offs[i + 1]] for i in range(len(W_NAT_SPLITS)))

    for hd in range(DIFF_HEADS):
        sl = slice(hd * HEAD_LANES, (hd + 1) * HEAD_LANES)
        kd_ref[:, sl] = _rope_lanes(dk[:, sl], rope_kd_ref, DIFF_HEAD_DIM // 2).astype(BF16)
    cos_d, sin_d = ropeT_d_ref[0], ropeT_d_ref[1]
    half = DIFF_HEAD_DIM // 2
    for g in range(2 * DIFF_HEADS):
        x1 = zT[g * DIFF_HEAD_DIM:g * DIFF_HEAD_DIM + half]
        x2 = zT[g * DIFF_HEAD_DIM + half:(g + 1) * DIFF_HEAD_DIM]
        qdT_ref[g * DIFF_HEAD_DIM:g * DIFF_HEAD_DIM + half, :] = (x1 * cos_d - x2 * sin_d).astype(BF16)
        qdT_ref[g * DIFF_HEAD_DIM + half:(g + 1) * DIFF_HEAD_DIM, :] = (x1 * sin_d + x2 * cos_d).astype(BF16)
    tail = jnp.where(lax.broadcasted_iota(jnp.int32, (DIFF_V_PAD, zT.shape[1]), 0) == 0, 1.0, 0.0).astype(BF16)
    for hd in range(DIFF_HEADS):
        r0 = DIFF_QK_W + hd * DIFF_V_DIM
        vdT_ref[hd, 0:DIFF_V_DIM, :] = zT[r0:r0 + DIFF_V_DIM].astype(BF16)
        vdT_ref[hd, DIFF_V_DIM:, :] = tail

    hy_ref[...] = hy
    glu_ref[...] = cf[:, :CONF_WIDTH] * jax.nn.sigmoid(cf[:, CONF_WIDTH:])

    cqn = (cq * lax.rsqrt(jnp.mean(cq * cq, axis=-1, keepdims=True) + EPS) * gq_ref[...]).astype(BF16)
    ckvn = (ckv * lax.rsqrt(jnp.mean(ckv * ckv, axis=-1, keepdims=True) + EPS) * gkv_ref[...]).astype(BF16)
    qT = lax.dot_general(wuqT_ref[...], cqn, _NT, preferred_element_type=F32)
    cos_m, sin_m = ropeT_m_ref[0], ropeT_m_ref[1]
    hr = MLA_ROPE // 2
    for hd in range(MLA_HEADS):
        base = hd * HEAD_LANES
        r1 = base + MLA_NOPE
        x1, x2 = qT[r1:r1 + hr], qT[r1 + hr:r1 + 2 * hr]
        qmT_ref[base:r1, :] = qT[base:r1].astype(BF16)
        qmT_ref[r1:r1 + hr, :] = (x1 * cos_m - x2 * sin_m).astype(BF16)
        qmT_ref[r1 + hr:r1 + 2 * hr, :] = (x1 * sin_m + x2 * cos_m).astype(BF16)
        qmT_ref[r1 + 2 * hr:base + HEAD_LANES, :] = jnp.zeros((HEAD_LANES - MLA_NOPE - MLA_ROPE, qT.shape[1]), BF16)
    kn = jnp.dot(ckvn, wukvk_ref[...], preferred_element_type=F32)
    kr = _rope_lanes(krp, rope_km_ref, hr)
    for hd in range(MLA_HEADS):
        sl = slice(hd * HEAD_LANES, (hd + 1) * HEAD_LANES)
        km_ref[:, sl] = (kn[:, sl] + kr).astype(BF16)
    vT = lax.dot_general(wuvT_ref[...], ckvn, _NT, preferred_element_type=F32)
    ones_row = lax.broadcasted_iota(jnp.int32, vT.shape, 0) % HEAD_LANES == MLA_V
    vT = jnp.where(ones_row, 1.0, vT).astype(BF16)
    for hd in range(MLA_HEADS):
        vmT_ref[hd] = vT[hd * HEAD_LANES:(hd + 1) * HEAD_LANES]


def _inproj(h, mod, wts, rope, *, tile):
    b, n, d = h.shape
    const2 = lambda bi, ti: (0, 0)
    tok = lambda w: pl.BlockSpec((None, tile, w), lambda bi, ti: (bi, ti, 0))
    tokT = lambda w: pl.BlockSpec((None, w, tile), lambda bi, ti: (bi, 0, ti))
    full = lambda a: pl.BlockSpec(a.shape, const2)
    in_specs = [tok(d), pl.BlockSpec((None, MOD_ROWS, d), lambda bi, ti: (bi, 0, 0))]
    in_specs += [full(wts[k]) for k in ('w_nat', 'w_T', 'w_uqT', 'w_ukvk', 'w_uvT', 'gq', 'gkv')]
    in_specs += [pl.BlockSpec((2, DIFF_HEAD_DIM // 2, tile), lambda bi, ti: (0, 0, ti)),
                 pl.BlockSpec((3, tile, HEAD_LANES), lambda bi, ti: (0, ti, 0)),
                 pl.BlockSpec((2, MLA_ROPE // 2, tile), lambda bi, ti: (0, 0, ti)),
                 pl.BlockSpec((3, tile, HEAD_LANES), lambda bi, ti: (0, ti, 0))]
    sds = jax.ShapeDtypeStruct
    vrows_d, vrows_m = DIFF_V_DIM + DIFF_V_PAD, HEAD_LANES
    headsT = lambda rows: pl.BlockSpec((None, DIFF_HEADS, rows, tile), lambda bi, ti: (bi, 0, 0, ti))
    out_shape = (sds((b, ATT_W, n), BF16), sds((b, n, ATT_W), BF16), sds((b, DIFF_HEADS, vrows_d, n), BF16),
                 sds((b, ATT_W, n), BF16), sds((b, n, ATT_W), BF16), sds((b, MLA_HEADS, vrows_m, n), BF16),
                 sds((b, n, HYENA_PROJ), F32), sds((b, n, CONF_WIDTH), F32))
    out_specs = (tokT(ATT_W), tok(ATT_W), headsT(vrows_d), tokT(ATT_W), tok(ATT_W), headsT(vrows_m),
                 tok(HYENA_PROJ), tok(CONF_WIDTH))
    return pl.pallas_call(
        _inproj_kernel, grid=(b, n // tile), in_specs=in_specs, out_specs=out_specs, out_shape=out_shape,
        compiler_params=_params(2), name="inproj",
    )(h, mod, wts['w_nat'], wts['w_T'], wts['w_uqT'], wts['w_ukvk'], wts['w_uvT'], wts['gq'], wts['gkv'],
      rope['T_d'], rope['k_d'], rope['T_m'], rope['k_m'])


def _pad_heads(w, width):
    rows = w.shape[0]
    w = w.reshape(rows, MLA_HEADS, width)
    return jnp.pad(w, ((0, 0), (0, 0), (0, HEAD_LANES - width))).reshape(rows, ATT_W)


def _inproj_weights(p):
    d = p['w_in'].shape[0]
    dq, dk, dv, hy, cf, cq, ckv, kr = _split(p['w_in'][:, :sum(IN_SPLITS)], IN_SPLITS)
    krp = jnp.zeros((d, HEAD_LANES), F32).at[:, MLA_NOPE:MLA_NOPE + MLA_ROPE].set(kr)
    w_ukv = p['mla_w_ukv'].reshape(MLA_KV_RANK, MLA_HEADS, MLA_NOPE + MLA_V)
    return dict(
        w_nat=jnp.concatenate([dk, hy, cf, cq, ckv, krp], axis=1).astype(BF16),
        w_T=jnp.concatenate([dq * (DIFF_HEAD_DIM ** -0.5 * LOG2E), dv], axis=1).T.astype(BF16),
        w_uqT=_pad_heads(p['mla_w_uq'] * (MLA_SCALE * LOG2E), MLA_NOPE + MLA_ROPE).T.astype(BF16),
        w_ukvk=_pad_heads(w_ukv[:, :, :MLA_NOPE].reshape(MLA_KV_RANK, -1), MLA_NOPE).astype(BF16),
        w_uvT=_pad_heads(w_ukv[:, :, MLA_NOPE:].reshape(MLA_KV_RANK, -1), MLA_V).T.astype(BF16),
        gq=p['mla_q_norm_g'][None, :], gkv=p['mla_kv_norm_g'][None, :])


def _rope_tables(n_tok, rot_dim):
    rows = n_tok // GRID_W
    row = jnp.repeat(jnp.arange(rows), GRID_W).astype(F32)
    col = jnp.tile(jnp.arange(GRID_W), rows).astype(F32)
    nf = rot_dim // 4
    inv = ROPE_BASE ** (-jnp.arange(nf, dtype=F32) / nf)
    ang = jnp.concatenate([row[:, None] * inv, col[:, None] * inv], axis=-1)
    return jnp.cos(ang), jnp.sin(ang)


def _rope_operands(n_tok, identity):
    if identity:
        cos_d, sin_d = jnp.ones((n_tok, DIFF_HEAD_DIM // 2), F32), jnp.zeros((n_tok, DIFF_HEAD_DIM // 2), F32)
        cos_m, sin_m = jnp.ones((n_tok, MLA_ROPE // 2), F32), jnp.zeros((n_tok, MLA_ROPE // 2), F32)
    else:
        cos_d, sin_d = _rope_tables(n_tok, DIFF_HEAD_DIM)
        cos_m, sin_m = _rope_tables(n_tok, MLA_ROPE)
    z_d, z_m = jnp.zeros_like(sin_d), jnp.zeros_like(sin_m)
    two = lambda a, bb: jnp.tile(jnp.concatenate([a, bb], axis=1), (1, 2))
    lo, hi = jnp.zeros((n_tok, MLA_NOPE), F32), jnp.zeros((n_tok, HEAD_LANES - MLA_NOPE - MLA_ROPE), F32)
    mid = lambda a, bb: jnp.concatenate([lo, a, bb, hi], axis=1)
    return dict(T_d=jnp.stack([cos_d.T, sin_d.T]), T_m=jnp.stack([cos_m.T, sin_m.T]),
                k_d=jnp.stack([two(cos_d, cos_d), two(z_d, sin_d), two(-sin_d, z_d)]),
                k_m=jnp.stack([mid(cos_m, cos_m), mid(z_m, sin_m), mid(-sin_m, z_m)]))


def _merge_kernel(h_ref, mod_ref, a_ref, hy_ref, cf_ref, m_ref, wg_ref, wbd_ref, wbh_ref, wbc_ref, wbm_ref,
                  wo_ref, wrh_ref, wrl_ref, hn_ref, u2_ref, lg_ref):
    h = h_ref[...]
    d = h.shape[1]
    u = _modulated_norm(h, mod_ref[0:1, :], mod_ref[1:2, :]).astype(BF16)
    gates = jax.nn.sigmoid(jnp.dot(u, wg_ref[...], preferred_element_type=F32))
    dot = lambda x, w_ref: jnp.dot(x, w_ref[...], preferred_element_type=F32)
    acc = gates[:, :d] * dot(a_ref[...], wbd_ref)
    acc += gates[:, d:2 * d] * dot(hy_ref[...].astype(BF16), wbh_ref)
    acc += gates[:, 2 * d:3 * d] * dot(cf_ref[...].astype(BF16), wbc_ref)
    acc += gates[:, 3 * d:] * dot(m_ref[...], wbm_ref)
    hn = h + mod_ref[2:3, :] * dot(acc.astype(BF16), wo_ref)
    hn_ref[...] = hn
    u2 = _modulated_norm(hn, mod_ref[3:4, :], mod_ref[4:5, :])
    u2h = u2.astype(BF16)
    u2l = (u2 - u2h.astype(F32)).astype(BF16)
    u2_ref[...] = u2h
    lg_ref[...] = dot(u2h, wrh_ref) + (dot(u2l, wrh_ref) + dot(u2h, wrl_ref))


def _merge(h, mod, a, hyv, cfv, m, wts, *, tile):
    b, n, d = h.shape
    const2 = lambda bi, ti: (0, 0)
    tok = lambda w: pl.BlockSpec((None, tile, w), lambda bi, ti: (bi, ti, 0))
    names = ('w_gate', 'w_bd', 'w_bh', 'w_bc', 'w_bm', 'w_out', 'w_rh', 'w_rl')
    in_specs = [tok(d), pl.BlockSpec((None, MOD_ROWS, d), lambda bi, ti: (bi, 0, 0)),
                tok(ATT_W), tok(HYENA_WIDTH), tok(CONF_WIDTH), tok(ATT_W)]
    in_specs += [pl.BlockSpec(wts[k].shape, const2, pipeline_mode=pl.Buffered(1)) for k in names]
    sds = jax.ShapeDtypeStruct
    return pl.pallas_call(
        _merge_kernel, grid=(b, n // tile), in_specs=in_specs,
        out_specs=(tok(d), tok(d), tok(HEAD_LANES)),
        out_shape=(sds((b, n, d), F32), sds((b, n, d), BF16), sds((b, n, HEAD_LANES), F32)),
        compiler_params=_params(2, 56 * 1024 * 1024), name="merge",
    )(h, mod, a, hyv, cfv, m, *[wts[k] for k in names])


def _merge_weights(p, lam_init):
    d = p['w_out'].shape[0]
    wb_d, wb_h, wb_c, wb_m = (w.T for w in _split(p['w_branch'].T, BRANCH_WIDTHS))
    wb_d = wb_d * (jnp.tile(p['diff_subln_g'], DIFF_HEADS) * (1.0 - lam_init))[:, None]
    wb_m = jnp.pad(wb_m.reshape(MLA_HEADS, MLA_V, d), ((0, 0), (0, HEAD_LANES - MLA_V), (0, 0))).reshape(ATT_W, d)
    w_r = jnp.pad(p['w_router'], ((0, 0), (0, HEAD_LANES - N_EXPERTS)))
    w_rh = w_r.astype(BF16)
    return dict(w_gate=p['w_in'][:, sum(IN_SPLITS):].astype(BF16), w_bd=wb_d.astype(BF16), w_bh=wb_h.astype(BF16),
                w_bc=wb_c.astype(BF16), w_bm=wb_m.astype(BF16), w_out=p['w_out'].astype(BF16),
                w_rh=w_rh, w_rl=(w_r - w_rh.astype(F32)).astype(BF16))


SUB_TOKENS = 256
GATHER_WINDOW = 272
COMBINE_TOKENS = 128
COMBINE_WINDOW = 256
ROUTE_MIN_ROWS = 8


def _excl_scan(x, lane, row):
    inc = x
    s = 1
    while s < HEAD_LANES:
        inc = inc + jnp.where(lane >= s, pltpu.roll(inc, s, 2), 0.0)
        s *= 2
    tot = jnp.sum(x, axis=2, keepdims=True) + jnp.zeros_like(x)
    off = tot
    s = 1
    while s < x.shape[1]:
        off = off + jnp.where(row >= s, pltpu.roll(off, s, 1), 0.0)
        s *= 2
    return inc - x + (off - tot)


def _route_kernel(lg_ref, pos_ref, aff_ref, *, n_valid, cap):
    lg = lg_ref[...]
    shape = lg.shape
    lane = lax.broadcasted_iota(jnp.int32, shape, 2)
    row = lax.broadcasted_iota(jnp.int32, shape, 1)
    e = jnp.exp(lg - jnp.max(lg, axis=0, keepdims=True))
    aff = e / jnp.sum(e, axis=0, keepdims=True)
    bits = jnp.where(row * HEAD_LANES + lane < n_valid, pltpu.bitcast(aff, jnp.int32), -1)

    def count(mask):
        c = jnp.sum(jnp.where(mask, 1.0, 0.0), axis=2, keepdims=True)
        return jnp.sum(c, axis=1, keepdims=True)

    def step(i, thr):
        cand = thr | (jnp.int32(1) << (30 - i))
        return jnp.where(count(bits >= cand) >= cap, cand, thr)
    thr = lax.fori_loop(0, 31, step, jnp.zeros((shape[0], 1, 1), jnp.int32))
    gt = bits > thr
    eq = bits == thr
    need = cap - count(gt)
    tie_rank = _excl_scan(jnp.where(eq, 1.0, 0.0), lane, row)
    sel = gt | (eq & (tie_rank < need))
    pos = _excl_scan(jnp.where(sel, 1.0, 0.0), lane, row)
    pos_ref[...] = jnp.where(sel, pos.astype(jnp.int32), -1)
    aff_ref[...] = aff


def _route(logits, cap):
    b, n, _ = logits.shape
    rows = max(ROUTE_MIN_ROWS, n // HEAD_LANES)
    lg = jnp.swapaxes(logits[..., :N_EXPERTS], 1, 2)
    lg = jnp.pad(lg, ((0, 0), (0, 0), (0, rows * HEAD_LANES - n))).reshape(b, N_EXPERTS, rows, HEAD_LANES)
    spec = pl.BlockSpec((None, N_EXPERTS, rows, HEAD_LANES), lambda bi: (bi, 0, 0, 0))
    pos, aff = pl.pallas_call(
        functools.partial(_route_kernel, n_valid=n, cap=cap), grid=(b,), in_specs=[spec], out_specs=(spec, spec),
        out_shape=(jax.ShapeDtypeStruct(lg.shape, jnp.int32), jax.ShapeDtypeStruct(lg.shape, F32)),
        compiler_params=_params(1), name="route",
    )(lg)
    flat = lambda a: a.reshape(b, N_EXPERTS, rows * HEAD_LANES)[..., :n]
    return flat(pos), flat(aff)


def _experts_kernel(base_ref, u_ref, mod_ref, aff_ref, pos_ref, win_ref, wout_ref, ye_ref, xe_ref, *,
                    n_sub, cap):
    bi, ei, kb = pl.program_id(0), pl.program_id(1), pl.program_id(2)
    d = u_ref.shape[1]

    @pl.when(kb == 0)
    def _():
        xe_ref[...] = jnp.zeros(xe_ref.shape, F32)

    slot = lax.broadcasted_iota(jnp.int32, (GATHER_WINDOW, SUB_TOKENS), 0)
    ones = jnp.ones((SUB_TOKENS, HEAD_LANES), BF16)
    for j in range(n_sub):
        tok = slice(j * SUB_TOKENS, (j + 1) * SUB_TOKENS)
        base = pl.multiple_of(base_ref[bi, ei, kb * n_sub + j], 16)
        match = slot == (pos_ref[:, tok] - base)
        onehot = jnp.where(match, 1.0, 0.0).astype(BF16)
        g = aff_ref[:, tok]
        g_hi = g.astype(BF16).astype(F32)
        sel_hi = jnp.where(match, g_hi, 0.0).astype(BF16)
        sel_lo = jnp.where(match, g - g_hi, 0.0).astype(BF16)
        rows = pl.ds(base, GATHER_WINDOW)
        xe_ref[rows, :d] += jnp.dot(onehot, u_ref[tok, :], preferred_element_type=F32)
        xe_ref[rows, d:d + HEAD_LANES] += jnp.dot(sel_hi, ones, preferred_element_type=F32)
        xe_ref[rows, d + HEAD_LANES:] += jnp.dot(sel_lo, ones, preferred_element_type=F32)

    @pl.when(kb == pl.num_programs(2) - 1)
    def _():
        f = wout_ref.shape[0]
        step = min(512, cap)
        for r0 in range(0, cap, step):
            x = xe_ref[r0:r0 + step, :d].astype(BF16)
            gate = xe_ref[r0:r0 + step, d:d + HEAD_LANES] + xe_ref[r0:r0 + step, d + HEAD_LANES:]
            hgu = jnp.dot(x, win_ref[...], preferred_element_type=F32)
            act = (jax.nn.silu(hgu[:, :f]) * hgu[:, f:]).astype(BF16)
            y = jnp.dot(act, wout_ref[...], preferred_element_type=F32)
            scale = jnp.concatenate([gate] * (d // HEAD_LANES), axis=1) * mod_ref[5:6, :]
            ye_ref[r0:r0 + step, :] = (y * scale).astype(BF16)
        ye_ref[cap:, :] = jnp.zeros((ye_ref.shape[0] - cap, ye_ref.shape[1]), BF16)


def _combine_kernel(base_ref, h_ref, mod_ref, posn_ref, ye_ref, hn_ref, *, n_sub, final_norm):
    bi, kb, ei = pl.program_id(0), pl.program_id(1), pl.program_id(2)

    @pl.when(ei == 0)
    def _():
        hn_ref[...] = h_ref[...]

    slot = lax.broadcasted_iota(jnp.int32, (COMBINE_TOKENS, COMBINE_WINDOW), 1)
    lane_e = lax.broadcasted_iota(jnp.int32, (COMBINE_TOKENS, N_EXPERTS), 1)
    for j in range(n_sub):
        tok = slice(j * COMBINE_TOKENS, (j + 1) * COMBINE_TOKENS)
        base = pl.multiple_of(base_ref[bi, ei, kb * n_sub + j], 16)
        rel = jnp.sum(jnp.where(lane_e == ei, posn_ref[tok, :], 0), axis=1, keepdims=True) - base
        onehot = jnp.where(slot == rel, 1.0, 0.0).astype(BF16)
        ye = ye_ref[pl.ds(base, COMBINE_WINDOW), :]
        hn_ref[tok, :] += jnp.dot(onehot, ye, preferred_element_type=F32)

    if final_norm:
        @pl.when(ei == pl.num_programs(2) - 1)
        def _():
            hn = hn_ref[...]
            hn_ref[...] = hn * lax.rsqrt(jnp.mean(hn * hn, axis=-1, keepdims=True) + EPS) * mod_ref[6:7, :]


def _expert_choice_ffn(h, mod, u2, logits, w_exp_in, w_exp_out, final_norm=False):
    b, n, d = u2.shape
    cap = max(1, EC_CAPACITY * n // N_EXPERTS)
    n_sub, n_sub_c = min(8, n // SUB_TOKENS), min(16, n // COMBINE_TOKENS)
    big, big_c = n_sub * SUB_TOKENS, n_sub_c * COMBINE_TOKENS
    n_big, n_big_c = n // big, n // big_c
    capp = cap + GATHER_WINDOW
    pos, aff = _route(logits, cap)

    def window_starts(sub):
        cnt = jnp.sum((pos >= 0).reshape(b, N_EXPERTS, n // sub, sub), axis=-1)
        return ((jnp.cumsum(cnt, axis=-1) - cnt) // 16 * 16).astype(jnp.int32)
    base, base_c = window_starts(SUB_TOKENS), window_starts(COMBINE_TOKENS)
    f = w_exp_out.shape[1]
    ye = pl.pallas_call(
        functools.partial(_experts_kernel, n_sub=n_sub, cap=cap),
        grid_spec=pltpu.PrefetchScalarGridSpec(
            num_scalar_prefetch=1, grid=(b, N_EXPERTS, n_big),
            in_specs=[pl.BlockSpec((None, big, d), lambda bi, ei, kb, base_r: (bi, kb, 0)),
                      pl.BlockSpec((None, MOD_ROWS, d), lambda bi, ei, kb, base_r: (bi, 0, 0)),
                      pl.BlockSpec((None, None, 1, big), lambda bi, ei, kb, base_r: (bi, ei, 0, kb)),
                      pl.BlockSpec((None, None, 1, big), lambda bi, ei, kb, base_r: (bi, ei, 0, kb)),
                      pl.BlockSpec((None, d, 2 * f), lambda bi, ei, kb, base_r: (ei, 0, 0)),
                      pl.BlockSpec((None, f, d), lambda bi, ei, kb, base_r: (ei, 0, 0))],
            out_specs=pl.BlockSpec((None, None, capp, d), lambda bi, ei, kb, base_r: (bi, ei, 0, 0)),
            scratch_shapes=[pltpu.VMEM((capp, d + 2 * HEAD_LANES), F32)]),
        out_shape=jax.ShapeDtypeStruct((b, N_EXPERTS, capp, d), BF16),
        compiler_params=_params(3, 56 * 1024 * 1024), name="experts",
    )(base, u2, mod, aff.reshape(b, N_EXPERTS, 1, n), pos.reshape(b, N_EXPERTS, 1, n), w_exp_in, w_exp_out)
    posn = jnp.swapaxes(pos, 1, 2)
    return pl.pallas_call(
        functools.partial(_combine_kernel, n_sub=n_sub_c, final_norm=final_norm),
        grid_spec=pltpu.PrefetchScalarGridSpec(
            num_scalar_prefetch=1, grid=(b, n_big_c, N_EXPERTS),
            in_specs=[pl.BlockSpec((None, big_c, d), lambda bi, kb, ei, base_r: (bi, kb, 0)),
                      pl.BlockSpec((None, MOD_ROWS, d), lambda bi, kb, ei, base_r: (bi, 0, 0)),
                      pl.BlockSpec((None, big_c, N_EXPERTS), lambda bi, kb, ei, base_r: (bi, kb, 0)),
                      pl.BlockSpec((None, None, capp, d), lambda bi, kb, ei, base_r: (bi, ei, 0, 0))],
            out_specs=pl.BlockSpec((None, big_c, d), lambda bi, kb, ei, base_r: (bi, kb, 0))),
        out_shape=jax.ShapeDtypeStruct((b, n, d), F32),
        compiler_params=_params(3, 56 * 1024 * 1024), name="combine",
    )(base_c, h, mod, posn, ye)


def _split(z, sizes):
    out, start = [], 0
    for s in sizes:
        out.append(z[..., start:start + s])
        start += s
    return out


HALO = 16


def _fill_ext(ext_ref, x_ref, prev_ref, next_ref):
    ti, nt = pl.program_id(1), pl.num_programs(1)
    tt = x_ref.shape[0]
    ext_ref[0:HALO, :] = jnp.where(ti > 0, prev_ref[...], 0.0)
    ext_ref[HALO:HALO + tt, :] = x_ref[...]
    ext_ref[HALO + tt:, :] = jnp.where(ti < nt - 1, next_ref[...], 0.0)


def _taps(ext_ref, w_ref, tt):
    k = w_ref.shape[0]
    acc = None
    for j in range(k):
        start = HALO - k // 2 + j
        term = w_ref[j:j + 1, :] * ext_ref[start:start + tt, :]
        acc = term if acc is None else acc + term
    return acc


def _short_conv_kernel(x_ref, prev_ref, next_ref, w_ref, b_ref, x1_ref, x2_ref, v_ref, ext_ref):
    _fill_ext(ext_ref, x_ref, prev_ref, next_ref)
    y = _taps(ext_ref, w_ref, x_ref.shape[0]) + b_ref[...]
    x1_ref[...] = y[:, :HYENA_WIDTH]
    x2_ref[...] = y[:, HYENA_WIDTH:2 * HYENA_WIDTH]
    v_ref[...] = y[:, 2 * HYENA_WIDTH:]


def _conformer_kernel(x_ref, prev_ref, next_ref, w_ref, g_ref, b_ref, o_ref, ext_ref):
    _fill_ext(ext_ref, x_ref, prev_ref, next_ref)
    u = _taps(ext_ref, w_ref, x_ref.shape[0])
    mu = jnp.mean(u, axis=-1, keepdims=True)
    var = jnp.mean(jnp.square(u - mu), axis=-1, keepdims=True)
    y = (u - mu) * lax.rsqrt(var + EPS) * g_ref[...] + b_ref[...]
    o_ref[...] = y * jax.nn.sigmoid(y)


def _token_conv(body, x, consts, out_widths, name):
    b, n, w = x.shape
    tt = min(1024, n)
    per = tt // HALO
    last = n // HALO - 1
    in_specs = [pl.BlockSpec((None, tt, w), lambda bi, ti: (bi, ti, 0)),
                pl.BlockSpec((None, HALO, w), lambda bi, ti: (bi, jnp.maximum(ti * per - 1, 0), 0)),
                pl.BlockSpec((None, HALO, w), lambda bi, ti: (bi, jnp.minimum((ti + 1) * per, last), 0))]
    in_specs += [pl.BlockSpec(cst.shape, lambda bi, ti: (0, 0)) for cst in consts]
    outs = tuple(jax.ShapeDtypeStruct((b, n, ow), F32) for ow in out_widths)
    out_specs = tuple(pl.BlockSpec((None, tt, ow), lambda bi, ti: (bi, ti, 0)) for ow in out_widths)
    return pl.pallas_call(body, grid=(b, n // tt), in_specs=in_specs, out_specs=out_specs, out_shape=outs,
                          scratch_shapes=[pltpu.VMEM((tt + 2 * HALO, w), F32)],
                          compiler_params=_params(2), name=name)(x, x, x, *consts)


def _conformer_branch(glu, p):
    return _token_conv(_conformer_kernel, glu, (p['conf_dw_w'], p['conf_ln_g'][None, :], p['conf_ln_b'][None, :]),
                       (CONF_WIDTH,), "conformer")[0]


FILT_LANES = 128
DFT_SHORT = 256


def _split_bf16(x):
    hi = x.astype(BF16)
    return hi, (x - hi.astype(F32)).astype(BF16)


def _dot_split(ah, al, bh, bl):
    dot = lambda u, v: jnp.dot(u, v, preferred_element_type=F32)
    return dot(ah, bh) + (dot(al, bh) + dot(ah, bl))


def _dot_const(mh, ml, x):
    xb = x.astype(BF16)
    return jnp.dot(mh, xb, preferred_element_type=F32) + jnp.dot(ml, xb, preferred_element_type=F32)


def _filter_kernel(z_ref, w1h, w1l, b1, f1, w2h, w2l, b2, f2, w3h, w3l, dl_ref, h_ref, asum_ref, *,
                   tiles_per_dir):
    z = z_ref[...]
    hid = jnp.sin(f1[...] * (_dot_split(*_split_bf16(z), w1h[...], w1l[...]) + b1[...]))
    hid = jnp.sin(f2[...] * (_dot_split(*_split_bf16(hid), w2h[...], w2l[...]) + b2[...]))
    h = _dot_split(*_split_bf16(hid), w3h[...], w3l[...])
    h = h * jnp.exp(-z[:, 0:1] * dl_ref[...])
    h_ref[...] = h

    @pl.when(pl.program_id(0) % tiles_per_dir == 0)
    def _():
        asum_ref[...] = jnp.zeros(asum_ref.shape, F32)
    asum_ref[...] += jnp.sum(jnp.abs(h), axis=0, keepdims=True)


def _normalise_kernel(h_ref, asum_ref, *o_refs, n):
    tt = h_ref.shape[0]
    row = pl.program_id(0) * tt + lax.broadcasted_iota(jnp.int32, h_ref.shape, 0)
    k = jnp.where(row == n, 0.0, h_ref[...] / asum_ref[...])
    for o, o_ref in enumerate(o_refs):
        o_ref[...] = k[:, o * HYENA_WIDTH:(o + 1) * HYENA_WIDTH]


def _hyena_taps(n, p):
    t = jnp.linspace(0.0, 1.0, n, dtype=F32)[:, None]
    bands = (FILT_EMB - 1) // 2
    w = (2.0 * math.pi / n) * jnp.arange(n, dtype=F32)[:, None]
    f = jnp.linspace(1e-4, bands - 1, bands, dtype=F32)[None, :]
    t2, w2pos = jnp.concatenate([t, t[::-1]], axis=0), jnp.concatenate([w, w[::-1]], axis=0)
    z2 = jnp.concatenate([t2, jnp.cos(f * w2pos), -jnp.sin(f * w2pos),
                          jnp.zeros((2 * n, FILT_LANES - FILT_EMB), F32)], axis=-1)
    padc = lambda a: jnp.pad(a, ((0, 0), (0, FILT_LANES - a.shape[1])))
    padr = lambda a: jnp.pad(a, ((0, FILT_LANES - a.shape[0]), (0, 0)))
    w1, w2, w3 = padc(padr(p['filt_w1'])), padc(padr(p['filt_w2'])), padr(p['filt_w3'])
    b1, b2 = padc(p['filt_b1'][None, :]), padc(p['filt_b2'][None, :])
    f1, f2 = padc(p['filt_freq'][0][None, :]), padc(p['filt_freq'][1][None, :])
    deltas = jnp.abs(jnp.linspace(math.log(DECAY_TARGET) / SLOW_DECAY, math.log(DECAY_TARGET) / FAST_DECAY,
                                  HYENA_WIDTH, dtype=F32))
    width = HYENA_ORDER * HYENA_WIDTH
    w3 = w3.reshape(FILT_LANES, HYENA_ORDER, 2, HYENA_WIDTH).transpose(2, 0, 1, 3).reshape(2, FILT_LANES, width)
    dl = jnp.tile(deltas, HYENA_ORDER)[None, :]
    w3h, w3l = _split_bf16(w3)
    tt = min(1024, n)
    tiles_per_dir = n // tt
    cspec = lambda a: pl.BlockSpec(a.shape, lambda i: (0, 0))
    dirspec = lambda rows: pl.BlockSpec((None, rows, width), lambda i: (i // tiles_per_dir, 0, 0))
    tile = lambda w: pl.BlockSpec((tt, w), lambda i: (i, 0))
    small = [*_split_bf16(w1), b1, f1, *_split_bf16(w2), b2, f2]
    h_raw, asum = pl.pallas_call(
        functools.partial(_filter_kernel, tiles_per_dir=tiles_per_dir), grid=(2 * tiles_per_dir,),
        in_specs=[tile(FILT_LANES)] + [cspec(a) for a in small] + [dirspec(FILT_LANES), dirspec(FILT_LANES), cspec(dl)],
        out_specs=(tile(width), dirspec(1)),
        out_shape=(jax.ShapeDtypeStruct((2 * n, width), F32), jax.ShapeDtypeStruct((2, 1, width), F32)),
        compiler_params=_params(1), name="hyena_filter_mlp")(z2, *small, w3h, w3l, dl)
    return pl.pallas_call(
        functools.partial(_normalise_kernel, n=n), grid=(2 * tiles_per_dir,),
        in_specs=[tile(width), dirspec(1)],
        out_specs=tuple(tile(HYENA_WIDTH) for _ in range(HYENA_ORDER)),
        out_shape=tuple(jax.ShapeDtypeStruct((2 * n, HYENA_WIDTH), F32) for _ in range(HYENA_ORDER)),
        compiler_params=_params(1), name="hyena_filter_norm",
    )(h_raw, asum)


def _dft_tables(n):
    n2 = DFT_SHORT if n >= 4 * DFT_SHORT else n
    n1 = n // n2

    def cis(idx):
        ang = (-2.0 * math.pi / n) * idx.astype(F32)
        return jnp.cos(ang), jnp.sin(ang)
    k2 = jnp.arange(n2)
    fr, fi = cis((k2[:, None] * k2[None, :]) % n2 * n1)
    tabs = dict(n1=n1, n2=n2)
    tabs['f_hi'], tabs['f_lo'] = _split_bf16(jnp.stack([fr, fi]))
    k1 = jnp.arange(n1)
    tr, ti = cis(k1[:, None] * k2[None, :])
    tabs['tw'] = jnp.broadcast_to(jnp.stack([tr, ti], axis=1)[..., None], (n1, 2, n2, HEAD_LANES))
    if n1 > 1:
        gr, gi = cis((k1[:, None] * k1[None, :]) % n1 * n2)
        half = n1 // 2
        grh, gih = gr[:, :half], gi[:, :half]
        tabs['m_fwd'] = _split_bf16(jnp.block([[grh, -gih], [gih, grh]]))
        tabs['m_real'] = _split_bf16(jnp.concatenate([gr, gi], axis=0))
        tabs['m_inv'] = _split_bf16(jnp.block([[grh.T, gih.T], [-gih.T, grh.T]]))
    return tabs


SLABS_PER_STEP = 8


def _rowmix_slabs_kernel(mh_ref, ml_ref, x_ref, o_ref):
    for j in range(x_ref.shape[1]):
        o_ref[:, j, :] = _dot_const(mh_ref[...], ml_ref[...], x_ref[:, j, :])


def _rowmix_slabs(m, x):
    mh, ml = m
    rin, n2, c = x.shape
    nb = min(SLABS_PER_STEP, n2)
    return pl.pallas_call(
        _rowmix_slabs_kernel, grid=(n2 // nb,),
        in_specs=[pl.BlockSpec(mh.shape, lambda i: (0, 0)), pl.BlockSpec(ml.shape, lambda i: (0, 0)),
                  pl.BlockSpec((rin, nb, c), lambda i: (0, i, 0))],
        out_specs=pl.BlockSpec((mh.shape[0], nb, c), lambda i: (0, i, 0)),
        out_shape=jax.ShapeDtypeStruct((mh.shape[0], n2, c), F32), compiler_params=_params(1),
        name="dft_rowmix_slabs",
    )(mh, ml, x)


def _spectral_kernel(x_ref, tw_ref, fh_ref, fl_ref, k_ref, o_ref, *, conv):
    xr, xi = x_ref[0], x_ref[1]
    reps = xr.shape[1] // HEAD_LANES
    tr = jnp.concatenate([tw_ref[0]] * reps, axis=1)
    ti = jnp.concatenate([tw_ref[1]] * reps, axis=1)
    frh, fih, frl, fil = fh_ref[0], fh_ref[1], fl_ref[0], fl_ref[1]

    def dft(ar, ai, conj):
        rr, ii = _dot_const(frh, frl, ar), _dot_const(fih, fil, ai)
        ri, ir = _dot_const(frh, frl, ai), _dot_const(fih, fil, ar)
        return (rr + ii, ri - ir) if conj else (rr - ii, ri + ir)

    yr, yi = dft(xr * tr - xi * ti, xr * ti + xi * tr, False)
    if not conv:
        o_ref[0] = yr * k_ref[...]
        o_ref[1] = yi * k_ref[...]
        return
    kr, ki = k_ref[0], k_ref[1]
    cr, ci = dft(yr * kr - yi * ki, yr * ki + yi * kr, True)
    o_ref[0] = cr * tr + ci * ti
    o_ref[1] = ci * tr - cr * ti


def _spectral(x, k, tabs, conv):
    _, n1, n2, c = x.shape
    slab = pl.BlockSpec((2, None, n2, c), lambda i: (0, i, 0, 0))
    kspec = slab if conv else pl.BlockSpec(k.shape, lambda i: (0, 0))
    return pl.pallas_call(
        functools.partial(_spectral_kernel, conv=conv), grid=(n1,),
        in_specs=[slab, pl.BlockSpec((None, 2, n2, HEAD_LANES), lambda i: (i, 0, 0, 0)),
                  pl.BlockSpec(tabs['f_hi'].shape, lambda i: (0, 0, 0)),
                  pl.BlockSpec(tabs['f_lo'].shape, lambda i: (0, 0, 0)), kspec],
        out_specs=slab, out_shape=jax.ShapeDtypeStruct(x.shape, F32), compiler_params=_params(1),
        name="dft_spectral_conv" if conv else "dft_spectral_filter",
    )(x, tabs['tw'], tabs['f_hi'], tabs['f_lo'], k)


def _filter_spectrum(k, tabs):
    n, c = k.shape
    n1, n2 = tabs['n1'], tabs['n2']
    if n1 > 1:
        x = _rowmix_slabs(tabs['m_real'], k.reshape(n1, n2, c)).reshape(2, n1, n2, c)
    else:
        x = jnp.stack([k, jnp.zeros_like(k)]).reshape(2, 1, n2, c)
    return _spectral(x, jnp.full((1, c), 1.0 / n, F32), tabs, conv=False)


def _rowmix_gate_kernel(mh_ref, ml_ref, x_ref, g_ref, v_ref, s_ref, o_ref):
    o_ref[...] = g_ref[...] * (_dot_const(mh_ref[...], ml_ref[...], x_ref[...]) + s_ref[...] * v_ref[...])


def _gated_long_conv(gate, v, kf, skip, tabs):
    b, n, c = v.shape
    assert b == 2
    n1, n2 = tabs['n1'], tabs['n2']
    if n1 == 1:
        x = jnp.concatenate([v, jnp.zeros_like(v)], axis=1).reshape(2, 1, n2, c)
        y = _spectral(x, kf, tabs, conv=True).reshape(2, n2, c)[:, :n]
        return _hyena_gate(gate, y, v, skip)
    x = _rowmix_slabs(tabs['m_fwd'], v.reshape(n1, n2, c)).reshape(2, n1, n2, c)
    y = _spectral(x, kf, tabs, conv=True).reshape(2 * n1, n2 * c)
    mh, ml = tabs['m_inv']
    cols = n2 * c
    ct = min(2048, cols)
    tile = lambda rows: pl.BlockSpec((rows, ct), lambda i: (0, i))
    return pl.pallas_call(
        _rowmix_gate_kernel, grid=(cols // ct,),
        in_specs=[pl.BlockSpec(mh.shape, lambda i: (0, 0)), pl.BlockSpec(ml.shape, lambda i: (0, 0)),
                  tile(2 * n1), tile(n1), tile(n1), pl.BlockSpec((1, ct), lambda i: (0, 0))],
        out_specs=tile(n1), out_shape=jax.ShapeDtypeStruct((n1, cols), F32), compiler_params=_params(1),
        name="dft_rowmix_gate",
    )(mh, ml, y, gate.reshape(n1, cols), v.reshape(n1, cols), jnp.tile(skip, (1, ct // c))).reshape(2, n, c)


def _gate_kernel(g_ref, y_ref, v_ref, s_ref, o_ref):
    o_ref[...] = g_ref[...] * (y_ref[...] + s_ref[...] * v_ref[...])


def _hyena_gate(gate, y, v, skip):
    b, n, c = v.shape
    tt = min(2048, n)
    tok = pl.BlockSpec((None, tt, c), lambda bi, ti: (bi, ti, 0))
    return pl.pallas_call(_gate_kernel, grid=(b, n // tt),
                          in_specs=[tok, tok, tok, pl.BlockSpec((1, c), lambda bi, ti: (0, 0))], out_specs=tok,
                          out_shape=jax.ShapeDtypeStruct(v.shape, F32), compiler_params=_params(2),
                          name="hyena_gate")(gate, y, v, skip)


def _hyena_branch(hy, p, tabs):
    n = hy.shape[1]
    x1, x2, v = _token_conv(_short_conv_kernel, hy, (p['hyena_short_w'], p['hyena_short_b'][None, :]),
                            (HYENA_WIDTH,) * 3, "hyena_short_conv")
    taps = _hyena_taps(n, p)
    for o, gate in enumerate((x1, x2)):
        v = _gated_long_conv(gate, v, _filter_spectrum(taps[o], tabs), p['hyena_skip'][o][None, :], tabs)
    return v


def _adaln_kernel(c_ref, w_ref, b_ref, o_ref):
    s = c_ref[...]
    s = s * jax.nn.sigmoid(s)
    o_ref[...] = _dot_split(*_split_bf16(s), *_split_bf16(w_ref[...])) + b_ref[...]


def _adaln(cond, w, b):
    d, width = w.shape
    ct = width // 6
    return pl.pallas_call(
        _adaln_kernel, grid=(6,),
        in_specs=[pl.BlockSpec(cond.shape, lambda i: (0, 0)), pl.BlockSpec((d, ct), lambda i: (0, i)),
                  pl.BlockSpec((1, ct), lambda i: (0, i))],
        out_specs=pl.BlockSpec((cond.shape[0], ct), lambda i: (0, i)),
        out_shape=jax.ShapeDtypeStruct((cond.shape[0], width), F32), compiler_params=_params(1), name="adaln",
    )(cond, w, b[None, :])


def _mod_rows(mod, norm_mix_g, norm_ffn_g, final_g, batch):
    sh1, sc1, g1, sh2, sc2, g2 = jnp.split(mod, 6, axis=-1)
    rows = jnp.stack([norm_mix_g * (1.0 + sc1), sh1, g1, norm_ffn_g * (1.0 + sc2), sh2, g2,
                      jnp.broadcast_to(final_g, g1.shape), jnp.zeros_like(g1)], axis=1)
    return jnp.broadcast_to(rows, (batch,) + rows.shape[1:])


def kernel(x, c, ctx, c_ctx, ada_w, ada_b, norm_mix_g, norm_ffn_g, w_in, diff_lambda, diff_subln_g, hyena_short_w, hyena_short_b, filt_w1, filt_b1, filt_freq, filt_w2, filt_b2, filt_w3, hyena_skip, conf_dw_w, conf_ln_g, conf_ln_b, mla_q_norm_g, mla_kv_norm_g, mla_w_uq, mla_w_ukv, w_branch, w_out, w_router, w_exp_in, w_exp_out, final_norm_g):
    depth = w_in.shape[0]
    batch, n_lat, d = x.shape
    n_ctx = ctx.shape[1]
    rope_lat = _rope_operands(n_lat, identity=False)
    rope_ctx = _rope_operands(n_ctx, identity=True)
    dft_lat, dft_ctx = _dft_tables(2 * n_lat), _dft_tables(2 * n_ctx)
    cond = jnp.concatenate([c, c_ctx[None], jnp.zeros((MOD_ROWS - batch - 1, d), F32)], axis=0)
    tile_lat, tile_ctx = min(512, n_lat), min(256, n_ctx)
    h_lat, h_ctx = x, ctx
    for l in range(depth):
        last = l == depth - 1
        p = dict(w_in=w_in[l], diff_subln_g=diff_subln_g[l], hyena_short_w=hyena_short_w[l],
                 hyena_short_b=hyena_short_b[l], filt_w1=filt_w1[l], filt_b1=filt_b1[l], filt_freq=filt_freq[l],
                 filt_w2=filt_w2[l], filt_b2=filt_b2[l], filt_w3=filt_w3[l], hyena_skip=hyena_skip[l],
                 conf_dw_w=conf_dw_w[l], conf_ln_g=conf_ln_g[l], conf_ln_b=conf_ln_b[l],
                 mla_q_norm_g=mla_q_norm_g[l], mla_kv_norm_g=mla_kv_norm_g[l], mla_w_uq=mla_w_uq[l],
                 mla_w_ukv=mla_w_ukv[l], w_branch=w_branch[l], w_out=w_out[l], w_router=w_router[l],
                 w_exp_in=w_exp_in[l], w_exp_out=w_exp_out[l])
        ada = _adaln(cond, ada_w[l], ada_b[l])
        mod_lat = _mod_rows(ada[:batch], norm_mix_g[l], norm_ffn_g[l], final_norm_g, batch)
        mod_ctx = _mod_rows(ada[batch:batch + 1], norm_mix_g[l], norm_ffn_g[l], final_norm_g, batch)
        lam_init = 0.8 - 0.6 * math.exp(-0.3 * l)
        lq1, lk1, lq2, lk2 = diff_lambda[l].astype(F32)
        lam = jnp.reshape(jnp.exp(jnp.sum(lq1 * lk1)) - jnp.exp(jnp.sum(lq2 * lk2)) + lam_init, (1,))
        w_inp, w_mrg = _inproj_weights(p), _merge_weights(p, lam_init)

        qdT_l, kd_l, vdT_l, qmT_l, km_l, vmT_l, hy_l, glu_l = _inproj(h_lat, mod_lat, w_inp, rope_lat, tile=tile_lat)
        qdT_c, kd_c, vdT_c, qmT_c, km_c, vmT_c, hy_c, glu_c = _inproj(h_ctx, mod_ctx, w_inp, rope_ctx, tile=tile_ctx)
        a_lat = _flash_attention(lam, qdT_l, kd_c, vdT_c, kd_l, vdT_l, n_maps=2, tq=min(512, n_lat))
        m_lat = _flash_attention(lam, qmT_l, km_c, vmT_c, km_l, vmT_l, n_maps=1, tq=min(1024, n_lat))
        h_lat, u2_lat, lg_lat = _merge(h_lat, mod_lat, a_lat, _hyena_branch(hy_l, p, dft_lat), _conformer_branch(glu_l, p),
                                       m_lat, w_mrg, tile=min(256, n_lat))
        w_ei, w_eo = p['w_exp_in'].astype(BF16), p['w_exp_out'].astype(BF16)
        h_lat = _expert_choice_ffn(h_lat, mod_lat, u2_lat, lg_lat, w_ei, w_eo, final_norm=last)
        if not last:
            a_ctx = _flash_attention(lam, qdT_c, kd_c, vdT_c, None, None, n_maps=2, tq=n_ctx)
            m_ctx = _flash_attention(lam, qmT_c, km_c, vmT_c, None, None, n_maps=1, tq=n_ctx)
            h_ctx, u2_ctx, lg_ctx = _merge(h_ctx, mod_ctx, a_ctx, _hyena_branch(hy_c, p, dft_ctx),
                                           _conformer_branch(glu_c, p), m_ctx, w_mrg, tile=tile_ctx)
            h_ctx = _expert_choice_ffn(h_ctx, mod_ctx, u2_ctx, lg_ctx, w_ei, w_eo)
    return h_lat
```

```python
import functools
import math

import jax
import jax.numpy as jnp
from jax import lax
from jax.experimental import pallas as pl
from jax.experimental.pallas import tpu as pltpu

GRID_W = 64
ROPE_BASE = 10000.0
EPS = 1e-6

DIFF_HEADS = 4
DIFF_HEAD_DIM = 64
DIFF_V_DIM = 2 * DIFF_HEAD_DIM
HYENA_WIDTH = 256
HYENA_ORDER = 2
FILT_EMB = 33
DECAY_TARGET = 1e-2
FAST_DECAY = 0.3
SLOW_DECAY = 1.5
CONF_WIDTH = 256
MLA_HEADS = 4
MLA_Q_RANK = 256
MLA_KV_RANK = 128
MLA_NOPE = 64
MLA_ROPE = 32
MLA_V = 64
MLA_SCALE = (MLA_NOPE + MLA_ROPE) ** -0.5
N_BRANCH = 4
N_EXPERTS = 16
EC_CAPACITY = 2

DIFF_QK_W = DIFF_HEADS * 2 * DIFF_HEAD_DIM
DIFF_V_W = DIFF_HEADS * DIFF_V_DIM
HYENA_PROJ = (HYENA_ORDER + 1) * HYENA_WIDTH
CONF_PROJ = 2 * CONF_WIDTH
IN_SPLITS = (DIFF_QK_W, DIFF_QK_W, DIFF_V_W, HYENA_PROJ, CONF_PROJ, MLA_Q_RANK, MLA_KV_RANK, MLA_ROPE)
BRANCH_WIDTHS = (DIFF_V_W, HYENA_WIDTH, CONF_WIDTH, MLA_HEADS * MLA_V)

HEAD_LANES = 128
DIFF_V_PAD = 16
ATT_W = DIFF_HEADS * HEAD_LANES
LOG2E = 1.4426950408889634
VMEM_LIMIT_BYTES = 48 * 1024 * 1024
MOD_ROWS = 8

F32 = jnp.float32
BF16 = jnp.bfloat16
_NT = (((1,), (1,)), ((), ()))


def _params(n_axes, vmem=VMEM_LIMIT_BYTES):
    return pltpu.CompilerParams(dimension_semantics=("arbitrary",) * n_axes, vmem_limit_bytes=vmem)


def _flash_kernel(lam_ref, qT_ref, kc_ref, vcT_ref, *rest, n_maps, n_lat_chunks, tk, sum_row):
    if n_lat_chunks:
        kl_ref, vlT_ref, o_ref, acc_ref, m_ref, q2_ref, s_ref = rest
    else:
        o_ref, acc_ref, m_ref, q2_ref = rest
    qT = qT_ref[...]
    tq = qT.shape[1]
    if n_maps == 2:
        row = lax.broadcasted_iota(jnp.int32, qT.shape, 0)
        zero = jnp.zeros_like(qT)
        q2_ref[:, :tq] = jnp.where(row < DIFF_HEAD_DIM, qT, zero)
        q2_ref[:, tq:] = jnp.where(row >= DIFF_HEAD_DIM, qT, zero)
    else:
        q2_ref[...] = qT
    m_ref[...] = jnp.full(m_ref.shape, -jnp.inf, F32)
    acc_ref[...] = jnp.zeros(acc_ref.shape, F32)

    def scores(k):
        return jnp.dot(k, q2_ref[...], preferred_element_type=F32)

    def absorb(s, vT):
        m_prev = m_ref[...]
        m_new = jnp.maximum(m_prev, jnp.max(s, axis=0, keepdims=True))
        alpha = jnp.exp2(m_prev - m_new)
        p = jnp.exp2(s - m_new).astype(BF16)
        acc_ref[...] = alpha * acc_ref[...] + jnp.dot(vT, p, preferred_element_type=F32)
        m_ref[...] = m_new

    def chunk(c):
        return pl.ds(c * tk if isinstance(c, int) else pl.multiple_of(c * tk, tk), tk)

    def keys(c):
        return kl_ref[chunk(c), :]

    def values_t(c):
        return vlT_ref[:, chunk(c)]

    absorb(scores(kc_ref[...]), vcT_ref[...])
    if n_lat_chunks:
        s_ref[0] = scores(keys(0))

        def pair(j, carry):
            c = 2 * j
            s_ref[1] = scores(keys(c + 1))
            absorb(s_ref[0], values_t(c))
            s_ref[0] = scores(keys(c + 2))
            absorb(s_ref[1], values_t(c + 1))
            return carry
        lax.fori_loop(0, n_lat_chunks // 2 - 1, pair, 0)
        s_ref[1] = scores(keys(n_lat_chunks - 1))
        absorb(s_ref[0], values_t(n_lat_chunks - 2))
        absorb(s_ref[1], values_t(n_lat_chunks - 1))
    o = acc_ref[0:HEAD_LANES, :] / acc_ref[sum_row:sum_row + 1, :]
    if n_maps == 2:
        o = o[:, :tq] - lam_ref[0] * o[:, tq:]
        o = o * lax.rsqrt(jnp.mean(o * o, axis=0, keepdims=True) + EPS)
    o_ref[...] = o.T.astype(BF16)


def _flash_stream_kernel(lam_ref, qT_ref, kc_ref, vcT_ref, kl_ref, vlT_ref, o_ref, acc_ref, m_ref, q2_ref, sc_ref,
                         s_ref, *, n_maps, n_lat_chunks, tk, tq, sum_row):
    n_q = qT_ref.shape[1] // tq

    def load_queries(qi):
        qT = qT_ref[:, pl.ds(pl.multiple_of(qi * tq, tq), tq)]
        if n_maps == 2:
            row = lax.broadcasted_iota(jnp.int32, qT.shape, 0)
            zero = jnp.zeros_like(qT)
            q2_ref[:, :tq] = jnp.where(row < DIFF_HEAD_DIM, qT, zero)
            q2_ref[:, tq:] = jnp.where(row >= DIFF_HEAD_DIM, qT, zero)
        else:
            q2_ref[...] = qT

    def scores(k):
        return jnp.dot(k, q2_ref[...], preferred_element_type=F32)

    def absorb(s, vT):
        m_prev = m_ref[...]
        m_new = jnp.maximum(m_prev, jnp.max(s, axis=0, keepdims=True))
        alpha = jnp.exp2(m_prev - m_new)
        p = jnp.exp2(s - m_new).astype(BF16)
        acc_ref[...] = alpha * acc_ref[...] + jnp.dot(vT, p, preferred_element_type=F32)
        m_ref[...] = m_new

    load_queries(0)
    sc_ref[...] = scores(kc_ref[...])

    def query_block(qi, carry):
        m_ref[...] = jnp.full(m_ref.shape, -jnp.inf, F32)
        acc_ref[...] = jnp.zeros(acc_ref.shape, F32)
        def chunk(c):
            return pl.ds(c * tk if isinstance(c, int) else pl.multiple_of(c * tk, tk), tk)

        r = q2_ref.shape[1]

        s_ref[0, :, 0:r] = scores(kl_ref[chunk(0), :])
        absorb(sc_ref[...], vcT_ref[...])

        def pair(j, inner):
            c = 2 * j
            s_ref[1, :, 0:r] = scores(kl_ref[chunk(c + 1), :])
            absorb(s_ref[0, :, 0:r], vlT_ref[:, chunk(c)])
            s_ref[0, :, 0:r] = scores(kl_ref[chunk(c + 2), :])
            absorb(s_ref[1, :, 0:r], vlT_ref[:, chunk(c + 1)])
            return inner
        lax.fori_loop(0, n_lat_chunks // 2 - 1, pair, 0)
        s_ref[1, :, 0:r] = scores(kl_ref[chunk(n_lat_chunks - 1), :])
        absorb(s_ref[0, :, 0:r], vlT_ref[:, chunk(n_lat_chunks - 2)])
        load_queries(jnp.minimum(qi + 1, n_q - 1))
        sc_ref[...] = scores(kc_ref[...])
        absorb(s_ref[1, :, 0:r], vlT_ref[:, chunk(n_lat_chunks - 1)])
        o = acc_ref[0:HEAD_LANES, :] / acc_ref[sum_row:sum_row + 1, :]
        if n_maps == 2:
            o = o[:, :tq] - lam_ref[0] * o[:, tq:]
            o = o * lax.rsqrt(jnp.mean(o * o, axis=0, keepdims=True) + EPS)
        o_ref[pl.ds(pl.multiple_of(qi * tq, tq), tq), :] = o.T.astype(BF16)
        return carry
    lax.fori_loop(0, n_q, query_block, 0)


def _flash_attention(lam, qT, kc, vcT, kl, vlT, *, n_maps, tq):
    b, _, s = qT.shape
    lc = kc.shape[1]
    mv = vcT.shape[2]
    sum_row = HEAD_LANES if n_maps == 2 else MLA_V
    r = n_maps * tq
    if kl is not None:
        sl = kl.shape[1]
        tk = _lat_chunk(sl)
        n_lat_chunks = sl // tk
        assert n_lat_chunks * tk == sl and s % tq == 0
        return pl.pallas_call(
            functools.partial(_flash_stream_kernel, n_maps=n_maps, n_lat_chunks=n_lat_chunks, tk=tk, tq=tq,
                              sum_row=sum_row),
            grid=(b, DIFF_HEADS),
            in_specs=[pl.BlockSpec(memory_space=pltpu.SMEM),
                      pl.BlockSpec((None, HEAD_LANES, s), lambda bi, hi: (bi, hi, 0)),
                      pl.BlockSpec((None, lc, HEAD_LANES), lambda bi, hi: (bi, 0, hi)),
                      pl.BlockSpec((None, None, mv, lc), lambda bi, hi: (bi, hi, 0, 0)),
                      pl.BlockSpec((None, sl, HEAD_LANES), lambda bi, hi: (bi, 0, hi)),
                      pl.BlockSpec((None, None, mv, sl), lambda bi, hi: (bi, hi, 0, 0))],
            out_specs=pl.BlockSpec((None, s, HEAD_LANES), lambda bi, hi: (bi, 0, hi)),
            out_shape=jax.ShapeDtypeStruct((b, s, ATT_W), BF16),
            scratch_shapes=[pltpu.VMEM((mv, r), F32), pltpu.VMEM((1, r), F32), pltpu.VMEM((HEAD_LANES, r), BF16),
                            pltpu.VMEM((lc, r), F32), pltpu.VMEM((2, tk, r + HEAD_LANES), F32)],
            compiler_params=_params(2, 56 * 1024 * 1024),
            name=f"flash_attention_{n_maps}map",
        )(lam, qT, kc, vcT, kl, vlT)
    in_specs = [
        pl.BlockSpec(memory_space=pltpu.SMEM),
        pl.BlockSpec((None, HEAD_LANES, tq), lambda bi, hi, qi: (bi, hi, qi)),
        pl.BlockSpec((None, lc, HEAD_LANES), lambda bi, hi, qi: (bi, 0, hi)),
        pl.BlockSpec((None, None, mv, lc), lambda bi, hi, qi: (bi, hi, 0, 0)),
    ]
    args = [lam, qT, kc, vcT]
    scratch = [pltpu.VMEM((mv, r), F32), pltpu.VMEM((1, r), F32), pltpu.VMEM((HEAD_LANES, r), BF16)]
    n_lat_chunks, tk = 0, 0
    if kl is not None:
        sl = kl.shape[1]
        tk = _lat_chunk(sl)
        n_lat_chunks = sl // tk
        assert n_lat_chunks % 2 == 0 and n_lat_chunks * tk == sl
        scratch.append(pltpu.VMEM((2, tk, r), F32))
        in_specs += [
            pl.BlockSpec((None, sl, HEAD_LANES), lambda bi, hi, qi: (bi, 0, hi)),
            pl.BlockSpec((None, None, mv, sl), lambda bi, hi, qi: (bi, hi, 0, 0)),
        ]
        args += [kl, vlT]
    return pl.pallas_call(
        functools.partial(_flash_kernel, n_maps=n_maps, n_lat_chunks=n_lat_chunks, tk=tk, sum_row=sum_row),
        grid=(b, DIFF_HEADS, s // tq),
        in_specs=in_specs,
        out_specs=pl.BlockSpec((None, tq, HEAD_LANES), lambda bi, hi, qi: (bi, qi, hi)),
        out_shape=jax.ShapeDtypeStruct((b, s, ATT_W), BF16),
        scratch_shapes=scratch,
        compiler_params=_params(3),
        name=f"flash_attention_{n_maps}map",
    )(*args)


def _lat_chunk(s):
    return min(1024, s // 2)


W_NAT_SPLITS = (DIFF_QK_W, HYENA_PROJ, CONF_PROJ, MLA_Q_RANK, MLA_KV_RANK, HEAD_LANES)


def _modulated_norm(h, a, shift):
    return h * lax.rsqrt(jnp.mean(h * h, axis=-1, keepdims=True) + EPS) * a + shift


def _rope_lanes(x, tab_ref, shift):
    return (x * tab_ref[0] + pltpu.roll(x, shift, 1) * tab_ref[1]
            + pltpu.roll(x, HEAD_LANES - shift, 1) * tab_ref[2])


def _inproj_kernel(h_ref, mod_ref, wnat_ref, wT_ref, wuqT_ref, wukvk_ref, wuvT_ref, gq_ref, gkv_ref,
                   ropeT_d_ref, rope_kd_ref, ropeT_m_ref, rope_km_ref,
                   qdT_ref, kd_ref, vdT_ref, qmT_ref, km_ref, vmT_ref, hy_ref, glu_ref):
    u = _modulated_norm(h_ref[...], mod_ref[0:1, :], mod_ref[1:2, :]).astype(BF16)
    z = jnp.dot(u, wnat_ref[...], preferred_element_type=F32)
    zT = lax.dot_general(wT_ref[...], u, _NT, preferred_element_type=F32)
    offs = [0]
    for w in W_NAT_SPLITS:
        offs.append(offs[-1] + w)
    dk, hy, cf, cq, ckv, krp = (z[:, offs[i]:offs[i + 1]] for i in range(len(W_NAT_SPLITS)))

    for hd in range(DIFF_HEADS):
        sl = slice(hd * HEAD_LANES, (hd + 1) * HEAD_LANES)
        kd_ref[:, sl] = _rope_lanes(dk[:, sl], rope_kd_ref, DIFF_HEAD_DIM // 2).astype(BF16)
    cos_d, sin_d = ropeT_d_ref[0], ropeT_d_ref[1]
    half = DIFF_HEAD_DIM // 2
    for g in range(2 * DIFF_HEADS):
        x1 = zT[g * DIFF_HEAD_DIM:g * DIFF_HEAD_DIM + half]
        x2 = zT[g * DIFF_HEAD_DIM + half:(g + 1) * DIFF_HEAD_DIM]
        qdT_ref[g * DIFF_HEAD_DIM:g * DIFF_HEAD_DIM + half, :] = (x1 * cos_d - x2 * sin_d).astype(BF16)
        qdT_ref[g * DIFF_HEAD_DIM + half:(g + 1) * DIFF_HEAD_DIM, :] = (x1 * sin_d + x2 * cos_d).astype(BF16)
    tail = jnp.where(lax.broadcasted_iota(jnp.int32, (DIFF_V_PAD, zT.shape[1]), 0) == 0, 1.0, 0.0).astype(BF16)
    for hd in range(DIFF_HEADS):
        r0 = DIFF_QK_W + hd * DIFF_V_DIM
        vdT_ref[hd, 0:DIFF_V_DIM, :] = zT[r0:r0 + DIFF_V_DIM].astype(BF16)
        vdT_ref[hd, DIFF_V_DIM:, :] = tail

    hy_ref[...] = hy
    glu_ref[...] = cf[:, :CONF_WIDTH] * jax.nn.sigmoid(cf[:, CONF_WIDTH:])

    cqn = (cq * lax.rsqrt(jnp.mean(cq * cq, axis=-1, keepdims=True) + EPS) * gq_ref[...]).astype(BF16)
    ckvn = (ckv * lax.rsqrt(jnp.mean(ckv * ckv, axis=-1, keepdims=True) + EPS) * gkv_ref[...]).astype(BF16)
    qT = lax.dot_general(wuqT_ref[...], cqn, _NT, preferred_element_type=F32)
    cos_m, sin_m = ropeT_m_ref[0], ropeT_m_ref[1]
    hr = MLA_ROPE // 2
    for hd in range(MLA_HEADS):
        base = hd * HEAD_LANES
        r1 = base + MLA_NOPE
        x1, x2 = qT[r1:r1 + hr], qT[r1 + hr:r1 + 2 * hr]
        qmT_ref[base:r1, :] = qT[base:r1].astype(BF16)
        qmT_ref[r1:r1 + hr, :] = (x1 * cos_m - x2 * sin_m).astype(BF16)
        qmT_ref[r1 + hr:r1 + 2 * hr, :] = (x1 * sin_m + x2 * cos_m).astype(BF16)
        qmT_ref[r1 + 2 * hr:base + HEAD_LANES, :] = jnp.zeros((HEAD_LANES - MLA_NOPE - MLA_ROPE, qT.shape[1]), BF16)
    kn = jnp.dot(ckvn, wukvk_ref[...], preferred_element_type=F32)
    kr = _rope_lanes(krp, rope_km_ref, hr)
    for hd in range(MLA_HEADS):
        sl = slice(hd * HEAD_LANES, (hd + 1) * HEAD_LANES)
        km_ref[:, sl] = (kn[:, sl] + kr).astype(BF16)
    vT = lax.dot_general(wuvT_ref[...], ckvn, _NT, preferred_element_type=F32)
    ones_row = lax.broadcasted_iota(jnp.int32, vT.shape, 0) % HEAD_LANES == MLA_V
    vT = jnp.where(ones_row, 1.0, vT).astype(BF16)
    for hd in range(MLA_HEADS):
        vmT_ref[hd] = vT[hd * HEAD_LANES:(hd + 1) * HEAD_LANES]


def _inproj(h, mod, wts, rope, *, tile):
    b, n, d = h.shape
    const2 = lambda bi, ti: (0, 0)
    tok = lambda w: pl.BlockSpec((None, tile, w), lambda bi, ti: (bi, ti, 0))
    tokT = lambda w: pl.BlockSpec((None, w, tile), lambda bi, ti: (bi, 0, ti))
    full = lambda a: pl.BlockSpec(a.shape, const2)
    in_specs = [tok(d), pl.BlockSpec((None, MOD_ROWS, d), lambda bi, ti: (bi, 0, 0))]
    in_specs += [full(wts[k]) for k in ('w_nat', 'w_T', 'w_uqT', 'w_ukvk', 'w_uvT', 'gq', 'gkv')]
    in_specs += [pl.BlockSpec((2, DIFF_HEAD_DIM // 2, tile), lambda bi, ti: (0, 0, ti)),
                 pl.BlockSpec((3, tile, HEAD_LANES), lambda bi, ti: (0, ti, 0)),
                 pl.BlockSpec((2, MLA_ROPE // 2, tile), lambda bi, ti: (0, 0, ti)),
                 pl.BlockSpec((3, tile, HEAD_LANES), lambda bi, ti: (0, ti, 0))]
    sds = jax.ShapeDtypeStruct
    vrows_d, vrows_m = DIFF_V_DIM + DIFF_V_PAD, HEAD_LANES
    headsT = lambda rows: pl.BlockSpec((None, DIFF_HEADS, rows, tile), lambda bi, ti: (bi, 0, 0, ti))
    out_shape = (sds((b, ATT_W, n), BF16), sds((b, n, ATT_W), BF16), sds((b, DIFF_HEADS, vrows_d, n), BF16),
                 sds((b, ATT_W, n), BF16), sds((b, n, ATT_W), BF16), sds((b, MLA_HEADS, vrows_m, n), BF16),
                 sds((b, n, HYENA_PROJ), F32), sds((b, n, CONF_WIDTH), F32))
    out_specs = (tokT(ATT_W), tok(ATT_W), headsT(vrows_d), tokT(ATT_W), tok(ATT_W), headsT(vrows_m),
                 tok(HYENA_PROJ), tok(CONF_WIDTH))
    return pl.pallas_call(
        _inproj_kernel, grid=(b, n // tile), in_specs=in_specs, out_specs=out_specs, out_shape=out_shape,
        compiler_params=_params(2), name="inproj",
    )(h, mod, wts['w_nat'], wts['w_T'], wts['w_uqT'], wts['w_ukvk'], wts['w_uvT'], wts['gq'], wts['gkv'],
      rope['T_d'], rope['k_d'], rope['T_m'], rope['k_m'])


def _pad_heads(w, width):
    rows = w.shape[0]
    w = w.reshape(rows, MLA_HEADS, width)
    return jnp.pad(w, ((0, 0), (0, 0), (0, HEAD_LANES - width))).reshape(rows, ATT_W)


def _inproj_weights(p):
    d = p['w_in'].shape[0]
    dq, dk, dv, hy, cf, cq, ckv, kr = _split(p['w_in'][:, :sum(IN_SPLITS)], IN_SPLITS)
    krp = jnp.zeros((d, HEAD_LANES), F32).at[:, MLA_NOPE:MLA_NOPE + MLA_ROPE].set(kr)
    w_ukv = p['mla_w_ukv'].reshape(MLA_KV_RANK, MLA_HEADS, MLA_NOPE + MLA_V)
    return dict(
        w_nat=jnp.concatenate([dk, hy, cf, cq, ckv, krp], axis=1).astype(BF16),
        w_T=jnp.concatenate([dq * (DIFF_HEAD_DIM ** -0.5 * LOG2E), dv], axis=1).T.astype(BF16),
        w_uqT=_pad_heads(p['mla_w_uq'] * (MLA_SCALE * LOG2E), MLA_NOPE + MLA_ROPE).T.astype(BF16),
        w_ukvk=_pad_heads(w_ukv[:, :, :MLA_NOPE].reshape(MLA_KV_RANK, -1), MLA_NOPE).astype(BF16),
        w_uvT=_pad_heads(w_ukv[:, :, MLA_NOPE:].reshape(MLA_KV_RANK, -1), MLA_V).T.astype(BF16),
        gq=p['mla_q_norm_g'][None, :], gkv=p['mla_kv_norm_g'][None, :])


def _rope_tables(n_tok, rot_dim):
    rows = n_tok // GRID_W
    row = jnp.repeat(jnp.arange(rows), GRID_W).astype(F32)
    col = jnp.tile(jnp.arange(GRID_W), rows).astype(F32)
    nf = rot_dim // 4
    inv = ROPE_BASE ** (-jnp.arange(nf, dtype=F32) / nf)
    ang = jnp.concatenate([row[:, None] * inv, col[:, None] * inv], axis=-1)
    return jnp.cos(ang), jnp.sin(ang)


def _rope_operands(n_tok, identity):
    if identity:
        cos_d, sin_d = jnp.ones((n_tok, DIFF_HEAD_DIM // 2), F32), jnp.zeros((n_tok, DIFF_HEAD_DIM // 2), F32)
        cos_m, sin_m = jnp.ones((n_tok, MLA_ROPE // 2), F32), jnp.zeros((n_tok, MLA_ROPE // 2), F32)
    else:
        cos_d, sin_d = _rope_tables(n_tok, DIFF_HEAD_DIM)
        cos_m, sin_m = _rope_tables(n_tok, MLA_ROPE)
    z_d, z_m = jnp.zeros_like(sin_d), jnp.zeros_like(sin_m)
    two = lambda a, bb: jnp.tile(jnp.concatenate([a, bb], axis=1), (1, 2))
    lo, hi = jnp.zeros((n_tok, MLA_NOPE), F32), jnp.zeros((n_tok, HEAD_LANES - MLA_NOPE - MLA_ROPE), F32)
    mid = lambda a, bb: jnp.concatenate([lo, a, bb, hi], axis=1)
    return dict(T_d=jnp.stack([cos_d.T, sin_d.T]), T_m=jnp.stack([cos_m.T, sin_m.T]),
                k_d=jnp.stack([two(cos_d, cos_d), two(z_d, sin_d), two(-sin_d, z_d)]),
                k_m=jnp.stack([mid(cos_m, cos_m), mid(z_m, sin_m), mid(-sin_m, z_m)]))


def _merge_kernel(h_ref, mod_ref, a_ref, hy_ref, cf_ref, m_ref, wg_ref, wbd_ref, wbh_ref, wbc_ref, wbm_ref,
                  wo_ref, wrh_ref, wrl_ref, hn_ref, u2_ref, lg_ref):
    h = h_ref[...]
    d = h.shape[1]
    u = _modulated_norm(h, mod_ref[0:1, :], mod_ref[1:2, :]).astype(BF16)
    gates = jax.nn.sigmoid(jnp.dot(u, wg_ref[...], preferred_element_type=F32))
    dot = lambda x, w_ref: jnp.dot(x, w_ref[...], preferred_element_type=F32)
    acc = gates[:, :d] * dot(a_ref[...], wbd_ref)
    acc += gates[:, d:2 * d] * dot(hy_ref[...].astype(BF16), wbh_ref)
    acc += gates[:, 2 * d:3 * d] * dot(cf_ref[...].astype(BF16), wbc_ref)
    acc += gates[:, 3 * d:] * dot(m_ref[...], wbm_ref)
    hn = h + mod_ref[2:3, :] * dot(acc.astype(BF16), wo_ref)
    hn_ref[...] = hn
    u2 = _modulated_norm(hn, mod_ref[3:4, :], mod_ref[4:5, :])
    u2h = u2.astype(BF16)
    u2l = (u2 - u2h.astype(F32)).astype(BF16)
    u2_ref[...] = u2h
    lg_ref[...] = dot(u2h, wrh_ref) + (dot(u2l, wrh_ref) + dot(u2h, wrl_ref))


def _merge(h, mod, a, hyv, cfv, m, wts, *, tile):
    b, n, d = h.shape
    const2 = lambda bi, ti: (0, 0)
    tok = lambda w: pl.BlockSpec((None, tile, w), lambda bi, ti: (bi, ti, 0))
    names = ('w_gate', 'w_bd', 'w_bh', 'w_bc', 'w_bm', 'w_out', 'w_rh', 'w_rl')
    in_specs = [tok(d), pl.BlockSpec((None, MOD_ROWS, d), lambda bi, ti: (bi, 0, 0)),
                tok(ATT_W), tok(HYENA_WIDTH), tok(CONF_WIDTH), tok(ATT_W)]
    in_specs += [pl.BlockSpec(wts[k].shape, const2, pipeline_mode=pl.Buffered(1)) for k in names]
    sds = jax.ShapeDtypeStruct
    return pl.pallas_call(
        _merge_kernel, grid=(b, n // tile), in_specs=in_specs,
        out_specs=(tok(d), tok(d), tok(HEAD_LANES)),
        out_shape=(sds((b, n, d), F32), sds((b, n, d), BF16), sds((b, n, HEAD_LANES), F32)),
        compiler_params=_params(2, 56 * 1024 * 1024), name="merge",
    )(h, mod, a, hyv, cfv, m, *[wts[k] for k in names])


def _merge_weights(p, lam_init):
    d = p['w_out'].shape[0]
    wb_d, wb_h, wb_c, wb_m = (w.T for w in _split(p['w_branch'].T, BRANCH_WIDTHS))
    wb_d = wb_d * (jnp.tile(p['diff_subln_g'], DIFF_HEADS) * (1.0 - lam_init))[:, None]
    wb_m = jnp.pad(wb_m.reshape(MLA_HEADS, MLA_V, d), ((0, 0), (0, HEAD_LANES - MLA_V), (0, 0))).reshape(ATT_W, d)
    w_r = jnp.pad(p['w_router'], ((0, 0), (0, HEAD_LANES - N_EXPERTS)))
    w_rh = w_r.astype(BF16)
    return dict(w_gate=p['w_in'][:, sum(IN_SPLITS):].astype(BF16), w_bd=wb_d.astype(BF16), w_bh=wb_h.astype(BF16),
                w_bc=wb_c.astype(BF16), w_bm=wb_m.astype(BF16), w_out=p['w_out'].astype(BF16),
                w_rh=w_rh, w_rl=(w_r - w_rh.astype(F32)).astype(BF16))


SUB_TOKENS = 256
GATHER_WINDOW = 272
COMBINE_TOKENS = 128
COMBINE_WINDOW = 256
ROUTE_MIN_ROWS = 8


def _excl_scan(x, lane, row):
    inc = x
    s = 1
    while s < HEAD_LANES:
        inc = inc + jnp.where(lane >= s, pltpu.roll(inc, s, 2), 0.0)
        s *= 2
    tot = jnp.sum(x, axis=2, keepdims=True) + jnp.zeros_like(x)
    off = tot
    s = 1
    while s < x.shape[1]:
        off = off + jnp.where(row >= s, pltpu.roll(off, s, 1), 0.0)
        s *= 2
    return inc - x + (off - tot)


def _route_kernel(lg_ref, pos_ref, aff_ref, *, n_valid, cap):
    lg = lg_ref[...]
    shape = lg.shape
    lane = lax.broadcasted_iota(jnp.int32, shape, 2)
    row = lax.broadcasted_iota(jnp.int32, shape, 1)
    e = jnp.exp(lg - jnp.max(lg, axis=0, keepdims=True))
    aff = e / jnp.sum(e, axis=0, keepdims=True)
    bits = jnp.where(row * HEAD_LANES + lane < n_valid, pltpu.bitcast(aff, jnp.int32), -1)

    def count(mask):
        c = jnp.sum(jnp.where(mask, 1.0, 0.0), axis=2, keepdims=True)
        return jnp.sum(c, axis=1, keepdims=True)

    def step(i, thr):
        cand = thr | (jnp.int32(1) << (30 - i))
        return jnp.where(count(bits >= cand) >= cap, cand, thr)
    thr = lax.fori_loop(0, 31, step, jnp.zeros((shape[0], 1, 1), jnp.int32))
    gt = bits > thr
    eq = bits == thr
    need = cap - count(gt)
    tie_rank = _excl_scan(jnp.where(eq, 1.0, 0.0), lane, row)
    sel = gt | (eq & (tie_rank < need))
    pos = _excl_scan(jnp.where(sel, 1.0, 0.0), lane, row)
    pos_ref[...] = jnp.where(sel, pos.astype(jnp.int32), -1)
    aff_ref[...] = aff


def _route(logits, cap):
    b, n, _ = logits.shape
    rows = max(ROUTE_MIN_ROWS, n // HEAD_LANES)
    lg = jnp.swapaxes(logits[..., :N_EXPERTS], 1, 2)
    lg = jnp.pad(lg, ((0, 0), (0, 0), (0, rows * HEAD_LANES - n))).reshape(b, N_EXPERTS, rows, HEAD_LANES)
    spec = pl.BlockSpec((None, N_EXPERTS, rows, HEAD_LANES), lambda bi: (bi, 0, 0, 0))
    pos, aff = pl.pallas_call(
        functools.partial(_route_kernel, n_valid=n, cap=cap), grid=(b,), in_specs=[spec], out_specs=(spec, spec),
        out_shape=(jax.ShapeDtypeStruct(lg.shape, jnp.int32), jax.ShapeDtypeStruct(lg.shape, F32)),
        compiler_params=_params(1), name="route",
    )(lg)
    flat = lambda a: a.reshape(b, N_EXPERTS, rows * HEAD_LANES)[..., :n]
    return flat(pos), flat(aff)


def _experts_kernel(base_ref, u_ref, mod_ref, aff_ref, pos_ref, win_ref, wout_ref, ye_ref, xe_ref, *,
                    n_sub, cap):
    bi, ei, kb = pl.program_id(0), pl.program_id(1), pl.program_id(2)
    d = u_ref.shape[1]

    @pl.when(kb == 0)
    def _():
        xe_ref[...] = jnp.zeros(xe_ref.shape, F32)

    slot = lax.broadcasted_iota(jnp.int32, (GATHER_WINDOW, SUB_TOKENS), 0)
    ones = jnp.ones((SUB_TOKENS, HEAD_LANES), BF16)
    for j in range(n_sub):
        tok = slice(j * SUB_TOKENS, (j + 1) * SUB_TOKENS)
        base = pl.multiple_of(base_ref[bi, ei, kb * n_sub + j], 16)
        match = slot == (pos_ref[:, tok] - base)
        onehot = jnp.where(match, 1.0, 0.0).astype(BF16)
        g = aff_ref[:, tok]
        g_hi = g.astype(BF16).astype(F32)
        sel_hi = jnp.where(match, g_hi, 0.0).astype(BF16)
        sel_lo = jnp.where(match, g - g_hi, 0.0).astype(BF16)
        rows = pl.ds(base, GATHER_WINDOW)
        xe_ref[rows, :d] += jnp.dot(onehot, u_ref[tok, :], preferred_element_type=F32)
        xe_ref[rows, d:d + HEAD_LANES] += jnp.dot(sel_hi, ones, preferred_element_type=F32)
        xe_ref[rows, d + HEAD_LANES:] += jnp.dot(sel_lo, ones, preferred_element_type=F32)

    @pl.when(kb == pl.num_programs(2) - 1)
    def _():
        f = wout_ref.shape[0]
        step = min(512, cap)
        for r0 in range(0, cap, step):
            x = xe_ref[r0:r0 + step, :d].astype(BF16)
            gate = xe_ref[r0:r0 + step, d:d + HEAD_LANES] + xe_ref[r0:r0 + step, d + HEAD_LANES:]
            hgu = jnp.dot(x, win_ref[...], preferred_element_type=F32)
            act = (jax.nn.silu(hgu[:, :f]) * hgu[:, f:]).astype(BF16)
            y = jnp.dot(act, wout_ref[...], preferred_element_type=F32)
            scale = jnp.concatenate([gate] * (d // HEAD_LANES), axis=1) * mod_ref[5:6, :]
            ye_ref[r0:r0 + step, :] = (y * scale).astype(BF16)
        ye_ref[cap:, :] = jnp.zeros((ye_ref.shape[0] - cap, ye_ref.shape[1]), BF16)


def _combine_kernel(base_ref, h_ref, mod_ref, posn_ref, ye_ref, hn_ref, *, n_sub, final_norm):
    bi, kb, ei = pl.program_id(0), pl.program_id(1), pl.program_id(2)

    @pl.when(ei == 0)
    def _():
        hn_ref[...] = h_ref[...]

    slot = lax.broadcasted_iota(jnp.int32, (COMBINE_TOKENS, COMBINE_WINDOW), 1)
    lane_e = lax.broadcasted_iota(jnp.int32, (COMBINE_TOKENS, N_EXPERTS), 1)
    for j in range(n_sub):
        tok = slice(j * COMBINE_TOKENS, (j + 1) * COMBINE_TOKENS)
        base = pl.multiple_of(base_ref[bi, ei, kb * n_sub + j], 16)
        rel = jnp.sum(jnp.where(lane_e == ei, posn_ref[tok, :], 0), axis=1, keepdims=True) - base
        onehot = jnp.where(slot == rel, 1.0, 0.0).astype(BF16)
        ye = ye_ref[pl.ds(base, COMBINE_WINDOW), :]
        hn_ref[tok, :] += jnp.dot(onehot, ye, preferred_element_type=F32)

    if final_norm:
        @pl.when(ei == pl.num_programs(2) - 1)
        def _():
            hn = hn_ref[...]
            hn_ref[...] = hn * lax.rsqrt(jnp.mean(hn * hn, axis=-1, keepdims=True) + EPS) * mod_ref[6:7, :]


def _expert_choice_ffn(h, mod, u2, logits, w_exp_in, w_exp_out, final_norm=False):
    b, n, d = u2.shape
    cap = max(1, EC_CAPACITY * n // N_EXPERTS)
    n_sub, n_sub_c = min(8, n // SUB_TOKENS), min(16, n // COMBINE_TOKENS)
    big, big_c = n_sub * SUB_TOKENS, n_sub_c * COMBINE_TOKENS
    n_big, n_big_c = n // big, n // big_c
    capp = cap + GATHER_WINDOW
    pos, aff = _route(logits, cap)

    def window_starts(sub):
        cnt = jnp.sum((pos >= 0).reshape(b, N_EXPERTS, n // sub, sub), axis=-1)
        return ((jnp.cumsum(cnt, axis=-1) - cnt) // 16 * 16).astype(jnp.int32)
    base, base_c = window_starts(SUB_TOKENS), window_starts(COMBINE_TOKENS)
    f = w_exp_out.shape[1]
    ye = pl.pallas_call(
        functools.partial(_experts_kernel, n_sub=n_sub, cap=cap),
        grid_spec=pltpu.PrefetchScalarGridSpec(
            num_scalar_prefetch=1, grid=(b, N_EXPERTS, n_big),
            in_specs=[pl.BlockSpec((None, big, d), lambda bi, ei, kb, base_r: (bi, kb, 0)),
                      pl.BlockSpec((None, MOD_ROWS, d), lambda bi, ei, kb, base_r: (bi, 0, 0)),
                      pl.BlockSpec((None, None, 1, big), lambda bi, ei, kb, base_r: (bi, ei, 0, kb)),
                      pl.BlockSpec((None, None, 1, big), lambda bi, ei, kb, base_r: (bi, ei, 0, kb)),
                      pl.BlockSpec((None, d, 2 * f), lambda bi, ei, kb, base_r: (ei, 0, 0)),
                      pl.BlockSpec((None, f, d), lambda bi, ei, kb, base_r: (ei, 0, 0))],
            out_specs=pl.BlockSpec((None, None, capp, d), lambda bi, ei, kb, base_r: (bi, ei, 0, 0)),
            scratch_shapes=[pltpu.VMEM((capp, d + 2 * HEAD_LANES), F32)]),
        out_shape=jax.ShapeDtypeStruct((b, N_EXPERTS, capp, d), BF16),
        compiler_params=_params(3, 56 * 1024 * 1024), name="experts",
    )(base, u2, mod, aff.reshape(b, N_EXPERTS, 1, n), pos.reshape(b, N_EXPERTS, 1, n), w_exp_in, w_exp_out)
    posn = jnp.swapaxes(pos, 1, 2)
    return pl.pallas_call(
        functools.partial(_combine_kernel, n_sub=n_sub_c, final_norm=final_norm),
        grid_spec=pltpu.PrefetchScalarGridSpec(
            num_scalar_prefetch=1, grid=(b, n_big_c, N_EXPERTS),
            in_specs=[pl.BlockSpec((None, big_c, d), lambda bi, kb, ei, base_r: (bi, kb, 0)),
                      pl.BlockSpec((None, MOD_ROWS, d), lambda bi, kb, ei, base_r: (bi, 0, 0)),
                      pl.BlockSpec((None, big_c, N_EXPERTS), lambda bi, kb, ei, base_r: (bi, kb, 0)),
                      pl.BlockSpec((None, None, capp, d), lambda bi, kb, ei, base_r: (bi, ei, 0, 0))],
            out_specs=pl.BlockSpec((None, big_c, d), lambda bi, kb, ei, base_r: (bi, kb, 0))),
        out_shape=jax.ShapeDtypeStruct((b, n, d), F32),
        compiler_params=_params(3, 56 * 1024 * 1024), name="combine",
    )(base_c, h, mod, posn, ye)


def _split(z, sizes):
    out, start = [], 0
    for s in sizes:
        out.append(z[..., start:start + s])
        start += s
    return out


HALO = 16


def _fill_ext(ext_ref, x_ref, prev_ref, next_ref):
    ti, nt = pl.program_id(1), pl.num_programs(1)
    tt = x_ref.shape[0]
    ext_ref[0:HALO, :] = jnp.where(ti > 0, prev_ref[...], 0.0)
    ext_ref[HALO:HALO + tt, :] = x_ref[...]
    ext_ref[HALO + tt:, :] = jnp.where(ti < nt - 1, next_ref[...], 0.0)


def _taps(ext_ref, w_ref, tt):
    k = w_ref.shape[0]
    acc = None
    for j in range(k):
        start = HALO - k // 2 + j
        term = w_ref[j:j + 1, :] * ext_ref[start:start + tt, :]
        acc = term if acc is None else acc + term
    return acc


def _short_conv_kernel(x_ref, prev_ref, next_ref, w_ref, b_ref, x1_ref, x2_ref, v_ref, ext_ref):
    _fill_ext(ext_ref, x_ref, prev_ref, next_ref)
    y = _taps(ext_ref, w_ref, x_ref.shape[0]) + b_ref[...]
    x1_ref[...] = y[:, :HYENA_WIDTH]
    x2_ref[...] = y[:, HYENA_WIDTH:2 * HYENA_WIDTH]
    v_ref[...] = y[:, 2 * HYENA_WIDTH:]


def _conformer_kernel(x_ref, prev_ref, next_ref, w_ref, g_ref, b_ref, o_ref, ext_ref):
    _fill_ext(ext_ref, x_ref, prev_ref, next_ref)
    u = _taps(ext_ref, w_ref, x_ref.shape[0])
    mu = jnp.mean(u, axis=-1, keepdims=True)
    var = jnp.mean(jnp.square(u - mu), axis=-1, keepdims=True)
    y = (u - mu) * lax.rsqrt(var + EPS) * g_ref[...] + b_ref[...]
    o_ref[...] = y * jax.nn.sigmoid(y)


def _token_conv(body, x, consts, out_widths, name):
    b, n, w = x.shape
    tt = min(1024, n)
    per = tt // HALO
    last = n // HALO - 1
    in_specs = [pl.BlockSpec((None, tt, w), lambda bi, ti: (bi, ti, 0)),
                pl.BlockSpec((None, HALO, w), lambda bi, ti: (bi, jnp.maximum(ti * per - 1, 0), 0)),
                pl.BlockSpec((None, HALO, w), lambda bi, ti: (bi, jnp.minimum((ti + 1) * per, last), 0))]
    in_specs += [pl.BlockSpec(cst.shape, lambda bi, ti: (0, 0)) for cst in consts]
    outs = tuple(jax.ShapeDtypeStruct((b, n, ow), F32) for ow in out_widths)
    out_specs = tuple(pl.BlockSpec((None, tt, ow), lambda bi, ti: (bi, ti, 0)) for ow in out_widths)
    return pl.pallas_call(body, grid=(b, n // tt), in_specs=in_specs, out_specs=out_specs, out_shape=outs,
                          scratch_shapes=[pltpu.VMEM((tt + 2 * HALO, w), F32)],
                          compiler_params=_params(2), name=name)(x, x, x, *consts)


def _conformer_branch(glu, p):
    return _token_conv(_conformer_kernel, glu, (p['conf_dw_w'], p['conf_ln_g'][None, :], p['conf_ln_b'][None, :]),
                       (CONF_WIDTH,), "conformer")[0]


FILT_LANES = 128
DFT_SHORT = 256


def _split_bf16(x):
    hi = x.astype(BF16)
    return hi, (x - hi.astype(F32)).astype(BF16)


def _dot_split(ah, al, bh, bl):
    dot = lambda u, v: jnp.dot(u, v, preferred_element_type=F32)
    return dot(ah, bh) + (dot(al, bh) + dot(ah, bl))


def _dot_const(mh, ml, x):
    xb = x.astype(BF16)
    return jnp.dot(mh, xb, preferred_element_type=F32) + jnp.dot(ml, xb, preferred_element_type=F32)


def _filter_kernel(z_ref, w1h, w1l, b1, f1, w2h, w2l, b2, f2, w3h, w3l, dl_ref, h_ref, asum_ref, *,
                   tiles_per_dir):
    z = z_ref[...]
    hid = jnp.sin(f1[...] * (_dot_split(*_split_bf16(z), w1h[...], w1l[...]) + b1[...]))
    hid = jnp.sin(f2[...] * (_dot_split(*_split_bf16(hid), w2h[...], w2l[...]) + b2[...]))
    h = _dot_split(*_split_bf16(hid), w3h[...], w3l[...])
    h = h * jnp.exp(-z[:, 0:1] * dl_ref[...])
    h_ref[...] = h

    @pl.when(pl.program_id(0) % tiles_per_dir == 0)
    def _():
        asum_ref[...] = jnp.zeros(asum_ref.shape, F32)
    asum_ref[...] += jnp.sum(jnp.abs(h), axis=0, keepdims=True)


def _normalise_kernel(h_ref, asum_ref, *o_refs, n):
    tt = h_ref.shape[0]
    row = pl.program_id(0) * tt + lax.broadcasted_iota(jnp.int32, h_ref.shape, 0)
    k = jnp.where(row == n, 0.0, h_ref[...] / asum_ref[...])
    for o, o_ref in enumerate(o_refs):
        o_ref[...] = k[:, o * HYENA_WIDTH:(o + 1) * HYENA_WIDTH]


def _hyena_taps(n, p):
    t = jnp.linspace(0.0, 1.0, n, dtype=F32)[:, None]
    bands = (FILT_EMB - 1) // 2
    w = (2.0 * math.pi / n) * jnp.arange(n, dtype=F32)[:, None]
    f = jnp.linspace(1e-4, bands - 1, bands, dtype=F32)[None, :]
    t2, w2pos = jnp.concatenate([t, t[::-1]], axis=0), jnp.concatenate([w, w[::-1]], axis=0)
    z2 = jnp.concatenate([t2, jnp.cos(f * w2pos), -jnp.sin(f * w2pos),
                          jnp.zeros((2 * n, FILT_LANES - FILT_EMB), F32)], axis=-1)
    padc = lambda a: jnp.pad(a, ((0, 0), (0, FILT_LANES - a.shape[1])))
    padr = lambda a: jnp.pad(a, ((0, FILT_LANES - a.shape[0]), (0, 0)))
    w1, w2, w3 = padc(padr(p['filt_w1'])), padc(padr(p['filt_w2'])), padr(p['filt_w3'])
    b1, b2 = padc(p['filt_b1'][None, :]), padc(p['filt_b2'][None, :])
    f1, f2 = padc(p['filt_freq'][0][None, :]), padc(p['filt_freq'][1][None, :])
    deltas = jnp.abs(jnp.linspace(math.log(DECAY_TARGET) / SLOW_DECAY, math.log(DECAY_TARGET) / FAST_DECAY,
                                  HYENA_WIDTH, dtype=F32))
    width = HYENA_ORDER * HYENA_WIDTH
    w3 = w3.reshape(FILT_LANES, HYENA_ORDER, 2, HYENA_WIDTH).transpose(2, 0, 1, 3).reshape(2, FILT_LANES, width)
    dl = jnp.tile(deltas, HYENA_ORDER)[None, :]
    w3h, w3l = _split_bf16(w3)
    tt = min(1024, n)
    tiles_per_dir = n // tt
    cspec = lambda a: pl.BlockSpec(a.shape, lambda i: (0, 0))
    dirspec = lambda rows: pl.BlockSpec((None, rows, width), lambda i: (i // tiles_per_dir, 0, 0))
    tile = lambda w: pl.BlockSpec((tt, w), lambda i: (i, 0))
    small = [*_split_bf16(w1), b1, f1, *_split_bf16(w2), b2, f2]
    h_raw, asum = pl.pallas_call(
        functools.partial(_filter_kernel, tiles_per_dir=tiles_per_dir), grid=(2 * tiles_per_dir,),
        in_specs=[tile(FILT_LANES)] + [cspec(a) for a in small] + [dirspec(FILT_LANES), dirspec(FILT_LANES), cspec(dl)],
        out_specs=(tile(width), dirspec(1)),
        out_shape=(jax.ShapeDtypeStruct((2 * n, width), F32), jax.ShapeDtypeStruct((2, 1, width), F32)),
        compiler_params=_params(1), name="hyena_filter_mlp")(z2, *small, w3h, w3l, dl)
    return pl.pallas_call(
        functools.partial(_normalise_kernel, n=n), grid=(2 * tiles_per_dir,),
        in_specs=[tile(width), dirspec(1)],
        out_specs=tuple(tile(HYENA_WIDTH) for _ in range(HYENA_ORDER)),
        out_shape=tuple(jax.ShapeDtypeStruct((2 * n, HYENA_WIDTH), F32) for _ in range(HYENA_ORDER)),
        compiler_params=_params(1), name="hyena_filter_norm",
    )(h_raw, asum)


def _dft_tables(n):
    n2 = DFT_SHORT if n >= 4 * DFT_SHORT else n
    n1 = n // n2

    def cis(idx):
        ang = (-2.0 * math.pi / n) * idx.astype(F32)
        return jnp.cos(ang), jnp.sin(ang)
    k2 = jnp.arange(n2)
    fr, fi = cis((k2[:, None] * k2[None, :]) % n2 * n1)
    tabs = dict(n1=n1, n2=n2)
    tabs['f_hi'], tabs['f_lo'] = _split_bf16(jnp.stack([fr, fi]))
    k1 = jnp.arange(n1)
    tr, ti = cis(k1[:, None] * k2[None, :])
    tabs['tw'] = jnp.broadcast_to(jnp.stack([tr, ti], axis=1)[..., None], (n1, 2, n2, HEAD_LANES))
    if n1 > 1:
        gr, gi = cis((k1[:, None] * k1[None, :]) % n1 * n2)
        half = n1 // 2
        grh, gih = gr[:, :half], gi[:, :half]
        tabs['m_fwd'] = _split_bf16(jnp.block([[grh, -gih], [gih, grh]]))
        tabs['m_real'] = _split_bf16(jnp.concatenate([gr, gi], axis=0))
        tabs['m_inv'] = _split_bf16(jnp.block([[grh.T, gih.T], [-gih.T, grh.T]]))
    return tabs


SLABS_PER_STEP = 8


def _rowmix_slabs_kernel(mh_ref, ml_ref, x_ref, o_ref):
    for j in range(x_ref.shape[1]):
        o_ref[:, j, :] = _dot_const(mh_ref[...], ml_ref[...], x_ref[:, j, :])


def _rowmix_slabs(m, x):
    mh, ml = m
    rin, n2, c = x.shape
    nb = min(SLABS_PER_STEP, n2)
    return pl.pallas_call(
        _rowmix_slabs_kernel, grid=(n2 // nb,),
        in_specs=[pl.BlockSpec(mh.shape, lambda i: (0, 0)), pl.BlockSpec(ml.shape, lambda i: (0, 0)),
                  pl.BlockSpec((rin, nb, c), lambda i: (0, i, 0))],
        out_specs=pl.BlockSpec((mh.shape[0], nb, c), lambda i: (0, i, 0)),
        out_shape=jax.ShapeDtypeStruct((mh.shape[0], n2, c), F32), compiler_params=_params(1),
        name="dft_rowmix_slabs",
    )(mh, ml, x)


def _spectral_kernel(x_ref, tw_ref, fh_ref, fl_ref, k_ref, o_ref, *, conv):
    xr, xi = x_ref[0], x_ref[1]
    reps = xr.shape[1] // HEAD_LANES
    tr = jnp.concatenate([tw_ref[0]] * reps, axis=1)
    ti = jnp.concatenate([tw_ref[1]] * reps, axis=1)
    frh, fih, frl, fil = fh_ref[0], fh_ref[1], fl_ref[0], fl_ref[1]

    def dft(ar, ai, conj):
        rr, ii = _dot_const(frh, frl, ar), _dot_const(fih, fil, ai)
        ri, ir = _dot_const(frh, frl, ai), _dot_const(fih, fil, ar)
        return (rr + ii, ri - ir) if conj else (rr - ii, ri + ir)

    yr, yi = dft(xr * tr - xi * ti, xr * ti + xi * tr, False)
    if not conv:
        o_ref[0] = yr * k_ref[...]
        o_ref[1] = yi * k_ref[...]
        return
    kr, ki = k_ref[0], k_ref[1]
    cr, ci = dft(yr * kr - yi * ki, yr * ki + yi * kr, True)
    o_ref[0] = cr * tr + ci * ti
    o_ref[1] = ci * tr - cr * ti


def _spectral(x, k, tabs, conv):
    _, n1, n2, c = x.shape
    slab = pl.BlockSpec((2, None, n2, c), lambda i: (0, i, 0, 0))
    kspec = slab if conv else pl.BlockSpec(k.shape, lambda i: (0, 0))
    return pl.pallas_call(
        functools.partial(_spectral_kernel, conv=conv), grid=(n1,),
        in_specs=[slab, pl.BlockSpec((None, 2, n2, HEAD_LANES), lambda i: (i, 0, 0, 0)),
                  pl.BlockSpec(tabs['f_hi'].shape, lambda i: (0, 0, 0)),
                  pl.BlockSpec(tabs['f_lo'].shape, lambda i: (0, 0, 0)), kspec],
        out_specs=slab, out_shape=jax.ShapeDtypeStruct(x.shape, F32), compiler_params=_params(1),
        name="dft_spectral_conv" if conv else "dft_spectral_filter",
    )(x, tabs['tw'], tabs['f_hi'], tabs['f_lo'], k)


def _filter_spectrum(k, tabs):
    n, c = k.shape
    n1, n2 = tabs['n1'], tabs['n2']
    if n1 > 1:
        x = _rowmix_slabs(tabs['m_real'], k.reshape(n1, n2, c)).reshape(2, n1, n2, c)
    else:
        x = jnp.stack([k, jnp.zeros_like(k)]).reshape(2, 1, n2, c)
    return _spectral(x, jnp.full((1, c), 1.0 / n, F32), tabs, conv=False)


def _rowmix_gate_kernel(mh_ref, ml_ref, x_ref, g_ref, v_ref, s_ref, o_ref):
    o_ref[...] = g_ref[...] * (_dot_const(mh_ref[...], ml_ref[...], x_ref[...]) + s_ref[...] * v_ref[...])


def _gated_long_conv(gate, v, kf, skip, tabs):
    b, n, c = v.shape
    assert b == 2
    n1, n2 = tabs['n1'], tabs['n2']
    if n1 == 1:
        x = jnp.concatenate([v, jnp.zeros_like(v)], axis=1).reshape(2, 1, n2, c)
        y = _spectral(x, kf, tabs, conv=True).reshape(2, n2, c)[:, :n]
        return _hyena_gate(gate, y, v, skip)
    x = _rowmix_slabs(tabs['m_fwd'], v.reshape(n1, n2, c)).reshape(2, n1, n2, c)
    y = _spectral(x, kf, tabs, conv=True).reshape(2 * n1, n2 * c)
    mh, ml = tabs['m_inv']
    cols = n2 * c
    ct = min(2048, cols)
    tile = lambda rows: pl.BlockSpec((rows, ct), lambda i: (0, i))
    return pl.pallas_call(
        _rowmix_gate_kernel, grid=(cols // ct,),
        in_specs=[pl.BlockSpec(mh.shape, lambda i: (0, 0)), pl.BlockSpec(ml.shape, lambda i: (0, 0)),
                  tile(2 * n1), tile(n1), tile(n1), pl.BlockSpec((1, ct), lambda i: (0, 0))],
        out_specs=tile(n1), out_shape=jax.ShapeDtypeStruct((n1, cols), F32), compiler_params=_params(1),
        name="dft_rowmix_gate",
    )(mh, ml, y, gate.reshape(n1, cols), v.reshape(n1, cols), jnp.tile(skip, (1, ct // c))).reshape(2, n, c)


def _gate_kernel(g_ref, y_ref, v_ref, s_ref, o_ref):
    o_ref[...] = g_ref[...] * (y_ref[...] + s_ref[...] * v_ref[...])


def _hyena_gate(gate, y, v, skip):
    b, n, c = v.shape
    tt = min(2048, n)
    tok = pl.BlockSpec((None, tt, c), lambda bi, ti: (bi, ti, 0))
    return pl.pallas_call(_gate_kernel, grid=(b, n // tt),
                          in_specs=[tok, tok, tok, pl.BlockSpec((1, c), lambda bi, ti: (0, 0))], out_specs=tok,
                          out_shape=jax.ShapeDtypeStruct(v.shape, F32), compiler_params=_params(2),
                          name="hyena_gate")(gate, y, v, skip)


def _hyena_branch(hy, p, tabs):
    n = hy.shape[1]
    x1, x2, v = _token_conv(_short_conv_kernel, hy, (p['hyena_short_w'], p['hyena_short_b'][None, :]),
                            (HYENA_WIDTH,) * 3, "hyena_short_conv")
    taps = _hyena_taps(n, p)
    for o, gate in enumerate((x1, x2)):
        v = _gated_long_conv(gate, v, _filter_spectrum(taps[o], tabs), p['hyena_skip'][o][None, :], tabs)
    return v


def _adaln_kernel(c_ref, w_ref, b_ref, o_ref):
    s = c_ref[...]
    s = s * jax.nn.sigmoid(s)
    o_ref[...] = _dot_split(*_split_bf16(s), *_split_bf16(w_ref[...])) + b_ref[...]


def _adaln(cond, w, b):
    d, width = w.shape
    ct = width // 6
    return pl.pallas_call(
        _adaln_kernel, grid=(6,),
        in_specs=[pl.BlockSpec(cond.shape, lambda i: (0, 0)), pl.BlockSpec((d, ct), lambda i: (0, i)),
                  pl.BlockSpec((1, ct), lambda i: (0, i))],
        out_specs=pl.BlockSpec((cond.shape[0], ct), lambda i: (0, i)),
        out_shape=jax.ShapeDtypeStruct((cond.shape[0], width), F32), compiler_params=_params(1), name="adaln",
    )(cond, w, b[None, :])


def _mod_rows(mod, norm_mix_g, norm_ffn_g, final_g, batch):
    sh1, sc1, g1, sh2, sc2, g2 = jnp.split(mod, 6, axis=-1)
    rows = jnp.stack([norm_mix_g * (1.0 + sc1), sh1, g1, norm_ffn_g * (1.0 + sc2), sh2, g2,
                      jnp.broadcast_to(final_g, g1.shape), jnp.zeros_like(g1)], axis=1)
    return jnp.broadcast_to(rows, (batch,) + rows.shape[1:])


def kernel(x, c, ctx, c_ctx, ada_w, ada_b, norm_mix_g, norm_ffn_g, w_in, diff_lambda, diff_subln_g, hyena_short_w, hyena_short_b, filt_w1, filt_b1, filt_freq, filt_w2, filt_b2, filt_w3, hyena_skip, conf_dw_w, conf_ln_g, conf_ln_b, mla_q_norm_g, mla_kv_norm_g, mla_w_uq, mla_w_ukv, w_branch, w_out, w_router, w_exp_in, w_exp_out, final_norm_g):
    depth = w_in.shape[0]
    batch, n_lat, d = x.shape
    n_ctx = ctx.shape[1]
    rope_lat = _rope_operands(n_lat, identity=False)
    rope_ctx = _rope_operands(n_ctx, identity=True)
    dft_lat, dft_ctx = _dft_tables(2 * n_lat), _dft_tables(2 * n_ctx)
    cond = jnp.concatenate([c, c_ctx[None], jnp.zeros((MOD_ROWS - batch - 1, d), F32)], axis=0)
    tile_lat, tile_ctx = min(512, n_lat), min(256, n_ctx)
    h_lat, h_ctx = x, ctx
    for l in range(depth):
        last = l == depth - 1
        p = dict(w_in=w_in[l], diff_subln_g=diff_subln_g[l], hyena_short_w=hyena_short_w[l],
                 hyena_short_b=hyena_short_b[l], filt_w1=filt_w1[l], filt_b1=filt_b1[l], filt_freq=filt_freq[l],
                 filt_w2=filt_w2[l], filt_b2=filt_b2[l], filt_w3=filt_w3[l], hyena_skip=hyena_skip[l],
                 conf_dw_w=conf_dw_w[l], conf_ln_g=conf_ln_g[l], conf_ln_b=conf_ln_b[l],
                 mla_q_norm_g=mla_q_norm_g[l], mla_kv_norm_g=mla_kv_norm_g[l], mla_w_uq=mla_w_uq[l],
                 mla_w_ukv=mla_w_ukv[l], w_branch=w_branch[l], w_out=w_out[l], w_router=w_router[l],
                 w_exp_in=w_exp_in[l], w_exp_out=w_exp_out[l])
        ada = _adaln(cond, ada_w[l], ada_b[l])
        mod_lat = _mod_rows(ada[:batch], norm_mix_g[l], norm_ffn_g[l], final_norm_g, batch)
        mod_ctx = _mod_rows(ada[batch:batch + 1], norm_mix_g[l], norm_ffn_g[l], final_norm_g, batch)
        lam_init = 0.8 - 0.6 * math.exp(-0.3 * l)
        lq1, lk1, lq2, lk2 = diff_lambda[l].astype(F32)
        lam = jnp.reshape(jnp.exp(jnp.sum(lq1 * lk1)) - jnp.exp(jnp.sum(lq2 * lk2)) + lam_init, (1,))
        w_inp, w_mrg = _inproj_weights(p), _merge_weights(p, lam_init)

        qdT_l, kd_l, vdT_l, qmT_l, km_l, vmT_l, hy_l, glu_l = _inproj(h_lat, mod_lat, w_inp, rope_lat, tile=tile_lat)
        qdT_c, kd_c, vdT_c, qmT_c, km_c, vmT_c, hy_c, glu_c = _inproj(h_ctx, mod_ctx, w_inp, rope_ctx, tile=tile_ctx)
        a_lat = _flash_attention(lam, qdT_l, kd_c, vdT_c, kd_l, vdT_l, n_maps=2, tq=min(512, n_lat))
        m_lat = _flash_attention(lam, qmT_l, km_c, vmT_c, km_l, vmT_l, n_maps=1, tq=min(1024, n_lat))
        h_lat, u2_lat, lg_lat = _merge(h_lat, mod_lat, a_lat, _hyena_branch(hy_l, p, dft_lat), _conformer_branch(glu_l, p),
                                       m_lat, w_mrg, tile=min(256, n_lat))
        w_ei, w_eo = p['w_exp_in'].astype(BF16), p['w_exp_out'].astype(BF16)
        h_lat = _expert_choice_ffn(h_lat, mod_lat, u2_lat, lg_lat, w_ei, w_eo, final_norm=last)
        if not last:
            a_ctx = _flash_attention(lam, qdT_c, kd_c, vdT_c, None, None, n_maps=2, tq=n_ctx)
            m_ctx = _flash_attention(lam, qmT_c, km_c, vmT_c, None, None, n_maps=1, tq=n_ctx)
            h_ctx, u2_ctx, lg_ctx = _merge(h_ctx, mod_ctx, a_ctx, _hyena_branch(hy_c, p, dft_ctx),
                                           _conformer_branch(glu_c, p), m_ctx, w_mrg, tile=tile_ctx)
            h_ctx = _expert_choice_ffn(h_ctx, mod_ctx, u2_ctx, lg_ctx, w_ei, w_eo)
    return h_lat
```

```python
import functools
import math

import jax
import jax.numpy as jnp
from jax import lax
from jax.experimental import pallas as pl
from jax.experimental.pallas import tpu as pltpu

GRID_W = 64
ROPE_BASE = 10000.0
EPS = 1e-6

DIFF_HEADS = 4
DIFF_HEAD_DIM = 64
DIFF_V_DIM = 2 * DIFF_HEAD_DIM
HYENA_WIDTH = 256
HYENA_ORDER = 2
FILT_EMB = 33
DECAY_TARGET = 1e-2
FAST_DECAY = 0.3
SLOW_DECAY = 1.5
CONF_WIDTH = 256
MLA_HEADS = 4
MLA_Q_RANK = 256
MLA_KV_RANK = 128
MLA_NOPE = 64
MLA_ROPE = 32
MLA_V = 64
MLA_SCALE = (MLA_NOPE + MLA_ROPE) ** -0.5
N_EXPERTS = 16
EC_CAPACITY = 2

DIFF_QK_W = DIFF_HEADS * 2 * DIFF_HEAD_DIM
DIFF_V_W = DIFF_HEADS * DIFF_V_DIM
HYENA_PROJ = (HYENA_ORDER + 1) * HYENA_WIDTH
CONF_PROJ = 2 * CONF_WIDTH
IN_SPLITS = (DIFF_QK_W, DIFF_QK_W, DIFF_V_W, HYENA_PROJ, CONF_PROJ, MLA_Q_RANK, MLA_KV_RANK, MLA_ROPE)
BRANCH_WIDTHS = (DIFF_V_W, HYENA_WIDTH, CONF_WIDTH, MLA_HEADS * MLA_V)

HEAD_LANES = 128
DIFF_V_PAD = 16
ATT_W = DIFF_HEADS * HEAD_LANES
LOG2E = 1.4426950408889634
V7X_VMEM_BYTES = 64 * 1024 * 1024
VMEM_LIMIT_BYTES = V7X_VMEM_BYTES * 3 // 4
VMEM_LIMIT_LARGE_BYTES = V7X_VMEM_BYTES * 7 // 8
MOD_ROWS = 8

F32 = jnp.float32
BF16 = jnp.bfloat16
_NT = (((1,), (1,)), ((), ()))


def _params(n_axes, vmem=VMEM_LIMIT_BYTES):
    return pltpu.CompilerParams(dimension_semantics=("arbitrary",) * n_axes, vmem_limit_bytes=vmem)


def _stack_queries(q2_ref, qT, n_maps):
    tq = qT.shape[1]
    if n_maps == 2:
        row = lax.broadcasted_iota(jnp.int32, qT.shape, 0)
        zero = jnp.zeros_like(qT)
        q2_ref[:, :tq] = jnp.where(row < DIFF_HEAD_DIM, qT, zero)
        q2_ref[:, tq:] = jnp.where(row >= DIFF_HEAD_DIM, qT, zero)
    else:
        q2_ref[...] = qT


def _reset(m_ref, acc_ref):
    m_ref[...] = jnp.full(m_ref.shape, -jnp.inf, F32)
    acc_ref[...] = jnp.zeros(acc_ref.shape, F32)


def _absorb(s, vT, m_ref, acc_ref):
    m_prev = m_ref[...]
    m_new = jnp.maximum(m_prev, jnp.max(s, axis=0, keepdims=True))
    alpha = jnp.exp2(m_prev - m_new)
    p = jnp.exp2(s - m_new).astype(BF16)
    acc_ref[...] = alpha * acc_ref[...] + jnp.dot(vT, p, preferred_element_type=F32)
    m_ref[...] = m_new


def _attention_rows(acc_ref, lam_ref, n_maps, tq, sum_row):
    o = acc_ref[0:HEAD_LANES, :] / acc_ref[sum_row:sum_row + 1, :]
    if n_maps == 2:
        o = o[:, :tq] - lam_ref[0] * o[:, tq:]
        o = o * lax.rsqrt(jnp.mean(o * o, axis=0, keepdims=True) + EPS)
    return o.T.astype(BF16)


def _flash_ctx_kernel(lam_ref, qT_ref, kc_ref, vcT_ref, o_ref, acc_ref, m_ref, q2_ref, *, n_maps, sum_row):
    _stack_queries(q2_ref, qT_ref[...], n_maps)
    _reset(m_ref, acc_ref)
    _absorb(jnp.dot(kc_ref[...], q2_ref[...], preferred_element_type=F32), vcT_ref[...], m_ref, acc_ref)
    o_ref[...] = _attention_rows(acc_ref, lam_ref, n_maps, qT_ref.shape[1], sum_row)


def _flash_stream_kernel(lam_ref, qT_ref, kc_ref, vcT_ref, kl_ref, vlT_ref, o_ref, acc_ref, m_ref, q2_ref, sc_ref,
                         s_ref, *, n_maps, n_lat_chunks, tk, tq, sum_row):
    n_q = qT_ref.shape[1] // tq

    def load_queries(qi):
        _stack_queries(q2_ref, qT_ref[:, pl.ds(pl.multiple_of(qi * tq, tq), tq)], n_maps)

    def scores(k):
        return jnp.dot(k, q2_ref[...], preferred_element_type=F32)

    def absorb(s, vT):
        _absorb(s, vT, m_ref, acc_ref)

    load_queries(0)
    sc_ref[...] = scores(kc_ref[...])

    def query_block(qi, carry):
        _reset(m_ref, acc_ref)

        def chunk(c):
            return pl.ds(c * tk if isinstance(c, int) else pl.multiple_of(c * tk, tk), tk)

        s_ref[0] = scores(kl_ref[chunk(0), :])
        absorb(sc_ref[...], vcT_ref[...])

        def pair(j, inner):
            c = 2 * j
            s_ref[1] = scores(kl_ref[chunk(c + 1), :])
            absorb(s_ref[0], vlT_ref[:, chunk(c)])
            s_ref[0] = scores(kl_ref[chunk(c + 2), :])
            absorb(s_ref[1], vlT_ref[:, chunk(c + 1)])
            return inner
        lax.fori_loop(0, n_lat_chunks // 2 - 1, pair, 0)
        s_ref[1] = scores(kl_ref[chunk(n_lat_chunks - 1), :])
        absorb(s_ref[0], vlT_ref[:, chunk(n_lat_chunks - 2)])
        load_queries(jnp.minimum(qi + 1, n_q - 1))
        sc_ref[...] = scores(kc_ref[...])
        absorb(s_ref[1], vlT_ref[:, chunk(n_lat_chunks - 1)])
        o_ref[pl.ds(pl.multiple_of(qi * tq, tq), tq), :] = _attention_rows(acc_ref, lam_ref, n_maps, tq, sum_row)
        return carry
    lax.fori_loop(0, n_q, query_block, 0)


def _flash_attention(lam, qT, kc, vcT, kl, vlT, *, n_maps, tq):
    b, _, s = qT.shape
    lc = kc.shape[1]
    mv = vcT.shape[2]
    sum_row = HEAD_LANES if n_maps == 2 else MLA_V
    r = n_maps * tq
    if kl is not None:
        sl = kl.shape[1]
        tk = _lat_chunk(sl)
        n_lat_chunks = sl // tk
        assert n_lat_chunks % 2 == 0 and n_lat_chunks * tk == sl and s % tq == 0
        return pl.pallas_call(
            functools.partial(_flash_stream_kernel, n_maps=n_maps, n_lat_chunks=n_lat_chunks, tk=tk, tq=tq,
                              sum_row=sum_row),
            grid=(b, DIFF_HEADS),
            in_specs=[pl.BlockSpec(memory_space=pltpu.SMEM),
                      pl.BlockSpec((None, HEAD_LANES, s), lambda bi, hi: (bi, hi, 0)),
                      pl.BlockSpec((None, lc, HEAD_LANES), lambda bi, hi: (bi, 0, hi)),
                      pl.BlockSpec((None, None, mv, lc), lambda bi, hi: (bi, hi, 0, 0)),
                      pl.BlockSpec((None, sl, HEAD_LANES), lambda bi, hi: (bi, 0, hi)),
                      pl.BlockSpec((None, None, mv, sl), lambda bi, hi: (bi, hi, 0, 0))],
            out_specs=pl.BlockSpec((None, s, HEAD_LANES), lambda bi, hi: (bi, 0, hi)),
            out_shape=jax.ShapeDtypeStruct((b, s, ATT_W), BF16),
            scratch_shapes=[pltpu.VMEM((mv, r), F32), pltpu.VMEM((1, r), F32), pltpu.VMEM((HEAD_LANES, r), BF16),
                            pltpu.VMEM((lc, r), F32), pltpu.VMEM((2, tk, r), F32)],
            compiler_params=_params(2, VMEM_LIMIT_LARGE_BYTES),
            name=f"flash_attention_{n_maps}map",
        )(lam, qT, kc, vcT, kl, vlT)
    assert s == tq
    return pl.pallas_call(
        functools.partial(_flash_ctx_kernel, n_maps=n_maps, sum_row=sum_row),
        grid=(b, DIFF_HEADS),
        in_specs=[pl.BlockSpec(memory_space=pltpu.SMEM),
                  pl.BlockSpec((None, HEAD_LANES, s), lambda bi, hi: (bi, hi, 0)),
                  pl.BlockSpec((None, lc, HEAD_LANES), lambda bi, hi: (bi, 0, hi)),
                  pl.BlockSpec((None, None, mv, lc), lambda bi, hi: (bi, hi, 0, 0))],
        out_specs=pl.BlockSpec((None, s, HEAD_LANES), lambda bi, hi: (bi, 0, hi)),
        out_shape=jax.ShapeDtypeStruct((b, s, ATT_W), BF16),
        scratch_shapes=[pltpu.VMEM((mv, r), F32), pltpu.VMEM((1, r), F32), pltpu.VMEM((HEAD_LANES, r), BF16)],
        compiler_params=_params(2),
        name=f"flash_attention_ctx_{n_maps}map",
    )(lam, qT, kc, vcT)


def _lat_chunk(s):
    return min(1024, s // 2)


W_NAT_SPLITS = (DIFF_QK_W, HYENA_PROJ, CONF_PROJ, MLA_Q_RANK, MLA_KV_RANK, HEAD_LANES)


def _modulated_norm(h, a, shift):
    return h * lax.rsqrt(jnp.mean(h * h, axis=-1, keepdims=True) + EPS) * a + shift


def _rope_lanes(x, tab_ref, shift):
    return (x * tab_ref[0] + pltpu.roll(x, shift, 1) * tab_ref[1]
            + pltpu.roll(x, HEAD_LANES - shift, 1) * tab_ref[2])


def _inproj_kernel(h_ref, mod_ref, wnat_ref, wT_ref, wuqT_ref, wukvk_ref, wuvT_ref, gq_ref, gkv_ref,
                   ropeT_d_ref, rope_kd_ref, ropeT_m_ref, rope_km_ref,
                   qdT_ref, kd_ref, vdT_ref, qmT_ref, km_ref, vmT_ref, hy_ref, glu_ref):
    u = _modulated_norm(h_ref[...], mod_ref[0:1, :], mod_ref[1:2, :]).astype(BF16)
    z = jnp.dot(u, wnat_ref[...], preferred_element_type=F32)
    zT = lax.dot_general(wT_ref[...], u, _NT, preferred_element_type=F32)
    offs = [0]
    for w in W_NAT_SPLITS:
        offs.append(offs[-1] + w)
    dk, hy, cf, cq, ckv, krp = (z[:, offs[i]:offs[i + 1]] for i in range(len(W_NAT_SPLITS)))

    for hd in range(DIFF_HEADS):
        sl = slice(hd * HEAD_LANES, (hd + 1) * HEAD_LANES)
        kd_ref[:, sl] = _rope_lanes(dk[:, sl], rope_kd_ref, DIFF_HEAD_DIM // 2).astype(BF16)
    cos_d, sin_d = ropeT_d_ref[0], ropeT_d_ref[1]
    half = DIFF_HEAD_DIM // 2
    for g in range(2 * DIFF_HEADS):
        x1 = zT[g * DIFF_HEAD_DIM:g * DIFF_HEAD_DIM + half]
        x2 = zT[g * DIFF_HEAD_DIM + half:(g + 1) * DIFF_HEAD_DIM]
        qdT_ref[g * DIFF_HEAD_DIM:g * DIFF_HEAD_DIM + half, :] = (x1 * cos_d - x2 * sin_d).astype(BF16)
        qdT_ref[g * DIFF_HEAD_DIM + half:(g + 1) * DIFF_HEAD_DIM, :] = (x1 * sin_d + x2 * cos_d).astype(BF16)
    tail = jnp.where(lax.broadcasted_iota(jnp.int32, (DIFF_V_PAD, zT.shape[1]), 0) == 0, 1.0, 0.0).astype(BF16)
    for hd in range(DIFF_HEADS):
        r0 = DIFF_QK_W + hd * DIFF_V_DIM
        vdT_ref[hd, 0:DIFF_V_DIM, :] = zT[r0:r0 + DIFF_V_DIM].astype(BF16)
        vdT_ref[hd, DIFF_V_DIM:, :] = tail

    hy_ref[...] = hy
    glu_ref[...] = cf[:, :CONF_WIDTH] * jax.nn.sigmoid(cf[:, CONF_WIDTH:])

    cqn = (cq * lax.rsqrt(jnp.mean(cq * cq, axis=-1, keepdims=True) + EPS) * gq_ref[...]).astype(BF16)
    ckvn = (ckv * lax.rsqrt(jnp.mean(ckv * ckv, axis=-1, keepdims=True) + EPS) * gkv_ref[...]).astype(BF16)
    qT = lax.dot_general(wuqT_ref[...], cqn, _NT, preferred_element_type=F32)
    cos_m, sin_m = ropeT_m_ref[0], ropeT_m_ref[1]
    hr = MLA_ROPE // 2
    for hd in range(MLA_HEADS):
        base = hd * HEAD_LANES
        r1 = base + MLA_NOPE
        x1, x2 = qT[r1:r1 + hr], qT[r1 + hr:r1 + 2 * hr]
        qmT_ref[base:r1, :] = qT[base:r1].astype(BF16)
        qmT_ref[r1:r1 + hr, :] = (x1 * cos_m - x2 * sin_m).astype(BF16)
        qmT_ref[r1 + hr:r1 + 2 * hr, :] = (x1 * sin_m + x2 * cos_m).astype(BF16)
        qmT_ref[r1 + 2 * hr:base + HEAD_LANES, :] = jnp.zeros((HEAD_LANES - MLA_NOPE - MLA_ROPE, qT.shape[1]), BF16)
    kn = jnp.dot(ckvn, wukvk_ref[...], preferred_element_type=F32)
    kr = _rope_lanes(krp, rope_km_ref, hr)
    for hd in range(MLA_HEADS):
        sl = slice(hd * HEAD_LANES, (hd + 1) * HEAD_LANES)
        km_ref[:, sl] = (kn[:, sl] + kr).astype(BF16)
    vT = lax.dot_general(wuvT_ref[...], ckvn, _NT, preferred_element_type=F32)
    ones_row = lax.broadcasted_iota(jnp.int32, vT.shape, 0) % HEAD_LANES == MLA_V
    vT = jnp.where(ones_row, 1.0, vT).astype(BF16)
    for hd in range(MLA_HEADS):
        vmT_ref[hd] = vT[hd * HEAD_LANES:(hd + 1) * HEAD_LANES]


def _inproj(h, mod, wts, rope, *, tile):
    b, n, d = h.shape
    const2 = lambda bi, ti: (0, 0)
    tok = lambda w: pl.BlockSpec((None, tile, w), lambda bi, ti: (bi, ti, 0))
    tokT = lambda w: pl.BlockSpec((None, w, tile), lambda bi, ti: (bi, 0, ti))
    full = lambda a: pl.BlockSpec(a.shape, const2)
    in_specs = [tok(d), pl.BlockSpec((None, MOD_ROWS, d), lambda bi, ti: (bi, 0, 0))]
    in_specs += [full(wts[k]) for k in ('w_nat', 'w_T', 'w_uqT', 'w_ukvk', 'w_uvT', 'gq', 'gkv')]
    in_specs += [pl.BlockSpec((2, DIFF_HEAD_DIM // 2, tile), lambda bi, ti: (0, 0, ti)),
                 pl.BlockSpec((3, tile, HEAD_LANES), lambda bi, ti: (0, ti, 0)),
                 pl.BlockSpec((2, MLA_ROPE // 2, tile), lambda bi, ti: (0, 0, ti)),
                 pl.BlockSpec((3, tile, HEAD_LANES), lambda bi, ti: (0, ti, 0))]
    sds = jax.ShapeDtypeStruct
    vrows_d, vrows_m = DIFF_V_DIM + DIFF_V_PAD, HEAD_LANES
    headsT = lambda rows: pl.BlockSpec((None, DIFF_HEADS, rows, tile), lambda bi, ti: (bi, 0, 0, ti))
    out_shape = (sds((b, ATT_W, n), BF16), sds((b, n, ATT_W), BF16), sds((b, DIFF_HEADS, vrows_d, n), BF16),
                 sds((b, ATT_W, n), BF16), sds((b, n, ATT_W), BF16), sds((b, MLA_HEADS, vrows_m, n), BF16),
                 sds((b, n, HYENA_PROJ), F32), sds((b, n, CONF_WIDTH), F32))
    out_specs = (tokT(ATT_W), tok(ATT_W), headsT(vrows_d), tokT(ATT_W), tok(ATT_W), headsT(vrows_m),
                 tok(HYENA_PROJ), tok(CONF_WIDTH))
    return pl.pallas_call(
        _inproj_kernel, grid=(b, n // tile), in_specs=in_specs, out_specs=out_specs, out_shape=out_shape,
        compiler_params=_params(2), name="inproj",
    )(h, mod, wts['w_nat'], wts['w_T'], wts['w_uqT'], wts['w_ukvk'], wts['w_uvT'], wts['gq'], wts['gkv'],
      rope['T_d'], rope['k_d'], rope['T_m'], rope['k_m'])


def _pad_heads(w, width):
    rows = w.shape[0]
    w = w.reshape(rows, MLA_HEADS, width)
    return jnp.pad(w, ((0, 0), (0, 0), (0, HEAD_LANES - width))).reshape(rows, ATT_W)


def _inproj_weights(p):
    d = p['w_in'].shape[0]
    dq, dk, dv, hy, cf, cq, ckv, kr = _split(p['w_in'][:, :sum(IN_SPLITS)], IN_SPLITS)
    krp = jnp.zeros((d, HEAD_LANES), F32).at[:, MLA_NOPE:MLA_NOPE + MLA_ROPE].set(kr)
    w_ukv = p['mla_w_ukv'].reshape(MLA_KV_RANK, MLA_HEADS, MLA_NOPE + MLA_V)
    return dict(
        w_nat=jnp.concatenate([dk, hy, cf, cq, ckv, krp], axis=1).astype(BF16),
        w_T=jnp.concatenate([dq * (DIFF_HEAD_DIM ** -0.5 * LOG2E), dv], axis=1).T.astype(BF16),
        w_uqT=_pad_heads(p['mla_w_uq'] * (MLA_SCALE * LOG2E), MLA_NOPE + MLA_ROPE).T.astype(BF16),
        w_ukvk=_pad_heads(w_ukv[:, :, :MLA_NOPE].reshape(MLA_KV_RANK, -1), MLA_NOPE).astype(BF16),
        w_uvT=_pad_heads(w_ukv[:, :, MLA_NOPE:].reshape(MLA_KV_RANK, -1), MLA_V).T.astype(BF16),
        gq=p['mla_q_norm_g'][None, :], gkv=p['mla_kv_norm_g'][None, :])


def _rope_tables(n_tok, rot_dim):
    rows = n_tok // GRID_W
    row = jnp.repeat(jnp.arange(rows), GRID_W).astype(F32)
    col = jnp.tile(jnp.arange(GRID_W), rows).astype(F32)
    nf = rot_dim // 4
    inv = ROPE_BASE ** (-jnp.arange(nf, dtype=F32) / nf)
    ang = jnp.concatenate([row[:, None] * inv, col[:, None] * inv], axis=-1)
    return jnp.cos(ang), jnp.sin(ang)


def _rope_operands(n_tok, identity):
    if identity:
        cos_d, sin_d = jnp.ones((n_tok, DIFF_HEAD_DIM // 2), F32), jnp.zeros((n_tok, DIFF_HEAD_DIM // 2), F32)
        cos_m, sin_m = jnp.ones((n_tok, MLA_ROPE // 2), F32), jnp.zeros((n_tok, MLA_ROPE // 2), F32)
    else:
        cos_d, sin_d = _rope_tables(n_tok, DIFF_HEAD_DIM)
        cos_m, sin_m = _rope_tables(n_tok, MLA_ROPE)
    z_d, z_m = jnp.zeros_like(sin_d), jnp.zeros_like(sin_m)
    two = lambda a, bb: jnp.tile(jnp.concatenate([a, bb], axis=1), (1, 2))
    lo, hi = jnp.zeros((n_tok, MLA_NOPE), F32), jnp.zeros((n_tok, HEAD_LANES - MLA_NOPE - MLA_ROPE), F32)
    mid = lambda a, bb: jnp.concatenate([lo, a, bb, hi], axis=1)
    return dict(T_d=jnp.stack([cos_d.T, sin_d.T]), T_m=jnp.stack([cos_m.T, sin_m.T]),
                k_d=jnp.stack([two(cos_d, cos_d), two(z_d, sin_d), two(-sin_d, z_d)]),
                k_m=jnp.stack([mid(cos_m, cos_m), mid(z_m, sin_m), mid(-sin_m, z_m)]))


def _merge_kernel(h_ref, mod_ref, a_ref, hy_ref, cf_ref, m_ref, wg_ref, wbd_ref, wbh_ref, wbc_ref, wbm_ref,
                  wo_ref, wrh_ref, wrl_ref, hn_ref, u2_ref, lg_ref):
    h = h_ref[...]
    d = h.shape[1]
    u = _modulated_norm(h, mod_ref[0:1, :], mod_ref[1:2, :]).astype(BF16)
    gates = jax.nn.sigmoid(jnp.dot(u, wg_ref[...], preferred_element_type=F32))
    dot = lambda x, w_ref: jnp.dot(x, w_ref[...], preferred_element_type=F32)
    acc = gates[:, :d] * dot(a_ref[...], wbd_ref)
    acc += gates[:, d:2 * d] * dot(hy_ref[...].astype(BF16), wbh_ref)
    acc += gates[:, 2 * d:3 * d] * dot(cf_ref[...].astype(BF16), wbc_ref)
    acc += gates[:, 3 * d:] * dot(m_ref[...], wbm_ref)
    hn = h + mod_ref[2:3, :] * dot(acc.astype(BF16), wo_ref)
    hn_ref[...] = hn
    u2 = _modulated_norm(hn, mod_ref[3:4, :], mod_ref[4:5, :])
    u2h = u2.astype(BF16)
    u2l = (u2 - u2h.astype(F32)).astype(BF16)
    u2_ref[...] = u2h
    lg_ref[...] = dot(u2h, wrh_ref) + (dot(u2l, wrh_ref) + dot(u2h, wrl_ref))


def _merge(h, mod, a, hyv, cfv, m, wts, *, tile):
    b, n, d = h.shape
    const2 = lambda bi, ti: (0, 0)
    tok = lambda w: pl.BlockSpec((None, tile, w), lambda bi, ti: (bi, ti, 0))
    names = ('w_gate', 'w_bd', 'w_bh', 'w_bc', 'w_bm', 'w_out', 'w_rh', 'w_rl')
    in_specs = [tok(d), pl.BlockSpec((None, MOD_ROWS, d), lambda bi, ti: (bi, 0, 0)),
                tok(ATT_W), tok(HYENA_WIDTH), tok(CONF_WIDTH), tok(ATT_W)]
    in_specs += [pl.BlockSpec(wts[k].shape, const2, pipeline_mode=pl.Buffered(1)) for k in names]
    sds = jax.ShapeDtypeStruct
    return pl.pallas_call(
        _merge_kernel, grid=(b, n // tile), in_specs=in_specs,
        out_specs=(tok(d), tok(d), tok(HEAD_LANES)),
        out_shape=(sds((b, n, d), F32), sds((b, n, d), BF16), sds((b, n, HEAD_LANES), F32)),
        compiler_params=_params(2, VMEM_LIMIT_LARGE_BYTES), name="merge",
    )(h, mod, a, hyv, cfv, m, *[wts[k] for k in names])


def _merge_weights(p, lam_init):
    d = p['w_out'].shape[0]
    wb_d, wb_h, wb_c, wb_m = (w.T for w in _split(p['w_branch'].T, BRANCH_WIDTHS))
    wb_d = wb_d * (jnp.tile(p['diff_subln_g'], DIFF_HEADS) * (1.0 - lam_init))[:, None]
    wb_m = jnp.pad(wb_m.reshape(MLA_HEADS, MLA_V, d), ((0, 0), (0, HEAD_LANES - MLA_V), (0, 0))).reshape(ATT_W, d)
    w_r = jnp.pad(p['w_router'], ((0, 0), (0, HEAD_LANES - N_EXPERTS)))
    w_rh = w_r.astype(BF16)
    return dict(w_gate=p['w_in'][:, sum(IN_SPLITS):].astype(BF16), w_bd=wb_d.astype(BF16), w_bh=wb_h.astype(BF16),
                w_bc=wb_c.astype(BF16), w_bm=wb_m.astype(BF16), w_out=p['w_out'].astype(BF16),
                w_rh=w_rh, w_rl=(w_r - w_rh.astype(F32)).astype(BF16))


WINDOW_ALIGN = 16
SUB_TOKENS = 256
GATHER_WINDOW = SUB_TOKENS + WINDOW_ALIGN
GATHER_SUBS_PER_STEP = 8
COMBINE_TOKENS = 128
COMBINE_WINDOW = 256
COMBINE_SUBS_PER_STEP = 16
FFN_ROWS = 512
ROUTE_MIN_ROWS = 8


def _excl_scan(x, lane, row):
    inc = x
    s = 1
    while s < HEAD_LANES:
        inc = inc + jnp.where(lane >= s, pltpu.roll(inc, s, 2), 0.0)
        s *= 2
    tot = jnp.sum(x, axis=2, keepdims=True) + jnp.zeros_like(x)
    off = tot
    s = 1
    while s < x.shape[1]:
        off = off + jnp.where(row >= s, pltpu.roll(off, s, 1), 0.0)
        s *= 2
    return inc - x + (off - tot)


def _route_kernel(lg_ref, pos_ref, aff_ref, *, n_valid, cap):
    lg = lg_ref[...]
    shape = lg.shape
    lane = lax.broadcasted_iota(jnp.int32, shape, 2)
    row = lax.broadcasted_iota(jnp.int32, shape, 1)
    e = jnp.exp(lg - jnp.max(lg, axis=0, keepdims=True))
    aff = e / jnp.sum(e, axis=0, keepdims=True)
    bits = jnp.where(row * HEAD_LANES + lane < n_valid, pltpu.bitcast(aff, jnp.int32), -1)

    def count(mask):
        c = jnp.sum(jnp.where(mask, 1.0, 0.0), axis=2, keepdims=True)
        return jnp.sum(c, axis=1, keepdims=True)

    def step(i, thr):
        cand = thr | (jnp.int32(1) << (30 - i))
        return jnp.where(count(bits >= cand) >= cap, cand, thr)
    thr = lax.fori_loop(0, 31, step, jnp.zeros((shape[0], 1, 1), jnp.int32))
    gt = bits > thr
    eq = bits == thr
    need = cap - count(gt)
    tie_rank = _excl_scan(jnp.where(eq, 1.0, 0.0), lane, row)
    sel = gt | (eq & (tie_rank < need))
    pos = _excl_scan(jnp.where(sel, 1.0, 0.0), lane, row)
    pos_ref[...] = jnp.where(sel, pos.astype(jnp.int32), -1)
    aff_ref[...] = aff


def _route(logits, cap):
    b, n, _ = logits.shape
    rows = max(ROUTE_MIN_ROWS, n // HEAD_LANES)
    lg = jnp.swapaxes(logits[..., :N_EXPERTS], 1, 2)
    lg = jnp.pad(lg, ((0, 0), (0, 0), (0, rows * HEAD_LANES - n))).reshape(b, N_EXPERTS, rows, HEAD_LANES)
    spec = pl.BlockSpec((None, N_EXPERTS, rows, HEAD_LANES), lambda bi: (bi, 0, 0, 0))
    pos, aff = pl.pallas_call(
        functools.partial(_route_kernel, n_valid=n, cap=cap), grid=(b,), in_specs=[spec], out_specs=(spec, spec),
        out_shape=(jax.ShapeDtypeStruct(lg.shape, jnp.int32), jax.ShapeDtypeStruct(lg.shape, F32)),
        compiler_params=_params(1), name="route",
    )(lg)
    flat = lambda a: a.reshape(b, N_EXPERTS, rows * HEAD_LANES)[..., :n]
    return flat(pos), flat(aff)


def _experts_kernel(base_ref, u_ref, mod_ref, aff_ref, pos_ref, win_ref, wout_ref, ye_ref, xe_ref, *,
                    n_sub, cap):
    bi, ei, kb = pl.program_id(0), pl.program_id(1), pl.program_id(2)
    d = u_ref.shape[1]

    @pl.when(kb == 0)
    def _():
        xe_ref[...] = jnp.zeros(xe_ref.shape, F32)

    slot = lax.broadcasted_iota(jnp.int32, (GATHER_WINDOW, SUB_TOKENS), 0)
    ones = jnp.ones((SUB_TOKENS, HEAD_LANES), BF16)
    for j in range(n_sub):
        tok = slice(j * SUB_TOKENS, (j + 1) * SUB_TOKENS)
        base = pl.multiple_of(base_ref[bi, ei, kb * n_sub + j], WINDOW_ALIGN)
        match = slot == (pos_ref[:, tok] - base)
        onehot = jnp.where(match, 1.0, 0.0).astype(BF16)
        g = aff_ref[:, tok]
        g_hi = g.astype(BF16).astype(F32)
        sel_hi = jnp.where(match, g_hi, 0.0).astype(BF16)
        sel_lo = jnp.where(match, g - g_hi, 0.0).astype(BF16)
        rows = pl.ds(base, GATHER_WINDOW)
        xe_ref[rows, :d] += jnp.dot(onehot, u_ref[tok, :], preferred_element_type=F32)
        xe_ref[rows, d:d + HEAD_LANES] += jnp.dot(sel_hi, ones, preferred_element_type=F32)
        xe_ref[rows, d + HEAD_LANES:] += jnp.dot(sel_lo, ones, preferred_element_type=F32)

    @pl.when(kb == pl.num_programs(2) - 1)
    def _():
        f = wout_ref.shape[0]
        step = min(FFN_ROWS, cap)
        for r0 in range(0, cap, step):
            x = xe_ref[r0:r0 + step, :d].astype(BF16)
            gate = xe_ref[r0:r0 + step, d:d + HEAD_LANES] + xe_ref[r0:r0 + step, d + HEAD_LANES:]
            hgu = jnp.dot(x, win_ref[...], preferred_element_type=F32)
            act = (jax.nn.silu(hgu[:, :f]) * hgu[:, f:]).astype(BF16)
            y = jnp.dot(act, wout_ref[...], preferred_element_type=F32)
            scale = jnp.concatenate([gate] * (d // HEAD_LANES), axis=1) * mod_ref[5:6, :]
            ye_ref[r0:r0 + step, :] = (y * scale).astype(BF16)
        ye_ref[cap:, :] = jnp.zeros((ye_ref.shape[0] - cap, ye_ref.shape[1]), BF16)


def _combine_kernel(base_ref, h_ref, mod_ref, posn_ref, ye_ref, hn_ref, *, n_sub, final_norm):
    bi, kb, ei = pl.program_id(0), pl.program_id(1), pl.program_id(2)

    @pl.when(ei == 0)
    def _():
        hn_ref[...] = h_ref[...]

    slot = lax.broadcasted_iota(jnp.int32, (COMBINE_TOKENS, COMBINE_WINDOW), 1)
    lane_e = lax.broadcasted_iota(jnp.int32, (COMBINE_TOKENS, N_EXPERTS), 1)
    for j in range(n_sub):
        tok = slice(j * COMBINE_TOKENS, (j + 1) * COMBINE_TOKENS)
        base = pl.multiple_of(base_ref[bi, ei, kb * n_sub + j], WINDOW_ALIGN)
        rel = jnp.sum(jnp.where(lane_e == ei, posn_ref[tok, :], 0), axis=1, keepdims=True) - base
        onehot = jnp.where(slot == rel, 1.0, 0.0).astype(BF16)
        ye = ye_ref[pl.ds(base, COMBINE_WINDOW), :]
        hn_ref[tok, :] += jnp.dot(onehot, ye, preferred_element_type=F32)

    if final_norm:
        @pl.when(ei == pl.num_programs(2) - 1)
        def _():
            hn = hn_ref[...]
            hn_ref[...] = hn * lax.rsqrt(jnp.mean(hn * hn, axis=-1, keepdims=True) + EPS) * mod_ref[6:7, :]


def _expert_choice_ffn(h, mod, u2, logits, w_exp_in, w_exp_out, final_norm=False):
    b, n, d = u2.shape
    cap = max(1, EC_CAPACITY * n // N_EXPERTS)
    n_sub = min(GATHER_SUBS_PER_STEP, n // SUB_TOKENS)
    n_sub_c = min(COMBINE_SUBS_PER_STEP, n // COMBINE_TOKENS)
    big, big_c = n_sub * SUB_TOKENS, n_sub_c * COMBINE_TOKENS
    n_big, n_big_c = n // big, n // big_c
    capp = cap + GATHER_WINDOW
    pos, aff = _route(logits, cap)

    def window_starts(sub):
        cnt = jnp.sum((pos >= 0).reshape(b, N_EXPERTS, n // sub, sub), axis=-1)
        return ((jnp.cumsum(cnt, axis=-1) - cnt) // WINDOW_ALIGN * WINDOW_ALIGN).astype(jnp.int32)
    base, base_c = window_starts(SUB_TOKENS), window_starts(COMBINE_TOKENS)
    f = w_exp_out.shape[1]
    ye = pl.pallas_call(
        functools.partial(_experts_kernel, n_sub=n_sub, cap=cap),
        grid_spec=pltpu.PrefetchScalarGridSpec(
            num_scalar_prefetch=1, grid=(b, N_EXPERTS, n_big),
            in_specs=[pl.BlockSpec((None, big, d), lambda bi, ei, kb, base_r: (bi, kb, 0)),
                      pl.BlockSpec((None, MOD_ROWS, d), lambda bi, ei, kb, base_r: (bi, 0, 0)),
                      pl.BlockSpec((None, None, 1, big), lambda bi, ei, kb, base_r: (bi, ei, 0, kb)),
                      pl.BlockSpec((None, None, 1, big), lambda bi, ei, kb, base_r: (bi, ei, 0, kb)),
                      pl.BlockSpec((None, d, 2 * f), lambda bi, ei, kb, base_r: (ei, 0, 0)),
                      pl.BlockSpec((None, f, d), lambda bi, ei, kb, base_r: (ei, 0, 0))],
            out_specs=pl.BlockSpec((None, None, capp, d), lambda bi, ei, kb, base_r: (bi, ei, 0, 0)),
            scratch_shapes=[pltpu.VMEM((capp, d + 2 * HEAD_LANES), F32)]),
        out_shape=jax.ShapeDtypeStruct((b, N_EXPERTS, capp, d), BF16),
        compiler_params=_params(3, VMEM_LIMIT_LARGE_BYTES), name="experts",
    )(base, u2, mod, aff.reshape(b, N_EXPERTS, 1, n), pos.reshape(b, N_EXPERTS, 1, n), w_exp_in, w_exp_out)
    posn = jnp.swapaxes(pos, 1, 2)
    return pl.pallas_call(
        functools.partial(_combine_kernel, n_sub=n_sub_c, final_norm=final_norm),
        grid_spec=pltpu.PrefetchScalarGridSpec(
            num_scalar_prefetch=1, grid=(b, n_big_c, N_EXPERTS),
            in_specs=[pl.BlockSpec((None, big_c, d), lambda bi, kb, ei, base_r: (bi, kb, 0)),
                      pl.BlockSpec((None, MOD_ROWS, d), lambda bi, kb, ei, base_r: (bi, 0, 0)),
                      pl.BlockSpec((None, big_c, N_EXPERTS), lambda bi, kb, ei, base_r: (bi, kb, 0)),
                      pl.BlockSpec((None, None, capp, d), lambda bi, kb, ei, base_r: (bi, ei, 0, 0))],
            out_specs=pl.BlockSpec((None, big_c, d), lambda bi, kb, ei, base_r: (bi, kb, 0))),
        out_shape=jax.ShapeDtypeStruct((b, n, d), F32),
        compiler_params=_params(3, VMEM_LIMIT_LARGE_BYTES), name="combine",
    )(base_c, h, mod, posn, ye)


def _split(z, sizes):
    out, start = [], 0
    for s in sizes:
        out.append(z[..., start:start + s])
        start += s
    return out


HALO = 16


def _fill_ext(ext_ref, x_ref, prev_ref, next_ref):
    ti, nt = pl.program_id(1), pl.num_programs(1)
    tt = x_ref.shape[0]
    ext_ref[0:HALO, :] = jnp.where(ti > 0, prev_ref[...], 0.0)
    ext_ref[HALO:HALO + tt, :] = x_ref[...]
    ext_ref[HALO + tt:, :] = jnp.where(ti < nt - 1, next_ref[...], 0.0)


def _taps(ext_ref, w_ref, tt):
    k = w_ref.shape[0]
    acc = None
    for j in range(k):
        start = HALO - k // 2 + j
        term = w_ref[j:j + 1, :] * ext_ref[start:start + tt, :]
        acc = term if acc is None else acc + term
    return acc


def _short_conv_kernel(x_ref, prev_ref, next_ref, w_ref, b_ref, x1_ref, x2_ref, v_ref, ext_ref):
    _fill_ext(ext_ref, x_ref, prev_ref, next_ref)
    y = _taps(ext_ref, w_ref, x_ref.shape[0]) + b_ref[...]
    x1_ref[...] = y[:, :HYENA_WIDTH]
    x2_ref[...] = y[:, HYENA_WIDTH:2 * HYENA_WIDTH]
    v_ref[...] = y[:, 2 * HYENA_WIDTH:]


def _conformer_kernel(x_ref, prev_ref, next_ref, w_ref, g_ref, b_ref, o_ref, ext_ref):
    _fill_ext(ext_ref, x_ref, prev_ref, next_ref)
    u = _taps(ext_ref, w_ref, x_ref.shape[0])
    mu = jnp.mean(u, axis=-1, keepdims=True)
    var = jnp.mean(jnp.square(u - mu), axis=-1, keepdims=True)
    y = (u - mu) * lax.rsqrt(var + EPS) * g_ref[...] + b_ref[...]
    o_ref[...] = y * jax.nn.sigmoid(y)


def _token_conv(body, x, consts, out_widths, name):
    b, n, w = x.shape
    tt = min(1024, n)
    per = tt // HALO
    last = n // HALO - 1
    in_specs = [pl.BlockSpec((None, tt, w), lambda bi, ti: (bi, ti, 0)),
                pl.BlockSpec((None, HALO, w), lambda bi, ti: (bi, jnp.maximum(ti * per - 1, 0), 0)),
                pl.BlockSpec((None, HALO, w), lambda bi, ti: (bi, jnp.minimum((ti + 1) * per, last), 0))]
    in_specs += [pl.BlockSpec(cst.shape, lambda bi, ti: (0, 0)) for cst in consts]
    outs = tuple(jax.ShapeDtypeStruct((b, n, ow), F32) for ow in out_widths)
    out_specs = tuple(pl.BlockSpec((None, tt, ow), lambda bi, ti: (bi, ti, 0)) for ow in out_widths)
    return pl.pallas_call(body, grid=(b, n // tt), in_specs=in_specs, out_specs=out_specs, out_shape=outs,
                          scratch_shapes=[pltpu.VMEM((tt + 2 * HALO, w), F32)],
                          compiler_params=_params(2), name=name)(x, x, x, *consts)


def _conformer_branch(glu, p):
    return _token_conv(_conformer_kernel, glu, (p['conf_dw_w'], p['conf_ln_g'][None, :], p['conf_ln_b'][None, :]),
                       (CONF_WIDTH,), "conformer")[0]


FILT_LANES = 128
DFT_SHORT = 256


def _split_bf16(x):
    hi = x.astype(BF16)
    return hi, (x - hi.astype(F32)).astype(BF16)


def _dot_split(ah, al, bh, bl):
    dot = lambda u, v: jnp.dot(u, v, preferred_element_type=F32)
    return dot(ah, bh) + (dot(al, bh) + dot(ah, bl))


def _dot_const(mh, ml, x):
    xb = x.astype(BF16)
    return jnp.dot(mh, xb, preferred_element_type=F32) + jnp.dot(ml, xb, preferred_element_type=F32)


def _filter_kernel(z_ref, w1h, w1l, b1, f1, w2h, w2l, b2, f2, w3h, w3l, dl_ref, h_ref, asum_ref, *,
                   tiles_per_dir):
    z = z_ref[...]
    hid = jnp.sin(f1[...] * (_dot_split(*_split_bf16(z), w1h[...], w1l[...]) + b1[...]))
    hid = jnp.sin(f2[...] * (_dot_split(*_split_bf16(hid), w2h[...], w2l[...]) + b2[...]))
    h = _dot_split(*_split_bf16(hid), w3h[...], w3l[...])
    h = h * jnp.exp(-z[:, 0:1] * dl_ref[...])
    h_ref[...] = h

    @pl.when(pl.program_id(0) % tiles_per_dir == 0)
    def _():
        asum_ref[...] = jnp.zeros(asum_ref.shape, F32)
    asum_ref[...] += jnp.sum(jnp.abs(h), axis=0, keepdims=True)


def _normalise_kernel(h_ref, asum_ref, *o_refs, n):
    tt = h_ref.shape[0]
    row = pl.program_id(0) * tt + lax.broadcasted_iota(jnp.int32, h_ref.shape, 0)
    k = jnp.where(row == n, 0.0, h_ref[...] / asum_ref[...])
    for o, o_ref in enumerate(o_refs):
        o_ref[...] = k[:, o * HYENA_WIDTH:(o + 1) * HYENA_WIDTH]


def _hyena_taps(n, p):
    t = jnp.linspace(0.0, 1.0, n, dtype=F32)[:, None]
    bands = (FILT_EMB - 1) // 2
    w = (2.0 * math.pi / n) * jnp.arange(n, dtype=F32)[:, None]
    f = jnp.linspace(1e-4, bands - 1, bands, dtype=F32)[None, :]
    t2, w2pos = jnp.concatenate([t, t[::-1]], axis=0), jnp.concatenate([w, w[::-1]], axis=0)
    z2 = jnp.concatenate([t2, jnp.cos(f * w2pos), -jnp.sin(f * w2pos),
                          jnp.zeros((2 * n, FILT_LANES - FILT_EMB), F32)], axis=-1)
    padc = lambda a: jnp.pad(a, ((0, 0), (0, FILT_LANES - a.shape[1])))
    padr = lambda a: jnp.pad(a, ((0, FILT_LANES - a.shape[0]), (0, 0)))
    w1, w2, w3 = padc(padr(p['filt_w1'])), padc(padr(p['filt_w2'])), padr(p['filt_w3'])
    b1, b2 = padc(p['filt_b1'][None, :]), padc(p['filt_b2'][None, :])
    f1, f2 = padc(p['filt_freq'][0][None, :]), padc(p['filt_freq'][1][None, :])
    deltas = jnp.abs(jnp.linspace(math.log(DECAY_TARGET) / SLOW_DECAY, math.log(DECAY_TARGET) / FAST_DECAY,
                                  HYENA_WIDTH, dtype=F32))
    width = HYENA_ORDER * HYENA_WIDTH
    w3 = w3.reshape(FILT_LANES, HYENA_ORDER, 2, HYENA_WIDTH).transpose(2, 0, 1, 3).reshape(2, FILT_LANES, width)
    dl = jnp.tile(deltas, HYENA_ORDER)[None, :]
    w3h, w3l = _split_bf16(w3)
    tt = min(1024, n)
    tiles_per_dir = n // tt
    cspec = lambda a: pl.BlockSpec(a.shape, lambda i: (0, 0))
    dirspec = lambda rows: pl.BlockSpec((None, rows, width), lambda i: (i // tiles_per_dir, 0, 0))
    tile = lambda w: pl.BlockSpec((tt, w), lambda i: (i, 0))
    small = [*_split_bf16(w1), b1, f1, *_split_bf16(w2), b2, f2]
    h_raw, asum = pl.pallas_call(
        functools.partial(_filter_kernel, tiles_per_dir=tiles_per_dir), grid=(2 * tiles_per_dir,),
        in_specs=[tile(FILT_LANES)] + [cspec(a) for a in small] + [dirspec(FILT_LANES), dirspec(FILT_LANES), cspec(dl)],
        out_specs=(tile(width), dirspec(1)),
        out_shape=(jax.ShapeDtypeStruct((2 * n, width), F32), jax.ShapeDtypeStruct((2, 1, width), F32)),
        compiler_params=_params(1), name="hyena_filter_mlp")(z2, *small, w3h, w3l, dl)
    return pl.pallas_call(
        functools.partial(_normalise_kernel, n=n), grid=(2 * tiles_per_dir,),
        in_specs=[tile(width), dirspec(1)],
        out_specs=tuple(tile(HYENA_WIDTH) for _ in range(HYENA_ORDER)),
        out_shape=tuple(jax.ShapeDtypeStruct((2 * n, HYENA_WIDTH), F32) for _ in range(HYENA_ORDER)),
        compiler_params=_params(1), name="hyena_filter_norm",
    )(h_raw, asum)


def _dft_tables(n):
    n2 = DFT_SHORT if n >= 4 * DFT_SHORT else n
    n1 = n // n2

    def cis(idx):
        ang = (-2.0 * math.pi / n) * idx.astype(F32)
        return jnp.cos(ang), jnp.sin(ang)
    k2 = jnp.arange(n2)
    fr, fi = cis((k2[:, None] * k2[None, :]) % n2 * n1)
    tabs = dict(n1=n1, n2=n2)
    tabs['f_hi'], tabs['f_lo'] = _split_bf16(jnp.stack([fr, fi]))
    k1 = jnp.arange(n1)
    tr, ti = cis(k1[:, None] * k2[None, :])
    tabs['tw'] = jnp.broadcast_to(jnp.stack([tr, ti], axis=1)[..., None], (n1, 2, n2, HEAD_LANES))
    if n1 > 1:
        gr, gi = cis((k1[:, None] * k1[None, :]) % n1 * n2)
        half = n1 // 2
        grh, gih = gr[:, :half], gi[:, :half]
        tabs['m_fwd'] = _split_bf16(jnp.block([[grh, -gih], [gih, grh]]))
        tabs['m_real'] = _split_bf16(jnp.concatenate([gr, gi], axis=0))
        tabs['m_inv'] = _split_bf16(jnp.block([[grh.T, gih.T], [-gih.T, grh.T]]))
    return tabs


SLABS_PER_STEP = 8


def _rowmix_slabs_kernel(mh_ref, ml_ref, x_ref, o_ref):
    for j in range(x_ref.shape[1]):
        o_ref[:, j, :] = _dot_const(mh_ref[...], ml_ref[...], x_ref[:, j, :])


def _rowmix_slabs(m, x):
    mh, ml = m
    rin, n2, c = x.shape
    nb = min(SLABS_PER_STEP, n2)
    return pl.pallas_call(
        _rowmix_slabs_kernel, grid=(n2 // nb,),
        in_specs=[pl.BlockSpec(mh.shape, lambda i: (0, 0)), pl.BlockSpec(ml.shape, lambda i: (0, 0)),
                  pl.BlockSpec((rin, nb, c), lambda i: (0, i, 0))],
        out_specs=pl.BlockSpec((mh.shape[0], nb, c), lambda i: (0, i, 0)),
        out_shape=jax.ShapeDtypeStruct((mh.shape[0], n2, c), F32), compiler_params=_params(1),
        name="dft_rowmix_slabs",
    )(mh, ml, x)


def _spectral_kernel(x_ref, tw_ref, fh_ref, fl_ref, k_ref, o_ref, *, conv):
    xr, xi = x_ref[0], x_ref[1]
    reps = xr.shape[1] // HEAD_LANES
    tr = jnp.concatenate([tw_ref[0]] * reps, axis=1)
    ti = jnp.concatenate([tw_ref[1]] * reps, axis=1)
    frh, fih, frl, fil = fh_ref[0], fh_ref[1], fl_ref[0], fl_ref[1]

    def dft(ar, ai, conj):
        rr, ii = _dot_const(frh, frl, ar), _dot_const(fih, fil, ai)
        ri, ir = _dot_const(frh, frl, ai), _dot_const(fih, fil, ar)
        return (rr + ii, ri - ir) if conj else (rr - ii, ri + ir)

    yr, yi = dft(xr * tr - xi * ti, xr * ti + xi * tr, False)
    if not conv:
        o_ref[0] = yr * k_ref[...]
        o_ref[1] = yi * k_ref[...]
        return
    kr, ki = k_ref[0], k_ref[1]
    cr, ci = dft(yr * kr - yi * ki, yr * ki + yi * kr, True)
    o_ref[0] = cr * tr + ci * ti
    o_ref[1] = ci * tr - cr * ti


def _spectral(x, k, tabs, conv):
    _, n1, n2, c = x.shape
    slab = pl.BlockSpec((2, None, n2, c), lambda i: (0, i, 0, 0))
    kspec = slab if conv else pl.BlockSpec(k.shape, lambda i: (0, 0))
    return pl.pallas_call(
        functools.partial(_spectral_kernel, conv=conv), grid=(n1,),
        in_specs=[slab, pl.BlockSpec((None, 2, n2, HEAD_LANES), lambda i: (i, 0, 0, 0)),
                  pl.BlockSpec(tabs['f_hi'].shape, lambda i: (0, 0, 0)),
                  pl.BlockSpec(tabs['f_lo'].shape, lambda i: (0, 0, 0)), kspec],
        out_specs=slab, out_shape=jax.ShapeDtypeStruct(x.shape, F32), compiler_params=_params(1),
        name="dft_spectral_conv" if conv else "dft_spectral_filter",
    )(x, tabs['tw'], tabs['f_hi'], tabs['f_lo'], k)


def _filter_spectrum(k, tabs):
    n, c = k.shape
    n1, n2 = tabs['n1'], tabs['n2']
    if n1 > 1:
        x = _rowmix_slabs(tabs['m_real'], k.reshape(n1, n2, c)).reshape(2, n1, n2, c)
    else:
        x = jnp.stack([k, jnp.zeros_like(k)]).reshape(2, 1, n2, c)
    return _spectral(x, jnp.full((1, c), 1.0 / n, F32), tabs, conv=False)


def _rowmix_gate_kernel(mh_ref, ml_ref, x_ref, g_ref, v_ref, s_ref, o_ref):
    o_ref[...] = g_ref[...] * (_dot_const(mh_ref[...], ml_ref[...], x_ref[...]) + s_ref[...] * v_ref[...])


def _gated_long_conv(gate, v, kf, skip, tabs):
    b, n, c = v.shape
    assert b == 2
    n1, n2 = tabs['n1'], tabs['n2']
    if n1 == 1:
        x = jnp.concatenate([v, jnp.zeros_like(v)], axis=1).reshape(2, 1, n2, c)
        y = _spectral(x, kf, tabs, conv=True).reshape(2, n2, c)[:, :n]
        return _hyena_gate(gate, y, v, skip)
    x = _rowmix_slabs(tabs['m_fwd'], v.reshape(n1, n2, c)).reshape(2, n1, n2, c)
    y = _spectral(x, kf, tabs, conv=True).reshape(2 * n1, n2 * c)
    mh, ml = tabs['m_inv']
    cols = n2 * c
    ct = min(2048, cols)
    tile = lambda rows: pl.BlockSpec((rows, ct), lambda i: (0, i))
    return pl.pallas_call(
        _rowmix_gate_kernel, grid=(cols // ct,),
        in_specs=[pl.BlockSpec(mh.shape, lambda i: (0, 0)), pl.BlockSpec(ml.shape, lambda i: (0, 0)),
                  tile(2 * n1), tile(n1), tile(n1), pl.BlockSpec((1, ct), lambda i: (0, 0))],
        out_specs=tile(n1), out_shape=jax.ShapeDtypeStruct((n1, cols), F32), compiler_params=_params(1),
        name="dft_rowmix_gate",
    )(mh, ml, y, gate.reshape(n1, cols), v.reshape(n1, cols), jnp.tile(skip, (1, ct // c))).reshape(2, n, c)


def _gate_kernel(g_ref, y_ref, v_ref, s_ref, o_ref):
    o_ref[...] = g_ref[...] * (y_ref[...] + s_ref[...] * v_ref[...])


def _hyena_gate(gate, y, v, skip):
    b, n, c = v.shape
    tt = min(2048, n)
    tok = pl.BlockSpec((None, tt, c), lambda bi, ti: (bi, ti, 0))
    return pl.pallas_call(_gate_kernel, grid=(b, n // tt),
                          in_specs=[tok, tok, tok, pl.BlockSpec((1, c), lambda bi, ti: (0, 0))], out_specs=tok,
                          out_shape=jax.ShapeDtypeStruct(v.shape, F32), compiler_params=_params(2),
                          name="hyena_gate")(gate, y, v, skip)


def _hyena_branch(hy, p, tabs):
    n = hy.shape[1]
    x1, x2, v = _token_conv(_short_conv_kernel, hy, (p['hyena_short_w'], p['hyena_short_b'][None, :]),
                            (HYENA_WIDTH,) * 3, "hyena_short_conv")
    taps = _hyena_taps(n, p)
    for o, gate in enumerate((x1, x2)):
        v = _gated_long_conv(gate, v, _filter_spectrum(taps[o], tabs), p['hyena_skip'][o][None, :], tabs)
    return v


def _adaln_kernel(c_ref, w_ref, b_ref, o_ref):
    s = c_ref[...]
    s = s * jax.nn.sigmoid(s)
    o_ref[...] = _dot_split(*_split_bf16(s), *_split_bf16(w_ref[...])) + b_ref[...]


def _adaln(cond, w, b):
    d, width = w.shape
    ct = width // 6
    return pl.pallas_call(
        _adaln_kernel, grid=(6,),
        in_specs=[pl.BlockSpec(cond.shape, lambda i: (0, 0)), pl.BlockSpec((d, ct), lambda i: (0, i)),
                  pl.BlockSpec((1, ct), lambda i: (0, i))],
        out_specs=pl.BlockSpec((cond.shape[0], ct), lambda i: (0, i)),
        out_shape=jax.ShapeDtypeStruct((cond.shape[0], width), F32), compiler_params=_params(1), name="adaln",
    )(cond, w, b[None, :])


def _mod_rows(mod, norm_mix_g, norm_ffn_g, final_g, batch):
    sh1, sc1, g1, sh2, sc2, g2 = jnp.split(mod, 6, axis=-1)
    rows = jnp.stack([norm_mix_g * (1.0 + sc1), sh1, g1, norm_ffn_g * (1.0 + sc2), sh2, g2,
                      jnp.broadcast_to(final_g, g1.shape), jnp.zeros_like(g1)], axis=1)
    return jnp.broadcast_to(rows, (batch,) + rows.shape[1:])


def kernel(x, c, ctx, c_ctx, ada_w, ada_b, norm_mix_g, norm_ffn_g, w_in, diff_lambda, diff_subln_g, hyena_short_w, hyena_short_b, filt_w1, filt_b1, filt_freq, filt_w2, filt_b2, filt_w3, hyena_skip, conf_dw_w, conf_ln_g, conf_ln_b, mla_q_norm_g, mla_kv_norm_g, mla_w_uq, mla_w_ukv, w_branch, w_out, w_router, w_exp_in, w_exp_out, final_norm_g):
    depth = w_in.shape[0]
    batch, n_lat, d = x.shape
    n_ctx = ctx.shape[1]
    rope_lat = _rope_operands(n_lat, identity=False)
    rope_ctx = _rope_operands(n_ctx, identity=True)
    dft_lat, dft_ctx = _dft_tables(2 * n_lat), _dft_tables(2 * n_ctx)
    cond = jnp.concatenate([c, c_ctx[None], jnp.zeros((MOD_ROWS - batch - 1, d), F32)], axis=0)
    tile_lat, tile_ctx = min(512, n_lat), min(256, n_ctx)
    h_lat, h_ctx = x, ctx
    for l in range(depth):
        last = l == depth - 1
        p = dict(w_in=w_in[l], diff_subln_g=diff_subln_g[l], hyena_short_w=hyena_short_w[l],
                 hyena_short_b=hyena_short_b[l], filt_w1=filt_w1[l], filt_b1=filt_b1[l], filt_freq=filt_freq[l],
                 filt_w2=filt_w2[l], filt_b2=filt_b2[l], filt_w3=filt_w3[l], hyena_skip=hyena_skip[l],
                 conf_dw_w=conf_dw_w[l], conf_ln_g=conf_ln_g[l], conf_ln_b=conf_ln_b[l],
                 mla_q_norm_g=mla_q_norm_g[l], mla_kv_norm_g=mla_kv_norm_g[l], mla_w_uq=mla_w_uq[l],
                 mla_w_ukv=mla_w_ukv[l], w_branch=w_branch[l], w_out=w_out[l], w_router=w_router[l],
                 w_exp_in=w_exp_in[l], w_exp_out=w_exp_out[l])
        ada = _adaln(cond, ada_w[l], ada_b[l])
        mod_lat = _mod_rows(ada[:batch], norm_mix_g[l], norm_ffn_g[l], final_norm_g, batch)
        mod_ctx = _mod_rows(ada[batch:batch + 1], norm_mix_g[l], norm_ffn_g[l], final_norm_g, batch)
        lam_init = 0.8 - 0.6 * math.exp(-0.3 * l)
        lq1, lk1, lq2, lk2 = diff_lambda[l].astype(F32)
        lam = jnp.reshape(jnp.exp(jnp.sum(lq1 * lk1)) - jnp.exp(jnp.sum(lq2 * lk2)) + lam_init, (1,))
        w_inp, w_mrg = _inproj_weights(p), _merge_weights(p, lam_init)

        qdT_l, kd_l, vdT_l, qmT_l, km_l, vmT_l, hy_l, glu_l = _inproj(h_lat, mod_lat, w_inp, rope_lat, tile=tile_lat)
        qdT_c, kd_c, vdT_c, qmT_c, km_c, vmT_c, hy_c, glu_c = _inproj(h_ctx, mod_ctx, w_inp, rope_ctx, tile=tile_ctx)
        a_lat = _flash_attention(lam, qdT_l, kd_c, vdT_c, kd_l, vdT_l, n_maps=2, tq=min(512, n_lat))
        m_lat = _flash_attention(lam, qmT_l, km_c, vmT_c, km_l, vmT_l, n_maps=1, tq=min(1024, n_lat))
        h_lat, u2_lat, lg_lat = _merge(h_lat, mod_lat, a_lat, _hyena_branch(hy_l, p, dft_lat), _conformer_branch(glu_l, p),
                                       m_lat, w_mrg, tile=min(256, n_lat))
        w_ei, w_eo = p['w_exp_in'].astype(BF16), p['w_exp_out'].astype(BF16)
        h_lat = _expert_choice_ffn(h_lat, mod_lat, u2_lat, lg_lat, w_ei, w_eo, final_norm=last)
        if not last:
            a_ctx = _flash_attention(lam, qdT_c, kd_c, vdT_c, None, None, n_maps=2, tq=n_ctx)
            m_ctx = _flash_attention(lam, qmT_c, km_c, vmT_c, None, None, n_maps=1, tq=n_ctx)
            h_ctx, u2_ctx, lg_ctx = _merge(h_ctx, mod_ctx, a_ctx, _hyena_branch(hy_c, p, dft_ctx),
                                           _conformer_branch(glu_c, p), m_ctx, w_mrg, tile=tile_ctx)
            h_ctx = _expert_choice_ffn(h_ctx, mod_ctx, u2_ctx, lg_ctx, w_ei, w_eo)
    return h_lat
```

```python
import functools
import math

import jax
import jax.numpy as jnp
from jax import lax
from jax.experimental import pallas as pl
from jax.experimental.pallas import tpu as pltpu

GRID_W = 64
ROPE_BASE = 10000.0
EPS = 1e-6

DIFF_HEADS = 4
DIFF_HEAD_DIM = 64
DIFF_V_DIM = 2 * DIFF_HEAD_DIM
HYENA_WIDTH = 256
HYENA_ORDER = 2
FILT_EMB = 33
DECAY_TARGET = 1e-2
FAST_DECAY = 0.3
SLOW_DECAY = 1.5
CONF_WIDTH = 256
MLA_HEADS = 4
MLA_Q_RANK = 256
MLA_KV_RANK = 128
MLA_NOPE = 64
MLA_ROPE = 32
MLA_V = 64
MLA_SCALE = (MLA_NOPE + MLA_ROPE) ** -0.5
N_EXPERTS = 16
EC_CAPACITY = 2

DIFF_QK_W = DIFF_HEADS * 2 * DIFF_HEAD_DIM
DIFF_V_W = DIFF_HEADS * DIFF_V_DIM
HYENA_PROJ = (HYENA_ORDER + 1) * HYENA_WIDTH
CONF_PROJ = 2 * CONF_WIDTH
IN_SPLITS = (DIFF_QK_W, DIFF_QK_W, DIFF_V_W, HYENA_PROJ, CONF_PROJ, MLA_Q_RANK, MLA_KV_RANK, MLA_ROPE)
BRANCH_WIDTHS = (DIFF_V_W, HYENA_WIDTH, CONF_WIDTH, MLA_HEADS * MLA_V)

HEAD_LANES = 128
DIFF_V_PAD = 16
ATT_W = DIFF_HEADS * HEAD_LANES
LOG2E = 1.4426950408889634
V7X_VMEM_BYTES = 64 * 1024 * 1024
VMEM_LIMIT_BYTES = V7X_VMEM_BYTES * 3 // 4
VMEM_LIMIT_LARGE_BYTES = V7X_VMEM_BYTES * 7 // 8
MOD_ROWS = 8

F32 = jnp.float32
BF16 = jnp.bfloat16
_NT = (((1,), (1,)), ((), ()))


def _params(n_axes, vmem=VMEM_LIMIT_BYTES):
    return pltpu.CompilerParams(dimension_semantics=("arbitrary",) * n_axes, vmem_limit_bytes=vmem)


def _stack_queries(q2_ref, qT, n_maps):
    tq = qT.shape[1]
    if n_maps == 2:
        row = lax.broadcasted_iota(jnp.int32, qT.shape, 0)
        zero = jnp.zeros_like(qT)
        q2_ref[:, :tq] = jnp.where(row < DIFF_HEAD_DIM, qT, zero)
        q2_ref[:, tq:] = jnp.where(row >= DIFF_HEAD_DIM, qT, zero)
    else:
        q2_ref[...] = qT


def _reset(m_ref, acc_ref):
    m_ref[...] = jnp.full(m_ref.shape, -jnp.inf, F32)
    acc_ref[...] = jnp.zeros(acc_ref.shape, F32)


def _absorb(s, vT, m_ref, acc_ref):
    m_prev = m_ref[...]
    m_new = jnp.maximum(m_prev, jnp.max(s, axis=0, keepdims=True))
    alpha = jnp.exp2(m_prev - m_new)
    p = jnp.exp2(s - m_new).astype(BF16)
    acc_ref[...] = alpha * acc_ref[...] + jnp.dot(vT, p, preferred_element_type=F32)
    m_ref[...] = m_new


def _attention_rows(acc_ref, lam_ref, n_maps, tq, sum_row):
    o = acc_ref[0:HEAD_LANES, :] / acc_ref[sum_row:sum_row + 1, :]
    if n_maps == 2:
        o = o[:, :tq] - lam_ref[0] * o[:, tq:]
        o = o * lax.rsqrt(jnp.mean(o * o, axis=0, keepdims=True) + EPS)
    return o.T.astype(BF16)


def _flash_ctx_kernel(lam_ref, qT_ref, kc_ref, vcT_ref, o_ref, acc_ref, m_ref, q2_ref, *, n_maps, sum_row):
    _stack_queries(q2_ref, qT_ref[...], n_maps)
    _reset(m_ref, acc_ref)
    _absorb(jnp.dot(kc_ref[...], q2_ref[...], preferred_element_type=F32), vcT_ref[...], m_ref, acc_ref)
    o_ref[...] = _attention_rows(acc_ref, lam_ref, n_maps, qT_ref.shape[1], sum_row)


def _flash_stream_kernel(lam_ref, qT_ref, kc_ref, vcT_ref, kl_ref, vlT_ref, o_ref, acc_ref, m_ref, q2_ref, sc_ref,
                         s_ref, *, n_maps, n_lat_chunks, tk, tq, sum_row):
    n_q = qT_ref.shape[1] // tq

    def load_queries(qi):
        _stack_queries(q2_ref, qT_ref[:, pl.ds(pl.multiple_of(qi * tq, tq), tq)], n_maps)

    def scores(k):
        return jnp.dot(k, q2_ref[...], preferred_element_type=F32)

    def absorb(s, vT):
        _absorb(s, vT, m_ref, acc_ref)

    load_queries(0)
    sc_ref[...] = scores(kc_ref[...])

    def query_block(qi, carry):
        _reset(m_ref, acc_ref)

        def chunk(c):
            return pl.ds(c * tk if isinstance(c, int) else pl.multiple_of(c * tk, tk), tk)

        s_ref[0] = scores(kl_ref[chunk(0), :])
        absorb(sc_ref[...], vcT_ref[...])

        def pair(j, inner):
            c = 2 * j
            s_ref[1] = scores(kl_ref[chunk(c + 1), :])
            absorb(s_ref[0], vlT_ref[:, chunk(c)])
            s_ref[0] = scores(kl_ref[chunk(c + 2), :])
            absorb(s_ref[1], vlT_ref[:, chunk(c + 1)])
            return inner
        lax.fori_loop(0, n_lat_chunks // 2 - 1, pair, 0)
        s_ref[1] = scores(kl_ref[chunk(n_lat_chunks - 1), :])
        absorb(s_ref[0], vlT_ref[:, chunk(n_lat_chunks - 2)])
        load_queries(jnp.minimum(qi + 1, n_q - 1))
        sc_ref[...] = scores(kc_ref[...])
        absorb(s_ref[1], vlT_ref[:, chunk(n_lat_chunks - 1)])
        o_ref[pl.ds(pl.multiple_of(qi * tq, tq), tq), :] = _attention_rows(acc_ref, lam_ref, n_maps, tq, sum_row)
        return carry
    lax.fori_loop(0, n_q, query_block, 0)


def _flash_attention(lam, qT, kc, vcT, kl, vlT, *, n_maps, tq):
    b, _, s = qT.shape
    lc = kc.shape[1]
    mv = vcT.shape[2]
    sum_row = HEAD_LANES if n_maps == 2 else MLA_V
    r = n_maps * tq
    if kl is not None:
        sl = kl.shape[1]
        tk = _lat_chunk(sl)
        n_lat_chunks = sl // tk
        assert n_lat_chunks % 2 == 0 and n_lat_chunks * tk == sl and s % tq == 0
        return pl.pallas_call(
            functools.partial(_flash_stream_kernel, n_maps=n_maps, n_lat_chunks=n_lat_chunks, tk=tk, tq=tq,
                              sum_row=sum_row),
            grid=(b, DIFF_HEADS),
            in_specs=[pl.BlockSpec(memory_space=pltpu.SMEM),
                      pl.BlockSpec((None, HEAD_LANES, s), lambda bi, hi: (bi, hi, 0)),
                      pl.BlockSpec((None, lc, HEAD_LANES), lambda bi, hi: (bi, 0, hi)),
                      pl.BlockSpec((None, None, mv, lc), lambda bi, hi: (bi, hi, 0, 0)),
                      pl.BlockSpec((None, sl, HEAD_LANES), lambda bi, hi: (bi, 0, hi)),
                      pl.BlockSpec((None, None, mv, sl), lambda bi, hi: (bi, hi, 0, 0))],
            out_specs=pl.BlockSpec((None, s, HEAD_LANES), lambda bi, hi: (bi, 0, hi)),
            out_shape=jax.ShapeDtypeStruct((b, s, ATT_W), BF16),
            scratch_shapes=[pltpu.VMEM((mv, r), F32), pltpu.VMEM((1, r), F32), pltpu.VMEM((HEAD_LANES, r), BF16),
                            pltpu.VMEM((lc, r), F32), pltpu.VMEM((2, tk, r), F32)],
            compiler_params=_params(2, VMEM_LIMIT_LARGE_BYTES),
            name=f"flash_attention_{n_maps}map",
        )(lam, qT, kc, vcT, kl, vlT)
    assert s == tq
    return pl.pallas_call(
        functools.partial(_flash_ctx_kernel, n_maps=n_maps, sum_row=sum_row),
        grid=(b, DIFF_HEADS),
        in_specs=[pl.BlockSpec(memory_space=pltpu.SMEM),
                  pl.BlockSpec((None, HEAD_LANES, s), lambda bi, hi: (bi, hi, 0)),
                  pl.BlockSpec((None, lc, HEAD_LANES), lambda bi, hi: (bi, 0, hi)),
                  pl.BlockSpec((None, None, mv, lc), lambda bi, hi: (bi, hi, 0, 0))],
        out_specs=pl.BlockSpec((None, s, HEAD_LANES), lambda bi, hi: (bi, 0, hi)),
        out_shape=jax.ShapeDtypeStruct((b, s, ATT_W), BF16),
        scratch_shapes=[pltpu.VMEM((mv, r), F32), pltpu.VMEM((1, r), F32), pltpu.VMEM((HEAD_LANES, r), BF16)],
        compiler_params=_params(2),
        name=f"flash_attention_ctx_{n_maps}map",
    )(lam, qT, kc, vcT)


def _lat_chunk(s):
    return min(1024, s // 2)


W_NAT_SPLITS = (DIFF_QK_W, HYENA_PROJ, CONF_PROJ, MLA_Q_RANK, MLA_KV_RANK, HEAD_LANES)


def _modulated_norm(h, a, shift):
    return h * lax.rsqrt(jnp.mean(h * h, axis=-1, keepdims=True) + EPS) * a + shift


def _rope_lanes(x, tab_ref, shift):
    return (x * tab_ref[0] + pltpu.roll(x, shift, 1) * tab_ref[1]
            + pltpu.roll(x, HEAD_LANES - shift, 1) * tab_ref[2])


def _inproj_kernel(h_ref, mod_ref, wnat_ref, wT_ref, wuqT_ref, wukvk_ref, wuvT_ref, gq_ref, gkv_ref,
                   ropeT_d_ref, rope_kd_ref, ropeT_m_ref, rope_km_ref,
                   qdT_ref, kd_ref, vdT_ref, qmT_ref, km_ref, vmT_ref, hy_ref, glu_ref):
    u = _modulated_norm(h_ref[...], mod_ref[0:1, :], mod_ref[1:2, :]).astype(BF16)
    z = jnp.dot(u, wnat_ref[...], preferred_element_type=F32)
    zT = lax.dot_general(wT_ref[...], u, _NT, preferred_element_type=F32)
    offs = [0]
    for w in W_NAT_SPLITS:
        offs.append(offs[-1] + w)
    dk, hy, cf, cq, ckv, krp = (z[:, offs[i]:offs[i + 1]] for i in range(len(W_NAT_SPLITS)))

    for hd in range(DIFF_HEADS):
        sl = slice(hd * HEAD_LANES, (hd + 1) * HEAD_LANES)
        kd_ref[:, sl] = _rope_lanes(dk[:, sl], rope_kd_ref, DIFF_HEAD_DIM // 2).astype(BF16)
    cos_d, sin_d = ropeT_d_ref[0], ropeT_d_ref[1]
    half = DIFF_HEAD_DIM // 2
    for g in range(2 * DIFF_HEADS):
        x1 = zT[g * DIFF_HEAD_DIM:g * DIFF_HEAD_DIM + half]
        x2 = zT[g * DIFF_HEAD_DIM + half:(g + 1) * DIFF_HEAD_DIM]
        qdT_ref[g * DIFF_HEAD_DIM:g * DIFF_HEAD_DIM + half, :] = (x1 * cos_d - x2 * sin_d).astype(BF16)
        qdT_ref[g * DIFF_HEAD_DIM + half:(g + 1) * DIFF_HEAD_DIM, :] = (x1 * sin_d + x2 * cos_d).astype(BF16)
    tail = jnp.where(lax.broadcasted_iota(jnp.int32, (DIFF_V_PAD, zT.shape[1]), 0) == 0, 1.0, 0.0).astype(BF16)
    for hd in range(DIFF_HEADS):
        r0 = DIFF_QK_W + hd * DIFF_V_DIM
        vdT_ref[hd, 0:DIFF_V_DIM, :] = zT[r0:r0 + DIFF_V_DIM].astype(BF16)
        vdT_ref[hd, DIFF_V_DIM:, :] = tail

    hy_ref[...] = hy
    glu_ref[...] = cf[:, :CONF_WIDTH] * jax.nn.sigmoid(cf[:, CONF_WIDTH:])

    cqn = (cq * lax.rsqrt(jnp.mean(cq * cq, axis=-1, keepdims=True) + EPS) * gq_ref[...]).astype(BF16)
    ckvn = (ckv * lax.rsqrt(jnp.mean(ckv * ckv, axis=-1, keepdims=True) + EPS) * gkv_ref[...]).astype(BF16)
    qT = lax.dot_general(wuqT_ref[...], cqn, _NT, preferred_element_type=F32)
    cos_m, sin_m = ropeT_m_ref[0], ropeT_m_ref[1]
    hr = MLA_ROPE // 2
    for hd in range(MLA_HEADS):
        base = hd * HEAD_LANES
        r1 = base + MLA_NOPE
        x1, x2 = qT[r1:r1 + hr], qT[r1 + hr:r1 + 2 * hr]
        qmT_ref[base:r1, :] = qT[base:r1].astype(BF16)
        qmT_ref[r1:r1 + hr, :] = (x1 * cos_m - x2 * sin_m).astype(BF16)
        qmT_ref[r1 + hr:r1 + 2 * hr, :] = (x1 * sin_m + x2 * cos_m).astype(BF16)
        qmT_ref[r1 + 2 * hr:base + HEAD_LANES, :] = jnp.zeros((HEAD_LANES - MLA_NOPE - MLA_ROPE, qT.shape[1]), BF16)
    kn = jnp.dot(ckvn, wukvk_ref[...], preferred_element_type=F32)
    kr = _rope_lanes(krp, rope_km_ref, hr)
    for hd in range(MLA_HEADS):
        sl = slice(hd * HEAD_LANES, (hd + 1) * HEAD_LANES)
        km_ref[:, sl] = (kn[:, sl] + kr).astype(BF16)
    vT = lax.dot_general(wuvT_ref[...], ckvn, _NT, preferred_element_type=F32)
    ones_row = lax.broadcasted_iota(jnp.int32, vT.shape, 0) % HEAD_LANES == MLA_V
    vT = jnp.where(ones_row, 1.0, vT).astype(BF16)
    for hd in range(MLA_HEADS):
        vmT_ref[hd] = vT[hd * HEAD_LANES:(hd + 1) * HEAD_LANES]


def _inproj(h, mod, wts, rope, *, tile):
    b, n, d = h.shape
    const2 = lambda bi, ti: (0, 0)
    tok = lambda w: pl.BlockSpec((None, tile, w), lambda bi, ti: (bi, ti, 0))
    tokT = lambda w: pl.BlockSpec((None, w, tile), lambda bi, ti: (bi, 0, ti))
    full = lambda a: pl.BlockSpec(a.shape, const2)
    in_specs = [tok(d), pl.BlockSpec((None, MOD_ROWS, d), lambda bi, ti: (bi, 0, 0))]
    in_specs += [full(wts[k]) for k in ('w_nat', 'w_T', 'w_uqT', 'w_ukvk', 'w_uvT', 'gq', 'gkv')]
    in_specs += [pl.BlockSpec((2, DIFF_HEAD_DIM // 2, tile), lambda bi, ti: (0, 0, ti)),
                 pl.BlockSpec((3, tile, HEAD_LANES), lambda bi, ti: (0, ti, 0)),
                 pl.BlockSpec((2, MLA_ROPE // 2, tile), lambda bi, ti: (0, 0, ti)),
                 pl.BlockSpec((3, tile, HEAD_LANES), lambda bi, ti: (0, ti, 0))]
    sds = jax.ShapeDtypeStruct
    vrows_d, vrows_m = DIFF_V_DIM + DIFF_V_PAD, HEAD_LANES
    headsT = lambda rows: pl.BlockSpec((None, DIFF_HEADS, rows, tile), lambda bi, ti: (bi, 0, 0, ti))
    out_shape = (sds((b, ATT_W, n), BF16), sds((b, n, ATT_W), BF16), sds((b, DIFF_HEADS, vrows_d, n), BF16),
                 sds((b, ATT_W, n), BF16), sds((b, n, ATT_W), BF16), sds((b, MLA_HEADS, vrows_m, n), BF16),
                 sds((b, n, HYENA_PROJ), F32), sds((b, n, CONF_WIDTH), F32))
    out_specs = (tokT(ATT_W), tok(ATT_W), headsT(vrows_d), tokT(ATT_W), tok(ATT_W), headsT(vrows_m),
                 tok(HYENA_PROJ), tok(CONF_WIDTH))
    return pl.pallas_call(
        _inproj_kernel, grid=(b, n // tile), in_specs=in_specs, out_specs=out_specs, out_shape=out_shape,
        compiler_params=_params(2), name="inproj",
    )(h, mod, wts['w_nat'], wts['w_T'], wts['w_uqT'], wts['w_ukvk'], wts['w_uvT'], wts['gq'], wts['gkv'],
      rope['T_d'], rope['k_d'], rope['T_m'], rope['k_m'])


def _pad_heads(w, width):
    rows = w.shape[0]
    w = w.reshape(rows, MLA_HEADS, width)
    return jnp.pad(w, ((0, 0), (0, 0), (0, HEAD_LANES - width))).reshape(rows, ATT_W)


def _inproj_weights(p):
    d = p['w_in'].shape[0]
    dq, dk, dv, hy, cf, cq, ckv, kr = _split(p['w_in'][:, :sum(IN_SPLITS)], IN_SPLITS)
    krp = jnp.zeros((d, HEAD_LANES), F32).at[:, MLA_NOPE:MLA_NOPE + MLA_ROPE].set(kr)
    w_ukv = p['mla_w_ukv'].reshape(MLA_KV_RANK, MLA_HEADS, MLA_NOPE + MLA_V)
    return dict(
        w_nat=jnp.concatenate([dk, hy, cf, cq, ckv, krp], axis=1).astype(BF16),
        w_T=jnp.concatenate([dq * (DIFF_HEAD_DIM ** -0.5 * LOG2E), dv], axis=1).T.astype(BF16),
        w_uqT=_pad_heads(p['mla_w_uq'] * (MLA_SCALE * LOG2E), MLA_NOPE + MLA_ROPE).T.astype(BF16),
        w_ukvk=_pad_heads(w_ukv[:, :, :MLA_NOPE].reshape(MLA_KV_RANK, -1), MLA_NOPE).astype(BF16),
        w_uvT=_pad_heads(w_ukv[:, :, MLA_NOPE:].reshape(MLA_KV_RANK, -1), MLA_V).T.astype(BF16),
        gq=p['mla_q_norm_g'][None, :], gkv=p['mla_kv_norm_g'][None, :])


def _rope_tables(n_tok, rot_dim):
    rows = n_tok // GRID_W
    row = jnp.repeat(jnp.arange(rows), GRID_W).astype(F32)
    col = jnp.tile(jnp.arange(GRID_W), rows).astype(F32)
    nf = rot_dim // 4
    inv = ROPE_BASE ** (-jnp.arange(nf, dtype=F32) / nf)
    ang = jnp.concatenate([row[:, None] * inv, col[:, None] * inv], axis=-1)
    return jnp.cos(ang), jnp.sin(ang)


def _rope_operands(n_tok, identity):
    if identity:
        cos_d, sin_d = jnp.ones((n_tok, DIFF_HEAD_DIM // 2), F32), jnp.zeros((n_tok, DIFF_HEAD_DIM // 2), F32)
        cos_m, sin_m = jnp.ones((n_tok, MLA_ROPE // 2), F32), jnp.zeros((n_tok, MLA_ROPE // 2), F32)
    else:
        cos_d, sin_d = _rope_tables(n_tok, DIFF_HEAD_DIM)
        cos_m, sin_m = _rope_tables(n_tok, MLA_ROPE)
    z_d, z_m = jnp.zeros_like(sin_d), jnp.zeros_like(sin_m)
    two = lambda a, bb: jnp.tile(jnp.concatenate([a, bb], axis=1), (1, 2))
    lo, hi = jnp.zeros((n_tok, MLA_NOPE), F32), jnp.zeros((n_tok, HEAD_LANES - MLA_NOPE - MLA_ROPE), F32)
    mid = lambda a, bb: jnp.concatenate([lo, a, bb, hi], axis=1)
    return dict(T_d=jnp.stack([cos_d.T, sin_d.T]), T_m=jnp.stack([cos_m.T, sin_m.T]),
                k_d=jnp.stack([two(cos_d, cos_d), two(z_d, sin_d), two(-sin_d, z_d)]),
                k_m=jnp.stack([mid(cos_m, cos_m), mid(z_m, sin_m), mid(-sin_m, z_m)]))


def _merge_kernel(h_ref, mod_ref, a_ref, hy_ref, cf_ref, m_ref, wg_ref, wbd_ref, wbh_ref, wbc_ref, wbm_ref,
                  wo_ref, wrh_ref, wrl_ref, hn_ref, u2_ref, lg_ref):
    h = h_ref[...]
    d = h.shape[1]
    u = _modulated_norm(h, mod_ref[0:1, :], mod_ref[1:2, :]).astype(BF16)
    gates = jax.nn.sigmoid(jnp.dot(u, wg_ref[...], preferred_element_type=F32))
    dot = lambda x, w_ref: jnp.dot(x, w_ref[...], preferred_element_type=F32)
    acc = gates[:, :d] * dot(a_ref[...], wbd_ref)
    acc += gates[:, d:2 * d] * dot(hy_ref[...].astype(BF16), wbh_ref)
    acc += gates[:, 2 * d:3 * d] * dot(cf_ref[...].astype(BF16), wbc_ref)
    acc += gates[:, 3 * d:] * dot(m_ref[...], wbm_ref)
    hn = h + mod_ref[2:3, :] * dot(acc.astype(BF16), wo_ref)
    hn_ref[...] = hn
    u2 = _modulated_norm(hn, mod_ref[3:4, :], mod_ref[4:5, :])
    u2h = u2.astype(BF16)
    u2l = (u2 - u2h.astype(F32)).astype(BF16)
    u2_ref[...] = u2h
    lg_ref[...] = dot(u2h, wrh_ref) + (dot(u2l, wrh_ref) + dot(u2h, wrl_ref))


def _merge(h, mod, a, hyv, cfv, m, wts, *, tile):
    b, n, d = h.shape
    const2 = lambda bi, ti: (0, 0)
    tok = lambda w: pl.BlockSpec((None, tile, w), lambda bi, ti: (bi, ti, 0))
    names = ('w_gate', 'w_bd', 'w_bh', 'w_bc', 'w_bm', 'w_out', 'w_rh', 'w_rl')
    in_specs = [tok(d), pl.BlockSpec((None, MOD_ROWS, d), lambda bi, ti: (bi, 0, 0)),
                tok(ATT_W), tok(HYENA_WIDTH), tok(CONF_WIDTH), tok(ATT_W)]
    in_specs += [pl.BlockSpec(wts[k].shape, const2, pipeline_mode=pl.Buffered(1)) for k in names]
    sds = jax.ShapeDtypeStruct
    return pl.pallas_call(
        _merge_kernel, grid=(b, n // tile), in_specs=in_specs,
        out_specs=(tok(d), tok(d), tok(HEAD_LANES)),
        out_shape=(sds((b, n, d), F32), sds((b, n, d), BF16), sds((b, n, HEAD_LANES), F32)),
        compiler_params=_params(2, VMEM_LIMIT_LARGE_BYTES), name="merge",
    )(h, mod, a, hyv, cfv, m, *[wts[k] for k in names])


def _merge_weights(p, lam_init):
    d = p['w_out'].shape[0]
    wb_d, wb_h, wb_c, wb_m = (w.T for w in _split(p['w_branch'].T, BRANCH_WIDTHS))
    wb_d = wb_d * (jnp.tile(p['diff_subln_g'], DIFF_HEADS) * (1.0 - lam_init))[:, None]
    wb_m = jnp.pad(wb_m.reshape(MLA_HEADS, MLA_V, d), ((0, 0), (0, HEAD_LANES - MLA_V), (0, 0))).reshape(ATT_W, d)
    w_r = jnp.pad(p['w_router'], ((0, 0), (0, HEAD_LANES - N_EXPERTS)))
    w_rh = w_r.astype(BF16)
    return dict(w_gate=p['w_in'][:, sum(IN_SPLITS):].astype(BF16), w_bd=wb_d.astype(BF16), w_bh=wb_h.astype(BF16),
                w_bc=wb_c.astype(BF16), w_bm=wb_m.astype(BF16), w_out=p['w_out'].astype(BF16),
                w_rh=w_rh, w_rl=(w_r - w_rh.astype(F32)).astype(BF16))


WINDOW_ALIGN = 16
SUB_TOKENS = 256
GATHER_WINDOW = SUB_TOKENS + WINDOW_ALIGN
GATHER_SUBS_PER_STEP = 8
COMBINE_TOKENS = 128
COMBINE_WINDOW = 256
COMBINE_SUBS_PER_STEP = 16
FFN_ROWS = 512
ROUTE_MIN_ROWS = 8


def _excl_scan(x, lane, row):
    inc = x
    s = 1
    while s < HEAD_LANES:
        inc = inc + jnp.where(lane >= s, pltpu.roll(inc, s, 2), 0.0)
        s *= 2
    tot = jnp.sum(x, axis=2, keepdims=True) + jnp.zeros_like(x)
    off = tot
    s = 1
    while s < x.shape[1]:
        off = off + jnp.where(row >= s, pltpu.roll(off, s, 1), 0.0)
        s *= 2
    return inc - x + (off - tot)


def _route_kernel(lg_ref, pos_ref, aff_ref, *, n_valid, cap):
    lg = lg_ref[...]
    shape = lg.shape
    lane = lax.broadcasted_iota(jnp.int32, shape, 2)
    row = lax.broadcasted_iota(jnp.int32, shape, 1)
    e = jnp.exp(lg - jnp.max(lg, axis=0, keepdims=True))
    aff = e / jnp.sum(e, axis=0, keepdims=True)
    bits = jnp.where(row * HEAD_LANES + lane < n_valid, pltpu.bitcast(aff, jnp.int32), -1)

    def count(mask):
        c = jnp.sum(jnp.where(mask, 1.0, 0.0), axis=2, keepdims=True)
        return jnp.sum(c, axis=1, keepdims=True)

    def step(i, thr):
        cand = thr | (jnp.int32(1) << (30 - i))
        return jnp.where(count(bits >= cand) >= cap, cand, thr)
    thr = lax.fori_loop(0, 31, step, jnp.zeros((shape[0], 1, 1), jnp.int32))
    gt = bits > thr
    eq = bits == thr
    need = cap - count(gt)
    tie_rank = _excl_scan(jnp.where(eq, 1.0, 0.0), lane, row)
    sel = gt | (eq & (tie_rank < need))
    pos = _excl_scan(jnp.where(sel, 1.0, 0.0), lane, row)
    pos_ref[...] = jnp.where(sel, pos.astype(jnp.int32), -1)
    aff_ref[...] = aff


def _route(logits, cap):
    b, n, _ = logits.shape
    rows = max(ROUTE_MIN_ROWS, n // HEAD_LANES)
    lg = jnp.swapaxes(logits[..., :N_EXPERTS], 1, 2)
    lg = jnp.pad(lg, ((0, 0), (0, 0), (0, rows * HEAD_LANES - n))).reshape(b, N_EXPERTS, rows, HEAD_LANES)
    spec = pl.BlockSpec((None, N_EXPERTS, rows, HEAD_LANES), lambda bi: (bi, 0, 0, 0))
    pos, aff = pl.pallas_call(
        functools.partial(_route_kernel, n_valid=n, cap=cap), grid=(b,), in_specs=[spec], out_specs=(spec, spec),
        out_shape=(jax.ShapeDtypeStruct(lg.shape, jnp.int32), jax.ShapeDtypeStruct(lg.shape, F32)),
        compiler_params=_params(1), name="route",
    )(lg)
    flat = lambda a: a.reshape(b, N_EXPERTS, rows * HEAD_LANES)[..., :n]
    return flat(pos), flat(aff)


def _experts_kernel(base_ref, u_ref, mod_ref, aff_ref, pos_ref, win_ref, wout_ref, ye_ref, xe_ref, *,
                    n_sub, cap):
    bi, ei, kb = pl.program_id(0), pl.program_id(1), pl.program_id(2)
    d = u_ref.shape[1]

    @pl.when(kb == 0)
    def _():
        xe_ref[...] = jnp.zeros(xe_ref.shape, F32)

    slot = lax.broadcasted_iota(jnp.int32, (GATHER_WINDOW, SUB_TOKENS), 0)
    ones = jnp.ones((SUB_TOKENS, HEAD_LANES), BF16)
    for j in range(n_sub):
        tok = slice(j * SUB_TOKENS, (j + 1) * SUB_TOKENS)
        base = pl.multiple_of(base_ref[bi, ei, kb * n_sub + j], WINDOW_ALIGN)
        match = slot == (pos_ref[:, tok] - base)
        onehot = jnp.where(match, 1.0, 0.0).astype(BF16)
        g = aff_ref[:, tok]
        g_hi = g.astype(BF16).astype(F32)
        sel_hi = jnp.where(match, g_hi, 0.0).astype(BF16)
        sel_lo = jnp.where(match, g - g_hi, 0.0).astype(BF16)
        rows = pl.ds(base, GATHER_WINDOW)
        xe_ref[rows, :d] += jnp.dot(onehot, u_ref[tok, :], preferred_element_type=F32)
        xe_ref[rows, d:d + HEAD_LANES] += jnp.dot(sel_hi, ones, preferred_element_type=F32)
        xe_ref[rows, d + HEAD_LANES:] += jnp.dot(sel_lo, ones, preferred_element_type=F32)

    @pl.when(kb == pl.num_programs(2) - 1)
    def _():
        f = wout_ref.shape[0]
        step = min(FFN_ROWS, cap)
        for r0 in range(0, cap, step):
            x = xe_ref[r0:r0 + step, :d].astype(BF16)
            gate = xe_ref[r0:r0 + step, d:d + HEAD_LANES] + xe_ref[r0:r0 + step, d + HEAD_LANES:]
            hgu = jnp.dot(x, win_ref[...], preferred_element_type=F32)
            act = (jax.nn.silu(hgu[:, :f]) * hgu[:, f:]).astype(BF16)
            y = jnp.dot(act, wout_ref[...], preferred_element_type=F32)
            scale = jnp.concatenate([gate] * (d // HEAD_LANES), axis=1) * mod_ref[5:6, :]
            ye_ref[r0:r0 + step, :] = (y * scale).astype(BF16)
        ye_ref[cap:, :] = jnp.zeros((ye_ref.shape[0] - cap, ye_ref.shape[1]), BF16)


def _combine_kernel(base_ref, h_ref, mod_ref, posn_ref, ye_ref, hn_ref, *, n_sub, final_norm):
    bi, kb, ei = pl.program_id(0), pl.program_id(1), pl.program_id(2)

    @pl.when(ei == 0)
    def _():
        hn_ref[...] = h_ref[...]

    slot = lax.broadcasted_iota(jnp.int32, (COMBINE_TOKENS, COMBINE_WINDOW), 1)
    lane_e = lax.broadcasted_iota(jnp.int32, (COMBINE_TOKENS, N_EXPERTS), 1)
    for j in range(n_sub):
        tok = slice(j * COMBINE_TOKENS, (j + 1) * COMBINE_TOKENS)
        base = pl.multiple_of(base_ref[bi, ei, kb * n_sub + j], WINDOW_ALIGN)
        rel = jnp.sum(jnp.where(lane_e == ei, posn_ref[tok, :], 0), axis=1, keepdims=True) - base
        onehot = jnp.where(slot == rel, 1.0, 0.0).astype(BF16)
        ye = ye_ref[pl.ds(base, COMBINE_WINDOW), :]
        hn_ref[tok, :] += jnp.dot(onehot, ye, preferred_element_type=F32)

    if final_norm:
        @pl.when(ei == pl.num_programs(2) - 1)
        def _():
            hn = hn_ref[...]
            hn_ref[...] = hn * lax.rsqrt(jnp.mean(hn * hn, axis=-1, keepdims=True) + EPS) * mod_ref[6:7, :]


def _expert_choice_ffn(h, mod, u2, logits, w_exp_in, w_exp_out, final_norm=False):
    b, n, d = u2.shape
    cap = max(1, EC_CAPACITY * n // N_EXPERTS)
    n_sub = min(GATHER_SUBS_PER_STEP, n // SUB_TOKENS)
    n_sub_c = min(COMBINE_SUBS_PER_STEP, n // COMBINE_TOKENS)
    big, big_c = n_sub * SUB_TOKENS, n_sub_c * COMBINE_TOKENS
    n_big, n_big_c = n // big, n // big_c
    capp = cap + GATHER_WINDOW
    pos, aff = _route(logits, cap)

    def window_starts(sub):
        cnt = jnp.sum((pos >= 0).reshape(b, N_EXPERTS, n // sub, sub), axis=-1)
        return ((jnp.cumsum(cnt, axis=-1) - cnt) // WINDOW_ALIGN * WINDOW_ALIGN).astype(jnp.int32)
    base, base_c = window_starts(SUB_TOKENS), window_starts(COMBINE_TOKENS)
    f = w_exp_out.shape[1]
    ye = pl.pallas_call(
        functools.partial(_experts_kernel, n_sub=n_sub, cap=cap),
        grid_spec=pltpu.PrefetchScalarGridSpec(
            num_scalar_prefetch=1, grid=(b, N_EXPERTS, n_big),
            in_specs=[pl.BlockSpec((None, big, d), lambda bi, ei, kb, base_r: (bi, kb, 0)),
                      pl.BlockSpec((None, MOD_ROWS, d), lambda bi, ei, kb, base_r: (bi, 0, 0)),
                      pl.BlockSpec((None, None, 1, big), lambda bi, ei, kb, base_r: (bi, ei, 0, kb)),
                      pl.BlockSpec((None, None, 1, big), lambda bi, ei, kb, base_r: (bi, ei, 0, kb)),
                      pl.BlockSpec((None, d, 2 * f), lambda bi, ei, kb, base_r: (ei, 0, 0)),
                      pl.BlockSpec((None, f, d), lambda bi, ei, kb, base_r: (ei, 0, 0))],
            out_specs=pl.BlockSpec((None, None, capp, d), lambda bi, ei, kb, base_r: (bi, ei, 0, 0)),
            scratch_shapes=[pltpu.VMEM((capp, d + 2 * HEAD_LANES), F32)]),
        out_shape=jax.ShapeDtypeStruct((b, N_EXPERTS, capp, d), BF16),
        compiler_params=_params(3, VMEM_LIMIT_LARGE_BYTES), name="experts",
    )(base, u2, mod, aff.reshape(b, N_EXPERTS, 1, n), pos.reshape(b, N_EXPERTS, 1, n), w_exp_in, w_exp_out)
    posn = jnp.swapaxes(pos, 1, 2)
    return pl.pallas_call(
        functools.partial(_combine_kernel, n_sub=n_sub_c, final_norm=final_norm),
        grid_spec=pltpu.PrefetchScalarGridSpec(
            num_scalar_prefetch=1, grid=(b, n_big_c, N_EXPERTS),
            in_specs=[pl.BlockSpec((None, big_c, d), lambda bi, kb, ei, base_r: (bi, kb, 0)),
                      pl.BlockSpec((None, MOD_ROWS, d), lambda bi, kb, ei, base_r: (bi, 0, 0)),
                      pl.BlockSpec((None, big_c, N_EXPERTS), lambda bi, kb, ei, base_r: (bi, kb, 0)),
                      pl.BlockSpec((None, None, capp, d), lambda bi, kb, ei, base_r: (bi, ei, 0, 0))],
            out_specs=pl.BlockSpec((None, big_c, d), lambda bi, kb, ei, base_r: (bi, kb, 0))),
        out_shape=jax.ShapeDtypeStruct((b, n, d), F32),
        compiler_params=_params(3, VMEM_LIMIT_LARGE_BYTES), name="combine",
    )(base_c, h, mod, posn, ye)


def _split(z, sizes):
    out, start = [], 0
    for s in sizes:
        out.append(z[..., start:start + s])
        start += s
    return out


HALO = 16


def _fill_ext(ext_ref, x_ref, prev_ref, next_ref):
    ti, nt = pl.program_id(1), pl.num_programs(1)
    tt = x_ref.shape[0]
    ext_ref[0:HALO, :] = jnp.where(ti > 0, prev_ref[...], 0.0)
    ext_ref[HALO:HALO + tt, :] = x_ref[...]
    ext_ref[HALO + tt:, :] = jnp.where(ti < nt - 1, next_ref[...], 0.0)


def _taps(ext_ref, w_ref, tt):
    k = w_ref.shape[0]
    acc = None
    for j in range(k):
        start = HALO - k // 2 + j
        term = w_ref[j:j + 1, :] * ext_ref[start:start + tt, :]
        acc = term if acc is None else acc + term
    return acc


def _short_conv_kernel(x_ref, prev_ref, next_ref, w_ref, b_ref, x1_ref, x2_ref, v_ref, ext_ref):
    _fill_ext(ext_ref, x_ref, prev_ref, next_ref)
    y = _taps(ext_ref, w_ref, x_ref.shape[0]) + b_ref[...]
    x1_ref[...] = y[:, :HYENA_WIDTH]
    x2_ref[...] = y[:, HYENA_WIDTH:2 * HYENA_WIDTH]
    v_ref[...] = y[:, 2 * HYENA_WIDTH:]


def _conformer_kernel(x_ref, prev_ref, next_ref, w_ref, g_ref, b_ref, o_ref, ext_ref):
    _fill_ext(ext_ref, x_ref, prev_ref, next_ref)
    u = _taps(ext_ref, w_ref, x_ref.shape[0])
    mu = jnp.mean(u, axis=-1, keepdims=True)
    var = jnp.mean(jnp.square(u - mu), axis=-1, keepdims=True)
    y = (u - mu) * lax.rsqrt(var + EPS) * g_ref[...] + b_ref[...]
    o_ref[...] = y * jax.nn.sigmoid(y)


def _token_conv(body, x, consts, out_widths, name):
    b, n, w = x.shape
    tt = min(1024, n)
    per = tt // HALO
    last = n // HALO - 1
    in_specs = [pl.BlockSpec((None, tt, w), lambda bi, ti: (bi, ti, 0)),
                pl.BlockSpec((None, HALO, w), lambda bi, ti: (bi, jnp.maximum(ti * per - 1, 0), 0)),
                pl.BlockSpec((None, HALO, w), lambda bi, ti: (bi, jnp.minimum((ti + 1) * per, last), 0))]
    in_specs += [pl.BlockSpec(cst.shape, lambda bi, ti: (0, 0)) for cst in consts]
    outs = tuple(jax.ShapeDtypeStruct((b, n, ow), F32) for ow in out_widths)
    out_specs = tuple(pl.BlockSpec((None, tt, ow), lambda bi, ti: (bi, ti, 0)) for ow in out_widths)
    return pl.pallas_call(body, grid=(b, n // tt), in_specs=in_specs, out_specs=out_specs, out_shape=outs,
                          scratch_shapes=[pltpu.VMEM((tt + 2 * HALO, w), F32)],
                          compiler_params=_params(2), name=name)(x, x, x, *consts)


def _conformer_branch(glu, p):
    return _token_conv(_conformer_kernel, glu, (p['conf_dw_w'], p['conf_ln_g'][None, :], p['conf_ln_b'][None, :]),
                       (CONF_WIDTH,), "conformer")[0]


FILT_LANES = 128
DFT_SHORT = 256


def _split_bf16(x):
    hi = x.astype(BF16)
    return hi, (x - hi.astype(F32)).astype(BF16)


def _dot_split(ah, al, bh, bl):
    dot = lambda u, v: jnp.dot(u, v, preferred_element_type=F32)
    return dot(ah, bh) + (dot(al, bh) + dot(ah, bl))


def _dot_const(mh, ml, x):
    xb = x.astype(BF16)
    return jnp.dot(mh, xb, preferred_element_type=F32) + jnp.dot(ml, xb, preferred_element_type=F32)


def _filter_kernel(z_ref, w1h, w1l, b1, f1, w2h, w2l, b2, f2, w3h, w3l, dl_ref, h_ref, asum_ref, *,
                   tiles_per_dir):
    z = z_ref[...]
    hid = jnp.sin(f1[...] * (_dot_split(*_split_bf16(z), w1h[...], w1l[...]) + b1[...]))
    hid = jnp.sin(f2[...] * (_dot_split(*_split_bf16(hid), w2h[...], w2l[...]) + b2[...]))
    h = _dot_split(*_split_bf16(hid), w3h[...], w3l[...])
    h = h * jnp.exp(-z[:, 0:1] * dl_ref[...])
    h_ref[...] = h

    @pl.when(pl.program_id(0) % tiles_per_dir == 0)
    def _():
        asum_ref[...] = jnp.zeros(asum_ref.shape, F32)
    asum_ref[...] += jnp.sum(jnp.abs(h), axis=0, keepdims=True)


def _normalise_kernel(h_ref, asum_ref, *o_refs, n):
    tt = h_ref.shape[0]
    row = pl.program_id(0) * tt + lax.broadcasted_iota(jnp.int32, h_ref.shape, 0)
    k = jnp.where(row == n, 0.0, h_ref[...] / asum_ref[...])
    for o, o_ref in enumerate(o_refs):
        o_ref[...] = k[:, o * HYENA_WIDTH:(o + 1) * HYENA_WIDTH]


def _hyena_taps(n, p):
    t = jnp.linspace(0.0, 1.0, n, dtype=F32)[:, None]
    bands = (FILT_EMB - 1) // 2
    w = (2.0 * math.pi / n) * jnp.arange(n, dtype=F32)[:, None]
    f = jnp.linspace(1e-4, bands - 1, bands, dtype=F32)[None, :]
    t2, w2pos = jnp.concatenate([t, t[::-1]], axis=0), jnp.concatenate([w, w[::-1]], axis=0)
    z2 = jnp.concatenate([t2, jnp.cos(f * w2pos), -jnp.sin(f * w2pos),
                          jnp.zeros((2 * n, FILT_LANES - FILT_EMB), F32)], axis=-1)
    padc = lambda a: jnp.pad(a, ((0, 0), (0, FILT_LANES - a.shape[1])))
    padr = lambda a: jnp.pad(a, ((0, FILT_LANES - a.shape[0]), (0, 0)))
    w1, w2, w3 = padc(padr(p['filt_w1'])), padc(padr(p['filt_w2'])), padr(p['filt_w3'])
    b1, b2 = padc(p['filt_b1'][None, :]), padc(p['filt_b2'][None, :])
    f1, f2 = padc(p['filt_freq'][0][None, :]), padc(p['filt_freq'][1][None, :])
    deltas = jnp.abs(jnp.linspace(math.log(DECAY_TARGET) / SLOW_DECAY, math.log(DECAY_TARGET) / FAST_DECAY,
                                  HYENA_WIDTH, dtype=F32))
    width = HYENA_ORDER * HYENA_WIDTH
    w3 = w3.reshape(FILT_LANES, HYENA_ORDER, 2, HYENA_WIDTH).transpose(2, 0, 1, 3).reshape(2, FILT_LANES, width)
    dl = jnp.tile(deltas, HYENA_ORDER)[None, :]
    w3h, w3l = _split_bf16(w3)
    tt = min(1024, n)
    tiles_per_dir = n // tt
    cspec = lambda a: pl.BlockSpec(a.shape, lambda i: (0, 0))
    dirspec = lambda rows: pl.BlockSpec((None, rows, width), lambda i: (i // tiles_per_dir, 0, 0))
    tile = lambda w: pl.BlockSpec((tt, w), lambda i: (i, 0))
    small = [*_split_bf16(w1), b1, f1, *_split_bf16(w2), b2, f2]
    h_raw, asum = pl.pallas_call(
        functools.partial(_filter_kernel, tiles_per_dir=tiles_per_dir), grid=(2 * tiles_per_dir,),
        in_specs=[tile(FILT_LANES)] + [cspec(a) for a in small] + [dirspec(FILT_LANES), dirspec(FILT_LANES), cspec(dl)],
        out_specs=(tile(width), dirspec(1)),
        out_shape=(jax.ShapeDtypeStruct((2 * n, width), F32), jax.ShapeDtypeStruct((2, 1, width), F32)),
        compiler_params=_params(1), name="hyena_filter_mlp")(z2, *small, w3h, w3l, dl)
    return pl.pallas_call(
        functools.partial(_normalise_kernel, n=n), grid=(2 * tiles_per_dir,),
        in_specs=[tile(width), dirspec(1)],
        out_specs=tuple(tile(HYENA_WIDTH) for _ in range(HYENA_ORDER)),
        out_shape=tuple(jax.ShapeDtypeStruct((2 * n, HYENA_WIDTH), F32) for _ in range(HYENA_ORDER)),
        compiler_params=_params(1), name="hyena_filter_norm",
    )(h_raw, asum)


def _dft_tables(n):
    n2 = DFT_SHORT if n >= 4 * DFT_SHORT else n
    n1 = n // n2

    def cis(idx):
        ang = (-2.0 * math.pi / n) * idx.astype(F32)
        return jnp.cos(ang), jnp.sin(ang)
    k2 = jnp.arange(n2)
    fr, fi = cis((k2[:, None] * k2[None, :]) % n2 * n1)
    tabs = dict(n1=n1, n2=n2)
    tabs['f_hi'], tabs['f_lo'] = _split_bf16(jnp.stack([fr, fi]))
    k1 = jnp.arange(n1)
    tr, ti = cis(k1[:, None] * k2[None, :])
    tabs['tw'] = jnp.broadcast_to(jnp.stack([tr, ti], axis=1)[..., None], (n1, 2, n2, HEAD_LANES))
    if n1 > 1:
        gr, gi = cis((k1[:, None] * k1[None, :]) % n1 * n2)
        half = n1 // 2
        grh, gih = gr[:, :half], gi[:, :half]
        tabs['m_fwd'] = _split_bf16(jnp.block([[grh, -gih], [gih, grh]]))
        tabs['m_real'] = _split_bf16(jnp.concatenate([gr, gi], axis=0))
        tabs['m_inv'] = _split_bf16(jnp.block([[grh.T, gih.T], [-gih.T, grh.T]]))
    return tabs


SLABS_PER_STEP = 16


def _rowmix_slabs_kernel(mh_ref, ml_ref, x_ref, o_ref):
    for j in range(x_ref.shape[1]):
        o_ref[:, j, :] = _dot_const(mh_ref[...], ml_ref[...], x_ref[:, j, :])


def _rowmix_slabs(m, x):
    mh, ml = m
    rin, n2, c = x.shape
    nb = min(SLABS_PER_STEP, n2)
    return pl.pallas_call(
        _rowmix_slabs_kernel, grid=(n2 // nb,),
        in_specs=[pl.BlockSpec(mh.shape, lambda i: (0, 0)), pl.BlockSpec(ml.shape, lambda i: (0, 0)),
                  pl.BlockSpec((rin, nb, c), lambda i: (0, i, 0))],
        out_specs=pl.BlockSpec((mh.shape[0], nb, c), lambda i: (0, i, 0)),
        out_shape=jax.ShapeDtypeStruct((mh.shape[0], n2, c), F32), compiler_params=_params(1),
        name="dft_rowmix_slabs",
    )(mh, ml, x)


K1_PER_STEP = 4


def _spectral_kernel(x_ref, tw_ref, fh_ref, fl_ref, k_ref, o_ref, *, conv):
    frh, fih, frl, fil = fh_ref[0], fh_ref[1], fl_ref[0], fl_ref[1]

    def dft(ar, ai, conj):
        rr, ii = _dot_const(frh, frl, ar), _dot_const(fih, fil, ai)
        ri, ir = _dot_const(frh, frl, ai), _dot_const(fih, fil, ar)
        return (rr + ii, ri - ir) if conj else (rr - ii, ri + ir)

    for q in range(x_ref.shape[1]):
        xr, xi = x_ref[0, q], x_ref[1, q]
        reps = xr.shape[1] // HEAD_LANES
        tr = jnp.concatenate([tw_ref[q, 0]] * reps, axis=1)
        ti = jnp.concatenate([tw_ref[q, 1]] * reps, axis=1)
        yr, yi = dft(xr * tr - xi * ti, xr * ti + xi * tr, False)
        if not conv:
            o_ref[0, q] = yr * k_ref[...]
            o_ref[1, q] = yi * k_ref[...]
            continue
        kr, ki = k_ref[0, q], k_ref[1, q]
        cr, ci = dft(yr * kr - yi * ki, yr * ki + yi * kr, True)
        o_ref[0, q] = cr * tr + ci * ti
        o_ref[1, q] = ci * tr - cr * ti


def _spectral(x, k, tabs, conv):
    _, n1, n2, c = x.shape
    kb = min(K1_PER_STEP, n1)
    slab = pl.BlockSpec((2, kb, n2, c), lambda i: (0, i, 0, 0))
    kspec = slab if conv else pl.BlockSpec(k.shape, lambda i: (0, 0))
    return pl.pallas_call(
        functools.partial(_spectral_kernel, conv=conv), grid=(n1 // kb,),
        in_specs=[slab, pl.BlockSpec((kb, 2, n2, HEAD_LANES), lambda i: (i, 0, 0, 0)),
                  pl.BlockSpec(tabs['f_hi'].shape, lambda i: (0, 0, 0)),
                  pl.BlockSpec(tabs['f_lo'].shape, lambda i: (0, 0, 0)), kspec],
        out_specs=slab, out_shape=jax.ShapeDtypeStruct(x.shape, F32), compiler_params=_params(1),
        name="dft_spectral_conv" if conv else "dft_spectral_filter",
    )(x, tabs['tw'], tabs['f_hi'], tabs['f_lo'], k)


def _filter_spectrum(k, tabs):
    n, c = k.shape
    n1, n2 = tabs['n1'], tabs['n2']
    if n1 > 1:
        x = _rowmix_slabs(tabs['m_real'], k.reshape(n1, n2, c)).reshape(2, n1, n2, c)
    else:
        x = jnp.stack([k, jnp.zeros_like(k)]).reshape(2, 1, n2, c)
    return _spectral(x, jnp.full((1, c), 1.0 / n, F32), tabs, conv=False)


def _rowmix_gate_kernel(mh_ref, ml_ref, x_ref, g_ref, v_ref, s_ref, o_ref):
    o_ref[...] = g_ref[...] * (_dot_const(mh_ref[...], ml_ref[...], x_ref[...]) + s_ref[...] * v_ref[...])


def _gated_long_conv(gate, v, kf, skip, tabs):
    b, n, c = v.shape
    assert b == 2
    n1, n2 = tabs['n1'], tabs['n2']
    if n1 == 1:
        x = jnp.concatenate([v, jnp.zeros_like(v)], axis=1).reshape(2, 1, n2, c)
        y = _spectral(x, kf, tabs, conv=True).reshape(2, n2, c)[:, :n]
        return _hyena_gate(gate, y, v, skip)
    x = _rowmix_slabs(tabs['m_fwd'], v.reshape(n1, n2, c)).reshape(2, n1, n2, c)
    y = _spectral(x, kf, tabs, conv=True).reshape(2 * n1, n2 * c)
    mh, ml = tabs['m_inv']
    cols = n2 * c
    ct = min(2048, cols)
    tile = lambda rows: pl.BlockSpec((rows, ct), lambda i: (0, i))
    return pl.pallas_call(
        _rowmix_gate_kernel, grid=(cols // ct,),
        in_specs=[pl.BlockSpec(mh.shape, lambda i: (0, 0)), pl.BlockSpec(ml.shape, lambda i: (0, 0)),
                  tile(2 * n1), tile(n1), tile(n1), pl.BlockSpec((1, ct), lambda i: (0, 0))],
        out_specs=tile(n1), out_shape=jax.ShapeDtypeStruct((n1, cols), F32), compiler_params=_params(1),
        name="dft_rowmix_gate",
    )(mh, ml, y, gate.reshape(n1, cols), v.reshape(n1, cols), jnp.tile(skip, (1, ct // c))).reshape(2, n, c)


def _gate_kernel(g_ref, y_ref, v_ref, s_ref, o_ref):
    o_ref[...] = g_ref[...] * (y_ref[...] + s_ref[...] * v_ref[...])


def _hyena_gate(gate, y, v, skip):
    b, n, c = v.shape
    tt = min(2048, n)
    tok = pl.BlockSpec((None, tt, c), lambda bi, ti: (bi, ti, 0))
    return pl.pallas_call(_gate_kernel, grid=(b, n // tt),
                          in_specs=[tok, tok, tok, pl.BlockSpec((1, c), lambda bi, ti: (0, 0))], out_specs=tok,
                          out_shape=jax.ShapeDtypeStruct(v.shape, F32), compiler_params=_params(2),
                          name="hyena_gate")(gate, y, v, skip)


def _hyena_branch(hy, p, tabs):
    n = hy.shape[1]
    x1, x2, v = _token_conv(_short_conv_kernel, hy, (p['hyena_short_w'], p['hyena_short_b'][None, :]),
                            (HYENA_WIDTH,) * 3, "hyena_short_conv")
    taps = _hyena_taps(n, p)
    for o, gate in enumerate((x1, x2)):
        v = _gated_long_conv(gate, v, _filter_spectrum(taps[o], tabs), p['hyena_skip'][o][None, :], tabs)
    return v


def _adaln_kernel(c_ref, w_ref, b_ref, o_ref):
    s = c_ref[...]
    s = s * jax.nn.sigmoid(s)
    o_ref[...] = _dot_split(*_split_bf16(s), *_split_bf16(w_ref[...])) + b_ref[...]


def _adaln(cond, w, b):
    d, width = w.shape
    ct = width // 6
    return pl.pallas_call(
        _adaln_kernel, grid=(6,),
        in_specs=[pl.BlockSpec(cond.shape, lambda i: (0, 0)), pl.BlockSpec((d, ct), lambda i: (0, i)),
                  pl.BlockSpec((1, ct), lambda i: (0, i))],
        out_specs=pl.BlockSpec((cond.shape[0], ct), lambda i: (0, i)),
        out_shape=jax.ShapeDtypeStruct((cond.shape[0], width), F32), compiler_params=_params(1), name="adaln",
    )(cond, w, b[None, :])


def _mod_rows(mod, norm_mix_g, norm_ffn_g, final_g, batch):
    sh1, sc1, g1, sh2, sc2, g2 = jnp.split(mod, 6, axis=-1)
    rows = jnp.stack([norm_mix_g * (1.0 + sc1), sh1, g1, norm_ffn_g * (1.0 + sc2), sh2, g2,
                      jnp.broadcast_to(final_g, g1.shape), jnp.zeros_like(g1)], axis=1)
    return jnp.broadcast_to(rows, (batch,) + rows.shape[1:])


def kernel(x, c, ctx, c_ctx, ada_w, ada_b, norm_mix_g, norm_ffn_g, w_in, diff_lambda, diff_subln_g, hyena_short_w, hyena_short_b, filt_w1, filt_b1, filt_freq, filt_w2, filt_b2, filt_w3, hyena_skip, conf_dw_w, conf_ln_g, conf_ln_b, mla_q_norm_g, mla_kv_norm_g, mla_w_uq, mla_w_ukv, w_branch, w_out, w_router, w_exp_in, w_exp_out, final_norm_g):
    depth = w_in.shape[0]
    batch, n_lat, d = x.shape
    n_ctx = ctx.shape[1]
    rope_lat = _rope_operands(n_lat, identity=False)
    rope_ctx = _rope_operands(n_ctx, identity=True)
    dft_lat, dft_ctx = _dft_tables(2 * n_lat), _dft_tables(2 * n_ctx)
    cond = jnp.concatenate([c, c_ctx[None], jnp.zeros((MOD_ROWS - batch - 1, d), F32)], axis=0)
    tile_lat, tile_ctx = min(512, n_lat), min(256, n_ctx)
    h_lat, h_ctx = x, ctx
    for l in range(depth):
        last = l == depth - 1
        p = dict(w_in=w_in[l], diff_subln_g=diff_subln_g[l], hyena_short_w=hyena_short_w[l],
                 hyena_short_b=hyena_short_b[l], filt_w1=filt_w1[l], filt_b1=filt_b1[l], filt_freq=filt_freq[l],
                 filt_w2=filt_w2[l], filt_b2=filt_b2[l], filt_w3=filt_w3[l], hyena_skip=hyena_skip[l],
                 conf_dw_w=conf_dw_w[l], conf_ln_g=conf_ln_g[l], conf_ln_b=conf_ln_b[l],
                 mla_q_norm_g=mla_q_norm_g[l], mla_kv_norm_g=mla_kv_norm_g[l], mla_w_uq=mla_w_uq[l],
                 mla_w_ukv=mla_w_ukv[l], w_branch=w_branch[l], w_out=w_out[l], w_router=w_router[l],
                 w_exp_in=w_exp_in[l], w_exp_out=w_exp_out[l])
        ada = _adaln(cond, ada_w[l], ada_b[l])
        mod_lat = _mod_rows(ada[:batch], norm_mix_g[l], norm_ffn_g[l], final_norm_g, batch)
        mod_ctx = _mod_rows(ada[batch:batch + 1], norm_mix_g[l], norm_ffn_g[l], final_norm_g, batch)
        lam_init = 0.8 - 0.6 * math.exp(-0.3 * l)
        lq1, lk1, lq2, lk2 = diff_lambda[l].astype(F32)
        lam = jnp.reshape(jnp.exp(jnp.sum(lq1 * lk1)) - jnp.exp(jnp.sum(lq2 * lk2)) + lam_init, (1,))
        w_inp, w_mrg = _inproj_weights(p), _merge_weights(p, lam_init)

        qdT_l, kd_l, vdT_l, qmT_l, km_l, vmT_l, hy_l, glu_l = _inproj(h_lat, mod_lat, w_inp, rope_lat, tile=tile_lat)
        qdT_c, kd_c, vdT_c, qmT_c, km_c, vmT_c, hy_c, glu_c = _inproj(h_ctx, mod_ctx, w_inp, rope_ctx, tile=tile_ctx)
        a_lat = _flash_attention(lam, qdT_l, kd_c, vdT_c, kd_l, vdT_l, n_maps=2, tq=min(512, n_lat))
        m_lat = _flash_attention(lam, qmT_l, km_c, vmT_c, km_l, vmT_l, n_maps=1, tq=min(1024, n_lat))
        h_lat, u2_lat, lg_lat = _merge(h_lat, mod_lat, a_lat, _hyena_branch(hy_l, p, dft_lat), _conformer_branch(glu_l, p),
                                       m_lat, w_mrg, tile=min(256, n_lat))
        w_ei, w_eo = p['w_exp_in'].astype(BF16), p['w_exp_out'].astype(BF16)
        h_lat = _expert_choice_ffn(h_lat, mod_lat, u2_lat, lg_lat, w_ei, w_eo, final_norm=last)
        if not last:
            a_ctx = _flash_attention(lam, qdT_c, kd_c, vdT_c, None, None, n_maps=2, tq=n_ctx)
            m_ctx = _flash_attention(lam, qmT_c, km_c, vmT_c, None, None, n_maps=1, tq=n_ctx)
            h_ctx, u2_ctx, lg_ctx = _merge(h_ctx, mod_ctx, a_ctx, _hyena_branch(hy_c, p, dft_ctx),
                                           _conformer_branch(glu_c, p), m_ctx, w_mrg, tile=tile_ctx)
            h_ctx = _expert_choice_ffn(h_ctx, mod_ctx, u2_ctx, lg_ctx, w_ei, w_eo)
    return h_lat
```

```python
import functools
import math

import jax
import jax.numpy as jnp
from jax import lax
from jax.experimental import pallas as pl
from jax.experimental.pallas import tpu as pltpu

GRID_W = 64
ROPE_BASE = 10000.0
EPS = 1e-6

DIFF_HEADS = 4
DIFF_HEAD_DIM = 64
DIFF_V_DIM = 2 * DIFF_HEAD_DIM
HYENA_WIDTH = 256
HYENA_ORDER = 2
FILT_EMB = 33
DECAY_TARGET = 1e-2
FAST_DECAY = 0.3
SLOW_DECAY = 1.5
CONF_WIDTH = 256
MLA_HEADS = 4
MLA_Q_RANK = 256
MLA_KV_RANK = 128
MLA_NOPE = 64
MLA_ROPE = 32
MLA_V = 64
MLA_SCALE = (MLA_NOPE + MLA_ROPE) ** -0.5
N_EXPERTS = 16
EC_CAPACITY = 2

DIFF_QK_W = DIFF_HEADS * 2 * DIFF_HEAD_DIM
DIFF_V_W = DIFF_HEADS * DIFF_V_DIM
HYENA_PROJ = (HYENA_ORDER + 1) * HYENA_WIDTH
CONF_PROJ = 2 * CONF_WIDTH
IN_SPLITS = (DIFF_QK_W, DIFF_QK_W, DIFF_V_W, HYENA_PROJ, CONF_PROJ, MLA_Q_RANK, MLA_KV_RANK, MLA_ROPE)
BRANCH_WIDTHS = (DIFF_V_W, HYENA_WIDTH, CONF_WIDTH, MLA_HEADS * MLA_V)

HEAD_LANES = 128
DIFF_V_PAD = 16
ATT_W = DIFF_HEADS * HEAD_LANES
LOG2E = 1.4426950408889634
V7X_VMEM_BYTES = 64 * 1024 * 1024
VMEM_LIMIT_BYTES = V7X_VMEM_BYTES * 3 // 4
VMEM_LIMIT_LARGE_BYTES = V7X_VMEM_BYTES * 7 // 8
MOD_ROWS = 8

F32 = jnp.float32
BF16 = jnp.bfloat16
_NT = (((1,), (1,)), ((), ()))


def _params(n_axes, vmem=VMEM_LIMIT_BYTES):
    return pltpu.CompilerParams(dimension_semantics=("arbitrary",) * n_axes, vmem_limit_bytes=vmem)


def _stack_queries(q2_ref, qT, n_maps):
    tq = qT.shape[1]
    if n_maps == 2:
        row = lax.broadcasted_iota(jnp.int32, qT.shape, 0)
        zero = jnp.zeros_like(qT)
        q2_ref[:, :tq] = jnp.where(row < DIFF_HEAD_DIM, qT, zero)
        q2_ref[:, tq:] = jnp.where(row >= DIFF_HEAD_DIM, qT, zero)
    else:
        q2_ref[...] = qT


def _reset(m_ref, acc_ref):
    m_ref[...] = jnp.full(m_ref.shape, -jnp.inf, F32)
    acc_ref[...] = jnp.zeros(acc_ref.shape, F32)


def _absorb(s, vT, m_ref, acc_ref):
    m_prev = m_ref[...]
    m_new = jnp.maximum(m_prev, jnp.max(s, axis=0, keepdims=True))
    alpha = jnp.exp2(m_prev - m_new)
    p = jnp.exp2(s - m_new).astype(BF16)
    acc_ref[...] = alpha * acc_ref[...] + jnp.dot(vT, p, preferred_element_type=F32)
    m_ref[...] = m_new


def _attention_rows(acc_ref, lam_ref, n_maps, tq, sum_row):
    o = acc_ref[0:HEAD_LANES, :] / acc_ref[sum_row:sum_row + 1, :]
    if n_maps == 2:
        o = o[:, :tq] - lam_ref[0] * o[:, tq:]
        o = o * lax.rsqrt(jnp.mean(o * o, axis=0, keepdims=True) + EPS)
    return o.T.astype(BF16)


def _flash_ctx_kernel(lam_ref, qT_ref, kc_ref, vcT_ref, o_ref, acc_ref, m_ref, q2_ref, *, n_maps, sum_row):
    _stack_queries(q2_ref, qT_ref[...], n_maps)
    _reset(m_ref, acc_ref)
    _absorb(jnp.dot(kc_ref[...], q2_ref[...], preferred_element_type=F32), vcT_ref[...], m_ref, acc_ref)
    o_ref[...] = _attention_rows(acc_ref, lam_ref, n_maps, qT_ref.shape[1], sum_row)


def _flash_stream_kernel(lam_ref, qT_ref, kc_ref, vcT_ref, kl_ref, vlT_ref, o_ref, acc_ref, m_ref, q2_ref, sc_ref,
                         s_ref, *, n_maps, n_lat_chunks, tk, tq, sum_row):
    n_q = qT_ref.shape[1] // tq

    def load_queries(qi):
        _stack_queries(q2_ref, qT_ref[:, pl.ds(pl.multiple_of(qi * tq, tq), tq)], n_maps)

    def scores(k):
        return jnp.dot(k, q2_ref[...], preferred_element_type=F32)

    def absorb(s, vT):
        _absorb(s, vT, m_ref, acc_ref)

    load_queries(0)
    sc_ref[...] = scores(kc_ref[...])

    def query_block(qi, carry):
        _reset(m_ref, acc_ref)

        def chunk(c):
            return pl.ds(c * tk if isinstance(c, int) else pl.multiple_of(c * tk, tk), tk)

        s_ref[0] = scores(kl_ref[chunk(0), :])
        absorb(sc_ref[...], vcT_ref[...])

        def pair(j, inner):
            c = 2 * j
            s_ref[1] = scores(kl_ref[chunk(c + 1), :])
            absorb(s_ref[0], vlT_ref[:, chunk(c)])
            s_ref[0] = scores(kl_ref[chunk(c + 2), :])
            absorb(s_ref[1], vlT_ref[:, chunk(c + 1)])
            return inner
        lax.fori_loop(0, n_lat_chunks // 2 - 1, pair, 0)
        s_ref[1] = scores(kl_ref[chunk(n_lat_chunks - 1), :])
        absorb(s_ref[0], vlT_ref[:, chunk(n_lat_chunks - 2)])
        load_queries(jnp.minimum(qi + 1, n_q - 1))
        sc_ref[...] = scores(kc_ref[...])
        absorb(s_ref[1], vlT_ref[:, chunk(n_lat_chunks - 1)])
        o_ref[pl.ds(pl.multiple_of(qi * tq, tq), tq), :] = _attention_rows(acc_ref, lam_ref, n_maps, tq, sum_row)
        return carry
    lax.fori_loop(0, n_q, query_block, 0)


def _flash_attention(lam, qT, kc, vcT, kl, vlT, *, n_maps, tq):
    b, _, s = qT.shape
    lc = kc.shape[1]
    mv = vcT.shape[2]
    sum_row = HEAD_LANES if n_maps == 2 else MLA_V
    r = n_maps * tq
    if kl is not None:
        sl = kl.shape[1]
        tk = _lat_chunk(sl)
        n_lat_chunks = sl // tk
        assert n_lat_chunks % 2 == 0 and n_lat_chunks * tk == sl and s % tq == 0
        return pl.pallas_call(
            functools.partial(_flash_stream_kernel, n_maps=n_maps, n_lat_chunks=n_lat_chunks, tk=tk, tq=tq,
                              sum_row=sum_row),
            grid=(b, DIFF_HEADS),
            in_specs=[pl.BlockSpec(memory_space=pltpu.SMEM),
                      pl.BlockSpec((None, HEAD_LANES, s), lambda bi, hi: (bi, hi, 0)),
                      pl.BlockSpec((None, lc, HEAD_LANES), lambda bi, hi: (bi, 0, hi)),
                      pl.BlockSpec((None, None, mv, lc), lambda bi, hi: (bi, hi, 0, 0)),
                      pl.BlockSpec((None, sl, HEAD_LANES), lambda bi, hi: (bi, 0, hi)),
                      pl.BlockSpec((None, None, mv, sl), lambda bi, hi: (bi, hi, 0, 0))],
            out_specs=pl.BlockSpec((None, s, HEAD_LANES), lambda bi, hi: (bi, 0, hi)),
            out_shape=jax.ShapeDtypeStruct((b, s, ATT_W), BF16),
            scratch_shapes=[pltpu.VMEM((mv, r), F32), pltpu.VMEM((1, r), F32), pltpu.VMEM((HEAD_LANES, r), BF16),
                            pltpu.VMEM((lc, r), F32), pltpu.VMEM((2, tk, r), F32)],
            compiler_params=_params(2, VMEM_LIMIT_LARGE_BYTES),
            name=f"flash_attention_{n_maps}map",
        )(lam, qT, kc, vcT, kl, vlT)
    assert s == tq
    return pl.pallas_call(
        functools.partial(_flash_ctx_kernel, n_maps=n_maps, sum_row=sum_row),
        grid=(b, DIFF_HEADS),
        in_specs=[pl.BlockSpec(memory_space=pltpu.SMEM),
                  pl.BlockSpec((None, HEAD_LANES, s), lambda bi, hi: (bi, hi, 0)),
                  pl.BlockSpec((None, lc, HEAD_LANES), lambda bi, hi: (bi, 0, hi)),
                  pl.BlockSpec((None, None, mv, lc), lambda bi, hi: (bi, hi, 0, 0))],
        out_specs=pl.BlockSpec((None, s, HEAD_LANES), lambda bi, hi: (bi, 0, hi)),
        out_shape=jax.ShapeDtypeStruct((b, s, ATT_W), BF16),
        scratch_shapes=[pltpu.VMEM((mv, r), F32), pltpu.VMEM((1, r), F32), pltpu.VMEM((HEAD_LANES, r), BF16)],
        compiler_params=_params(2),
        name=f"flash_attention_ctx_{n_maps}map",
    )(lam, qT, kc, vcT)


def _lat_chunk(s):
    return min(1024, s // 2)


W_NAT_SPLITS = (DIFF_QK_W, HYENA_PROJ, CONF_PROJ, MLA_Q_RANK, MLA_KV_RANK, HEAD_LANES)


def _modulated_norm(h, a, shift):
    return h * lax.rsqrt(jnp.mean(h * h, axis=-1, keepdims=True) + EPS) * a + shift


def _rope_lanes(x, tab_ref, shift):
    return (x * tab_ref[0] + pltpu.roll(x, shift, 1) * tab_ref[1]
            + pltpu.roll(x, HEAD_LANES - shift, 1) * tab_ref[2])


def _inproj_kernel(h_ref, mod_ref, wnat_ref, wT_ref, wuqT_ref, wukvk_ref, wuvT_ref, gq_ref, gkv_ref,
                   ropeT_d_ref, rope_kd_ref, ropeT_m_ref, rope_km_ref,
                   qdT_ref, kd_ref, vdT_ref, qmT_ref, km_ref, vmT_ref, hy_ref, glu_ref):
    u = _modulated_norm(h_ref[...], mod_ref[0:1, :], mod_ref[1:2, :]).astype(BF16)
    z = jnp.dot(u, wnat_ref[...], preferred_element_type=F32)
    zT = lax.dot_general(wT_ref[...], u, _NT, preferred_element_type=F32)
    offs = [0]
    for w in W_NAT_SPLITS:
        offs.append(offs[-1] + w)
    dk, hy, cf, cq, ckv, krp = (z[:, offs[i]:offs[i + 1]] for i in range(len(W_NAT_SPLITS)))

    for hd in range(DIFF_HEADS):
        sl = slice(hd * HEAD_LANES, (hd + 1) * HEAD_LANES)
        kd_ref[:, sl] = _rope_lanes(dk[:, sl], rope_kd_ref, DIFF_HEAD_DIM // 2).astype(BF16)
    cos_d, sin_d = ropeT_d_ref[0], ropeT_d_ref[1]
    half = DIFF_HEAD_DIM // 2
    for g in range(2 * DIFF_HEADS):
        x1 = zT[g * DIFF_HEAD_DIM:g * DIFF_HEAD_DIM + half]
        x2 = zT[g * DIFF_HEAD_DIM + half:(g + 1) * DIFF_HEAD_DIM]
        qdT_ref[g * DIFF_HEAD_DIM:g * DIFF_HEAD_DIM + half, :] = (x1 * cos_d - x2 * sin_d).astype(BF16)
        qdT_ref[g * DIFF_HEAD_DIM + half:(g + 1) * DIFF_HEAD_DIM, :] = (x1 * sin_d + x2 * cos_d).astype(BF16)
    tail = jnp.where(lax.broadcasted_iota(jnp.int32, (DIFF_V_PAD, zT.shape[1]), 0) == 0, 1.0, 0.0).astype(BF16)
    for hd in range(DIFF_HEADS):
        r0 = DIFF_QK_W + hd * DIFF_V_DIM
        vdT_ref[hd, 0:DIFF_V_DIM, :] = zT[r0:r0 + DIFF_V_DIM].astype(BF16)
        vdT_ref[hd, DIFF_V_DIM:, :] = tail

    hy_ref[...] = hy
    glu_ref[...] = cf[:, :CONF_WIDTH] * jax.nn.sigmoid(cf[:, CONF_WIDTH:])

    cqn = (cq * lax.rsqrt(jnp.mean(cq * cq, axis=-1, keepdims=True) + EPS) * gq_ref[...]).astype(BF16)
    ckvn = (ckv * lax.rsqrt(jnp.mean(ckv * ckv, axis=-1, keepdims=True) + EPS) * gkv_ref[...]).astype(BF16)
    qT = lax.dot_general(wuqT_ref[...], cqn, _NT, preferred_element_type=F32)
    cos_m, sin_m = ropeT_m_ref[0], ropeT_m_ref[1]
    hr = MLA_ROPE // 2
    for hd in range(MLA_HEADS):
        base = hd * HEAD_LANES
        r1 = base + MLA_NOPE
        x1, x2 = qT[r1:r1 + hr], qT[r1 + hr:r1 + 2 * hr]
        qmT_ref[base:r1, :] = qT[base:r1].astype(BF16)
        qmT_ref[r1:r1 + hr, :] = (x1 * cos_m - x2 * sin_m).astype(BF16)
        qmT_ref[r1 + hr:r1 + 2 * hr, :] = (x1 * sin_m + x2 * cos_m).astype(BF16)
        qmT_ref[r1 + 2 * hr:base + HEAD_LANES, :] = jnp.zeros((HEAD_LANES - MLA_NOPE - MLA_ROPE, qT.shape[1]), BF16)
    kn = jnp.dot(ckvn, wukvk_ref[...], preferred_element_type=F32)
    kr = _rope_lanes(krp, rope_km_ref, hr)
    for hd in range(MLA_HEADS):
        sl = slice(hd * HEAD_LANES, (hd + 1) * HEAD_LANES)
        km_ref[:, sl] = (kn[:, sl] + kr).astype(BF16)
    vT = lax.dot_general(wuvT_ref[...], ckvn, _NT, preferred_element_type=F32)
    ones_row = lax.broadcasted_iota(jnp.int32, vT.shape, 0) % HEAD_LANES == MLA_V
    vT = jnp.where(ones_row, 1.0, vT).astype(BF16)
    for hd in range(MLA_HEADS):
        vmT_ref[hd] = vT[hd * HEAD_LANES:(hd + 1) * HEAD_LANES]


def _inproj(h, mod, wts, rope, *, tile):
    b, n, d = h.shape
    const2 = lambda bi, ti: (0, 0)
    tok = lambda w: pl.BlockSpec((None, tile, w), lambda bi, ti: (bi, ti, 0))
    tokT = lambda w: pl.BlockSpec((None, w, tile), lambda bi, ti: (bi, 0, ti))
    full = lambda a: pl.BlockSpec(a.shape, const2)
    in_specs = [tok(d), pl.BlockSpec((None, MOD_ROWS, d), lambda bi, ti: (bi, 0, 0))]
    in_specs += [full(wts[k]) for k in ('w_nat', 'w_T', 'w_uqT', 'w_ukvk', 'w_uvT', 'gq', 'gkv')]
    in_specs += [pl.BlockSpec((2, DIFF_HEAD_DIM // 2, tile), lambda bi, ti: (0, 0, ti)),
                 pl.BlockSpec((3, tile, HEAD_LANES), lambda bi, ti: (0, ti, 0)),
                 pl.BlockSpec((2, MLA_ROPE // 2, tile), lambda bi, ti: (0, 0, ti)),
                 pl.BlockSpec((3, tile, HEAD_LANES), lambda bi, ti: (0, ti, 0))]
    sds = jax.ShapeDtypeStruct
    vrows_d, vrows_m = DIFF_V_DIM + DIFF_V_PAD, HEAD_LANES
    headsT = lambda rows: pl.BlockSpec((None, DIFF_HEADS, rows, tile), lambda bi, ti: (bi, 0, 0, ti))
    out_shape = (sds((b, ATT_W, n), BF16), sds((b, n, ATT_W), BF16), sds((b, DIFF_HEADS, vrows_d, n), BF16),
                 sds((b, ATT_W, n), BF16), sds((b, n, ATT_W), BF16), sds((b, MLA_HEADS, vrows_m, n), BF16),
                 sds((b, n, HYENA_PROJ), F32), sds((b, n, CONF_WIDTH), F32))
    out_specs = (tokT(ATT_W), tok(ATT_W), headsT(vrows_d), tokT(ATT_W), tok(ATT_W), headsT(vrows_m),
                 tok(HYENA_PROJ), tok(CONF_WIDTH))
    return pl.pallas_call(
        _inproj_kernel, grid=(b, n // tile), in_specs=in_specs, out_specs=out_specs, out_shape=out_shape,
        compiler_params=_params(2), name="inproj",
    )(h, mod, wts['w_nat'], wts['w_T'], wts['w_uqT'], wts['w_ukvk'], wts['w_uvT'], wts['gq'], wts['gkv'],
      rope['T_d'], rope['k_d'], rope['T_m'], rope['k_m'])


def _pad_heads(w, width):
    rows = w.shape[0]
    w = w.reshape(rows, MLA_HEADS, width)
    return jnp.pad(w, ((0, 0), (0, 0), (0, HEAD_LANES - width))).reshape(rows, ATT_W)


def _inproj_weights(p):
    d = p['w_in'].shape[0]
    dq, dk, dv, hy, cf, cq, ckv, kr = _split(p['w_in'][:, :sum(IN_SPLITS)], IN_SPLITS)
    krp = jnp.zeros((d, HEAD_LANES), F32).at[:, MLA_NOPE:MLA_NOPE + MLA_ROPE].set(kr)
    w_ukv = p['mla_w_ukv'].reshape(MLA_KV_RANK, MLA_HEADS, MLA_NOPE + MLA_V)
    return dict(
        w_nat=jnp.concatenate([dk, hy, cf, cq, ckv, krp], axis=1).astype(BF16),
        w_T=jnp.concatenate([dq * (DIFF_HEAD_DIM ** -0.5 * LOG2E), dv], axis=1).T.astype(BF16),
        w_uqT=_pad_heads(p['mla_w_uq'] * (MLA_SCALE * LOG2E), MLA_NOPE + MLA_ROPE).T.astype(BF16),
        w_ukvk=_pad_heads(w_ukv[:, :, :MLA_NOPE].reshape(MLA_KV_RANK, -1), MLA_NOPE).astype(BF16),
        w_uvT=_pad_heads(w_ukv[:, :, MLA_NOPE:].reshape(MLA_KV_RANK, -1), MLA_V).T.astype(BF16),
        gq=p['mla_q_norm_g'][None, :], gkv=p['mla_kv_norm_g'][None, :])


def _rope_tables(n_tok, rot_dim):
    rows = n_tok // GRID_W
    row = jnp.repeat(jnp.arange(rows), GRID_W).astype(F32)
    col = jnp.tile(jnp.arange(GRID_W), rows).astype(F32)
    nf = rot_dim // 4
    inv = ROPE_BASE ** (-jnp.arange(nf, dtype=F32) / nf)
    ang = jnp.concatenate([row[:, None] * inv, col[:, None] * inv], axis=-1)
    return jnp.cos(ang), jnp.sin(ang)


def _rope_operands(n_tok, identity):
    if identity:
        cos_d, sin_d = jnp.ones((n_tok, DIFF_HEAD_DIM // 2), F32), jnp.zeros((n_tok, DIFF_HEAD_DIM // 2), F32)
        cos_m, sin_m = jnp.ones((n_tok, MLA_ROPE // 2), F32), jnp.zeros((n_tok, MLA_ROPE // 2), F32)
    else:
        cos_d, sin_d = _rope_tables(n_tok, DIFF_HEAD_DIM)
        cos_m, sin_m = _rope_tables(n_tok, MLA_ROPE)
    z_d, z_m = jnp.zeros_like(sin_d), jnp.zeros_like(sin_m)
    two = lambda a, bb: jnp.tile(jnp.concatenate([a, bb], axis=1), (1, 2))
    lo, hi = jnp.zeros((n_tok, MLA_NOPE), F32), jnp.zeros((n_tok, HEAD_LANES - MLA_NOPE - MLA_ROPE), F32)
    mid = lambda a, bb: jnp.concatenate([lo, a, bb, hi], axis=1)
    return dict(T_d=jnp.stack([cos_d.T, sin_d.T]), T_m=jnp.stack([cos_m.T, sin_m.T]),
                k_d=jnp.stack([two(cos_d, cos_d), two(z_d, sin_d), two(-sin_d, z_d)]),
                k_m=jnp.stack([mid(cos_m, cos_m), mid(z_m, sin_m), mid(-sin_m, z_m)]))


def _merge_kernel(h_ref, mod_ref, a_ref, hy_ref, cf_ref, m_ref, wg_ref, wbd_ref, wbh_ref, wbc_ref, wbm_ref,
                  wo_ref, wrh_ref, wrl_ref, hn_ref, u2_ref, lg_ref):
    h = h_ref[...]
    d = h.shape[1]
    u = _modulated_norm(h, mod_ref[0:1, :], mod_ref[1:2, :]).astype(BF16)
    gates = jax.nn.sigmoid(jnp.dot(u, wg_ref[...], preferred_element_type=F32))
    dot = lambda x, w_ref: jnp.dot(x, w_ref[...], preferred_element_type=F32)
    acc = gates[:, :d] * dot(a_ref[...], wbd_ref)
    acc += gates[:, d:2 * d] * dot(hy_ref[...].astype(BF16), wbh_ref)
    acc += gates[:, 2 * d:3 * d] * dot(cf_ref[...].astype(BF16), wbc_ref)
    acc += gates[:, 3 * d:] * dot(m_ref[...], wbm_ref)
    hn = h + mod_ref[2:3, :] * dot(acc.astype(BF16), wo_ref)
    hn_ref[...] = hn
    u2 = _modulated_norm(hn, mod_ref[3:4, :], mod_ref[4:5, :])
    u2h = u2.astype(BF16)
    u2l = (u2 - u2h.astype(F32)).astype(BF16)
    u2_ref[...] = u2h
    lg_ref[...] = dot(u2h, wrh_ref) + (dot(u2l, wrh_ref) + dot(u2h, wrl_ref))


def _merge(h, mod, a, hyv, cfv, m, wts, *, tile):
    b, n, d = h.shape
    const2 = lambda bi, ti: (0, 0)
    tok = lambda w: pl.BlockSpec((None, tile, w), lambda bi, ti: (bi, ti, 0))
    names = ('w_gate', 'w_bd', 'w_bh', 'w_bc', 'w_bm', 'w_out', 'w_rh', 'w_rl')
    in_specs = [tok(d), pl.BlockSpec((None, MOD_ROWS, d), lambda bi, ti: (bi, 0, 0)),
                tok(ATT_W), tok(HYENA_WIDTH), tok(CONF_WIDTH), tok(ATT_W)]
    in_specs += [pl.BlockSpec(wts[k].shape, const2, pipeline_mode=pl.Buffered(1)) for k in names]
    sds = jax.ShapeDtypeStruct
    return pl.pallas_call(
        _merge_kernel, grid=(b, n // tile), in_specs=in_specs,
        out_specs=(tok(d), tok(d), tok(HEAD_LANES)),
        out_shape=(sds((b, n, d), F32), sds((b, n, d), BF16), sds((b, n, HEAD_LANES), F32)),
        compiler_params=_params(2, VMEM_LIMIT_LARGE_BYTES), name="merge",
    )(h, mod, a, hyv, cfv, m, *[wts[k] for k in names])


def _merge_weights(p, lam_init):
    d = p['w_out'].shape[0]
    wb_d, wb_h, wb_c, wb_m = (w.T for w in _split(p['w_branch'].T, BRANCH_WIDTHS))
    wb_d = wb_d * (jnp.tile(p['diff_subln_g'], DIFF_HEADS) * (1.0 - lam_init))[:, None]
    wb_m = jnp.pad(wb_m.reshape(MLA_HEADS, MLA_V, d), ((0, 0), (0, HEAD_LANES - MLA_V), (0, 0))).reshape(ATT_W, d)
    w_r = jnp.pad(p['w_router'], ((0, 0), (0, HEAD_LANES - N_EXPERTS)))
    w_rh = w_r.astype(BF16)
    return dict(w_gate=p['w_in'][:, sum(IN_SPLITS):].astype(BF16), w_bd=wb_d.astype(BF16), w_bh=wb_h.astype(BF16),
                w_bc=wb_c.astype(BF16), w_bm=wb_m.astype(BF16), w_out=p['w_out'].astype(BF16),
                w_rh=w_rh, w_rl=(w_r - w_rh.astype(F32)).astype(BF16))


WINDOW_ALIGN = 16
SUB_TOKENS = 256
GATHER_WINDOW = SUB_TOKENS + WINDOW_ALIGN
GATHER_SUBS_PER_STEP = 8
COMBINE_TOKENS = 128
COMBINE_WINDOW = 256
COMBINE_SUBS_PER_STEP = 16
FFN_ROWS = 512
ROUTE_MIN_ROWS = 8


def _excl_scan(x, lane, row):
    inc = x
    s = 1
    while s < HEAD_LANES:
        inc = inc + jnp.where(lane >= s, pltpu.roll(inc, s, 2), 0.0)
        s *= 2
    tot = jnp.sum(x, axis=2, keepdims=True) + jnp.zeros_like(x)
    off = tot
    s = 1
    while s < x.shape[1]:
        off = off + jnp.where(row >= s, pltpu.roll(off, s, 1), 0.0)
        s *= 2
    return inc - x + (off - tot)


def _route_kernel(lg_ref, pos_ref, aff_ref, *, n_valid, cap):
    lg = lg_ref[...]
    shape = lg.shape
    lane = lax.broadcasted_iota(jnp.int32, shape, 2)
    row = lax.broadcasted_iota(jnp.int32, shape, 1)
    e = jnp.exp(lg - jnp.max(lg, axis=0, keepdims=True))
    aff = e / jnp.sum(e, axis=0, keepdims=True)
    bits = jnp.where(row * HEAD_LANES + lane < n_valid, pltpu.bitcast(aff, jnp.int32), -1)

    def count(mask):
        c = jnp.sum(jnp.where(mask, 1.0, 0.0), axis=2, keepdims=True)
        return jnp.sum(c, axis=1, keepdims=True)

    def step(i, thr):
        cand = thr | (jnp.int32(1) << (30 - i))
        return jnp.where(count(bits >= cand) >= cap, cand, thr)
    thr = lax.fori_loop(0, 31, step, jnp.zeros((shape[0], 1, 1), jnp.int32))
    gt = bits > thr
    eq = bits == thr
    need = cap - count(gt)
    tie_rank = _excl_scan(jnp.where(eq, 1.0, 0.0), lane, row)
    sel = gt | (eq & (tie_rank < need))
    pos = _excl_scan(jnp.where(sel, 1.0, 0.0), lane, row)
    pos_ref[...] = jnp.where(sel, pos.astype(jnp.int32), -1)
    aff_ref[...] = aff


def _route(logits, cap):
    b, n, _ = logits.shape
    rows = max(ROUTE_MIN_ROWS, n // HEAD_LANES)
    lg = jnp.swapaxes(logits[..., :N_EXPERTS], 1, 2)
    lg = jnp.pad(lg, ((0, 0), (0, 0), (0, rows * HEAD_LANES - n))).reshape(b, N_EXPERTS, rows, HEAD_LANES)
    spec = pl.BlockSpec((None, N_EXPERTS, rows, HEAD_LANES), lambda bi: (bi, 0, 0, 0))
    pos, aff = pl.pallas_call(
        functools.partial(_route_kernel, n_valid=n, cap=cap), grid=(b,), in_specs=[spec], out_specs=(spec, spec),
        out_shape=(jax.ShapeDtypeStruct(lg.shape, jnp.int32), jax.ShapeDtypeStruct(lg.shape, F32)),
        compiler_params=_params(1), name="route",
    )(lg)
    flat = lambda a: a.reshape(b, N_EXPERTS, rows * HEAD_LANES)[..., :n]
    return flat(pos), flat(aff)


def _experts_kernel(base_ref, u_ref, mod_ref, aff_ref, pos_ref, win_ref, wout_ref, ye_ref, xe_ref, *,
                    n_sub, cap):
    bi, ei, kb = pl.program_id(0), pl.program_id(1), pl.program_id(2)
    d = u_ref.shape[1]

    @pl.when(kb == 0)
    def _():
        xe_ref[...] = jnp.zeros(xe_ref.shape, F32)

    slot = lax.broadcasted_iota(jnp.int32, (GATHER_WINDOW, SUB_TOKENS), 0)
    ones = jnp.ones((SUB_TOKENS, HEAD_LANES), BF16)
    for j in range(n_sub):
        tok = slice(j * SUB_TOKENS, (j + 1) * SUB_TOKENS)
        base = pl.multiple_of(base_ref[bi, ei, kb * n_sub + j], WINDOW_ALIGN)
        match = slot == (pos_ref[:, tok] - base)
        onehot = jnp.where(match, 1.0, 0.0).astype(BF16)
        g = aff_ref[:, tok]
        g_hi = g.astype(BF16).astype(F32)
        sel_hi = jnp.where(match, g_hi, 0.0).astype(BF16)
        sel_lo = jnp.where(match, g - g_hi, 0.0).astype(BF16)
        rows = pl.ds(base, GATHER_WINDOW)
        xe_ref[rows, :d] += jnp.dot(onehot, u_ref[tok, :], preferred_element_type=F32)
        xe_ref[rows, d:d + HEAD_LANES] += jnp.dot(sel_hi, ones, preferred_element_type=F32)
        xe_ref[rows, d + HEAD_LANES:] += jnp.dot(sel_lo, ones, preferred_element_type=F32)

    @pl.when(kb == pl.num_programs(2) - 1)
    def _():
        f = wout_ref.shape[0]
        step = min(FFN_ROWS, cap)
        for r0 in range(0, cap, step):
            x = xe_ref[r0:r0 + step, :d].astype(BF16)
            gate = xe_ref[r0:r0 + step, d:d + HEAD_LANES] + xe_ref[r0:r0 + step, d + HEAD_LANES:]
            hgu = jnp.dot(x, win_ref[...], preferred_element_type=F32)
            act = (jax.nn.silu(hgu[:, :f]) * hgu[:, f:]).astype(BF16)
            y = jnp.dot(act, wout_ref[...], preferred_element_type=F32)
            scale = jnp.concatenate([gate] * (d // HEAD_LANES), axis=1) * mod_ref[5:6, :]
            ye_ref[r0:r0 + step, :] = (y * scale).astype(BF16)
        ye_ref[cap:, :] = jnp.zeros((ye_ref.shape[0] - cap, ye_ref.shape[1]), BF16)


def _combine_kernel(base_ref, h_ref, mod_ref, posn_ref, ye_ref, hn_ref, *, n_sub, final_norm):
    bi, kb, ei = pl.program_id(0), pl.program_id(1), pl.program_id(2)

    @pl.when(ei == 0)
    def _():
        hn_ref[...] = h_ref[...]

    slot = lax.broadcasted_iota(jnp.int32, (COMBINE_TOKENS, COMBINE_WINDOW), 1)
    lane_e = lax.broadcasted_iota(jnp.int32, (COMBINE_TOKENS, N_EXPERTS), 1)
    for j in range(n_sub):
        tok = slice(j * COMBINE_TOKENS, (j + 1) * COMBINE_TOKENS)
        base = pl.multiple_of(base_ref[bi, ei, kb * n_sub + j], WINDOW_ALIGN)
        rel = jnp.sum(jnp.where(lane_e == ei, posn_ref[tok, :], 0), axis=1, keepdims=True) - base
        onehot = jnp.where(slot == rel, 1.0, 0.0).astype(BF16)
        ye = ye_ref[pl.ds(base, COMBINE_WINDOW), :]
        hn_ref[tok, :] += jnp.dot(onehot, ye, preferred_element_type=F32)

    if final_norm:
        @pl.when(ei == pl.num_programs(2) - 1)
        def _():
            hn = hn_ref[...]
            hn_ref[...] = hn * lax.rsqrt(jnp.mean(hn * hn, axis=-1, keepdims=True) + EPS) * mod_ref[6:7, :]


def _expert_choice_ffn(h, mod, u2, logits, w_exp_in, w_exp_out, final_norm=False):
    b, n, d = u2.shape
    cap = max(1, EC_CAPACITY * n // N_EXPERTS)
    n_sub = min(GATHER_SUBS_PER_STEP, n // SUB_TOKENS)
    n_sub_c = min(COMBINE_SUBS_PER_STEP, n // COMBINE_TOKENS)
    big, big_c = n_sub * SUB_TOKENS, n_sub_c * COMBINE_TOKENS
    n_big, n_big_c = n // big, n // big_c
    capp = cap + GATHER_WINDOW
    pos, aff = _route(logits, cap)

    def window_starts(sub):
        cnt = jnp.sum((pos >= 0).reshape(b, N_EXPERTS, n // sub, sub), axis=-1)
        return ((jnp.cumsum(cnt, axis=-1) - cnt) // WINDOW_ALIGN * WINDOW_ALIGN).astype(jnp.int32)
    base, base_c = window_starts(SUB_TOKENS), window_starts(COMBINE_TOKENS)
    f = w_exp_out.shape[1]
    ye = pl.pallas_call(
        functools.partial(_experts_kernel, n_sub=n_sub, cap=cap),
        grid_spec=pltpu.PrefetchScalarGridSpec(
            num_scalar_prefetch=1, grid=(b, N_EXPERTS, n_big),
            in_specs=[pl.BlockSpec((None, big, d), lambda bi, ei, kb, base_r: (bi, kb, 0)),
                      pl.BlockSpec((None, MOD_ROWS, d), lambda bi, ei, kb, base_r: (bi, 0, 0)),
                      pl.BlockSpec((None, None, 1, big), lambda bi, ei, kb, base_r: (bi, ei, 0, kb)),
                      pl.BlockSpec((None, None, 1, big), lambda bi, ei, kb, base_r: (bi, ei, 0, kb)),
                      pl.BlockSpec((None, d, 2 * f), lambda bi, ei, kb, base_r: (ei, 0, 0)),
                      pl.BlockSpec((None, f, d), lambda bi, ei, kb, base_r: (ei, 0, 0))],
            out_specs=pl.BlockSpec((None, None, capp, d), lambda bi, ei, kb, base_r: (bi, ei, 0, 0)),
            scratch_shapes=[pltpu.VMEM((capp, d + 2 * HEAD_LANES), F32)]),
        out_shape=jax.ShapeDtypeStruct((b, N_EXPERTS, capp, d), BF16),
        compiler_params=_params(3, VMEM_LIMIT_LARGE_BYTES), name="experts",
    )(base, u2, mod, aff.reshape(b, N_EXPERTS, 1, n), pos.reshape(b, N_EXPERTS, 1, n), w_exp_in, w_exp_out)
    posn = jnp.swapaxes(pos, 1, 2)
    return pl.pallas_call(
        functools.partial(_combine_kernel, n_sub=n_sub_c, final_norm=final_norm),
        grid_spec=pltpu.PrefetchScalarGridSpec(
            num_scalar_prefetch=1, grid=(b, n_big_c, N_EXPERTS),
            in_specs=[pl.BlockSpec((None, big_c, d), lambda bi, kb, ei, base_r: (bi, kb, 0)),
                      pl.BlockSpec((None, MOD_ROWS, d), lambda bi, kb, ei, base_r: (bi, 0, 0)),
                      pl.BlockSpec((None, big_c, N_EXPERTS), lambda bi, kb, ei, base_r: (bi, kb, 0)),
                      pl.BlockSpec((None, None, capp, d), lambda bi, kb, ei, base_r: (bi, ei, 0, 0))],
            out_specs=pl.BlockSpec((None, big_c, d), lambda bi, kb, ei, base_r: (bi, kb, 0))),
        out_shape=jax.ShapeDtypeStruct((b, n, d), F32),
        compiler_params=_params(3, VMEM_LIMIT_LARGE_BYTES), name="combine",
    )(base_c, h, mod, posn, ye)


def _split(z, sizes):
    out, start = [], 0
    for s in sizes:
        out.append(z[..., start:start + s])
        start += s
    return out


HALO = 16


def _fill_ext(ext_ref, x_ref, prev_ref, next_ref):
    ti, nt = pl.program_id(1), pl.num_programs(1)
    tt = x_ref.shape[0]
    ext_ref[0:HALO, :] = jnp.where(ti > 0, prev_ref[...], 0.0)
    ext_ref[HALO:HALO + tt, :] = x_ref[...]
    ext_ref[HALO + tt:, :] = jnp.where(ti < nt - 1, next_ref[...], 0.0)


def _taps(ext_ref, w_ref, tt):
    k = w_ref.shape[0]
    acc = None
    for j in range(k):
        start = HALO - k // 2 + j
        term = w_ref[j:j + 1, :] * ext_ref[start:start + tt, :]
        acc = term if acc is None else acc + term
    return acc


def _short_conv_kernel(x_ref, prev_ref, next_ref, w_ref, b_ref, x1_ref, x2_ref, v_ref, ext_ref):
    _fill_ext(ext_ref, x_ref, prev_ref, next_ref)
    y = _taps(ext_ref, w_ref, x_ref.shape[0]) + b_ref[...]
    x1_ref[...] = y[:, :HYENA_WIDTH]
    x2_ref[...] = y[:, HYENA_WIDTH:2 * HYENA_WIDTH]
    v_ref[...] = y[:, 2 * HYENA_WIDTH:]


def _conformer_kernel(x_ref, prev_ref, next_ref, w_ref, g_ref, b_ref, o_ref, ext_ref):
    _fill_ext(ext_ref, x_ref, prev_ref, next_ref)
    u = _taps(ext_ref, w_ref, x_ref.shape[0])
    mu = jnp.mean(u, axis=-1, keepdims=True)
    var = jnp.mean(jnp.square(u - mu), axis=-1, keepdims=True)
    y = (u - mu) * lax.rsqrt(var + EPS) * g_ref[...] + b_ref[...]
    o_ref[...] = y * jax.nn.sigmoid(y)


def _token_conv(body, x, consts, out_widths, name):
    b, n, w = x.shape
    tt = min(2048, n)
    per = tt // HALO
    last = n // HALO - 1
    in_specs = [pl.BlockSpec((None, tt, w), lambda bi, ti: (bi, ti, 0)),
                pl.BlockSpec((None, HALO, w), lambda bi, ti: (bi, jnp.maximum(ti * per - 1, 0), 0)),
                pl.BlockSpec((None, HALO, w), lambda bi, ti: (bi, jnp.minimum((ti + 1) * per, last), 0))]
    in_specs += [pl.BlockSpec(cst.shape, lambda bi, ti: (0, 0)) for cst in consts]
    outs = tuple(jax.ShapeDtypeStruct((b, n, ow), F32) for ow in out_widths)
    out_specs = tuple(pl.BlockSpec((None, tt, ow), lambda bi, ti: (bi, ti, 0)) for ow in out_widths)
    return pl.pallas_call(body, grid=(b, n // tt), in_specs=in_specs, out_specs=out_specs, out_shape=outs,
                          scratch_shapes=[pltpu.VMEM((tt + 2 * HALO, w), F32)],
                          compiler_params=_params(2), name=name)(x, x, x, *consts)


def _conformer_branch(glu, p):
    return _token_conv(_conformer_kernel, glu, (p['conf_dw_w'], p['conf_ln_g'][None, :], p['conf_ln_b'][None, :]),
                       (CONF_WIDTH,), "conformer")[0]


FILT_LANES = 128
DFT_SHORT = 256


def _split_bf16(x):
    hi = x.astype(BF16)
    return hi, (x - hi.astype(F32)).astype(BF16)


def _dot_split(ah, al, bh, bl):
    dot = lambda u, v: jnp.dot(u, v, preferred_element_type=F32)
    return dot(ah, bh) + (dot(al, bh) + dot(ah, bl))


def _dot_const(mh, ml, x):
    xb = x.astype(BF16)
    return jnp.dot(mh, xb, preferred_element_type=F32) + jnp.dot(ml, xb, preferred_element_type=F32)


def _filter_kernel(z_ref, w1h, w1l, b1, f1, w2h, w2l, b2, f2, w3h, w3l, dl_ref, h_ref, asum_ref, *,
                   tiles_per_dir):
    z = z_ref[...]
    hid = jnp.sin(f1[...] * (_dot_split(*_split_bf16(z), w1h[...], w1l[...]) + b1[...]))
    hid = jnp.sin(f2[...] * (_dot_split(*_split_bf16(hid), w2h[...], w2l[...]) + b2[...]))
    h = _dot_split(*_split_bf16(hid), w3h[...], w3l[...])
    h = h * jnp.exp(-z[:, 0:1] * dl_ref[...])
    h_ref[...] = h

    @pl.when(pl.program_id(0) % tiles_per_dir == 0)
    def _():
        asum_ref[...] = jnp.zeros(asum_ref.shape, F32)
    asum_ref[...] += jnp.sum(jnp.abs(h), axis=0, keepdims=True)


def _normalise_kernel(h_ref, asum_ref, *o_refs, n):
    tt = h_ref.shape[0]
    row = pl.program_id(0) * tt + lax.broadcasted_iota(jnp.int32, h_ref.shape, 0)
    k = jnp.where(row == n, 0.0, h_ref[...] / asum_ref[...])
    for o, o_ref in enumerate(o_refs):
        o_ref[...] = k[:, o * HYENA_WIDTH:(o + 1) * HYENA_WIDTH]


def _hyena_taps(n, p):
    t = jnp.linspace(0.0, 1.0, n, dtype=F32)[:, None]
    bands = (FILT_EMB - 1) // 2
    w = (2.0 * math.pi / n) * jnp.arange(n, dtype=F32)[:, None]
    f = jnp.linspace(1e-4, bands - 1, bands, dtype=F32)[None, :]
    t2, w2pos = jnp.concatenate([t, t[::-1]], axis=0), jnp.concatenate([w, w[::-1]], axis=0)
    z2 = jnp.concatenate([t2, jnp.cos(f * w2pos), -jnp.sin(f * w2pos),
                          jnp.zeros((2 * n, FILT_LANES - FILT_EMB), F32)], axis=-1)
    padc = lambda a: jnp.pad(a, ((0, 0), (0, FILT_LANES - a.shape[1])))
    padr = lambda a: jnp.pad(a, ((0, FILT_LANES - a.shape[0]), (0, 0)))
    w1, w2, w3 = padc(padr(p['filt_w1'])), padc(padr(p['filt_w2'])), padr(p['filt_w3'])
    b1, b2 = padc(p['filt_b1'][None, :]), padc(p['filt_b2'][None, :])
    f1, f2 = padc(p['filt_freq'][0][None, :]), padc(p['filt_freq'][1][None, :])
    deltas = jnp.abs(jnp.linspace(math.log(DECAY_TARGET) / SLOW_DECAY, math.log(DECAY_TARGET) / FAST_DECAY,
                                  HYENA_WIDTH, dtype=F32))
    width = HYENA_ORDER * HYENA_WIDTH
    w3 = w3.reshape(FILT_LANES, HYENA_ORDER, 2, HYENA_WIDTH).transpose(2, 0, 1, 3).reshape(2, FILT_LANES, width)
    dl = jnp.tile(deltas, HYENA_ORDER)[None, :]
    w3h, w3l = _split_bf16(w3)
    tt = min(2048, n)
    tiles_per_dir = n // tt
    cspec = lambda a: pl.BlockSpec(a.shape, lambda i: (0, 0))
    dirspec = lambda rows: pl.BlockSpec((None, rows, width), lambda i: (i // tiles_per_dir, 0, 0))
    tile = lambda w: pl.BlockSpec((tt, w), lambda i: (i, 0))
    small = [*_split_bf16(w1), b1, f1, *_split_bf16(w2), b2, f2]
    h_raw, asum = pl.pallas_call(
        functools.partial(_filter_kernel, tiles_per_dir=tiles_per_dir), grid=(2 * tiles_per_dir,),
        in_specs=[tile(FILT_LANES)] + [cspec(a) for a in small] + [dirspec(FILT_LANES), dirspec(FILT_LANES), cspec(dl)],
        out_specs=(tile(width), dirspec(1)),
        out_shape=(jax.ShapeDtypeStruct((2 * n, width), F32), jax.ShapeDtypeStruct((2, 1, width), F32)),
        compiler_params=_params(1), name="hyena_filter_mlp")(z2, *small, w3h, w3l, dl)
    return pl.pallas_call(
        functools.partial(_normalise_kernel, n=n), grid=(2 * tiles_per_dir,),
        in_specs=[tile(width), dirspec(1)],
        out_specs=tuple(tile(HYENA_WIDTH) for _ in range(HYENA_ORDER)),
        out_shape=tuple(jax.ShapeDtypeStruct((2 * n, HYENA_WIDTH), F32) for _ in range(HYENA_ORDER)),
        compiler_params=_params(1), name="hyena_filter_norm",
    )(h_raw, asum)


def _dft_tables(n):
    n2 = DFT_SHORT if n >= 4 * DFT_SHORT else n
    n1 = n // n2

    def cis(idx):
        ang = (-2.0 * math.pi / n) * idx.astype(F32)
        return jnp.cos(ang), jnp.sin(ang)
    k2 = jnp.arange(n2)
    fr, fi = cis((k2[:, None] * k2[None, :]) % n2 * n1)
    tabs = dict(n1=n1, n2=n2)
    tabs['f_hi'], tabs['f_lo'] = _split_bf16(jnp.stack([fr, fi]))
    k1 = jnp.arange(n1)
    tr, ti = cis(k1[:, None] * k2[None, :])
    tabs['tw'] = jnp.broadcast_to(jnp.stack([tr, ti], axis=1)[..., None], (n1, 2, n2, HEAD_LANES))
    if n1 > 1:
        gr, gi = cis((k1[:, None] * k1[None, :]) % n1 * n2)
        half = n1 // 2
        grh, gih = gr[:, :half], gi[:, :half]
        tabs['m_fwd'] = _split_bf16(jnp.block([[grh, -gih], [gih, grh]]))
        tabs['m_real'] = _split_bf16(jnp.concatenate([gr, gi], axis=0))
        tabs['m_inv'] = _split_bf16(jnp.block([[grh.T, gih.T], [-gih.T, grh.T]]))
    return tabs


SLABS_PER_STEP = 32


def _rowmix_slabs_kernel(mh_ref, ml_ref, x_ref, o_ref):
    for j in range(x_ref.shape[1]):
        o_ref[:, j, :] = _dot_const(mh_ref[...], ml_ref[...], x_ref[:, j, :])


def _rowmix_slabs(m, x):
    mh, ml = m
    rin, n2, c = x.shape
    nb = min(SLABS_PER_STEP, n2)
    return pl.pallas_call(
        _rowmix_slabs_kernel, grid=(n2 // nb,),
        in_specs=[pl.BlockSpec(mh.shape, lambda i: (0, 0)), pl.BlockSpec(ml.shape, lambda i: (0, 0)),
                  pl.BlockSpec((rin, nb, c), lambda i: (0, i, 0))],
        out_specs=pl.BlockSpec((mh.shape[0], nb, c), lambda i: (0, i, 0)),
        out_shape=jax.ShapeDtypeStruct((mh.shape[0], n2, c), F32), compiler_params=_params(1),
        name="dft_rowmix_slabs",
    )(mh, ml, x)


K1_PER_STEP = 8


def _spectral_kernel(x_ref, tw_ref, fh_ref, fl_ref, k_ref, o_ref, *, conv):
    frh, fih, frl, fil = fh_ref[0], fh_ref[1], fl_ref[0], fl_ref[1]

    def dft(ar, ai, conj):
        rr, ii = _dot_const(frh, frl, ar), _dot_const(fih, fil, ai)
        ri, ir = _dot_const(frh, frl, ai), _dot_const(fih, fil, ar)
        return (rr + ii, ri - ir) if conj else (rr - ii, ri + ir)

    for q in range(x_ref.shape[1]):
        xr, xi = x_ref[0, q], x_ref[1, q]
        reps = xr.shape[1] // HEAD_LANES
        tr = jnp.concatenate([tw_ref[q, 0]] * reps, axis=1)
        ti = jnp.concatenate([tw_ref[q, 1]] * reps, axis=1)
        yr, yi = dft(xr * tr - xi * ti, xr * ti + xi * tr, False)
        if not conv:
            o_ref[0, q] = yr * k_ref[...]
            o_ref[1, q] = yi * k_ref[...]
            continue
        kr, ki = k_ref[0, q], k_ref[1, q]
        cr, ci = dft(yr * kr - yi * ki, yr * ki + yi * kr, True)
        o_ref[0, q] = cr * tr + ci * ti
        o_ref[1, q] = ci * tr - cr * ti


def _spectral(x, k, tabs, conv):
    _, n1, n2, c = x.shape
    kb = min(K1_PER_STEP, n1)
    slab = pl.BlockSpec((2, kb, n2, c), lambda i: (0, i, 0, 0))
    kspec = slab if conv else pl.BlockSpec(k.shape, lambda i: (0, 0))
    return pl.pallas_call(
        functools.partial(_spectral_kernel, conv=conv), grid=(n1 // kb,),
        in_specs=[slab, pl.BlockSpec((kb, 2, n2, HEAD_LANES), lambda i: (i, 0, 0, 0)),
                  pl.BlockSpec(tabs['f_hi'].shape, lambda i: (0, 0, 0)),
                  pl.BlockSpec(tabs['f_lo'].shape, lambda i: (0, 0, 0)), kspec],
        out_specs=slab, out_shape=jax.ShapeDtypeStruct(x.shape, F32), compiler_params=_params(1),
        name="dft_spectral_conv" if conv else "dft_spectral_filter",
    )(x, tabs['tw'], tabs['f_hi'], tabs['f_lo'], k)


def _filter_spectrum(k, tabs):
    n, c = k.shape
    n1, n2 = tabs['n1'], tabs['n2']
    if n1 > 1:
        x = _rowmix_slabs(tabs['m_real'], k.reshape(n1, n2, c)).reshape(2, n1, n2, c)
    else:
        x = jnp.stack([k, jnp.zeros_like(k)]).reshape(2, 1, n2, c)
    return _spectral(x, jnp.full((1, c), 1.0 / n, F32), tabs, conv=False)


def _rowmix_gate_kernel(mh_ref, ml_ref, x_ref, g_ref, v_ref, s_ref, o_ref):
    o_ref[...] = g_ref[...] * (_dot_const(mh_ref[...], ml_ref[...], x_ref[...]) + s_ref[...] * v_ref[...])


def _gated_long_conv(gate, v, kf, skip, tabs):
    b, n, c = v.shape
    assert b == 2
    n1, n2 = tabs['n1'], tabs['n2']
    if n1 == 1:
        x = jnp.concatenate([v, jnp.zeros_like(v)], axis=1).reshape(2, 1, n2, c)
        y = _spectral(x, kf, tabs, conv=True).reshape(2, n2, c)[:, :n]
        return _hyena_gate(gate, y, v, skip)
    x = _rowmix_slabs(tabs['m_fwd'], v.reshape(n1, n2, c)).reshape(2, n1, n2, c)
    y = _spectral(x, kf, tabs, conv=True).reshape(2 * n1, n2 * c)
    mh, ml = tabs['m_inv']
    cols = n2 * c
    ct = min(4096, cols)
    tile = lambda rows: pl.BlockSpec((rows, ct), lambda i: (0, i))
    return pl.pallas_call(
        _rowmix_gate_kernel, grid=(cols // ct,),
        in_specs=[pl.BlockSpec(mh.shape, lambda i: (0, 0)), pl.BlockSpec(ml.shape, lambda i: (0, 0)),
                  tile(2 * n1), tile(n1), tile(n1), pl.BlockSpec((1, ct), lambda i: (0, 0))],
        out_specs=tile(n1), out_shape=jax.ShapeDtypeStruct((n1, cols), F32), compiler_params=_params(1),
        name="dft_rowmix_gate",
    )(mh, ml, y, gate.reshape(n1, cols), v.reshape(n1, cols), jnp.tile(skip, (1, ct // c))).reshape(2, n, c)


def _gate_kernel(g_ref, y_ref, v_ref, s_ref, o_ref):
    o_ref[...] = g_ref[...] * (y_ref[...] + s_ref[...] * v_ref[...])


def _hyena_gate(gate, y, v, skip):
    b, n, c = v.shape
    tt = min(2048, n)
    tok = pl.BlockSpec((None, tt, c), lambda bi, ti: (bi, ti, 0))
    return pl.pallas_call(_gate_kernel, grid=(b, n // tt),
                          in_specs=[tok, tok, tok, pl.BlockSpec((1, c), lambda bi, ti: (0, 0))], out_specs=tok,
                          out_shape=jax.ShapeDtypeStruct(v.shape, F32), compiler_params=_params(2),
                          name="hyena_gate")(gate, y, v, skip)


def _hyena_branch(hy, p, tabs):
    n = hy.shape[1]
    x1, x2, v = _token_conv(_short_conv_kernel, hy, (p['hyena_short_w'], p['hyena_short_b'][None, :]),
                            (HYENA_WIDTH,) * 3, "hyena_short_conv")
    taps = _hyena_taps(n, p)
    for o, gate in enumerate((x1, x2)):
        v = _gated_long_conv(gate, v, _filter_spectrum(taps[o], tabs), p['hyena_skip'][o][None, :], tabs)
    return v


def _adaln_kernel(c_ref, w_ref, b_ref, o_ref):
    s = c_ref[...]
    s = s * jax.nn.sigmoid(s)
    o_ref[...] = _dot_split(*_split_bf16(s), *_split_bf16(w_ref[...])) + b_ref[...]


def _adaln(cond, w, b):
    d, width = w.shape
    ct = width // 6
    return pl.pallas_call(
        _adaln_kernel, grid=(6,),
        in_specs=[pl.BlockSpec(cond.shape, lambda i: (0, 0)), pl.BlockSpec((d, ct), lambda i: (0, i)),
                  pl.BlockSpec((1, ct), lambda i: (0, i))],
        out_specs=pl.BlockSpec((cond.shape[0], ct), lambda i: (0, i)),
        out_shape=jax.ShapeDtypeStruct((cond.shape[0], width), F32), compiler_params=_params(1), name="adaln",
    )(cond, w, b[None, :])


def _mod_rows(mod, norm_mix_g, norm_ffn_g, final_g, batch):
    sh1, sc1, g1, sh2, sc2, g2 = jnp.split(mod, 6, axis=-1)
    rows = jnp.stack([norm_mix_g * (1.0 + sc1), sh1, g1, norm_ffn_g * (1.0 + sc2), sh2, g2,
                      jnp.broadcast_to(final_g, g1.shape), jnp.zeros_like(g1)], axis=1)
    return jnp.broadcast_to(rows, (batch,) + rows.shape[1:])


def kernel(x, c, ctx, c_ctx, ada_w, ada_b, norm_mix_g, norm_ffn_g, w_in, diff_lambda, diff_subln_g, hyena_short_w, hyena_short_b, filt_w1, filt_b1, filt_freq, filt_w2, filt_b2, filt_w3, hyena_skip, conf_dw_w, conf_ln_g, conf_ln_b, mla_q_norm_g, mla_kv_norm_g, mla_w_uq, mla_w_ukv, w_branch, w_out, w_router, w_exp_in, w_exp_out, final_norm_g):
    depth = w_in.shape[0]
    batch, n_lat, d = x.shape
    n_ctx = ctx.shape[1]
    rope_lat = _rope_operands(n_lat, identity=False)
    rope_ctx = _rope_operands(n_ctx, identity=True)
    dft_lat, dft_ctx = _dft_tables(2 * n_lat), _dft_tables(2 * n_ctx)
    cond = jnp.concatenate([c, c_ctx[None], jnp.zeros((MOD_ROWS - batch - 1, d), F32)], axis=0)
    tile_lat, tile_ctx = min(512, n_lat), min(256, n_ctx)
    h_lat, h_ctx = x, ctx
    for l in range(depth):
        last = l == depth - 1
        p = dict(w_in=w_in[l], diff_subln_g=diff_subln_g[l], hyena_short_w=hyena_short_w[l],
                 hyena_short_b=hyena_short_b[l], filt_w1=filt_w1[l], filt_b1=filt_b1[l], filt_freq=filt_freq[l],
                 filt_w2=filt_w2[l], filt_b2=filt_b2[l], filt_w3=filt_w3[l], hyena_skip=hyena_skip[l],
                 conf_dw_w=conf_dw_w[l], conf_ln_g=conf_ln_g[l], conf_ln_b=conf_ln_b[l],
                 mla_q_norm_g=mla_q_norm_g[l], mla_kv_norm_g=mla_kv_norm_g[l], mla_w_uq=mla_w_uq[l],
                 mla_w_ukv=mla_w_ukv[l], w_branch=w_branch[l], w_out=w_out[l], w_router=w_router[l],
                 w_exp_in=w_exp_in[l], w_exp_out=w_exp_out[l])
        ada = _adaln(cond, ada_w[l], ada_b[l])
        mod_lat = _mod_rows(ada[:batch], norm_mix_g[l], norm_ffn_g[l], final_norm_g, batch)
        mod_ctx = _mod_rows(ada[batch:batch + 1], norm_mix_g[l], norm_ffn_g[l], final_norm_g, batch)
        lam_init = 0.8 - 0.6 * math.exp(-0.3 * l)
        lq1, lk1, lq2, lk2 = diff_lambda[l].astype(F32)
        lam = jnp.reshape(jnp.exp(jnp.sum(lq1 * lk1)) - jnp.exp(jnp.sum(lq2 * lk2)) + lam_init, (1,))
        w_inp, w_mrg = _inproj_weights(p), _merge_weights(p, lam_init)

        qdT_l, kd_l, vdT_l, qmT_l, km_l, vmT_l, hy_l, glu_l = _inproj(h_lat, mod_lat, w_inp, rope_lat, tile=tile_lat)
        qdT_c, kd_c, vdT_c, qmT_c, km_c, vmT_c, hy_c, glu_c = _inproj(h_ctx, mod_ctx, w_inp, rope_ctx, tile=tile_ctx)
        a_lat = _flash_attention(lam, qdT_l, kd_c, vdT_c, kd_l, vdT_l, n_maps=2, tq=min(512, n_lat))
        m_lat = _flash_attention(lam, qmT_l, km_c, vmT_c, km_l, vmT_l, n_maps=1, tq=min(1024, n_lat))
        h_lat, u2_lat, lg_lat = _merge(h_lat, mod_lat, a_lat, _hyena_branch(hy_l, p, dft_lat), _conformer_branch(glu_l, p),
                                       m_lat, w_mrg, tile=min(256, n_lat))
        w_ei, w_eo = p['w_exp_in'].astype(BF16), p['w_exp_out'].astype(BF16)
        h_lat = _expert_choice_ffn(h_lat, mod_lat, u2_lat, lg_lat, w_ei, w_eo, final_norm=last)
        if not last:
            a_ctx = _flash_attention(lam, qdT_c, kd_c, vdT_c, None, None, n_maps=2, tq=n_ctx)
            m_ctx = _flash_attention(lam, qmT_c, km_c, vmT_c, None, None, n_maps=1, tq=n_ctx)
            h_ctx, u2_ctx, lg_ctx = _merge(h_ctx, mod_ctx, a_ctx, _hyena_branch(hy_c, p, dft_ctx),
                                           _conformer_branch(glu_c, p), m_ctx, w_mrg, tile=tile_ctx)
            h_ctx = _expert_choice_ffn(h_ctx, mod_ctx, u2_ctx, lg_ctx, w_ei, w_eo)
    return h_lat
```

```python
import functools
import math

import jax
import jax.numpy as jnp
from jax import lax
from jax.experimental import pallas as pl
from jax.experimental.pallas import tpu as pltpu

GRID_W = 64
ROPE_BASE = 10000.0
EPS = 1e-6

DIFF_HEADS = 4
DIFF_HEAD_DIM = 64
DIFF_V_DIM = 2 * DIFF_HEAD_DIM
HYENA_WIDTH = 256
HYENA_ORDER = 2
FILT_EMB = 33
DECAY_TARGET = 1e-2
FAST_DECAY = 0.3
SLOW_DECAY = 1.5
CONF_WIDTH = 256
MLA_HEADS = 4
MLA_Q_RANK = 256
MLA_KV_RANK = 128
MLA_NOPE = 64
MLA_ROPE = 32
MLA_V = 64
MLA_SCALE = (MLA_NOPE + MLA_ROPE) ** -0.5
N_EXPERTS = 16
EC_CAPACITY = 2

DIFF_QK_W = DIFF_HEADS * 2 * DIFF_HEAD_DIM
DIFF_V_W = DIFF_HEADS * DIFF_V_DIM
HYENA_PROJ = (HYENA_ORDER + 1) * HYENA_WIDTH
CONF_PROJ = 2 * CONF_WIDTH
IN_SPLITS = (DIFF_QK_W, DIFF_QK_W, DIFF_V_W, HYENA_PROJ, CONF_PROJ, MLA_Q_RANK, MLA_KV_RANK, MLA_ROPE)
BRANCH_WIDTHS = (DIFF_V_W, HYENA_WIDTH, CONF_WIDTH, MLA_HEADS * MLA_V)

HEAD_LANES = 128
DIFF_V_PAD = 16
ATT_W = DIFF_HEADS * HEAD_LANES
LOG2E = 1.4426950408889634
V7X_VMEM_BYTES = 64 * 1024 * 1024
VMEM_LIMIT_BYTES = V7X_VMEM_BYTES * 3 // 4
VMEM_LIMIT_LARGE_BYTES = V7X_VMEM_BYTES * 7 // 8
MOD_ROWS = 8

F32 = jnp.float32
BF16 = jnp.bfloat16
_NT = (((1,), (1,)), ((), ()))


def _params(n_axes, vmem=VMEM_LIMIT_BYTES):
    return pltpu.CompilerParams(dimension_semantics=("arbitrary",) * n_axes, vmem_limit_bytes=vmem)


def _stack_queries(q2_ref, qT, n_maps):
    tq = qT.shape[1]
    if n_maps == 2:
        row = lax.broadcasted_iota(jnp.int32, qT.shape, 0)
        zero = jnp.zeros_like(qT)
        q2_ref[:, :tq] = jnp.where(row < DIFF_HEAD_DIM, qT, zero)
        q2_ref[:, tq:] = jnp.where(row >= DIFF_HEAD_DIM, qT, zero)
    else:
        q2_ref[...] = qT


def _reset(m_ref, acc_ref):
    m_ref[...] = jnp.full(m_ref.shape, -jnp.inf, F32)
    acc_ref[...] = jnp.zeros(acc_ref.shape, F32)


def _absorb(s, vT, m_ref, acc_ref):
    m_prev = m_ref[...]
    m_new = jnp.maximum(m_prev, jnp.max(s, axis=0, keepdims=True))
    alpha = jnp.exp2(m_prev - m_new)
    p = jnp.exp2(s - m_new).astype(BF16)
    acc_ref[...] = alpha * acc_ref[...] + jnp.dot(vT, p, preferred_element_type=F32)
    m_ref[...] = m_new


def _attention_rows(acc_ref, lam_ref, n_maps, tq, sum_row):
    o = acc_ref[0:HEAD_LANES, :] / acc_ref[sum_row:sum_row + 1, :]
    if n_maps == 2:
        o = o[:, :tq] - lam_ref[0] * o[:, tq:]
        o = o * lax.rsqrt(jnp.mean(o * o, axis=0, keepdims=True) + EPS)
    return o.T.astype(BF16)


def _flash_ctx_kernel(lam_ref, qT_ref, kc_ref, vcT_ref, o_ref, acc_ref, m_ref, q2_ref, *, n_maps, sum_row):
    _stack_queries(q2_ref, qT_ref[...], n_maps)
    _reset(m_ref, acc_ref)
    _absorb(jnp.dot(kc_ref[...], q2_ref[...], preferred_element_type=F32), vcT_ref[...], m_ref, acc_ref)
    o_ref[...] = _attention_rows(acc_ref, lam_ref, n_maps, qT_ref.shape[1], sum_row)


def _flash_stream_kernel(lam_ref, qT_ref, kc_ref, vcT_ref, kl_ref, vlT_ref, o_ref, acc_ref, m_ref, q2_ref, sc_ref,
                         s_ref, *, n_maps, n_lat_chunks, tk, tq, sum_row):
    n_q = qT_ref.shape[1] // tq

    def load_queries(qi):
        _stack_queries(q2_ref, qT_ref[:, pl.ds(pl.multiple_of(qi * tq, tq), tq)], n_maps)

    def scores(k):
        return jnp.dot(k, q2_ref[...], preferred_element_type=F32)

    def absorb(s, vT):
        _absorb(s, vT, m_ref, acc_ref)

    load_queries(0)
    sc_ref[...] = scores(kc_ref[...])

    def query_block(qi, carry):
        _reset(m_ref, acc_ref)

        def chunk(c):
            return pl.ds(c * tk if isinstance(c, int) else pl.multiple_of(c * tk, tk), tk)

        s_ref[0] = scores(kl_ref[chunk(0), :])
        absorb(sc_ref[...], vcT_ref[...])

        def pair(j, inner):
            c = 2 * j
            s_ref[1] = scores(kl_ref[chunk(c + 1), :])
            absorb(s_ref[0], vlT_ref[:, chunk(c)])
            s_ref[0] = scores(kl_ref[chunk(c + 2), :])
            absorb(s_ref[1], vlT_ref[:, chunk(c + 1)])
            return inner
        lax.fori_loop(0, n_lat_chunks // 2 - 1, pair, 0)
        s_ref[1] = scores(kl_ref[chunk(n_lat_chunks - 1), :])
        absorb(s_ref[0], vlT_ref[:, chunk(n_lat_chunks - 2)])
        load_queries(jnp.minimum(qi + 1, n_q - 1))
        sc_ref[...] = scores(kc_ref[...])
        absorb(s_ref[1], vlT_ref[:, chunk(n_lat_chunks - 1)])
        o_ref[pl.ds(pl.multiple_of(qi * tq, tq), tq), :] = _attention_rows(acc_ref, lam_ref, n_maps, tq, sum_row)
        return carry
    lax.fori_loop(0, n_q, query_block, 0)


def _flash_attention(lam, qT, kc, vcT, kl, vlT, *, n_maps, tq):
    b, _, s = qT.shape
    lc = kc.shape[1]
    mv = vcT.shape[2]
    sum_row = HEAD_LANES if n_maps == 2 else MLA_V
    r = n_maps * tq
    if kl is not None:
        sl = kl.shape[1]
        tk = _lat_chunk(sl)
        n_lat_chunks = sl // tk
        assert n_lat_chunks % 2 == 0 and n_lat_chunks * tk == sl and s % tq == 0
        return pl.pallas_call(
            functools.partial(_flash_stream_kernel, n_maps=n_maps, n_lat_chunks=n_lat_chunks, tk=tk, tq=tq,
                              sum_row=sum_row),
            grid=(b, DIFF_HEADS),
            in_specs=[pl.BlockSpec(memory_space=pltpu.SMEM),
                      pl.BlockSpec((None, HEAD_LANES, s), lambda bi, hi: (bi, hi, 0)),
                      pl.BlockSpec((None, lc, HEAD_LANES), lambda bi, hi: (bi, 0, hi)),
                      pl.BlockSpec((None, None, mv, lc), lambda bi, hi: (bi, hi, 0, 0)),
                      pl.BlockSpec((None, sl, HEAD_LANES), lambda bi, hi: (bi, 0, hi)),
                      pl.BlockSpec((None, None, mv, sl), lambda bi, hi: (bi, hi, 0, 0))],
            out_specs=pl.BlockSpec((None, s, HEAD_LANES), lambda bi, hi: (bi, 0, hi)),
            out_shape=jax.ShapeDtypeStruct((b, s, ATT_W), BF16),
            scratch_shapes=[pltpu.VMEM((mv, r), F32), pltpu.VMEM((1, r), F32), pltpu.VMEM((HEAD_LANES, r), BF16),
                            pltpu.VMEM((lc, r), F32), pltpu.VMEM((2, tk, r), F32)],
            compiler_params=_params(2, VMEM_LIMIT_LARGE_BYTES),
            name=f"flash_attention_{n_maps}map",
        )(lam, qT, kc, vcT, kl, vlT)
    assert s == tq
    return pl.pallas_call(
        functools.partial(_flash_ctx_kernel, n_maps=n_maps, sum_row=sum_row),
        grid=(b, DIFF_HEADS),
        in_specs=[pl.BlockSpec(memory_space=pltpu.SMEM),
                  pl.BlockSpec((None, HEAD_LANES, s), lambda bi, hi: (bi, hi, 0)),
                  pl.BlockSpec((None, lc, HEAD_LANES), lambda bi, hi: (bi, 0, hi)),
                  pl.BlockSpec((None, None, mv, lc), lambda bi, hi: (bi, hi, 0, 0))],
        out_specs=pl.BlockSpec((None, s, HEAD_LANES), lambda bi, hi: (bi, 0, hi)),
        out_shape=jax.ShapeDtypeStruct((b, s, ATT_W), BF16),
        scratch_shapes=[pltpu.VMEM((mv, r), F32), pltpu.VMEM((1, r), F32), pltpu.VMEM((HEAD_LANES, r), BF16)],
        compiler_params=_params(2),
        name=f"flash_attention_ctx_{n_maps}map",
    )(lam, qT, kc, vcT)


def _lat_chunk(s):
    return min(1024, s // 2)


W_NAT_SPLITS = (DIFF_QK_W, HYENA_PROJ, CONF_PROJ, MLA_Q_RANK, MLA_KV_RANK, HEAD_LANES)


def _modulated_norm(h, a, shift):
    return h * lax.rsqrt(jnp.mean(h * h, axis=-1, keepdims=True) + EPS) * a + shift


def _rope_lanes(x, tab_ref, shift):
    return (x * tab_ref[0] + pltpu.roll(x, shift, 1) * tab_ref[1]
            + pltpu.roll(x, HEAD_LANES - shift, 1) * tab_ref[2])


def _inproj_kernel(h_ref, mod_ref, wnat_ref, wT_ref, wuqT_ref, wukvk_ref, wuvT_ref, gq_ref, gkv_ref,
                   ropeT_d_ref, rope_kd_ref, ropeT_m_ref, rope_km_ref,
                   qdT_ref, kd_ref, vdT_ref, qmT_ref, km_ref, vmT_ref, hy_ref, glu_ref):
    u = _modulated_norm(h_ref[...], mod_ref[0:1, :], mod_ref[1:2, :]).astype(BF16)
    z = jnp.dot(u, wnat_ref[...], preferred_element_type=F32)
    zT = lax.dot_general(wT_ref[...], u, _NT, preferred_element_type=F32)
    offs = [0]
    for w in W_NAT_SPLITS:
        offs.append(offs[-1] + w)
    dk, hy, cf, cq, ckv, krp = (z[:, offs[i]:offs[i + 1]] for i in range(len(W_NAT_SPLITS)))

    for hd in range(DIFF_HEADS):
        sl = slice(hd * HEAD_LANES, (hd + 1) * HEAD_LANES)
        kd_ref[:, sl] = _rope_lanes(dk[:, sl], rope_kd_ref, DIFF_HEAD_DIM // 2).astype(BF16)
    cos_d, sin_d = ropeT_d_ref[0], ropeT_d_ref[1]
    half = DIFF_HEAD_DIM // 2
    for g in range(2 * DIFF_HEADS):
        x1 = zT[g * DIFF_HEAD_DIM:g * DIFF_HEAD_DIM + half]
        x2 = zT[g * DIFF_HEAD_DIM + half:(g + 1) * DIFF_HEAD_DIM]
        qdT_ref[g * DIFF_HEAD_DIM:g * DIFF_HEAD_DIM + half, :] = (x1 * cos_d - x2 * sin_d).astype(BF16)
        qdT_ref[g * DIFF_HEAD_DIM + half:(g + 1) * DIFF_HEAD_DIM, :] = (x1 * sin_d + x2 * cos_d).astype(BF16)
    tail = jnp.where(lax.broadcasted_iota(jnp.int32, (DIFF_V_PAD, zT.shape[1]), 0) == 0, 1.0, 0.0).astype(BF16)
    for hd in range(DIFF_HEADS):
        r0 = DIFF_QK_W + hd * DIFF_V_DIM
        vdT_ref[hd, 0:DIFF_V_DIM, :] = zT[r0:r0 + DIFF_V_DIM].astype(BF16)
        vdT_ref[hd, DIFF_V_DIM:, :] = tail

    hy_ref[...] = hy
    glu_ref[...] = cf[:, :CONF_WIDTH] * jax.nn.sigmoid(cf[:, CONF_WIDTH:])

    cqn = (cq * lax.rsqrt(jnp.mean(cq * cq, axis=-1, keepdims=True) + EPS) * gq_ref[...]).astype(BF16)
    ckvn = (ckv * lax.rsqrt(jnp.mean(ckv * ckv, axis=-1, keepdims=True) + EPS) * gkv_ref[...]).astype(BF16)
    qT = lax.dot_general(wuqT_ref[...], cqn, _NT, preferred_element_type=F32)
    cos_m, sin_m = ropeT_m_ref[0], ropeT_m_ref[1]
    hr = MLA_ROPE // 2
    for hd in range(MLA_HEADS):
        base = hd * HEAD_LANES
        r1 = base + MLA_NOPE
        x1, x2 = qT[r1:r1 + hr], qT[r1 + hr:r1 + 2 * hr]
        qmT_ref[base:r1, :] = qT[base:r1].astype(BF16)
        qmT_ref[r1:r1 + hr, :] = (x1 * cos_m - x2 * sin_m).astype(BF16)
        qmT_ref[r1 + hr:r1 + 2 * hr, :] = (x1 * sin_m + x2 * cos_m).astype(BF16)
        qmT_ref[r1 + 2 * hr:base + HEAD_LANES, :] = jnp.zeros((HEAD_LANES - MLA_NOPE - MLA_ROPE, qT.shape[1]), BF16)
    kn = jnp.dot(ckvn, wukvk_ref[...], preferred_element_type=F32)
    kr = _rope_lanes(krp, rope_km_ref, hr)
    for hd in range(MLA_HEADS):
        sl = slice(hd * HEAD_LANES, (hd + 1) * HEAD_LANES)
        km_ref[:, sl] = (kn[:, sl] + kr).astype(BF16)
    vT = lax.dot_general(wuvT_ref[...], ckvn, _NT, preferred_element_type=F32)
    ones_row = lax.broadcasted_iota(jnp.int32, vT.shape, 0) % HEAD_LANES == MLA_V
    vT = jnp.where(ones_row, 1.0, vT).astype(BF16)
    for hd in range(MLA_HEADS):
        vmT_ref[hd] = vT[hd * HEAD_LANES:(hd + 1) * HEAD_LANES]


def _inproj(h, mod, wts, rope, *, tile):
    b, n, d = h.shape
    const2 = lambda bi, ti: (0, 0)
    tok = lambda w: pl.BlockSpec((None, tile, w), lambda bi, ti: (bi, ti, 0))
    tokT = lambda w: pl.BlockSpec((None, w, tile), lambda bi, ti: (bi, 0, ti))
    full = lambda a: pl.BlockSpec(a.shape, const2)
    in_specs = [tok(d), pl.BlockSpec((None, MOD_ROWS, d), lambda bi, ti: (bi, 0, 0))]
    in_specs += [full(wts[k]) for k in ('w_nat', 'w_T', 'w_uqT', 'w_ukvk', 'w_uvT', 'gq', 'gkv')]
    in_specs += [pl.BlockSpec((2, DIFF_HEAD_DIM // 2, tile), lambda bi, ti: (0, 0, ti)),
                 pl.BlockSpec((3, tile, HEAD_LANES), lambda bi, ti: (0, ti, 0)),
                 pl.BlockSpec((2, MLA_ROPE // 2, tile), lambda bi, ti: (0, 0, ti)),
                 pl.BlockSpec((3, tile, HEAD_LANES), lambda bi, ti: (0, ti, 0))]
    sds = jax.ShapeDtypeStruct
    vrows_d, vrows_m = DIFF_V_DIM + DIFF_V_PAD, HEAD_LANES
    headsT = lambda rows: pl.BlockSpec((None, DIFF_HEADS, rows, tile), lambda bi, ti: (bi, 0, 0, ti))
    out_shape = (sds((b, ATT_W, n), BF16), sds((b, n, ATT_W), BF16), sds((b, DIFF_HEADS, vrows_d, n), BF16),
                 sds((b, ATT_W, n), BF16), sds((b, n, ATT_W), BF16), sds((b, MLA_HEADS, vrows_m, n), BF16),
                 sds((b, n, HYENA_PROJ), F32), sds((b, n, CONF_WIDTH), F32))
    out_specs = (tokT(ATT_W), tok(ATT_W), headsT(vrows_d), tokT(ATT_W), tok(ATT_W), headsT(vrows_m),
                 tok(HYENA_PROJ), tok(CONF_WIDTH))
    return pl.pallas_call(
        _inproj_kernel, grid=(b, n // tile), in_specs=in_specs, out_specs=out_specs, out_shape=out_shape,
        compiler_params=_params(2), name="inproj",
    )(h, mod, wts['w_nat'], wts['w_T'], wts['w_uqT'], wts['w_ukvk'], wts['w_uvT'], wts['gq'], wts['gkv'],
      rope['T_d'], rope['k_d'], rope['T_m'], rope['k_m'])


def _pad_heads(w, width):
    rows = w.shape[0]
    w = w.reshape(rows, MLA_HEADS, width)
    return jnp.pad(w, ((0, 0), (0, 0), (0, HEAD_LANES - width))).reshape(rows, ATT_W)


def _inproj_weights(p):
    d = p['w_in'].shape[0]
    dq, dk, dv, hy, cf, cq, ckv, kr = _split(p['w_in'][:, :sum(IN_SPLITS)], IN_SPLITS)
    krp = jnp.zeros((d, HEAD_LANES), F32).at[:, MLA_NOPE:MLA_NOPE + MLA_ROPE].set(kr)
    w_ukv = p['mla_w_ukv'].reshape(MLA_KV_RANK, MLA_HEADS, MLA_NOPE + MLA_V)
    return dict(
        w_nat=jnp.concatenate([dk, hy, cf, cq, ckv, krp], axis=1).astype(BF16),
        w_T=jnp.concatenate([dq * (DIFF_HEAD_DIM ** -0.5 * LOG2E), dv], axis=1).T.astype(BF16),
        w_uqT=_pad_heads(p['mla_w_uq'] * (MLA_SCALE * LOG2E), MLA_NOPE + MLA_ROPE).T.astype(BF16),
        w_ukvk=_pad_heads(w_ukv[:, :, :MLA_NOPE].reshape(MLA_KV_RANK, -1), MLA_NOPE).astype(BF16),
        w_uvT=_pad_heads(w_ukv[:, :, MLA_NOPE:].reshape(MLA_KV_RANK, -1), MLA_V).T.astype(BF16),
        gq=p['mla_q_norm_g'][None, :], gkv=p['mla_kv_norm_g'][None, :])


def _rope_tables(n_tok, rot_dim):
    rows = n_tok // GRID_W
    row = jnp.repeat(jnp.arange(rows), GRID_W).astype(F32)
    col = jnp.tile(jnp.arange(GRID_W), rows).astype(F32)
    nf = rot_dim // 4
    inv = ROPE_BASE ** (-jnp.arange(nf, dtype=F32) / nf)
    ang = jnp.concatenate([row[:, None] * inv, col[:, None] * inv], axis=-1)
    return jnp.cos(ang), jnp.sin(ang)


def _rope_operands(n_tok, identity):
    if identity:
        cos_d, sin_d = jnp.ones((n_tok, DIFF_HEAD_DIM // 2), F32), jnp.zeros((n_tok, DIFF_HEAD_DIM // 2), F32)
        cos_m, sin_m = jnp.ones((n_tok, MLA_ROPE // 2), F32), jnp.zeros((n_tok, MLA_ROPE // 2), F32)
    else:
        cos_d, sin_d = _rope_tables(n_tok, DIFF_HEAD_DIM)
        cos_m, sin_m = _rope_tables(n_tok, MLA_ROPE)
    z_d, z_m = jnp.zeros_like(sin_d), jnp.zeros_like(sin_m)
    two = lambda a, bb: jnp.tile(jnp.concatenate([a, bb], axis=1), (1, 2))
    lo, hi = jnp.zeros((n_tok, MLA_NOPE), F32), jnp.zeros((n_tok, HEAD_LANES - MLA_NOPE - MLA_ROPE), F32)
    mid = lambda a, bb: jnp.concatenate([lo, a, bb, hi], axis=1)
    return dict(T_d=jnp.stack([cos_d.T, sin_d.T]), T_m=jnp.stack([cos_m.T, sin_m.T]),
                k_d=jnp.stack([two(cos_d, cos_d), two(z_d, sin_d), two(-sin_d, z_d)]),
                k_m=jnp.stack([mid(cos_m, cos_m), mid(z_m, sin_m), mid(-sin_m, z_m)]))


def _merge_kernel(h_ref, mod_ref, a_ref, hy_ref, cf_ref, m_ref, wg_ref, wbd_ref, wbh_ref, wbc_ref, wbm_ref,
                  wo_ref, wrh_ref, wrl_ref, hn_ref, u2_ref, lg_ref):
    h = h_ref[...]
    d = h.shape[1]
    u = _modulated_norm(h, mod_ref[0:1, :], mod_ref[1:2, :]).astype(BF16)
    gates = jax.nn.sigmoid(jnp.dot(u, wg_ref[...], preferred_element_type=F32))
    dot = lambda x, w_ref: jnp.dot(x, w_ref[...], preferred_element_type=F32)
    acc = gates[:, :d] * dot(a_ref[...], wbd_ref)
    acc += gates[:, d:2 * d] * dot(hy_ref[...].astype(BF16), wbh_ref)
    acc += gates[:, 2 * d:3 * d] * dot(cf_ref[...].astype(BF16), wbc_ref)
    acc += gates[:, 3 * d:] * dot(m_ref[...], wbm_ref)
    hn = h + mod_ref[2:3, :] * dot(acc.astype(BF16), wo_ref)
    hn_ref[...] = hn
    u2 = _modulated_norm(hn, mod_ref[3:4, :], mod_ref[4:5, :])
    u2h = u2.astype(BF16)
    u2l = (u2 - u2h.astype(F32)).astype(BF16)
    u2_ref[...] = u2h
    lg_ref[...] = dot(u2h, wrh_ref) + (dot(u2l, wrh_ref) + dot(u2h, wrl_ref))


def _merge(h, mod, a, hyv, cfv, m, wts, *, tile):
    b, n, d = h.shape
    const2 = lambda bi, ti: (0, 0)
    tok = lambda w: pl.BlockSpec((None, tile, w), lambda bi, ti: (bi, ti, 0))
    names = ('w_gate', 'w_bd', 'w_bh', 'w_bc', 'w_bm', 'w_out', 'w_rh', 'w_rl')
    in_specs = [tok(d), pl.BlockSpec((None, MOD_ROWS, d), lambda bi, ti: (bi, 0, 0)),
                tok(ATT_W), tok(HYENA_WIDTH), tok(CONF_WIDTH), tok(ATT_W)]
    in_specs += [pl.BlockSpec(wts[k].shape, const2, pipeline_mode=pl.Buffered(1)) for k in names]
    sds = jax.ShapeDtypeStruct
    return pl.pallas_call(
        _merge_kernel, grid=(b, n // tile), in_specs=in_specs,
        out_specs=(tok(d), tok(d), tok(HEAD_LANES)),
        out_shape=(sds((b, n, d), F32), sds((b, n, d), BF16), sds((b, n, HEAD_LANES), F32)),
        compiler_params=_params(2, VMEM_LIMIT_LARGE_BYTES), name="merge",
    )(h, mod, a, hyv, cfv, m, *[wts[k] for k in names])


def _merge_weights(p, lam_init):
    d = p['w_out'].shape[0]
    wb_d, wb_h, wb_c, wb_m = (w.T for w in _split(p['w_branch'].T, BRANCH_WIDTHS))
    wb_d = wb_d * (jnp.tile(p['diff_subln_g'], DIFF_HEADS) * (1.0 - lam_init))[:, None]
    wb_m = jnp.pad(wb_m.reshape(MLA_HEADS, MLA_V, d), ((0, 0), (0, HEAD_LANES - MLA_V), (0, 0))).reshape(ATT_W, d)
    w_r = jnp.pad(p['w_router'], ((0, 0), (0, HEAD_LANES - N_EXPERTS)))
    w_rh = w_r.astype(BF16)
    return dict(w_gate=p['w_in'][:, sum(IN_SPLITS):].astype(BF16), w_bd=wb_d.astype(BF16), w_bh=wb_h.astype(BF16),
                w_bc=wb_c.astype(BF16), w_bm=wb_m.astype(BF16), w_out=p['w_out'].astype(BF16),
                w_rh=w_rh, w_rl=(w_r - w_rh.astype(F32)).astype(BF16))


WINDOW_ALIGN = 16
SUB_TOKENS = 256
GATHER_WINDOW = SUB_TOKENS + WINDOW_ALIGN
GATHER_SUBS_PER_STEP = 8
COMBINE_TOKENS = 128
COMBINE_WINDOW = 256
COMBINE_SUBS_PER_STEP = 16
FFN_ROWS = 512
ROUTE_MIN_ROWS = 8


def _excl_scan(x, lane, row):
    inc = x
    s = 1
    while s < HEAD_LANES:
        inc = inc + jnp.where(lane >= s, pltpu.roll(inc, s, 2), 0.0)
        s *= 2
    tot = jnp.sum(x, axis=2, keepdims=True) + jnp.zeros_like(x)
    off = tot
    s = 1
    while s < x.shape[1]:
        off = off + jnp.where(row >= s, pltpu.roll(off, s, 1), 0.0)
        s *= 2
    return inc - x + (off - tot)


def _route_kernel(lg_ref, pos_ref, aff_ref, *, n_valid, cap):
    lg = lg_ref[...]
    shape = lg.shape
    lane = lax.broadcasted_iota(jnp.int32, shape, 2)
    row = lax.broadcasted_iota(jnp.int32, shape, 1)
    e = jnp.exp(lg - jnp.max(lg, axis=0, keepdims=True))
    aff = e / jnp.sum(e, axis=0, keepdims=True)
    bits = jnp.where(row * HEAD_LANES + lane < n_valid, pltpu.bitcast(aff, jnp.int32), -1)

    def count(mask):
        c = jnp.sum(jnp.where(mask, 1.0, 0.0), axis=2, keepdims=True)
        return jnp.sum(c, axis=1, keepdims=True)

    def step(i, thr):
        cand = thr | (jnp.int32(1) << (30 - i))
        return jnp.where(count(bits >= cand) >= cap, cand, thr)
    thr = lax.fori_loop(0, 31, step, jnp.zeros((shape[0], 1, 1), jnp.int32))
    gt = bits > thr
    eq = bits == thr
    need = cap - count(gt)
    tie_rank = _excl_scan(jnp.where(eq, 1.0, 0.0), lane, row)
    sel = gt | (eq & (tie_rank < need))
    pos = _excl_scan(jnp.where(sel, 1.0, 0.0), lane, row)
    pos_ref[...] = jnp.where(sel, pos.astype(jnp.int32), -1)
    aff_ref[...] = aff


def _route(logits, cap):
    b, n, _ = logits.shape
    rows = max(ROUTE_MIN_ROWS, n // HEAD_LANES)
    lg = jnp.swapaxes(logits[..., :N_EXPERTS], 1, 2)
    lg = jnp.pad(lg, ((0, 0), (0, 0), (0, rows * HEAD_LANES - n))).reshape(b, N_EXPERTS, rows, HEAD_LANES)
    spec = pl.BlockSpec((None, N_EXPERTS, rows, HEAD_LANES), lambda bi: (bi, 0, 0, 0))
    pos, aff = pl.pallas_call(
        functools.partial(_route_kernel, n_valid=n, cap=cap), grid=(b,), in_specs=[spec], out_specs=(spec, spec),
        out_shape=(jax.ShapeDtypeStruct(lg.shape, jnp.int32), jax.ShapeDtypeStruct(lg.shape, F32)),
        compiler_params=_params(1), name="route",
    )(lg)
    flat = lambda a: a.reshape(b, N_EXPERTS, rows * HEAD_LANES)[..., :n]
    return flat(pos), flat(aff)


def _experts_kernel(base_ref, u_ref, mod_ref, aff_ref, pos_ref, win_ref, wout_ref, ye_ref, xe_ref, *,
                    n_sub, cap):
    bi, ei, kb = pl.program_id(0), pl.program_id(1), pl.program_id(2)
    d = u_ref.shape[1]

    @pl.when(kb == 0)
    def _():
        xe_ref[...] = jnp.zeros(xe_ref.shape, F32)

    slot = lax.broadcasted_iota(jnp.int32, (GATHER_WINDOW, SUB_TOKENS), 0)
    ones = jnp.ones((SUB_TOKENS, HEAD_LANES), BF16)
    for j in range(n_sub):
        tok = slice(j * SUB_TOKENS, (j + 1) * SUB_TOKENS)
        base = pl.multiple_of(base_ref[bi, ei, kb * n_sub + j], WINDOW_ALIGN)
        match = slot == (pos_ref[:, tok] - base)
        onehot = jnp.where(match, 1.0, 0.0).astype(BF16)
        g = aff_ref[:, tok]
        g_hi = g.astype(BF16).astype(F32)
        sel_hi = jnp.where(match, g_hi, 0.0).astype(BF16)
        sel_lo = jnp.where(match, g - g_hi, 0.0).astype(BF16)
        rows = pl.ds(base, GATHER_WINDOW)
        xe_ref[rows, :d] += jnp.dot(onehot, u_ref[tok, :], preferred_element_type=F32)
        xe_ref[rows, d:d + HEAD_LANES] += jnp.dot(sel_hi, ones, preferred_element_type=F32)
        xe_ref[rows, d + HEAD_LANES:] += jnp.dot(sel_lo, ones, preferred_element_type=F32)

    @pl.when(kb == pl.num_programs(2) - 1)
    def _():
        f = wout_ref.shape[0]
        step = min(FFN_ROWS, cap)
        for r0 in range(0, cap, step):
            x = xe_ref[r0:r0 + step, :d].astype(BF16)
            gate = xe_ref[r0:r0 + step, d:d + HEAD_LANES] + xe_ref[r0:r0 + step, d + HEAD_LANES:]
            hgu = jnp.dot(x, win_ref[...], preferred_element_type=F32)
            act = (jax.nn.silu(hgu[:, :f]) * hgu[:, f:]).astype(BF16)
            y = jnp.dot(act, wout_ref[...], preferred_element_type=F32)
            scale = jnp.concatenate([gate] * (d // HEAD_LANES), axis=1) * mod_ref[5:6, :]
            ye_ref[r0:r0 + step, :] = (y * scale).astype(BF16)
        ye_ref[cap:, :] = jnp.zeros((ye_ref.shape[0] - cap, ye_ref.shape[1]), BF16)


def _combine_kernel(base_ref, h_ref, mod_ref, posn_ref, ye_ref, hn_ref, *, n_sub, final_norm):
    bi, kb, ei = pl.program_id(0), pl.program_id(1), pl.program_id(2)

    @pl.when(ei == 0)
    def _():
        hn_ref[...] = h_ref[...]

    slot = lax.broadcasted_iota(jnp.int32, (COMBINE_TOKENS, COMBINE_WINDOW), 1)
    lane_e = lax.broadcasted_iota(jnp.int32, (COMBINE_TOKENS, N_EXPERTS), 1)
    for j in range(n_sub):
        tok = slice(j * COMBINE_TOKENS, (j + 1) * COMBINE_TOKENS)
        base = pl.multiple_of(base_ref[bi, ei, kb * n_sub + j], WINDOW_ALIGN)
        rel = jnp.sum(jnp.where(lane_e == ei, posn_ref[tok, :], 0), axis=1, keepdims=True) - base
        onehot = jnp.where(slot == rel, 1.0, 0.0).astype(BF16)
        ye = ye_ref[pl.ds(base, COMBINE_WINDOW), :]
        hn_ref[tok, :] += jnp.dot(onehot, ye, preferred_element_type=F32)

    if final_norm:
        @pl.when(ei == pl.num_programs(2) - 1)
        def _():
            hn = hn_ref[...]
            hn_ref[...] = hn * lax.rsqrt(jnp.mean(hn * hn, axis=-1, keepdims=True) + EPS) * mod_ref[6:7, :]


def _expert_choice_ffn(h, mod, u2, logits, w_exp_in, w_exp_out, final_norm=False):
    b, n, d = u2.shape
    cap = max(1, EC_CAPACITY * n // N_EXPERTS)
    n_sub = min(GATHER_SUBS_PER_STEP, n // SUB_TOKENS)
    n_sub_c = min(COMBINE_SUBS_PER_STEP, n // COMBINE_TOKENS)
    big, big_c = n_sub * SUB_TOKENS, n_sub_c * COMBINE_TOKENS
    n_big, n_big_c = n // big, n // big_c
    capp = cap + GATHER_WINDOW
    pos, aff = _route(logits, cap)

    def window_starts(sub):
        cnt = jnp.sum((pos >= 0).reshape(b, N_EXPERTS, n // sub, sub), axis=-1)
        return ((jnp.cumsum(cnt, axis=-1) - cnt) // WINDOW_ALIGN * WINDOW_ALIGN).astype(jnp.int32)
    base, base_c = window_starts(SUB_TOKENS), window_starts(COMBINE_TOKENS)
    f = w_exp_out.shape[1]
    ye = pl.pallas_call(
        functools.partial(_experts_kernel, n_sub=n_sub, cap=cap),
        grid_spec=pltpu.PrefetchScalarGridSpec(
            num_scalar_prefetch=1, grid=(b, N_EXPERTS, n_big),
            in_specs=[pl.BlockSpec((None, big, d), lambda bi, ei, kb, base_r: (bi, kb, 0)),
                      pl.BlockSpec((None, MOD_ROWS, d), lambda bi, ei, kb, base_r: (bi, 0, 0)),
                      pl.BlockSpec((None, None, 1, big), lambda bi, ei, kb, base_r: (bi, ei, 0, kb)),
                      pl.BlockSpec((None, None, 1, big), lambda bi, ei, kb, base_r: (bi, ei, 0, kb)),
                      pl.BlockSpec((None, d, 2 * f), lambda bi, ei, kb, base_r: (ei, 0, 0)),
                      pl.BlockSpec((None, f, d), lambda bi, ei, kb, base_r: (ei, 0, 0))],
            out_specs=pl.BlockSpec((None, None, capp, d), lambda bi, ei, kb, base_r: (bi, ei, 0, 0)),
            scratch_shapes=[pltpu.VMEM((capp, d + 2 * HEAD_LANES), F32)]),
        out_shape=jax.ShapeDtypeStruct((b, N_EXPERTS, capp, d), BF16),
        compiler_params=_params(3, VMEM_LIMIT_LARGE_BYTES), name="experts",
    )(base, u2, mod, aff.reshape(b, N_EXPERTS, 1, n), pos.reshape(b, N_EXPERTS, 1, n), w_exp_in, w_exp_out)
    posn = jnp.swapaxes(pos, 1, 2)
    return pl.pallas_call(
        functools.partial(_combine_kernel, n_sub=n_sub_c, final_norm=final_norm),
        grid_spec=pltpu.PrefetchScalarGridSpec(
            num_scalar_prefetch=1, grid=(b, n_big_c, N_EXPERTS),
            in_specs=[pl.BlockSpec((None, big_c, d), lambda bi, kb, ei, base_r: (bi, kb, 0)),
                      pl.BlockSpec((None, MOD_ROWS, d), lambda bi, kb, ei, base_r: (bi, 0, 0)),
                      pl.BlockSpec((None, big_c, N_EXPERTS), lambda bi, kb, ei, base_r: (bi, kb, 0)),
                      pl.BlockSpec((None, None, capp, d), lambda bi, kb, ei, base_r: (bi, ei, 0, 0))],
            out_specs=pl.BlockSpec((None, big_c, d), lambda bi, kb, ei, base_r: (bi, kb, 0))),
        out_shape=jax.ShapeDtypeStruct((b, n, d), F32),
        compiler_params=_params(3, VMEM_LIMIT_LARGE_BYTES), name="combine",
    )(base_c, h, mod, posn, ye)


def _split(z, sizes):
    out, start = [], 0
    for s in sizes:
        out.append(z[..., start:start + s])
        start += s
    return out


HALO = 16


def _fill_ext(ext_ref, x_ref, prev_ref, next_ref):
    ti, nt = pl.program_id(1), pl.num_programs(1)
    tt = x_ref.shape[0]
    ext_ref[0:HALO, :] = jnp.where(ti > 0, prev_ref[...], 0.0)
    ext_ref[HALO:HALO + tt, :] = x_ref[...]
    ext_ref[HALO + tt:, :] = jnp.where(ti < nt - 1, next_ref[...], 0.0)


def _taps(ext_ref, w_ref, tt):
    k = w_ref.shape[0]
    acc = None
    for j in range(k):
        start = HALO - k // 2 + j
        term = w_ref[j:j + 1, :] * ext_ref[start:start + tt, :]
        acc = term if acc is None else acc + term
    return acc


def _short_conv_kernel(x_ref, prev_ref, next_ref, w_ref, b_ref, x1_ref, x2_ref, v_ref, ext_ref):
    _fill_ext(ext_ref, x_ref, prev_ref, next_ref)
    y = _taps(ext_ref, w_ref, x_ref.shape[0]) + b_ref[...]
    x1_ref[...] = y[:, :HYENA_WIDTH]
    x2_ref[...] = y[:, HYENA_WIDTH:2 * HYENA_WIDTH]
    v_ref[...] = y[:, 2 * HYENA_WIDTH:]


def _conformer_kernel(x_ref, prev_ref, next_ref, w_ref, g_ref, b_ref, o_ref, ext_ref):
    _fill_ext(ext_ref, x_ref, prev_ref, next_ref)
    u = _taps(ext_ref, w_ref, x_ref.shape[0])
    mu = jnp.mean(u, axis=-1, keepdims=True)
    var = jnp.mean(jnp.square(u - mu), axis=-1, keepdims=True)
    y = (u - mu) * lax.rsqrt(var + EPS) * g_ref[...] + b_ref[...]
    o_ref[...] = y * jax.nn.sigmoid(y)


def _token_conv(body, x, consts, out_widths, name):
    b, n, w = x.shape
    tt = min(1024, n)
    per = tt // HALO
    last = n // HALO - 1
    in_specs = [pl.BlockSpec((None, tt, w), lambda bi, ti: (bi, ti, 0)),
                pl.BlockSpec((None, HALO, w), lambda bi, ti: (bi, jnp.maximum(ti * per - 1, 0), 0)),
                pl.BlockSpec((None, HALO, w), lambda bi, ti: (bi, jnp.minimum((ti + 1) * per, last), 0))]
    in_specs += [pl.BlockSpec(cst.shape, lambda bi, ti: (0, 0)) for cst in consts]
    outs = tuple(jax.ShapeDtypeStruct((b, n, ow), F32) for ow in out_widths)
    out_specs = tuple(pl.BlockSpec((None, tt, ow), lambda bi, ti: (bi, ti, 0)) for ow in out_widths)
    return pl.pallas_call(body, grid=(b, n // tt), in_specs=in_specs, out_specs=out_specs, out_shape=outs,
                          scratch_shapes=[pltpu.VMEM((tt + 2 * HALO, w), F32)],
                          compiler_params=_params(2), name=name)(x, x, x, *consts)


def _conformer_branch(glu, p):
    return _token_conv(_conformer_kernel, glu, (p['conf_dw_w'], p['conf_ln_g'][None, :], p['conf_ln_b'][None, :]),
                       (CONF_WIDTH,), "conformer")[0]


FILT_LANES = 128
DFT_SHORT = 256


def _split_bf16(x):
    hi = x.astype(BF16)
    return hi, (x - hi.astype(F32)).astype(BF16)


def _dot_split(ah, al, bh, bl):
    dot = lambda u, v: jnp.dot(u, v, preferred_element_type=F32)
    return dot(ah, bh) + (dot(al, bh) + dot(ah, bl))


def _dot_const(mh, ml, x):
    xb = x.astype(BF16)
    return jnp.dot(mh, xb, preferred_element_type=F32) + jnp.dot(ml, xb, preferred_element_type=F32)


def _filter_kernel(z_ref, w1h, w1l, b1, f1, w2h, w2l, b2, f2, w3h, w3l, dl_ref, h_ref, asum_ref, *,
                   tiles_per_dir):
    z = z_ref[...]
    hid = jnp.sin(f1[...] * (_dot_split(*_split_bf16(z), w1h[...], w1l[...]) + b1[...]))
    hid = jnp.sin(f2[...] * (_dot_split(*_split_bf16(hid), w2h[...], w2l[...]) + b2[...]))
    h = _dot_split(*_split_bf16(hid), w3h[...], w3l[...])
    h = h * jnp.exp(-z[:, 0:1] * dl_ref[...])
    h_ref[...] = h

    @pl.when(pl.program_id(0) % tiles_per_dir == 0)
    def _():
        asum_ref[...] = jnp.zeros(asum_ref.shape, F32)
    asum_ref[...] += jnp.sum(jnp.abs(h), axis=0, keepdims=True)


def _normalise_kernel(h_ref, asum_ref, *o_refs, n):
    tt = h_ref.shape[0]
    row = pl.program_id(0) * tt + lax.broadcasted_iota(jnp.int32, h_ref.shape, 0)
    k = jnp.where(row == n, 0.0, h_ref[...] / asum_ref[...])
    for o, o_ref in enumerate(o_refs):
        o_ref[...] = k[:, o * HYENA_WIDTH:(o + 1) * HYENA_WIDTH]


def _hyena_taps(n, p):
    t = jnp.linspace(0.0, 1.0, n, dtype=F32)[:, None]
    bands = (FILT_EMB - 1) // 2
    w = (2.0 * math.pi / n) * jnp.arange(n, dtype=F32)[:, None]
    f = jnp.linspace(1e-4, bands - 1, bands, dtype=F32)[None, :]
    t2, w2pos = jnp.concatenate([t, t[::-1]], axis=0), jnp.concatenate([w, w[::-1]], axis=0)
    z2 = jnp.concatenate([t2, jnp.cos(f * w2pos), -jnp.sin(f * w2pos),
                          jnp.zeros((2 * n, FILT_LANES - FILT_EMB), F32)], axis=-1)
    padc = lambda a: jnp.pad(a, ((0, 0), (0, FILT_LANES - a.shape[1])))
    padr = lambda a: jnp.pad(a, ((0, FILT_LANES - a.shape[0]), (0, 0)))
    w1, w2, w3 = padc(padr(p['filt_w1'])), padc(padr(p['filt_w2'])), padr(p['filt_w3'])
    b1, b2 = padc(p['filt_b1'][None, :]), padc(p['filt_b2'][None, :])
    f1, f2 = padc(p['filt_freq'][0][None, :]), padc(p['filt_freq'][1][None, :])
    deltas = jnp.abs(jnp.linspace(math.log(DECAY_TARGET) / SLOW_DECAY, math.log(DECAY_TARGET) / FAST_DECAY,
                                  HYENA_WIDTH, dtype=F32))
    width = HYENA_ORDER * HYENA_WIDTH
    w3 = w3.reshape(FILT_LANES, HYENA_ORDER, 2, HYENA_WIDTH).transpose(2, 0, 1, 3).reshape(2, FILT_LANES, width)
    dl = jnp.tile(deltas, HYENA_ORDER)[None, :]
    w3h, w3l = _split_bf16(w3)
    tt = min(1024, n)
    tiles_per_dir = n // tt
    cspec = lambda a: pl.BlockSpec(a.shape, lambda i: (0, 0))
    dirspec = lambda rows: pl.BlockSpec((None, rows, width), lambda i: (i // tiles_per_dir, 0, 0))
    tile = lambda w: pl.BlockSpec((tt, w), lambda i: (i, 0))
    small = [*_split_bf16(w1), b1, f1, *_split_bf16(w2), b2, f2]
    h_raw, asum = pl.pallas_call(
        functools.partial(_filter_kernel, tiles_per_dir=tiles_per_dir), grid=(2 * tiles_per_dir,),
        in_specs=[tile(FILT_LANES)] + [cspec(a) for a in small] + [dirspec(FILT_LANES), dirspec(FILT_LANES), cspec(dl)],
        out_specs=(tile(width), dirspec(1)),
        out_shape=(jax.ShapeDtypeStruct((2 * n, width), F32), jax.ShapeDtypeStruct((2, 1, width), F32)),
        compiler_params=_params(1), name="hyena_filter_mlp")(z2, *small, w3h, w3l, dl)
    return pl.pallas_call(
        functools.partial(_normalise_kernel, n=n), grid=(2 * tiles_per_dir,),
        in_specs=[tile(width), dirspec(1)],
        out_specs=tuple(tile(HYENA_WIDTH) for _ in range(HYENA_ORDER)),
        out_shape=tuple(jax.ShapeDtypeStruct((2 * n, HYENA_WIDTH), F32) for _ in range(HYENA_ORDER)),
        compiler_params=_params(1), name="hyena_filter_norm",
    )(h_raw, asum)


def _dft_tables(n):
    n2 = DFT_SHORT if n >= 4 * DFT_SHORT else n
    n1 = n // n2

    def cis(idx):
        ang = (-2.0 * math.pi / n) * idx.astype(F32)
        return jnp.cos(ang), jnp.sin(ang)
    k2 = jnp.arange(n2)
    fr, fi = cis((k2[:, None] * k2[None, :]) % n2 * n1)
    tabs = dict(n1=n1, n2=n2)
    tabs['f_hi'], tabs['f_lo'] = _split_bf16(jnp.stack([fr, fr + fi, fi - fr]))
    k1 = jnp.arange(n1)
    tr, ti = cis(k1[:, None] * k2[None, :])
    tabs['tw'] = jnp.broadcast_to(jnp.stack([tr, ti], axis=1)[..., None], (n1, 2, n2, HEAD_LANES))
    if n1 > 1:
        gr, gi = cis((k1[:, None] * k1[None, :]) % n1 * n2)
        half = n1 // 2
        grh, gih = gr[:, :half], gi[:, :half]
        tabs['m_fwd'] = _split_bf16(jnp.block([[grh, -gih], [gih, grh]]))
        tabs['m_real'] = _split_bf16(jnp.concatenate([gr, gi], axis=0))
        tabs['m_inv'] = _split_bf16(jnp.block([[grh.T, gih.T], [-gih.T, grh.T]]))
    return tabs


SLABS_PER_STEP = 16


def _rowmix_slabs_kernel(mh_ref, ml_ref, x_ref, o_ref):
    for j in range(x_ref.shape[1]):
        o_ref[:, j, :] = _dot_const(mh_ref[...], ml_ref[...], x_ref[:, j, :])


def _rowmix_slabs(m, x):
    mh, ml = m
    rin, n2, c = x.shape
    nb = min(SLABS_PER_STEP, n2)
    return pl.pallas_call(
        _rowmix_slabs_kernel, grid=(n2 // nb,),
        in_specs=[pl.BlockSpec(mh.shape, lambda i: (0, 0)), pl.BlockSpec(ml.shape, lambda i: (0, 0)),
                  pl.BlockSpec((rin, nb, c), lambda i: (0, i, 0))],
        out_specs=pl.BlockSpec((mh.shape[0], nb, c), lambda i: (0, i, 0)),
        out_shape=jax.ShapeDtypeStruct((mh.shape[0], n2, c), F32), compiler_params=_params(1),
        name="dft_rowmix_slabs",
    )(mh, ml, x)


K1_PER_STEP = 8


def _spectral_kernel(x_ref, tw_ref, fh_ref, fl_ref, k_ref, o_ref, *, conv):
    def dft(ar, ai, conj):
        k = _dot_const(fh_ref[0], fl_ref[0], ar + ai)
        if conj:
            return k + _dot_const(fh_ref[2], fl_ref[2], ai), k - _dot_const(fh_ref[1], fl_ref[1], ar)
        return k - _dot_const(fh_ref[1], fl_ref[1], ai), k + _dot_const(fh_ref[2], fl_ref[2], ar)

    for q in range(x_ref.shape[1]):
        xr, xi = x_ref[0, q], x_ref[1, q]
        reps = xr.shape[1] // HEAD_LANES
        tr = jnp.concatenate([tw_ref[q, 0]] * reps, axis=1)
        ti = jnp.concatenate([tw_ref[q, 1]] * reps, axis=1)
        yr, yi = dft(xr * tr - xi * ti, xr * ti + xi * tr, False)
        if not conv:
            o_ref[0, q] = yr * k_ref[...]
            o_ref[1, q] = yi * k_ref[...]
            continue
        kr, ki = k_ref[0, q], k_ref[1, q]
        cr, ci = dft(yr * kr - yi * ki, yr * ki + yi * kr, True)
        o_ref[0, q] = cr * tr + ci * ti
        o_ref[1, q] = ci * tr - cr * ti


def _spectral(x, k, tabs, conv):
    _, n1, n2, c = x.shape
    kb = min(K1_PER_STEP, n1)
    slab = pl.BlockSpec((2, kb, n2, c), lambda i: (0, i, 0, 0))
    kspec = slab if conv else pl.BlockSpec(k.shape, lambda i: (0, 0))
    return pl.pallas_call(
        functools.partial(_spectral_kernel, conv=conv), grid=(n1 // kb,),
        in_specs=[slab, pl.BlockSpec((kb, 2, n2, HEAD_LANES), lambda i: (i, 0, 0, 0)),
                  pl.BlockSpec(tabs['f_hi'].shape, lambda i: (0, 0, 0)),
                  pl.BlockSpec(tabs['f_lo'].shape, lambda i: (0, 0, 0)), kspec],
        out_specs=slab, out_shape=jax.ShapeDtypeStruct(x.shape, F32), compiler_params=_params(1),
        name="dft_spectral_conv" if conv else "dft_spectral_filter",
    )(x, tabs['tw'], tabs['f_hi'], tabs['f_lo'], k)


def _filter_spectrum(k, tabs):
    n, c = k.shape
    n1, n2 = tabs['n1'], tabs['n2']
    if n1 > 1:
        x = _rowmix_slabs(tabs['m_real'], k.reshape(n1, n2, c)).reshape(2, n1, n2, c)
    else:
        x = jnp.stack([k, jnp.zeros_like(k)]).reshape(2, 1, n2, c)
    return _spectral(x, jnp.full((1, c), 1.0 / n, F32), tabs, conv=False)


def _rowmix_gate_kernel(mh_ref, ml_ref, x_ref, g_ref, v_ref, s_ref, o_ref):
    o_ref[...] = g_ref[...] * (_dot_const(mh_ref[...], ml_ref[...], x_ref[...]) + s_ref[...] * v_ref[...])


def _gated_long_conv(gate, v, kf, skip, tabs):
    b, n, c = v.shape
    assert b == 2
    n1, n2 = tabs['n1'], tabs['n2']
    if n1 == 1:
        x = jnp.concatenate([v, jnp.zeros_like(v)], axis=1).reshape(2, 1, n2, c)
        y = _spectral(x, kf, tabs, conv=True).reshape(2, n2, c)[:, :n]
        return _hyena_gate(gate, y, v, skip)
    x = _rowmix_slabs(tabs['m_fwd'], v.reshape(n1, n2, c)).reshape(2, n1, n2, c)
    y = _spectral(x, kf, tabs, conv=True).reshape(2 * n1, n2 * c)
    mh, ml = tabs['m_inv']
    cols = n2 * c
    ct = min(4096, cols)
    tile = lambda rows: pl.BlockSpec((rows, ct), lambda i: (0, i))
    return pl.pallas_call(
        _rowmix_gate_kernel, grid=(cols // ct,),
        in_specs=[pl.BlockSpec(mh.shape, lambda i: (0, 0)), pl.BlockSpec(ml.shape, lambda i: (0, 0)),
                  tile(2 * n1), tile(n1), tile(n1), pl.BlockSpec((1, ct), lambda i: (0, 0))],
        out_specs=tile(n1), out_shape=jax.ShapeDtypeStruct((n1, cols), F32), compiler_params=_params(1),
        name="dft_rowmix_gate",
    )(mh, ml, y, gate.reshape(n1, cols), v.reshape(n1, cols), jnp.tile(skip, (1, ct // c))).reshape(2, n, c)


def _gate_kernel(g_ref, y_ref, v_ref, s_ref, o_ref):
    o_ref[...] = g_ref[...] * (y_ref[...] + s_ref[...] * v_ref[...])


def _hyena_gate(gate, y, v, skip):
    b, n, c = v.shape
    tt = min(2048, n)
    tok = pl.BlockSpec((None, tt, c), lambda bi, ti: (bi, ti, 0))
    return pl.pallas_call(_gate_kernel, grid=(b, n // tt),
                          in_specs=[tok, tok, tok, pl.BlockSpec((1, c), lambda bi, ti: (0, 0))], out_specs=tok,
                          out_shape=jax.ShapeDtypeStruct(v.shape, F32), compiler_params=_params(2),
                          name="hyena_gate")(gate, y, v, skip)


def _hyena_branch(hy, p, tabs):
    n = hy.shape[1]
    x1, x2, v = _token_conv(_short_conv_kernel, hy, (p['hyena_short_w'], p['hyena_short_b'][None, :]),
                            (HYENA_WIDTH,) * 3, "hyena_short_conv")
    taps = _hyena_taps(n, p)
    for o, gate in enumerate((x1, x2)):
        v = _gated_long_conv(gate, v, _filter_spectrum(taps[o], tabs), p['hyena_skip'][o][None, :], tabs)
    return v


def _adaln_kernel(c_ref, w_ref, b_ref, o_ref):
    s = c_ref[...]
    s = s * jax.nn.sigmoid(s)
    o_ref[...] = _dot_split(*_split_bf16(s), *_split_bf16(w_ref[...])) + b_ref[...]


def _adaln(cond, w, b):
    d, width = w.shape
    ct = width // 6
    return pl.pallas_call(
        _adaln_kernel, grid=(6,),
        in_specs=[pl.BlockSpec(cond.shape, lambda i: (0, 0)), pl.BlockSpec((d, ct), lambda i: (0, i)),
                  pl.BlockSpec((1, ct), lambda i: (0, i))],
        out_specs=pl.BlockSpec((cond.shape[0], ct), lambda i: (0, i)),
        out_shape=jax.ShapeDtypeStruct((cond.shape[0], width), F32), compiler_params=_params(1), name="adaln",
    )(cond, w, b[None, :])


def _mod_rows(mod, norm_mix_g, norm_ffn_g, final_g, batch):
    sh1, sc1, g1, sh2, sc2, g2 = jnp.split(mod, 6, axis=-1)
    rows = jnp.stack([norm_mix_g * (1.0 + sc1), sh1, g1, norm_ffn_g * (1.0 + sc2), sh2, g2,
                      jnp.broadcast_to(final_g, g1.shape), jnp.zeros_like(g1)], axis=1)
    return jnp.broadcast_to(rows, (batch,) + rows.shape[1:])


def kernel(x, c, ctx, c_ctx, ada_w, ada_b, norm_mix_g, norm_ffn_g, w_in, diff_lambda, diff_subln_g, hyena_short_w, hyena_short_b, filt_w1, filt_b1, filt_freq, filt_w2, filt_b2, filt_w3, hyena_skip, conf_dw_w, conf_ln_g, conf_ln_b, mla_q_norm_g, mla_kv_norm_g, mla_w_uq, mla_w_ukv, w_branch, w_out, w_router, w_exp_in, w_exp_out, final_norm_g):
    depth = w_in.shape[0]
    batch, n_lat, d = x.shape
    n_ctx = ctx.shape[1]
    rope_lat = _rope_operands(n_lat, identity=False)
    rope_ctx = _rope_operands(n_ctx, identity=True)
    dft_lat, dft_ctx = _dft_tables(2 * n_lat), _dft_tables(2 * n_ctx)
    cond = jnp.concatenate([c, c_ctx[None], jnp.zeros((MOD_ROWS - batch - 1, d), F32)], axis=0)
    tile_lat, tile_ctx = min(1024, n_lat), min(256, n_ctx)
    h_lat, h_ctx = x, ctx
    for l in range(depth):
        last = l == depth - 1
        p = dict(w_in=w_in[l], diff_subln_g=diff_subln_g[l], hyena_short_w=hyena_short_w[l],
                 hyena_short_b=hyena_short_b[l], filt_w1=filt_w1[l], filt_b1=filt_b1[l], filt_freq=filt_freq[l],
                 filt_w2=filt_w2[l], filt_b2=filt_b2[l], filt_w3=filt_w3[l], hyena_skip=hyena_skip[l],
                 conf_dw_w=conf_dw_w[l], conf_ln_g=conf_ln_g[l], conf_ln_b=conf_ln_b[l],
                 mla_q_norm_g=mla_q_norm_g[l], mla_kv_norm_g=mla_kv_norm_g[l], mla_w_uq=mla_w_uq[l],
                 mla_w_ukv=mla_w_ukv[l], w_branch=w_branch[l], w_out=w_out[l], w_router=w_router[l],
                 w_exp_in=w_exp_in[l], w_exp_out=w_exp_out[l])
        ada = _adaln(cond, ada_w[l], ada_b[l])
        mod_lat = _mod_rows(ada[:batch], norm_mix_g[l], norm_ffn_g[l], final_norm_g, batch)
        mod_ctx = _mod_rows(ada[batch:batch + 1], norm_mix_g[l], norm_ffn_g[l], final_norm_g, batch)
        lam_init = 0.8 - 0.6 * math.exp(-0.3 * l)
        lq1, lk1, lq2, lk2 = diff_lambda[l].astype(F32)
        lam = jnp.reshape(jnp.exp(jnp.sum(lq1 * lk1)) - jnp.exp(jnp.sum(lq2 * lk2)) + lam_init, (1,))
        w_inp, w_mrg = _inproj_weights(p), _merge_weights(p, lam_init)

        qdT_l, kd_l, vdT_l, qmT_l, km_l, vmT_l, hy_l, glu_l = _inproj(h_lat, mod_lat, w_inp, rope_lat, tile=tile_lat)
        qdT_c, kd_c, vdT_c, qmT_c, km_c, vmT_c, hy_c, glu_c = _inproj(h_ctx, mod_ctx, w_inp, rope_ctx, tile=tile_ctx)
        a_lat = _flash_attention(lam, qdT_l, kd_c, vdT_c, kd_l, vdT_l, n_maps=2, tq=min(512, n_lat))
        m_lat = _flash_attention(lam, qmT_l, km_c, vmT_c, km_l, vmT_l, n_maps=1, tq=min(1024, n_lat))
        h_lat, u2_lat, lg_lat = _merge(h_lat, mod_lat, a_lat, _hyena_branch(hy_l, p, dft_lat), _conformer_branch(glu_l, p),
                                       m_lat, w_mrg, tile=min(256, n_lat))
        w_ei, w_eo = p['w_exp_in'].astype(BF16), p['w_exp_out'].astype(BF16)
        h_lat = _expert_choice_ffn(h_lat, mod_lat, u2_lat, lg_lat, w_ei, w_eo, final_norm=last)
        if not last:
            a_ctx = _flash_attention(lam, qdT_c, kd_c, vdT_c, None, None, n_maps=2, tq=n_ctx)
            m_ctx = _flash_attention(lam, qmT_c, km_c, vmT_c, None, None, n_maps=1, tq=n_ctx)
            h_ctx, u2_ctx, lg_ctx = _merge(h_ctx, mod_ctx, a_ctx, _hyena_branch(hy_c, p, dft_ctx),
                                           _conformer_branch(glu_c, p), m_ctx, w_mrg, tile=tile_ctx)
            h_ctx = _expert_choice_ffn(h_ctx, mod_ctx, u2_ctx, lg_ctx, w_ei, w_eo)
    return h_lat
```
